```python
import math
import functools
import jax
import jax.numpy as jnp
from jax import lax
import numpy as np

D_MODEL = 1024
BATCH = 1
SEQ = 16384
DEPTH = 1
DEC_BATCH = 128
DEC_SEQ = 1
PAST_LEN = 8192
PAGE_SIZE = 128

HEAD_DIM = 64
HEADS_PER_GROUP = 8
DIL_GROUPS = ((128, 1), (512, 4), (2048, 16))
N_DIL_GROUPS = len(DIL_GROUPS)
N_ATTN_HEADS = N_DIL_GROUPS * HEADS_PER_GROUP
QKV_WIDTH = N_ATTN_HEADS * HEAD_DIM
ATTN_OUT_WIDTH = HEADS_PER_GROUP * HEAD_DIM
SCALE = HEAD_DIM ** -0.5
POOL_WINDOWS = (2, 4, 8, 16)
N_POOL_GROUPS = len(POOL_WINDOWS)
POOL_WIDTH = D_MODEL // 2
POOL_GROUP_WIDTH = POOL_WIDTH // N_POOL_GROUPS
POOL_BUF = max(POOL_WINDOWS) - 1
SPLITS = (QKV_WIDTH, 2 * QKV_WIDTH, 3 * QKV_WIDTH, 3 * QKV_WIDTH + POOL_WIDTH,
          3 * QKV_WIDTH + POOL_WIDTH + D_MODEL)
IN_WIDTH = 3 * QKV_WIDTH + POOL_WIDTH + 2 * D_MODEL
NUM_BUCKETS = 32
MAX_DISTANCE = 2048
N_EXPERTS = 32
TOP_K = 4
D_FF = D_MODEL
SWIGLU_LIMIT = 7.0
SWIGLU_ALPHA = 1.702
MOE_BLOCK = 128
N_ADA = 6
EPS = 1e-6
NEG_INF = -1e30

kernel_name = "hybrid_dilated_attn_pool_moe_step"


def rms_norm(x, g):
    xf = x.astype(jnp.float32)
    y = xf * lax.rsqrt(jnp.mean(xf * xf, axis=-1, keepdims=True) + EPS)
    return (y * g.astype(jnp.float32)).astype(x.dtype)


def t5_bucket(dist):
    max_exact = NUM_BUCKETS // 2
    d = dist.astype(jnp.int32)
    ratio = jnp.log(jnp.maximum(d, 1).astype(jnp.float32) / max_exact) / math.log(MAX_DISTANCE / max_exact)
    large = jnp.minimum(max_exact + (ratio * (NUM_BUCKETS - max_exact)).astype(jnp.int32), NUM_BUCKETS - 1)
    return jnp.where(d < max_exact, d, large)


def softmax_lse(logits):
    m = jnp.max(logits, axis=-1, keepdims=True)
    p = jnp.exp(logits - m)
    l = jnp.sum(p, axis=-1, keepdims=True)
    return p / l, (m + jnp.log(l))[..., 0]


def dilated_group_prompt(q, k, v, window, dil, bias_tab):
    n_b, s, h, e = q.shape
    blk = window // dil
    sub_len = -(-s // (dil * blk)) * blk
    n_blk = sub_len // blk
    pad = sub_len * dil - s

    def strided(t):
        t = jnp.pad(t, ((0, 0), (0, pad), (0, 0), (0, 0)))
        return t.reshape(n_b, n_blk, blk, dil, h, e).transpose(0, 3, 1, 2, 4, 5)

    def with_prev(t):
        prev = jnp.pad(t, ((0, 0), (0, 0), (1, 0), (0, 0), (0, 0), (0, 0)))[:, :, :-1]
        return jnp.concatenate([prev, t], axis=3)

    qs = strided(q)
    ks = with_prev(strided(k))
    vs = with_prev(strided(v))
    steps = jnp.arange(blk)[:, None] + blk - jnp.arange(2 * blk)[None, :]
    band = (steps >= 0) & (steps <= blk)
    has_prev = (jnp.arange(2 * blk) >= blk)[None, None, :] | (jnp.arange(n_blk) > 0)[:, None, None]
    mask = band[None] & has_prev
    bias = bias_tab[t5_bucket(jnp.maximum(steps, 0) * dil)].astype(jnp.float32).transpose(2, 0, 1)
    logits = jnp.einsum('brnqhe,brnkhe->brnhqk', qs, ks).astype(jnp.float32) * SCALE + bias
    logits = jnp.where(mask[None, None, :, None], logits, NEG_INF)
    p, lse = softmax_lse(logits)
    o = jnp.einsum('brnhqk,brnkhe->brnqhe', p.astype(v.dtype), vs)
    o = o.transpose(0, 2, 3, 1, 4, 5).reshape(n_b, sub_len * dil, h, e)[:, :s]
    lse = lse.transpose(0, 2, 4, 1, 3).reshape(n_b, sub_len * dil, h)[:, :s]
    return o, lse


def dilated_group_sample(q, k, v, kv_buf, window, dil, bias_tab):
    t_new = q.shape[1]
    w_buf = kv_buf.shape[1]
    n_keys = window // dil + 1
    kv_all = jnp.concatenate([kv_buf.astype(k.dtype), jnp.stack([k, v], axis=2)], axis=1)
    j = jnp.arange(n_keys)
    idx = w_buf + jnp.arange(t_new)[:, None] - dil * j[None, :]
    kv_sel = jnp.take(kv_all, jnp.maximum(idx, 0), axis=1)
    bias = bias_tab[t5_bucket(dil * j)].astype(jnp.float32).T
    logits = jnp.einsum('nthe,ntjhe->nthj', q, kv_sel[:, :, :, 0]).astype(jnp.float32) * SCALE + bias
    logits = jnp.where((idx >= 0)[None, :, None, :], logits, NEG_INF)
    p, lse = softmax_lse(logits)
    o = jnp.einsum('nthj,ntjhe->nthe', p.astype(v.dtype), kv_sel[:, :, :, 1])
    return o, lse


def combine_groups(outs, lses):
    w = jax.nn.softmax(jnp.stack(lses, 0), axis=0)
    o = jnp.einsum('gnlh,gnlhe->nlhe', w.astype(outs[0].dtype), jnp.stack(outs, 0))
    return o.reshape(o.shape[0], o.shape[1], ATTN_OUT_WIDTH)


def attn_prompt(q, k, v, rel_bias):
    outs, lses, states = [], [], []
    for g, (window, dil) in enumerate(DIL_GROUPS):
        tab = rel_bias[:, g * HEADS_PER_GROUP:(g + 1) * HEADS_PER_GROUP]
        o, lse = dilated_group_prompt(q[:, :, g], k[:, :, g], v[:, :, g], window, dil, tab)
        keep = min(window, q.shape[1])
        outs.append(o)
        lses.append(lse)
        states.append(jnp.stack([k[:, -keep:, g], v[:, -keep:, g]], axis=2))
    return combine_groups(outs, lses), states


def attn_sample(q, k, v, kv_bufs, rel_bias):
    outs, lses, states = [], [], []
    for g, (window, dil) in enumerate(DIL_GROUPS):
        tab = rel_bias[:, g * HEADS_PER_GROUP:(g + 1) * HEADS_PER_GROUP]
        o, lse = dilated_group_sample(q[:, :, g], k[:, :, g], v[:, :, g], kv_bufs[g], window, dil, tab)
        outs.append(o)
        lses.append(lse)
        states.append(jnp.stack([k[:, :, g], v[:, :, g]], axis=2))
    return combine_groups(outs, lses), states


def multi_scale_pool(u, first_pos):
    n_b, length, _ = u.shape
    ug = u.reshape(n_b, length, N_POOL_GROUPS, POOL_GROUP_WIDTH).astype(jnp.float32)
    cs = jnp.pad(jnp.cumsum(ug, axis=1), ((0, 0), (1, 0), (0, 0), (0, 0)))
    hi = jnp.arange(length) + 1
    outs = []
    for g, w in enumerate(POOL_WINDOWS):
        lo = jnp.maximum(hi - w, 0)
        cnt = jnp.minimum(hi + first_pos, w).astype(jnp.float32)
        s = cs[:, hi, g] - cs[:, lo, g]
        outs.append(s / cnt[None, :, None] - ug[:, :, g])
    return jnp.stack(outs, axis=2)


def pool_prompt(u):
    return multi_scale_pool(u, 0), u[:, -POOL_BUF:]


def pool_sample(u, buf):
    cat = jnp.concatenate([buf.astype(u.dtype), u], axis=1)
    pooled = multi_scale_pool(cat, PAST_LEN - POOL_BUF)[:, POOL_BUF:]
    return pooled, cat[:, -POOL_BUF:]


def moe_ffn(x, w_router, b_router, w_gate_up, b_gate_up, w_down, b_down):
    n_tok = x.shape[0]
    n_asg = n_tok * TOP_K
    logits = (x @ w_router).astype(jnp.float32) + b_router.astype(jnp.float32)
    top_val, top_idx = lax.top_k(logits, TOP_K)
    gates = jax.nn.softmax(top_val, axis=-1)
    flat_e = top_idx.reshape(n_asg)
    flat_tok = jnp.repeat(jnp.arange(n_tok, dtype=jnp.int32), TOP_K)
    order = jnp.argsort(flat_e)
    se = flat_e[order]
    counts = jnp.bincount(flat_e, length=N_EXPERTS)
    start = jnp.cumsum(counts) - counts
    padded = (counts + MOE_BLOCK - 1) // MOE_BLOCK * MOE_BLOCK
    pad_end = jnp.cumsum(padded)
    pad_start = pad_end - padded
    slot = pad_start[se] + jnp.arange(n_asg) - start[se]
    n_blocks = -(-n_asg // MOE_BLOCK) + N_EXPERTS
    cap = n_blocks * MOE_BLOCK
    tok_buf = jnp.zeros((cap,), jnp.int32).at[slot].set(flat_tok[order])
    gate_buf = jnp.zeros((cap,), jnp.float32).at[slot].set(gates.reshape(n_asg)[order])
    block_expert = jnp.minimum(
        jnp.searchsorted(pad_end, jnp.arange(n_blocks) * MOE_BLOCK, side='right'), N_EXPERTS - 1)
    xb = x[tok_buf].reshape(n_blocks, MOE_BLOCK, D_MODEL)

    def expert_block(args):
        xi, e = args
        hgu = xi @ w_gate_up[e] + b_gate_up[e]
        hg, hu = jnp.split(hgu, 2, axis=-1)
        hg = jnp.minimum(hg, SWIGLU_LIMIT)
        hu = jnp.clip(hu, -SWIGLU_LIMIT, SWIGLU_LIMIT)
        act = hg * jax.nn.sigmoid(SWIGLU_ALPHA * hg) * (hu + 1)
        return act @ w_down[e] + b_down[e]

    yb = lax.map(expert_block, (xb, block_expert)).reshape(cap, D_MODEL)
    return jnp.zeros_like(x).at[tok_buf].add(yb * gate_buf[:, None].astype(yb.dtype))


def decoder_layer(x, c, attn_fn, pool_fn, w_ada, b_ada, norm_mix_g, norm_ffn_g, w_in, q_norm_g,
                  k_norm_g, w_pool_mix, pool_scale, w_up_attn, w_up_pool, w_out, w_router, b_router,
                  w_gate_up, b_gate_up, w_down, b_down):
    n_b, length, _ = x.shape
    mod = (jax.nn.silu(c) @ w_ada + b_ada)[:, None, :]
    sh1, sc1, gt1, sh2, sc2, gt2 = jnp.split(mod, N_ADA, axis=-1)
    h = rms_norm(x, norm_mix_g) * (1 + sc1) + sh1
    q, k, v, u, g_a, g_p = jnp.split(h @ w_in, list(SPLITS), axis=-1)
    shp = (n_b, length, N_DIL_GROUPS, HEADS_PER_GROUP, HEAD_DIM)
    q = rms_norm(q.reshape(shp), q_norm_g)
    k = rms_norm(k.reshape(shp), k_norm_g)
    v = v.reshape(shp)
    attn_o, attn_state = attn_fn(q, k, v)
    pooled, pool_state = pool_fn(u)
    pool_o = jnp.einsum('nlgc,gcd->nlgd', pooled.astype(x.dtype), w_pool_mix).reshape(
        n_b, length, POOL_WIDTH) * pool_scale
    merged = jax.nn.sigmoid(g_a) * (attn_o @ w_up_attn) + jax.nn.sigmoid(g_p) * (pool_o @ w_up_pool)
    x = x + gt1 * (merged @ w_out)
    h2 = rms_norm(x, norm_ffn_g) * (1 + sc2) + sh2
    ffn = moe_ffn(h2.reshape(n_b * length, D_MODEL), w_router, b_router, w_gate_up, b_gate_up,
                  w_down, b_down).reshape(n_b, length, D_MODEL)
    x = x + gt2 * ffn
    return x, attn_state, pool_state


def setup_inputs(seed: int = 0) -> dict:
    key = jax.random.key(seed)
    ks = jax.random.split(key, 27)

    def nrm(k, shape, scale):
        return jax.random.normal(k, shape, jnp.float32) * scale

    def kv_shape(window):
        return (DEPTH, DEC_BATCH, min(window, PAST_LEN), 2, HEADS_PER_GROUP, HEAD_DIM)

    L = DEPTH
    return {
        'x_prompt': nrm(ks[0], (BATCH, SEQ, D_MODEL), 1.0),
        'x_sample': nrm(ks[1], (DEC_BATCH, DEC_SEQ, D_MODEL), 1.0),
        'cache_kv_w128': nrm(ks[2], kv_shape(DIL_GROUPS[0][0]), 1.0),
        'cache_kv_w512': nrm(ks[3], kv_shape(DIL_GROUPS[1][0]), 1.0),
        'cache_kv_w2048': nrm(ks[4], kv_shape(DIL_GROUPS[2][0]), 1.0),
        'state_pool': nrm(ks[5], (L, DEC_BATCH, POOL_BUF, POOL_WIDTH), 1.0),
        'c_prompt': nrm(ks[6], (BATCH, D_MODEL), 1.0),
        'c_sample': nrm(ks[7], (DEC_BATCH, D_MODEL), 1.0),
        'w_ada': nrm(ks[8], (L, D_MODEL, N_ADA * D_MODEL), 0.25 * D_MODEL ** -0.5),
        'b_ada': nrm(ks[9], (L, N_ADA * D_MODEL), 0.01),
        'norm_mix_g': 1.0 + nrm(ks[10], (L, D_MODEL), 0.02),
        'norm_ffn_g': 1.0 + nrm(ks[11], (L, D_MODEL), 0.02),
        'w_in': nrm(ks[12], (L, D_MODEL, IN_WIDTH), D_MODEL ** -0.5),
        'q_norm_g': 1.0 + nrm(ks[13], (L, HEAD_DIM), 0.02),
        'k_norm_g': 1.0 + nrm(ks[14], (L, HEAD_DIM), 0.02),
        'rel_bias': nrm(ks[15], (NUM_BUCKETS, N_ATTN_HEADS), 0.2),
        'w_pool_mix': nrm(ks[16], (L, N_POOL_GROUPS, POOL_GROUP_WIDTH, POOL_GROUP_WIDTH),
                          POOL_GROUP_WIDTH ** -0.5),
        'pool_scale': 1.0 + nrm(ks[17], (L, POOL_WIDTH), 0.02),
        'w_up_attn': nrm(ks[18], (L, ATTN_OUT_WIDTH, D_MODEL), ATTN_OUT_WIDTH ** -0.5),
        'w_up_pool': nrm(ks[19], (L, POOL_WIDTH, D_MODEL), POOL_WIDTH ** -0.5),
        'w_out': nrm(ks[20], (L, D_MODEL, D_MODEL), D_MODEL ** -0.5),
        'w_router': nrm(ks[21], (L, D_MODEL, N_EXPERTS), D_MODEL ** -0.5),
        'b_router': nrm(ks[22], (L, N_EXPERTS), 0.01),
        'w_gate_up': nrm(ks[23], (L, N_EXPERTS, D_MODEL, 2 * D_FF), D_MODEL ** -0.5),
        'b_gate_up': nrm(ks[24], (L, N_EXPERTS, 2 * D_FF), 0.01),
        'w_down': nrm(ks[25], (L, N_EXPERTS, D_FF, D_MODEL), D_FF ** -0.5),
        'b_down': nrm(ks[26], (L, N_EXPERTS, D_MODEL), 0.01),
    }


def reference(x_prompt, x_sample, cache_kv_w128, cache_kv_w512, cache_kv_w2048, state_pool, c_prompt,
              c_sample, w_ada, b_ada, norm_mix_g, norm_ffn_g, w_in, q_norm_g, k_norm_g, rel_bias,
              w_pool_mix, pool_scale, w_up_attn, w_up_pool, w_out, w_router, b_router, w_gate_up,
              b_gate_up, w_down, b_down):
    y_prompt, y_sample = x_prompt, x_sample
    kv_p = [[] for _ in DIL_GROUPS]
    kv_s = [[] for _ in DIL_GROUPS]
    pool_p, pool_s = [], []
    for layer in range(DEPTH):
        lw = (w_ada[layer], b_ada[layer], norm_mix_g[layer], norm_ffn_g[layer], w_in[layer],
              q_norm_g[layer], k_norm_g[layer], w_pool_mix[layer], pool_scale[layer],
              w_up_attn[layer], w_up_pool[layer], w_out[layer], w_router[layer], b_router[layer],
              w_gate_up[layer], b_gate_up[layer], w_down[layer], b_down[layer])
        y_prompt, st_a, st_p = decoder_layer(
            y_prompt, c_prompt, functools.partial(attn_prompt, rel_bias=rel_bias), pool_prompt, *lw)
        bufs = (cache_kv_w128[layer], cache_kv_w512[layer], cache_kv_w2048[layer])
        y_sample, sa, sp = decoder_layer(
            y_sample, c_sample, functools.partial(attn_sample, kv_bufs=bufs, rel_bias=rel_bias),
            functools.partial(pool_sample, buf=state_pool[layer]), *lw)
        for g in range(N_DIL_GROUPS):
            kv_p[g].append(st_a[g])
            kv_s[g].append(sa[g])
        pool_p.append(st_p)
        pool_s.append(sp)
    return (y_prompt, y_sample, jnp.stack(kv_p[0]), jnp.stack(kv_p[1]), jnp.stack(kv_p[2]),
            jnp.stack(pool_p), jnp.stack(kv_s[0]), jnp.stack(kv_s[1]), jnp.stack(kv_s[2]),
            jnp.stack(pool_s))
```

```python
import functools
import math

import jax
import jax.numpy as jnp
from jax import lax
from jax.experimental import pallas as pl
from jax.experimental.pallas import tpu as pltpu

F32 = jnp.float32
BF16 = jnp.bfloat16

D_MODEL = 1024
HEAD_DIM = 64
HEADS = 8
GROUP_W = HEADS * HEAD_DIM
DIL_GROUPS = ((128, 1), (512, 4), (2048, 16))
N_GROUPS = len(DIL_GROUPS)
QKV_W = N_GROUPS * GROUP_W
ATT_BLK = 128
POOL_WINDOWS = (2, 4, 8, 16)
POOL_W = 512
POOL_GW = 128
POOL_BUF = 15
OFF_K, OFF_V = QKV_W, 2 * QKV_W
OFF_U = 3 * QKV_W
OFF_GA = OFF_U + POOL_W
OFF_GP = OFF_GA + D_MODEL
IN_W = OFF_GP + D_MODEL
NUM_BUCKETS = 32
MAX_DISTANCE = 2048
N_EXPERTS = 32
TOP_K = 4
SWIGLU_LIMIT = 7.0
SWIGLU_ALPHA = 1.702
N_ADA = 6
EPS = 1e-6
NEG_INF = -1e30
PAST_LEN = 8192
SCALE = HEAD_DIM ** -0.5

LANES = 128
ROW_CHUNK = 16
TM_PROMPT = 512
TM_SAMPLE = 128
FFN_BLOCK = 256
SEL_CHUNK = 512
VMEM_LIMIT = 56 * 1024 * 1024


def _cparams(n_axes):
    return pltpu.CompilerParams(dimension_semantics=("arbitrary",) * n_axes,
                                vmem_limit_bytes=VMEM_LIMIT)


def _const_spec(shape):
    nd = len(shape)
    return pl.BlockSpec(shape, lambda *_: (0,) * nd)


def _ada_kernel(c_ref, w_ref, b_ref, o_ref):
    c = c_ref[...]
    s = c * jax.nn.sigmoid(c)
    o_ref[...] = jnp.dot(s.astype(BF16), w_ref[...].astype(BF16),
                         preferred_element_type=F32) + b_ref[...]


def _ada(c_all, w_ada, b_ada):
    rows = c_all.shape[0]
    n = w_ada.shape[1]
    tn = 1536
    return pl.pallas_call(
        _ada_kernel,
        grid=(n // tn,),
        in_specs=[pl.BlockSpec((rows, D_MODEL), lambda j: (0, 0)),
                  pl.BlockSpec((D_MODEL, tn), lambda j: (0, j)),
                  pl.BlockSpec((1, tn), lambda j: (0, j))],
        out_specs=pl.BlockSpec((rows, tn), lambda j: (0, j)),
        out_shape=jax.ShapeDtypeStruct((rows, n), F32),
        compiler_params=_cparams(1),
        name="ada",
    )(c_all, w_ada, b_ada.reshape(1, n))


def _proj_kernel(x_ref, g_ref, sc_ref, sh_ref, w_ref, bd_ref, qg_ref, kg_ref,
                 *refs, tm, dils, st_rows):
    q_refs, k_refs, v_refs = refs[0:3], refs[3:6], refs[6:9]
    u_ref, sga_ref, sgp_ref = refs[9:12]
    st_refs = refs[12:15]
    scr = refs[15]

    x = x_ref[...]
    ms = jnp.mean(x * x, axis=-1, keepdims=True)
    h = x * lax.rsqrt(ms + EPS) * g_ref[...] * (1.0 + sc_ref[...]) + sh_ref[...]
    hb = h.astype(BF16)

    def proj(off, width):
        return jnp.dot(hb, w_ref[:, off:off + width], preferred_element_type=F32)

    def head_norm(z, gain_ref):
        ss = jnp.dot((z * z).astype(BF16), bd_ref[...], preferred_element_type=F32)
        return z * lax.rsqrt(ss * (1.0 / HEAD_DIM) + EPS) * gain_ref[...]

    def put(out_ref, val, d):
        if d == 1:
            out_ref[...] = val.astype(out_ref.dtype)
        else:
            for c in range(GROUP_W // LANES):
                scr[c] = val[:, c * LANES:(c + 1) * LANES]
            for r in range(d):
                for c in range(GROUP_W // LANES):
                    col = r * GROUP_W + c * LANES
                    out_ref[:, col:col + LANES] = (
                        scr[c, pl.ds(r, tm // d, stride=d), :].astype(out_ref.dtype))

    for g, d in enumerate(dils):
        qn = head_norm(proj(g * GROUP_W, GROUP_W), qg_ref)
        put(q_refs[g], qn, d)
        kn = head_norm(proj(OFF_K + g * GROUP_W, GROUP_W), kg_ref)
        put(k_refs[g], kn, d)
        v = proj(OFF_V + g * GROUP_W, GROUP_W)
        put(v_refs[g], v, d)
        rb = st_rows[g]
        st_refs[g][:, 0:GROUP_W] = kn[tm - rb:, :]
        st_refs[g][:, GROUP_W:2 * GROUP_W] = v[tm - rb:, :]

    u_ref[...] = proj(OFF_U, POOL_W)
    sga_ref[...] = jax.nn.sigmoid(proj(OFF_GA, D_MODEL)).astype(BF16)
    sgp_ref[...] = jax.nn.sigmoid(proj(OFF_GP, D_MODEL)).astype(BF16)


def _mod_spec(per_row, tm):
    if per_row:
        return pl.BlockSpec((tm, D_MODEL), lambda i: (i, 0))
    return pl.BlockSpec((1, D_MODEL), lambda i: (0, 0))


def _proj(x, g1, sc1, sh1, w_in_bf, bdiag, qg, kg, *, tm, dils, per_row):
    s = x.shape[0]
    nt = s // tm
    wins = tuple(min(w, s) for w, _ in DIL_GROUPS)
    st_rows = tuple(min(tm, w) for w in wins)

    def res_spec(d):
        return pl.BlockSpec((tm // d, d * GROUP_W), lambda i: (i, 0))

    def st_spec(w, rb):
        first = nt - w // rb
        return pl.BlockSpec((rb, 2 * GROUP_W), lambda i: (jnp.maximum(i - first, 0), 0))

    qkv_shapes = [jax.ShapeDtypeStruct((s // d, d * GROUP_W), BF16) for d in dils]
    out_shape = (qkv_shapes * 3
                 + [jax.ShapeDtypeStruct((s, POOL_W), F32),
                    jax.ShapeDtypeStruct((s, D_MODEL), BF16),
                    jax.ShapeDtypeStruct((s, D_MODEL), BF16)]
                 + [jax.ShapeDtypeStruct((w, 2 * GROUP_W), F32) for w in wins])
    out_specs = ([res_spec(d) for d in dils] * 3
                 + [pl.BlockSpec((tm, POOL_W), lambda i: (i, 0)),
                    pl.BlockSpec((tm, D_MODEL), lambda i: (i, 0)),
                    pl.BlockSpec((tm, D_MODEL), lambda i: (i, 0))]
                 + [st_spec(w, rb) for w, rb in zip(wins, st_rows)])
    in_specs = [pl.BlockSpec((tm, D_MODEL), lambda i: (i, 0)),
                _const_spec((1, D_MODEL)),
                _mod_spec(per_row, tm), _mod_spec(per_row, tm),
                pl.BlockSpec((D_MODEL, IN_W), lambda i: (0, 0), pipeline_mode=pl.Buffered(1)),
                _const_spec((GROUP_W, GROUP_W)),
                _const_spec((1, GROUP_W)), _const_spec((1, GROUP_W))]
    outs = pl.pallas_call(
        functools.partial(_proj_kernel, tm=tm, dils=dils, st_rows=st_rows),
        grid=(nt,),
        in_specs=in_specs,
        out_specs=out_specs,
        out_shape=out_shape,
        scratch_shapes=[pltpu.VMEM((GROUP_W // LANES, tm, LANES), F32)],
        compiler_params=_cparams(1),
        name="proj",
    )(x, g1, sc1, sh1, w_in_bf, bdiag, qg, kg)
    return outs[0:3], outs[3:6], outs[6:9], outs[9], outs[10], outs[11], outs[12:15]


def _attn_kernel(q_ref, kp_ref, kc_ref, vp_ref, vc_ref, bias_ref, o_ref, lse_ref):
    i = pl.program_id(1)
    q = q_ref[...]
    k = jnp.concatenate([kp_ref[...], kc_ref[...]], axis=0)
    v = jnp.concatenate([vp_ref[...], vc_ref[...]], axis=0)
    col = lax.broadcasted_iota(jnp.int32, (ATT_BLK, 2 * ATT_BLK), 1)
    no_prev = jnp.where((col < ATT_BLK) & (i == 0), NEG_INF, 0.0)
    outs, lses = [], []
    for h in range(HEADS):
        sl = slice(h * HEAD_DIM, (h + 1) * HEAD_DIM)
        s = lax.dot_general(q[:, sl], k[:, sl], (((1,), (1,)), ((), ())),
                            preferred_element_type=F32)
        s = s + bias_ref[h] + no_prev
        m = jnp.max(s, axis=-1, keepdims=True)
        p = jnp.exp(s - m)
        l = jnp.sum(p, axis=-1, keepdims=True)
        o = jnp.dot(p.astype(BF16), v[:, sl], preferred_element_type=F32)
        outs.append(o / l)
        lses.append(m + jnp.log(l))
    o_ref[...] = jnp.concatenate(outs, axis=-1).astype(o_ref.dtype)
    lse_ref[...] = jnp.concatenate(
        lses + [jnp.zeros((ATT_BLK, LANES - HEADS), F32)], axis=-1)


def _attn_prompt(q, k, v, bias, d):
    rows = q.shape[0]
    nblk = rows // ATT_BLK
    cur = pl.BlockSpec((ATT_BLK, GROUP_W), lambda r, i: (i, r))
    prev = pl.BlockSpec((ATT_BLK, GROUP_W), lambda r, i: (jnp.maximum(i - 1, 0), r))
    return pl.pallas_call(
        _attn_kernel,
        grid=(d, nblk),
        in_specs=[cur, prev, cur, prev, cur,
                  pl.BlockSpec((HEADS, ATT_BLK, 2 * ATT_BLK), lambda r, i: (0, 0, 0))],
        out_specs=[pl.BlockSpec((ATT_BLK, GROUP_W), lambda r, i: (i, r)),
                   pl.BlockSpec((ATT_BLK, LANES), lambda r, i: (i, r))],
        out_shape=[jax.ShapeDtypeStruct((rows, d * GROUP_W), BF16),
                   jax.ShapeDtypeStruct((rows, d * LANES), F32)],
        compiler_params=_cparams(2),
        name=f"attn_d{d}",
    )(q, k, k, v, v, bias)


def _attn_sample_kernel(q_ref, kn_ref, vn_ref, bias_ref, c0_ref, c1_ref, c2_ref,
                        o_ref, lse_ref, m_scr, l_scr, acc_scr):
    s = pl.program_id(0)
    n_steps = pl.num_programs(0)
    shape = q_ref.shape[1:]

    def bsum(a):
        return jnp.broadcast_to(jnp.sum(a, axis=-1, keepdims=True), shape)

    @pl.when(s == 0)
    def _():
        for g in range(N_GROUPS):
            m_scr[g] = bsum(q_ref[g] * kn_ref[g]) + bias_ref[g, 0]
            l_scr[g] = jnp.ones(shape, F32)
            acc_scr[g] = vn_ref[g]

    for g, c_ref in enumerate((c0_ref, c1_ref, c2_ref)):
        kj = c_ref[0, :, 0, 0]
        vj = c_ref[0, :, 0, 1]
        b = bias_ref[g, pl.ds(ATT_BLK - s, 1)][0]
        sc = bsum(q_ref[g] * kj) + b
        m_old = m_scr[g]
        m_new = jnp.maximum(m_old, sc)
        alpha = jnp.exp(m_old - m_new)
        p = jnp.exp(sc - m_new)
        l_scr[g] = l_scr[g] * alpha + p
        acc_scr[g] = acc_scr[g] * alpha + p * vj
        m_scr[g] = m_new

    @pl.when(s == n_steps - 1)
    def _():
        for g in range(N_GROUPS):
            o_ref[g] = acc_scr[g] / l_scr[g]
            lse_ref[g] = m_scr[g] + jnp.log(l_scr[g])


def _attn_sample(q3, kn3, vn3, bias3, caches):
    n = q3.shape[1]
    blk = (N_GROUPS, n, HEADS, HEAD_DIM)
    full = pl.BlockSpec(blk, lambda s: (0, 0, 0, 0))

    def cache_spec(d):
        return pl.BlockSpec((1, n, 1, 2, HEADS, HEAD_DIM), lambda s: (0, 0, s * d, 0, 0, 0))

    return pl.pallas_call(
        _attn_sample_kernel,
        grid=(ATT_BLK,),
        in_specs=[full, full, full,
                  pl.BlockSpec((N_GROUPS, ATT_BLK + 1, HEADS, HEAD_DIM), lambda s: (0, 0, 0, 0))]
                 + [cache_spec(d) for _, d in DIL_GROUPS],
        out_specs=[full, full],
        out_shape=[jax.ShapeDtypeStruct(blk, F32), jax.ShapeDtypeStruct(blk, F32)],
        scratch_shapes=[pltpu.VMEM(blk, F32)] * 3,
        compiler_params=_cparams(1),
        name="attn_sample",
    )(q3, kn3, vn3, bias3, *caches)


def _pool_sample_kernel(st_ref, u_ref, pooled_ref, new_ref):
    u = u_ref[...]
    rows = [st_ref[0, :, j, :] for j in range(POOL_BUF)]
    outs = []
    for g, w in enumerate(POOL_WINDOWS):
        sl = slice(g * POOL_GW, (g + 1) * POOL_GW)
        acc = u[:, sl]
        for j in range(POOL_BUF - (w - 1), POOL_BUF):
            acc = acc + rows[j][:, sl]
        outs.append(acc / float(w) - u[:, sl])
    pooled_ref[...] = jnp.concatenate(outs, axis=-1)
    for j in range(POOL_BUF - 1):
        new_ref[0, :, j, :] = rows[j + 1]
    new_ref[0, :, POOL_BUF - 1, :] = u


def _pool_sample(state, u):
    n = u.shape[0]
    return pl.pallas_call(
        _pool_sample_kernel,
        grid=(1,),
        in_specs=[_const_spec(state.shape), _const_spec(u.shape)],
        out_specs=[_const_spec(u.shape), _const_spec(state.shape)],
        out_shape=[jax.ShapeDtypeStruct((n, POOL_W), F32),
                   jax.ShapeDtypeStruct(state.shape, F32)],
        compiler_params=_cparams(1),
        name="pool_sample",
    )(state, u)


def _post_kernel(*refs, tm, dils, pooled_given, n_valid_steps, aliased):
    it = iter(refs)
    x_ref = next(it)
    o_refs = [next(it) for _ in range(N_GROUPS)]
    lse_refs = [next(it) for _ in range(N_GROUPS)]
    if pooled_given:
        pooled_ref = next(it)
    else:
        u_ref, uh_ref = next(it), next(it)
    sga_ref, sgp_ref = next(it), next(it)
    wpm_ref, psc_ref, wua_ref, wup_ref, wout_ref, exp_ref = (next(it) for _ in range(6))
    gt1_ref, g2_ref, sc2_ref, sh2_ref = (next(it) for _ in range(4))
    wrh_ref, wrl_ref, br_ref = (next(it) for _ in range(3))
    if aliased:
        for _ in range(6):
            next(it)
    x1_ref, h2_ref, a_ref, idx_ref, gk_ref, cnt_ref = (next(it) for _ in range(6))
    ob_scr, ls_scr = next(it), next(it)

    i = pl.program_id(0)

    def compute():
        obs, lss = [], []
        for g, d in enumerate(dils):
            if d == 1:
                obs.append(o_refs[g][...].astype(F32))
                lss.append(lse_refs[g][...])
            else:
                for r in range(d):
                    for c in range(GROUP_W // LANES):
                        col = r * GROUP_W + c * LANES
                        ob_scr[c, pl.ds(r, tm // d, stride=d), :] = (
                            o_refs[g][:, col:col + LANES].astype(F32))
                    ls_scr[pl.ds(r, tm // d, stride=d), :] = (
                        lse_refs[g][:, r * LANES:(r + 1) * LANES])
                obs.append(jnp.concatenate([ob_scr[c] for c in range(GROUP_W // LANES)],
                                           axis=-1))
                lss.append(ls_scr[...])
        mx = jnp.maximum(jnp.maximum(lss[0], lss[1]), lss[2])
        es = [jnp.exp(l - mx) for l in lss]
        den = es[0] + es[1] + es[2]
        attn_o = jnp.zeros((tm, GROUP_W), F32)
        for g in range(N_GROUPS):
            w = es[g] / den
            w_hi = w.astype(BF16)
            w_lo = (w - w_hi.astype(F32)).astype(BF16)
            wexp = (jnp.dot(w_hi, exp_ref[...], preferred_element_type=F32)
                    + jnp.dot(w_lo, exp_ref[...], preferred_element_type=F32))
            attn_o = attn_o + wexp * obs[g]

        if pooled_given:
            pooled = pooled_ref[...]
        else:
            u = u_ref[...]
            halo = jnp.where(i == 0, 0.0, uh_ref[...])
            pos = (lax.broadcasted_iota(jnp.int32, (tm, 1), 0) + i * tm + 1).astype(F32)
            outs = []
            for g, w in enumerate(POOL_WINDOWS):
                sl = slice(g * POOL_GW, (g + 1) * POOL_GW)
                a = jnp.concatenate([halo[:, sl], u[:, sl]], axis=0)
                span = 1
                while span < w:
                    n = a.shape[0] - span
                    a = a[span:, :] + a[:n, :]
                    span *= 2
                off = a.shape[0] - tm
                win_sum = a[off:, :]
                outs.append(win_sum / jnp.minimum(pos, float(w)) - u[:, sl])
            pooled = jnp.concatenate(outs, axis=-1)
        pool_parts = []
        for g in range(len(POOL_WINDOWS)):
            sl = slice(g * POOL_GW, (g + 1) * POOL_GW)
            pool_parts.append(jnp.dot(pooled[:, sl].astype(BF16), wpm_ref[g],
                                      preferred_element_type=F32))
        pool_o = jnp.concatenate(pool_parts, axis=-1) * psc_ref[...]

        up_a = jnp.dot(attn_o.astype(BF16), wua_ref[...], preferred_element_type=F32)
        up_p = jnp.dot(pool_o.astype(BF16), wup_ref[...], preferred_element_type=F32)
        merged = sga_ref[...].astype(F32) * up_a + sgp_ref[...].astype(F32) * up_p
        mo = jnp.dot(merged.astype(BF16), wout_ref[...], preferred_element_type=F32)
        x1 = x_ref[...] + gt1_ref[...] * mo
        x1_ref[...] = x1

        ms = jnp.mean(x1 * x1, axis=-1, keepdims=True)
        h2 = x1 * lax.rsqrt(ms + EPS) * g2_ref[...] * (1.0 + sc2_ref[...]) + sh2_ref[...]
        h2_hi = h2.astype(BF16)
        h2_ref[...] = h2_hi
        h2_lo = (h2 - h2_hi.astype(F32)).astype(BF16)
        logits = (jnp.dot(h2_hi, wrh_ref[...], preferred_element_type=F32)
                  + jnp.dot(h2_lo, wrh_ref[...], preferred_element_type=F32)
                  + jnp.dot(h2_hi, wrl_ref[...], preferred_element_type=F32)
                  + br_ref[...])
        lane = lax.broadcasted_iota(jnp.int32, (tm, LANES), 1).astype(F32)
        work = logits
        vals, ids = [], []
        for _ in range(TOP_K):
            m = jnp.max(work, axis=-1, keepdims=True)
            ik = jnp.min(jnp.where(work == m, lane, float(LANES)), axis=-1, keepdims=True)
            vals.append(m)
            ids.append(ik)
            work = jnp.where(lane == ik, -3e38, work)
        ex = [jnp.exp(v - vals[0]) for v in vals]
        den_k = ex[0] + ex[1] + ex[2] + ex[3]
        a = jnp.zeros((tm, LANES), F32)
        idx = jnp.zeros((tm, LANES), F32)
        gk = jnp.zeros((tm, LANES), F32)
        for kk in range(TOP_K):
            gate = ex[kk] / den_k
            a = a + jnp.where(lane == ids[kk], gate, 0.0)
            idx = jnp.where(lane == float(kk), ids[kk], idx)
            gk = jnp.where(lane == float(kk), gate, gk)
        a_ref[...] = a
        idx_ref[...] = idx
        gk_ref[...] = gk
        cnt = jnp.sum((a > 0.0).astype(F32), axis=0, keepdims=True)
        row = lax.broadcasted_iota(jnp.int32, (8, LANES), 0)
        cnt_ref[0] = jnp.where(row == 0, jnp.broadcast_to(cnt, (8, LANES)), 0.0)

    if n_valid_steps is None:
        compute()
    else:
        pl.when(i < n_valid_steps)(compute)

        @pl.when(i >= n_valid_steps)
        def _():
            x1_ref[...] = jnp.zeros(x1_ref.shape, x1_ref.dtype)
            h2_ref[...] = jnp.zeros(h2_ref.shape, h2_ref.dtype)
            a_ref[...] = jnp.zeros(a_ref.shape, a_ref.dtype)
            idx_ref[...] = jnp.zeros(idx_ref.shape, idx_ref.dtype)
            gk_ref[...] = jnp.zeros(gk_ref.shape, gk_ref.dtype)


def _post(x, o_list, lse_list, pool_in, sga, sgp, wts, mods, *, tm, dils, per_row,
          rows_total, row_block0, cnt_tiles, cnt_block, grid, n_valid_steps, alias_bufs):
    pooled_given = not isinstance(pool_in, tuple)
    nv = grid if n_valid_steps is None else n_valid_steps

    def clamp(i):
        return jnp.minimum(i, nv - 1)

    def tile_spec(width):
        return pl.BlockSpec((tm, width), lambda i: (clamp(i), 0))

    in_specs = [tile_spec(D_MODEL)]
    in_specs += [pl.BlockSpec((tm // d, d * GROUP_W), lambda i: (clamp(i), 0)) for d in dils]
    in_specs += [pl.BlockSpec((tm // d, d * LANES), lambda i: (clamp(i), 0)) for d in dils]
    args = [x, *o_list, *lse_list]
    if pooled_given:
        in_specs.append(tile_spec(POOL_W))
        args.append(pool_in)
    else:
        u = pool_in[0]
        in_specs += [tile_spec(POOL_W),
                     pl.BlockSpec((16, POOL_W),
                                  lambda i: (jnp.maximum(i * (tm // 16) - 1, 0), 0))]
        args += [u, u]
    in_specs += [tile_spec(D_MODEL), tile_spec(D_MODEL)]
    args += [sga, sgp]
    wpm, psc, wua, wup, wout, expand, g2, wrh, wrl, br = wts
    gt1, sc2, sh2 = mods

    def mspec():
        if per_row:
            return pl.BlockSpec((tm, D_MODEL), lambda i: (clamp(i), 0))
        return _const_spec((1, D_MODEL))

    in_specs += [_const_spec(wpm.shape), _const_spec(psc.shape), _const_spec(wua.shape),
                 _const_spec(wup.shape), _const_spec(wout.shape), _const_spec(expand.shape),
                 mspec(), _const_spec(g2.shape), mspec(), mspec(),
                 _const_spec(wrh.shape), _const_spec(wrl.shape), _const_spec(br.shape)]
    args += [wpm, psc, wua, wup, wout, expand, gt1, g2, sc2, sh2, wrh, wrl, br]
    aliases = {}
    if alias_bufs is not None:
        base = len(args)
        in_specs += [pl.BlockSpec(memory_space=pl.ANY)] * 6
        args += list(alias_bufs)
        aliases = {base + j: j for j in range(6)}

    def out_spec(width):
        return pl.BlockSpec((tm, width), lambda i: (row_block0 + i, 0))

    out_specs = [out_spec(D_MODEL), out_spec(D_MODEL), out_spec(LANES), out_spec(LANES),
                 out_spec(LANES),
                 pl.BlockSpec((1, 8, LANES),
                              lambda i: (cnt_block if cnt_block is not None else i, 0, 0))]
    out_shape = [jax.ShapeDtypeStruct((rows_total, D_MODEL), F32),
                 jax.ShapeDtypeStruct((rows_total, D_MODEL), BF16),
                 jax.ShapeDtypeStruct((rows_total, LANES), F32),
                 jax.ShapeDtypeStruct((rows_total, LANES), F32),
                 jax.ShapeDtypeStruct((rows_total, LANES), F32),
                 jax.ShapeDtypeStruct((cnt_tiles, 8, LANES), F32)]
    return pl.pallas_call(
        functools.partial(_post_kernel, tm=tm, dils=dils, pooled_given=pooled_given,
                          n_valid_steps=n_valid_steps, aliased=alias_bufs is not None),
        grid=(grid,),
        in_specs=in_specs,
        out_specs=out_specs,
        out_shape=out_shape,
        scratch_shapes=[pltpu.VMEM((GROUP_W // LANES, tm, LANES), F32),
                        pltpu.VMEM((tm, LANES), F32)],
        input_output_aliases=aliases,
        compiler_params=_cparams(1),
        name="post_sample" if per_row else "post",
    )(*args)


def _sort_rows(tm):
    return -(-(TOP_K * tm + N_EXPERTS * (ROW_CHUNK - 1)) // SEL_CHUNK) * SEL_CHUNK


def _moe_sort_kernel(seg_s, goff_s, nch_s, ntot_s, tstart_s, tnch_s,
                     a_ref, idx_ref, h2_ref, segv_ref, lt_ref,
                     xb_hbm, dst_ref, xs_scr, zero_scr, sem, *, tm, n_rows):
    i = pl.program_id(0)
    nt = pl.num_programs(0)
    sel = a_ref[...] > 0.0
    ahead = jnp.dot(lt_ref[...], sel.astype(BF16), preferred_element_type=F32)
    slot1 = jnp.where(sel, segv_ref[0] + ahead + 1.0, 0.0)
    lane = lax.broadcasted_iota(jnp.int32, (tm, LANES), 1).astype(F32)
    idx = idx_ref[...]
    dst = jnp.full((tm, LANES), -1.0, F32)
    for kk in range(TOP_K):
        hit = lane == idx[:, kk:kk + 1]
        dk = jnp.sum(jnp.where(hit, slot1, 0.0), axis=-1, keepdims=True) - 1.0
        dst = jnp.where(lane == float(kk), dk, dst)
    dst_ref[...] = dst
    dst_t = dst.T
    h2 = h2_ref[...]
    for c in range(n_rows // SEL_CHUNK):
        rows = (lax.broadcasted_iota(jnp.int32, (SEL_CHUNK, tm), 0) + c * SEL_CHUNK).astype(F32)
        p = rows == dst_t[0:1, :]
        for kk in range(1, TOP_K):
            p = p | (rows == dst_t[kk:kk + 1, :])
        xs = jnp.dot(jnp.where(p, 1.0, 0.0).astype(BF16), h2, preferred_element_type=F32)
        xs_scr[c * SEL_CHUNK:(c + 1) * SEL_CHUNK, :] = xs.astype(BF16)

    def chunk_copy(src_row, dst_row):
        return pltpu.make_async_copy(
            xs_scr.at[pl.ds(pl.multiple_of(src_row, ROW_CHUNK), ROW_CHUNK)],
            xb_hbm.at[pl.ds(pl.multiple_of(dst_row, ROW_CHUNK), ROW_CHUNK)], sem)

    def per_expert(e, carry):
        so = seg_s[i * N_EXPERTS + e]
        go = goff_s[i * N_EXPERTS + e]

        def per_chunk(c, carry2):
            chunk_copy(so + c * ROW_CHUNK, go + c * ROW_CHUNK).start()
            return carry2

        return lax.fori_loop(0, nch_s[i * N_EXPERTS + e], per_chunk, carry)

    lax.fori_loop(0, N_EXPERTS, per_expert, 0)

    def wait_one(c, carry):
        chunk_copy(0, 0).wait()
        return carry

    lax.fori_loop(0, ntot_s[i], wait_one, 0)

    @pl.when(i == nt - 1)
    def _():
        zero_scr[...] = jnp.zeros(zero_scr.shape, zero_scr.dtype)

        def tail_copy(dst_row):
            return pltpu.make_async_copy(
                zero_scr, xb_hbm.at[pl.ds(pl.multiple_of(dst_row, ROW_CHUNK), ROW_CHUNK)], sem)

        def per_expert_tail(e, carry):
            def per_chunk(c, carry2):
                tail_copy(tstart_s[e] + c * ROW_CHUNK).start()
                return carry2

            lax.fori_loop(0, tnch_s[e], per_chunk, 0)

            def wait_chunk(c, carry2):
                tail_copy(0).wait()
                return carry2

            return lax.fori_loop(0, tnch_s[e], wait_chunk, carry)

        lax.fori_loop(0, N_EXPERTS, per_expert_tail, 0)


def _moe_sort(meta, a_all, idx_all, h2_all, *, tm, cap):
    t_all = a_all.shape[0]
    nt = t_all // tm
    n_rows = _sort_rows(tm)
    lt = jnp.tril(jnp.ones((tm, tm), BF16), -1)
    grid_spec = pltpu.PrefetchScalarGridSpec(
        num_scalar_prefetch=6,
        grid=(nt,),
        in_specs=[pl.BlockSpec((tm, LANES), lambda i, *_: (i, 0)),
                  pl.BlockSpec((tm, LANES), lambda i, *_: (i, 0)),
                  pl.BlockSpec((tm, D_MODEL), lambda i, *_: (i, 0)),
                  pl.BlockSpec((1, 1, LANES), lambda i, *_: (i, 0, 0)),
                  pl.BlockSpec((tm, tm), lambda i, *_: (0, 0))],
        out_specs=[pl.BlockSpec(memory_space=pl.ANY),
                   pl.BlockSpec((tm, LANES), lambda i, *_: (i, 0))],
        scratch_shapes=[pltpu.VMEM((n_rows, D_MODEL), BF16),
                        pltpu.VMEM((ROW_CHUNK, D_MODEL), BF16),
                        pltpu.SemaphoreType.DMA(())],
    )
    return pl.pallas_call(
        functools.partial(_moe_sort_kernel, tm=tm, n_rows=n_rows),
        grid_spec=grid_spec,
        out_shape=[jax.ShapeDtypeStruct((cap, D_MODEL), BF16),
                   jax.ShapeDtypeStruct((t_all, LANES), F32)],
        compiler_params=_cparams(1),
        name="moe_sort",
    )(meta["seg"], meta["goff"], meta["nch"], meta["ntot"], meta["tstart"], meta["tnch"],
      a_all, idx_all, h2_all, meta["segv"], lt)


def _moe_ffn_kernel(be_s, nused_s, x_ref, wgu_ref, bgu_ref, wd_ref, bd_ref, y_ref,
                    wgu_bf, wd_bf):
    b = pl.program_id(0)

    @pl.when(b < nused_s[0])
    def _():
        e = be_s[b]
        e_prev = be_s[jnp.maximum(b - 1, 0)]

        @pl.when((b == 0) | (e != e_prev))
        def _():
            wgu_bf[...] = wgu_ref[0].astype(BF16)
            wd_bf[...] = wd_ref[0].astype(BF16)

        hgu = jnp.dot(x_ref[...], wgu_bf[...], preferred_element_type=F32) + bgu_ref[0]
        d_ff = hgu.shape[1] // 2
        hg = jnp.minimum(hgu[:, :d_ff], SWIGLU_LIMIT)
        hu = jnp.clip(hgu[:, d_ff:], -SWIGLU_LIMIT, SWIGLU_LIMIT)
        act = hg * jax.nn.sigmoid(SWIGLU_ALPHA * hg) * (hu + 1.0)
        y = jnp.dot(act.astype(BF16), wd_bf[...], preferred_element_type=F32) + bd_ref[0]
        y_ref[...] = y.astype(y_ref.dtype)


def _moe_ffn(meta, xb, w_gate_up, b_gate_up, w_down, b_down):
    cap = xb.shape[0]
    nb = cap // FFN_BLOCK
    d_ff2 = w_gate_up.shape[2]

    def blk(b, be, nu):
        return jnp.minimum(b, nu[0] - 1)

    grid_spec = pltpu.PrefetchScalarGridSpec(
        num_scalar_prefetch=2,
        grid=(nb,),
        in_specs=[pl.BlockSpec((FFN_BLOCK, D_MODEL), lambda b, be, nu: (blk(b, be, nu), 0)),
                  pl.BlockSpec((1, D_MODEL, d_ff2), lambda b, be, nu: (be[blk(b, be, nu)], 0, 0)),
                  pl.BlockSpec((1, 1, d_ff2), lambda b, be, nu: (be[blk(b, be, nu)], 0, 0)),
                  pl.BlockSpec((1, d_ff2 // 2, D_MODEL),
                               lambda b, be, nu: (be[blk(b, be, nu)], 0, 0)),
                  pl.BlockSpec((1, 1, D_MODEL), lambda b, be, nu: (be[blk(b, be, nu)], 0, 0))],
        out_specs=pl.BlockSpec((FFN_BLOCK, D_MODEL), lambda b, be, nu: (blk(b, be, nu), 0)),
        scratch_shapes=[pltpu.VMEM((D_MODEL, d_ff2), BF16),
                        pltpu.VMEM((d_ff2 // 2, D_MODEL), BF16)],
    )
    return pl.pallas_call(
        _moe_ffn_kernel,
        grid_spec=grid_spec,
        out_shape=jax.ShapeDtypeStruct((cap, D_MODEL), BF16),
        compiler_params=_cparams(1),
        name="moe_ffn",
    )(meta["block_expert"], meta["n_used"], xb, w_gate_up,
      b_gate_up.reshape(N_EXPERTS, 1, d_ff2), w_down, b_down.reshape(N_EXPERTS, 1, D_MODEL))


def _moe_unsort_kernel(seg_s, goff_s, nch_s, ntot_s,
                       dst_ref, gk_ref, x1_ref, g2p_ref, g2s_ref, yb_hbm,
                       yp_ref, ys_ref, ybuf, sem, *, tm, n_rows, n_prompt_tiles):
    i = pl.program_id(0)

    @pl.when(i == 0)
    def _():
        ybuf[...] = jnp.zeros(ybuf.shape, ybuf.dtype)

    def chunk_copy(src_row, dst_row):
        return pltpu.make_async_copy(
            yb_hbm.at[pl.ds(pl.multiple_of(src_row, ROW_CHUNK), ROW_CHUNK)],
            ybuf.at[pl.ds(pl.multiple_of(dst_row, ROW_CHUNK), ROW_CHUNK)], sem)

    def per_expert(e, carry):
        so = seg_s[i * N_EXPERTS + e]
        go = goff_s[i * N_EXPERTS + e]

        def per_chunk(c, carry2):
            chunk_copy(go + c * ROW_CHUNK, so + c * ROW_CHUNK).start()
            return carry2

        return lax.fori_loop(0, nch_s[i * N_EXPERTS + e], per_chunk, carry)

    lax.fori_loop(0, N_EXPERTS, per_expert, 0)

    def wait_one(c, carry):
        chunk_copy(0, 0).wait()
        return carry

    lax.fori_loop(0, ntot_s[i], wait_one, 0)

    dst = dst_ref[...]
    gk = gk_ref[...]
    acc = jnp.zeros((tm, D_MODEL), F32)
    for c in range(n_rows // SEL_CHUNK):
        cols = (lax.broadcasted_iota(jnp.int32, (tm, SEL_CHUNK), 1) + c * SEL_CHUNK).astype(F32)
        q = jnp.zeros((tm, SEL_CHUNK), F32)
        for kk in range(TOP_K):
            q = q + jnp.where(cols == dst[:, kk:kk + 1], gk[:, kk:kk + 1], 0.0)
        acc = acc + jnp.dot(q.astype(BF16), ybuf[c * SEL_CHUNK:(c + 1) * SEL_CHUNK, :],
                            preferred_element_type=F32)

    @pl.when(i < n_prompt_tiles)
    def _():
        yp_ref[...] = x1_ref[...] + g2p_ref[...] * acc

    @pl.when(i >= n_prompt_tiles)
    def _():
        ys_ref[...] = x1_ref[...] + g2s_ref[...] * acc


def _moe_unsort(meta, dst_all, gk_all, x1_all, gt2_p, gt2_s, yb, *, tm, n_prompt_tiles):
    t_all = dst_all.shape[0]
    nt = t_all // tm
    n_rows = _sort_rows(tm)
    last_p = n_prompt_tiles - 1
    grid_spec = pltpu.PrefetchScalarGridSpec(
        num_scalar_prefetch=4,
        grid=(nt,),
        in_specs=[pl.BlockSpec((tm, LANES), lambda i, *_: (i, 0)),
                  pl.BlockSpec((tm, LANES), lambda i, *_: (i, 0)),
                  pl.BlockSpec((tm, D_MODEL), lambda i, *_: (i, 0)),
                  pl.BlockSpec((1, D_MODEL), lambda i, *_: (0, 0)),
                  pl.BlockSpec((tm, D_MODEL), lambda i, *_: (0, 0)),
                  pl.BlockSpec(memory_space=pl.ANY)],
        out_specs=[pl.BlockSpec((tm, D_MODEL), lambda i, *_: (jnp.minimum(i, last_p), 0)),
                   pl.BlockSpec((tm, D_MODEL), lambda i, *_: (0, 0))],
        scratch_shapes=[pltpu.VMEM((n_rows, D_MODEL), BF16),
                        pltpu.SemaphoreType.DMA(())],
    )
    return pl.pallas_call(
        functools.partial(_moe_unsort_kernel, tm=tm, n_rows=n_rows,
                          n_prompt_tiles=n_prompt_tiles),
        grid_spec=grid_spec,
        out_shape=[jax.ShapeDtypeStruct((n_prompt_tiles * tm, D_MODEL), F32),
                   jax.ShapeDtypeStruct((tm, D_MODEL), F32)],
        compiler_params=_cparams(1),
        name="moe_unsort",
    )(meta["seg"], meta["goff"], meta["nch"], meta["ntot"],
      dst_all, gk_all, x1_all, gt2_p, gt2_s, yb)


def _moe_meta(cnt, tm):
    nt = cnt.shape[0]
    cnt = cnt.astype(jnp.int32)
    cnt_pad = (cnt + ROW_CHUNK - 1) // ROW_CHUNK * ROW_CHUNK
    seg = jnp.cumsum(cnt_pad, axis=1) - cnt_pad
    rows_e = jnp.sum(cnt_pad, axis=0)
    region = (rows_e + FFN_BLOCK - 1) // FFN_BLOCK * FFN_BLOCK
    gstart = jnp.cumsum(region) - region
    goff = gstart[None, :] + jnp.cumsum(cnt_pad, axis=0) - cnt_pad
    nblk_e = region // FFN_BLOCK
    blk_end = jnp.cumsum(nblk_e)
    cap = _moe_cap(nt * tm, tm)
    block_expert = jnp.minimum(
        jnp.searchsorted(blk_end, jnp.arange(cap // FFN_BLOCK, dtype=jnp.int32), side="right"),
        N_EXPERTS - 1).astype(jnp.int32)
    segv = jnp.zeros((nt, 1, LANES), F32).at[:, 0, :N_EXPERTS].set(seg.astype(F32))
    return {
        "seg": seg.reshape(-1), "goff": goff.reshape(-1).astype(jnp.int32),
        "nch": (cnt_pad // ROW_CHUNK).reshape(-1),
        "ntot": jnp.sum(cnt_pad, axis=1) // ROW_CHUNK,
        "tstart": (gstart + rows_e).astype(jnp.int32),
        "tnch": (region - rows_e) // ROW_CHUNK,
        "block_expert": block_expert,
        "n_used": blk_end[-1:].astype(jnp.int32),
        "segv": segv,
    }


def _moe_cap(t_all, tm):
    nt = t_all // tm
    worst = TOP_K * t_all + nt * N_EXPERTS * (ROW_CHUNK - 1) + N_EXPERTS * (FFN_BLOCK - ROW_CHUNK)
    return -(-worst // FFN_BLOCK) * FFN_BLOCK


def _t5_bucket(dist):
    max_exact = NUM_BUCKETS // 2
    d = dist.astype(jnp.int32)
    ratio = (jnp.log(jnp.maximum(d, 1).astype(F32) / max_exact)
             / math.log(MAX_DISTANCE / max_exact))
    large = jnp.minimum(max_exact + (ratio * (NUM_BUCKETS - max_exact)).astype(jnp.int32),
                        NUM_BUCKETS - 1)
    return jnp.where(d < max_exact, d, large)


def _band_bias(tab, dil):
    steps = (jnp.arange(ATT_BLK)[:, None] + ATT_BLK) - jnp.arange(2 * ATT_BLK)[None, :]
    band = (steps >= 0) & (steps <= ATT_BLK)
    b = tab[_t5_bucket(jnp.maximum(steps, 0) * dil)].astype(F32).transpose(2, 0, 1)
    return jnp.where(band[None], b, NEG_INF)


def _step_bias(tab, dil):
    b = tab[_t5_bucket(dil * jnp.arange(ATT_BLK + 1))].astype(F32)
    return jnp.broadcast_to(b[:, :, None], (ATT_BLK + 1, HEADS, HEAD_DIM))


def kernel(x_prompt, x_sample, cache_kv_w128, cache_kv_w512, cache_kv_w2048, state_pool, c_prompt,
           c_sample, w_ada, b_ada, norm_mix_g, norm_ffn_g, w_in, q_norm_g, k_norm_g, rel_bias,
           w_pool_mix, pool_scale, w_up_attn, w_up_pool, w_out, w_router, b_router, w_gate_up,
           b_gate_up, w_down, b_down):
    assert w_ada.shape[0] == 1, "one layer"
    seq = x_prompt.shape[1]
    n_s = x_sample.shape[0]
    assert x_prompt.shape[0] == 1 and x_sample.shape[1] == 1
    assert seq % (DIL_GROUPS[-1][1] * ATT_BLK) == 0 and seq % TM_PROMPT == 0
    assert n_s == TM_SAMPLE
    dils = tuple(d for _, d in DIL_GROUPS)
    caches = (cache_kv_w128, cache_kv_w512, cache_kv_w2048)

    w_in_bf = w_in[0].astype(BF16)
    heads_of = jnp.arange(GROUP_W) // HEAD_DIM
    bdiag = (heads_of[:, None] == heads_of[None, :]).astype(BF16)
    qg = (jnp.tile(q_norm_g[0], HEADS) * SCALE).reshape(1, GROUP_W)
    kg = jnp.tile(k_norm_g[0], HEADS).reshape(1, GROUP_W)
    expand = (jnp.arange(LANES)[:, None] == heads_of[None, :]).astype(BF16)
    wr = jnp.zeros((D_MODEL, LANES), F32).at[:, :N_EXPERTS].set(w_router[0])
    wr_hi = wr.astype(BF16)
    wr_lo = (wr - wr_hi.astype(F32)).astype(BF16)
    br = jnp.full((1, LANES), NEG_INF, F32).at[0, :N_EXPERTS].set(b_router[0])
    wts = (w_pool_mix[0].astype(BF16), pool_scale[0].reshape(1, POOL_W),
           w_up_attn[0].astype(BF16), w_up_pool[0].astype(BF16), w_out[0].astype(BF16), expand,
           norm_ffn_g[0].reshape(1, D_MODEL), wr_hi, wr_lo, br)
    g1 = norm_mix_g[0].reshape(1, D_MODEL)

    n_c = 1 + n_s
    c_all = jnp.zeros((-(-n_c // 8) * 8, D_MODEL), F32).at[0:1].set(c_prompt).at[1:n_c].set(c_sample)
    mod = _ada(c_all, w_ada[0], b_ada[0])
    sh1, sc1, gt1, sh2, sc2, gt2 = jnp.split(mod, N_ADA, axis=-1)

    def prow(m):
        return m[0:1]

    def srows(m):
        return m[1:n_c]

    xp = x_prompt[0]
    q_p, k_p, v_p, u_p, sga_p, sgp_p, st_p = _proj(
        xp, g1, prow(sc1), prow(sh1), w_in_bf, bdiag, qg, kg,
        tm=TM_PROMPT, dils=dils, per_row=False)
    o_p, lse_p = [], []
    for g, (_, d) in enumerate(DIL_GROUPS):
        tab = rel_bias[:, g * HEADS:(g + 1) * HEADS]
        o, lse = _attn_prompt(q_p[g], k_p[g], v_p[g], _band_bias(tab, d), d)
        o_p.append(o)
        lse_p.append(lse)

    xs = x_sample[:, 0]
    ones = (1, 1, 1)
    q_s, _, _, u_s, sga_s, sgp_s, st_s = _proj(
        xs, g1, srows(sc1), srows(sh1), w_in_bf, bdiag, qg, kg,
        tm=TM_SAMPLE, dils=ones, per_row=True)
    to_heads = lambda a: a.astype(F32).reshape(n_s, HEADS, HEAD_DIM)
    q3 = jnp.stack([to_heads(q) for q in q_s])
    kn3 = jnp.stack([to_heads(st[:, :GROUP_W]) for st in st_s])
    vn3 = jnp.stack([to_heads(st[:, GROUP_W:]) for st in st_s])
    bias3 = jnp.stack([_step_bias(rel_bias[:, g * HEADS:(g + 1) * HEADS], d)
                       for g, (_, d) in enumerate(DIL_GROUPS)])
    o3, lse3 = _attn_sample(q3, kn3, vn3, bias3, caches)
    o_s = [o3[g].reshape(n_s, GROUP_W) for g in range(N_GROUPS)]
    lse_s = [jnp.zeros((n_s, LANES), F32).at[:, :HEADS].set(lse3[g][:, :, 0])
             for g in range(N_GROUPS)]
    pooled_s, pool_state_s = _pool_sample(state_pool, u_s)

    nt_p = seq // TM_PROMPT
    t_all = seq + TM_PROMPT
    bufs = _post(xp, o_p, lse_p, (u_p,), sga_p, sgp_p, wts, (prow(gt1), prow(sc2), prow(sh2)),
                 tm=TM_PROMPT, dils=dils, per_row=False, rows_total=t_all, row_block0=0,
                 cnt_tiles=nt_p + 1, cnt_block=None, grid=nt_p, n_valid_steps=None,
                 alias_bufs=None)
    bufs = _post(xs, o_s, lse_s, pooled_s, sga_s, sgp_s, wts,
                 (srows(gt1), srows(sc2), srows(sh2)),
                 tm=TM_SAMPLE, dils=ones, per_row=True, rows_total=t_all,
                 row_block0=seq // TM_SAMPLE, cnt_tiles=nt_p + 1, cnt_block=nt_p,
                 grid=TM_PROMPT // TM_SAMPLE, n_valid_steps=1, alias_bufs=bufs)
    x1_all, h2_all, a_all, idx_all, gk_all, cnt = bufs

    meta = _moe_meta(cnt[:, 0, :N_EXPERTS], TM_PROMPT)
    cap = _moe_cap(t_all, TM_PROMPT)
    xb, dst_all = _moe_sort(meta, a_all, idx_all, h2_all, tm=TM_PROMPT, cap=cap)
    yb = _moe_ffn(meta, xb, w_gate_up[0], b_gate_up[0], w_down[0], b_down[0])
    gt2_s = jnp.zeros((TM_PROMPT, D_MODEL), F32).at[:n_s].set(srows(gt2))
    y_p, y_s = _moe_unsort(meta, dst_all, gk_all, x1_all, prow(gt2), gt2_s, yb,
                           tm=TM_PROMPT, n_prompt_tiles=nt_p)

    def kv_state(st, rows):
        return st.reshape(1, 1, rows, 2, HEADS, HEAD_DIM)

    kv_p = [kv_state(st, st.shape[0]) for st in st_p]
    kv_s = [st.reshape(1, n_s, 1, 2, HEADS, HEAD_DIM) for st in st_s]
    pool_p = u_p[seq - POOL_BUF:].reshape(1, 1, POOL_BUF, POOL_W)
    return (y_p.reshape(1, seq, D_MODEL), y_s[:n_s].reshape(n_s, 1, D_MODEL),
            kv_p[0], kv_p[1], kv_p[2], pool_p, kv_s[0], kv_s[1], kv_s[2], pool_state_s)
```

```python
import functools
import math

import jax
import jax.numpy as jnp
from jax import lax
from jax.experimental import pallas as pl
from jax.experimental.pallas import tpu as pltpu

F32 = jnp.float32
BF16 = jnp.bfloat16

D_MODEL = 1024
HEAD_DIM = 64
HEADS = 8
GROUP_W = HEADS * HEAD_DIM
DIL_GROUPS = ((128, 1), (512, 4), (2048, 16))
N_GROUPS = len(DIL_GROUPS)
QKV_W = N_GROUPS * GROUP_W
ATT_BLK = 128
POOL_WINDOWS = (2, 4, 8, 16)
POOL_W = 512
POOL_GW = 128
POOL_BUF = 15
OFF_K, OFF_V = QKV_W, 2 * QKV_W
OFF_U = 3 * QKV_W
OFF_GA = OFF_U + POOL_W
OFF_GP = OFF_GA + D_MODEL
IN_W = OFF_GP + D_MODEL
NUM_BUCKETS = 32
MAX_DISTANCE = 2048
N_EXPERTS = 32
TOP_K = 4
SWIGLU_LIMIT = 7.0
SWIGLU_ALPHA = 1.702
N_ADA = 6
EPS = 1e-6
NEG_INF = -1e30
PAST_LEN = 8192
SCALE = HEAD_DIM ** -0.5

LANES = 128
ROW_CHUNK = 16
TM_PROMPT = 512
TM_SAMPLE = 128
FFN_BLOCK = 256
SEL_CHUNK = 512
VMEM_LIMIT = 56 * 1024 * 1024


def _cparams(n_axes):
    return pltpu.CompilerParams(dimension_semantics=("arbitrary",) * n_axes,
                                vmem_limit_bytes=VMEM_LIMIT)


def _const_spec(shape):
    nd = len(shape)
    return pl.BlockSpec(shape, lambda *_: (0,) * nd)


def _ada_kernel(c_ref, w_ref, b_ref, o_ref):
    c = c_ref[...]
    s = c * jax.nn.sigmoid(c)
    o_ref[...] = jnp.dot(s.astype(BF16), w_ref[...].astype(BF16),
                         preferred_element_type=F32) + b_ref[...]


def _ada(c_all, w_ada, b_ada):
    rows = c_all.shape[0]
    n = w_ada.shape[1]
    tn = 1536
    return pl.pallas_call(
        _ada_kernel,
        grid=(n // tn,),
        in_specs=[pl.BlockSpec((rows, D_MODEL), lambda j: (0, 0)),
                  pl.BlockSpec((D_MODEL, tn), lambda j: (0, j)),
                  pl.BlockSpec((1, tn), lambda j: (0, j))],
        out_specs=pl.BlockSpec((rows, tn), lambda j: (0, j)),
        out_shape=jax.ShapeDtypeStruct((rows, n), F32),
        compiler_params=_cparams(1),
        name="ada",
    )(c_all, w_ada, b_ada.reshape(1, n))


def _proj_kernel(x_ref, g_ref, sc_ref, sh_ref, w_ref, bd_ref, qg_ref, kg_ref,
                 *refs, tm, dils, st_rows):
    q_refs, k_refs, v_refs = refs[0:3], refs[3:6], refs[6:9]
    u_ref, sga_ref, sgp_ref = refs[9:12]
    st_refs = refs[12:15]
    scr = refs[15]

    x = x_ref[...]
    ms = jnp.mean(x * x, axis=-1, keepdims=True)
    h = x * lax.rsqrt(ms + EPS) * g_ref[...] * (1.0 + sc_ref[...]) + sh_ref[...]
    hb = h.astype(BF16)

    def proj(off, width):
        return jnp.dot(hb, w_ref[:, off:off + width], preferred_element_type=F32)

    def head_norm(z, gain_ref):
        ss = jnp.dot((z * z).astype(BF16), bd_ref[...], preferred_element_type=F32)
        return z * lax.rsqrt(ss * (1.0 / HEAD_DIM) + EPS) * gain_ref[...]

    def put(out_ref, val, d):
        if d == 1:
            out_ref[...] = val.astype(out_ref.dtype)
        else:
            for c in range(GROUP_W // LANES):
                scr[c] = val[:, c * LANES:(c + 1) * LANES]
            for r in range(d):
                for c in range(GROUP_W // LANES):
                    col = r * GROUP_W + c * LANES
                    out_ref[:, col:col + LANES] = (
                        scr[c, pl.ds(r, tm // d, stride=d), :].astype(out_ref.dtype))

    for g, d in enumerate(dils):
        qn = head_norm(proj(g * GROUP_W, GROUP_W), qg_ref)
        put(q_refs[g], qn, d)
        kn = head_norm(proj(OFF_K + g * GROUP_W, GROUP_W), kg_ref)
        put(k_refs[g], kn, d)
        v = proj(OFF_V + g * GROUP_W, GROUP_W)
        put(v_refs[g], v, d)
        rb = st_rows[g]
        st_refs[g][:, 0:GROUP_W] = kn[tm - rb:, :]
        st_refs[g][:, GROUP_W:2 * GROUP_W] = v[tm - rb:, :]

    u_ref[...] = proj(OFF_U, POOL_W)
    sga_ref[...] = jax.nn.sigmoid(proj(OFF_GA, D_MODEL)).astype(BF16)
    sgp_ref[...] = jax.nn.sigmoid(proj(OFF_GP, D_MODEL)).astype(BF16)


def _mod_spec(per_row, tm):
    if per_row:
        return pl.BlockSpec((tm, D_MODEL), lambda i: (i, 0))
    return pl.BlockSpec((1, D_MODEL), lambda i: (0, 0))


def _proj(x, g1, sc1, sh1, w_in_bf, bdiag, qg, kg, *, tm, dils, per_row):
    s = x.shape[0]
    nt = s // tm
    wins = tuple(min(w, s) for w, _ in DIL_GROUPS)
    st_rows = tuple(min(tm, w) for w in wins)

    def res_spec(d):
        return pl.BlockSpec((tm // d, d * GROUP_W), lambda i: (i, 0))

    def st_spec(w, rb):
        first = nt - w // rb
        return pl.BlockSpec((rb, 2 * GROUP_W), lambda i: (jnp.maximum(i - first, 0), 0))

    qkv_shapes = [jax.ShapeDtypeStruct((s // d, d * GROUP_W), BF16) for d in dils]
    out_shape = (qkv_shapes * 3
                 + [jax.ShapeDtypeStruct((s, POOL_W), F32),
                    jax.ShapeDtypeStruct((s, D_MODEL), BF16),
                    jax.ShapeDtypeStruct((s, D_MODEL), BF16)]
                 + [jax.ShapeDtypeStruct((w, 2 * GROUP_W), F32) for w in wins])
    out_specs = ([res_spec(d) for d in dils] * 3
                 + [pl.BlockSpec((tm, POOL_W), lambda i: (i, 0)),
                    pl.BlockSpec((tm, D_MODEL), lambda i: (i, 0)),
                    pl.BlockSpec((tm, D_MODEL), lambda i: (i, 0))]
                 + [st_spec(w, rb) for w, rb in zip(wins, st_rows)])
    in_specs = [pl.BlockSpec((tm, D_MODEL), lambda i: (i, 0)),
                _const_spec((1, D_MODEL)),
                _mod_spec(per_row, tm), _mod_spec(per_row, tm),
                pl.BlockSpec((D_MODEL, IN_W), lambda i: (0, 0), pipeline_mode=pl.Buffered(1)),
                _const_spec((GROUP_W, GROUP_W)),
                _const_spec((1, GROUP_W)), _const_spec((1, GROUP_W))]
    outs = pl.pallas_call(
        functools.partial(_proj_kernel, tm=tm, dils=dils, st_rows=st_rows),
        grid=(nt,),
        in_specs=in_specs,
        out_specs=out_specs,
        out_shape=out_shape,
        scratch_shapes=[pltpu.VMEM((GROUP_W // LANES, tm, LANES), F32)],
        compiler_params=_cparams(1),
        name="proj",
    )(x, g1, sc1, sh1, w_in_bf, bdiag, qg, kg)
    return outs[0:3], outs[3:6], outs[6:9], outs[9], outs[10], outs[11], outs[12:15]


def _attn_kernel(q_ref, kp_ref, kc_ref, vp_ref, vc_ref, r_ref, o_ref, lse_ref, bias_ref):
    i = pl.program_id(1)

    @pl.when((pl.program_id(0) == 0) & (i == 0))
    def _():
        for h in range(HEADS):
            row = jnp.broadcast_to(r_ref[h:h + 1, :], (ATT_BLK, 2 * ATT_BLK))
            bias_ref[h] = pltpu.roll(row, 0, 1, stride=1, stride_axis=0)

    q = q_ref[...]
    k = jnp.concatenate([kp_ref[...], kc_ref[...]], axis=0)
    v = jnp.concatenate([vp_ref[...], vc_ref[...]], axis=0)
    col = lax.broadcasted_iota(jnp.int32, (ATT_BLK, 2 * ATT_BLK), 1)
    no_prev = jnp.where((col < ATT_BLK) & (i == 0), NEG_INF, 0.0)
    outs, lses = [], []
    for h in range(HEADS):
        sl = slice(h * HEAD_DIM, (h + 1) * HEAD_DIM)
        s = lax.dot_general(q[:, sl], k[:, sl], (((1,), (1,)), ((), ())),
                            preferred_element_type=F32)
        s = s + bias_ref[h] + no_prev
        m = jnp.max(s, axis=-1, keepdims=True)
        p = jnp.exp(s - m)
        l = jnp.sum(p, axis=-1, keepdims=True)
        o = jnp.dot(p.astype(BF16), v[:, sl], preferred_element_type=F32)
        outs.append(o / l)
        lses.append(m + jnp.log(l))
    o_ref[...] = jnp.concatenate(outs, axis=-1).astype(o_ref.dtype)
    lse_ref[...] = jnp.concatenate(
        lses + [jnp.zeros((ATT_BLK, LANES - HEADS), F32)], axis=-1)


def _attn_prompt(q, k, v, r_tab, d):
    rows = q.shape[0]
    nblk = rows // ATT_BLK
    cur = pl.BlockSpec((ATT_BLK, GROUP_W), lambda r, i: (i, r))
    prev = pl.BlockSpec((ATT_BLK, GROUP_W), lambda r, i: (jnp.maximum(i - 1, 0), r))
    return pl.pallas_call(
        _attn_kernel,
        grid=(d, nblk),
        in_specs=[cur, prev, cur, prev, cur,
                  pl.BlockSpec((HEADS, 2 * ATT_BLK), lambda r, i: (0, 0))],
        out_specs=[pl.BlockSpec((ATT_BLK, GROUP_W), lambda r, i: (i, r)),
                   pl.BlockSpec((ATT_BLK, LANES), lambda r, i: (i, r))],
        out_shape=[jax.ShapeDtypeStruct((rows, d * GROUP_W), BF16),
                   jax.ShapeDtypeStruct((rows, d * LANES), F32)],
        scratch_shapes=[pltpu.VMEM((HEADS, ATT_BLK, 2 * ATT_BLK), F32)],
        compiler_params=_cparams(2),
        name=f"attn_d{d}",
    )(q, k, k, v, v, r_tab)


N_SLOTS = N_GROUPS * HEADS


def _attn_sample_kernel(x_ref, bself_ref, b0_ref, b1_ref, b2_ref, c0_ref, c1_ref, c2_ref,
                        o_ref, lse_ref):
    xt = x_ref[0].T
    lane = lax.broadcasted_iota(jnp.int32, (1, LANES), 1)
    lane_hd = lax.broadcasted_iota(jnp.int32, (HEAD_DIM, LANES), 1)
    for g, (c_ref, b_ref) in enumerate(((c0_ref, b0_ref), (c1_ref, b1_ref), (c2_ref, b2_ref))):
        o_t = jnp.zeros((HEAD_DIM, LANES), F32)
        lse_row = jnp.zeros((1, LANES), F32)
        for h in range(HEADS):
            j = g * HEADS + h
            qc = xt[0:HEAD_DIM, j:j + 1]
            kc = xt[0:HEAD_DIM, N_SLOTS + j:N_SLOTS + j + 1]
            vc = xt[0:HEAD_DIM, 2 * N_SLOTS + j:2 * N_SLOTS + j + 1]
            s0 = jnp.sum(qc * kc, axis=0, keepdims=True) + bself_ref[j:j + 1, 0:1]
            s = jnp.sum(c_ref[0, 0, 0, h] * qc, axis=0, keepdims=True) + b_ref[h:h + 1, :]
            m = jnp.maximum(jnp.max(s, axis=1, keepdims=True), s0)
            p = jnp.exp(s - m)
            p0 = jnp.exp(s0 - m)
            l = jnp.sum(p, axis=1, keepdims=True) + p0
            oc = (jnp.sum(c_ref[0, 0, 1, h] * p, axis=1, keepdims=True) + p0 * vc) / l
            o_t = jnp.where(lane_hd == h, oc, o_t)
            lse_row = jnp.where(lane == h, m + jnp.log(l), lse_row)
        o_sq = jnp.concatenate([o_t, jnp.zeros((LANES - HEAD_DIM, LANES), F32)], axis=0).T
        o_ref[0, g] = o_sq[0:HEADS, 0:HEAD_DIM]
        lse_ref[0, g] = lse_row


def _attn_sample(x_pack, bself, btabs, caches_t):
    n = x_pack.shape[0]

    def cache_spec(c):
        return pl.BlockSpec((1, 1) + c.shape[2:], lambda t: (0, t, 0, 0, 0, 0))

    return pl.pallas_call(
        _attn_sample_kernel,
        grid=(n,),
        in_specs=[pl.BlockSpec((1, LANES, LANES), lambda t: (t, 0, 0)),
                  _const_spec(bself.shape)]
                 + [_const_spec(b.shape) for b in btabs]
                 + [cache_spec(c) for c in caches_t],
        out_specs=[pl.BlockSpec((1, N_GROUPS, HEADS, HEAD_DIM), lambda t: (t, 0, 0, 0)),
                   pl.BlockSpec((1, N_GROUPS, 1, LANES), lambda t: (t, 0, 0, 0))],
        out_shape=[jax.ShapeDtypeStruct((n, N_GROUPS, HEADS, HEAD_DIM), F32),
                   jax.ShapeDtypeStruct((n, N_GROUPS, 1, LANES), F32)],
        compiler_params=_cparams(1),
        name="attn_sample",
    )(x_pack, bself, *btabs, *caches_t)


def _pool_sample_kernel(st_ref, u_ref, pooled_ref, new_ref):
    u = u_ref[...]
    rows = [st_ref[0, j] for j in range(POOL_BUF)]
    outs = []
    for g, w in enumerate(POOL_WINDOWS):
        sl = slice(g * POOL_GW, (g + 1) * POOL_GW)
        acc = u[:, sl]
        for j in range(POOL_BUF - (w - 1), POOL_BUF):
            acc = acc + rows[j][:, sl]
        outs.append(acc / float(w) - u[:, sl])
    pooled_ref[...] = jnp.concatenate(outs, axis=-1)
    for j in range(POOL_BUF - 1):
        new_ref[0, j] = rows[j + 1]
    new_ref[0, POOL_BUF - 1] = u


def _pool_sample(state, u):
    n = u.shape[0]
    return pl.pallas_call(
        _pool_sample_kernel,
        grid=(1,),
        in_specs=[_const_spec(state.shape), _const_spec(u.shape)],
        out_specs=[_const_spec(u.shape), _const_spec(state.shape)],
        out_shape=[jax.ShapeDtypeStruct((n, POOL_W), F32),
                   jax.ShapeDtypeStruct(state.shape, F32)],
        compiler_params=_cparams(1),
        name="pool_sample",
    )(state, u)


def _post_kernel(*refs, tm, dils, pooled_given, n_valid_steps, aliased):
    it = iter(refs)
    x_ref = next(it)
    o_refs = [next(it) for _ in range(N_GROUPS)]
    lse_refs = [next(it) for _ in range(N_GROUPS)]
    if pooled_given:
        pooled_ref = next(it)
    else:
        u_ref, uh_ref = next(it), next(it)
    sga_ref, sgp_ref = next(it), next(it)
    wpm_ref, psc_ref, wua_ref, wup_ref, wout_ref, exp_ref = (next(it) for _ in range(6))
    gt1_ref, g2_ref, sc2_ref, sh2_ref = (next(it) for _ in range(4))
    wrh_ref, wrl_ref, br_ref = (next(it) for _ in range(3))
    if aliased:
        for _ in range(6):
            next(it)
    x1_ref, h2_ref, a_ref, idx_ref, gk_ref, cnt_ref = (next(it) for _ in range(6))
    ob_scr, ls_scr = next(it), next(it)

    i = pl.program_id(0)

    def compute():
        obs, lss = [], []
        for g, d in enumerate(dils):
            if d == 1:
                obs.append(o_refs[g][...].astype(F32))
                lss.append(lse_refs[g][...])
            else:
                for r in range(d):
                    for c in range(GROUP_W // LANES):
                        col = r * GROUP_W + c * LANES
                        ob_scr[c, pl.ds(r, tm // d, stride=d), :] = (
                            o_refs[g][:, col:col + LANES].astype(F32))
                    ls_scr[pl.ds(r, tm // d, stride=d), :] = (
                        lse_refs[g][:, r * LANES:(r + 1) * LANES])
                obs.append(jnp.concatenate([ob_scr[c] for c in range(GROUP_W // LANES)],
                                           axis=-1))
                lss.append(ls_scr[...])
        mx = jnp.maximum(jnp.maximum(lss[0], lss[1]), lss[2])
        es = [jnp.exp(l - mx) for l in lss]
        den = es[0] + es[1] + es[2]
        attn_o = jnp.zeros((tm, GROUP_W), F32)
        for g in range(N_GROUPS):
            w = es[g] / den
            w_hi = w.astype(BF16)
            w_lo = (w - w_hi.astype(F32)).astype(BF16)
            wexp = (jnp.dot(w_hi, exp_ref[...], preferred_element_type=F32)
                    + jnp.dot(w_lo, exp_ref[...], preferred_element_type=F32))
            attn_o = attn_o + wexp * obs[g]

        if pooled_given:
            pooled = pooled_ref[...]
        else:
            u = u_ref[...]
            halo = jnp.where(i == 0, 0.0, uh_ref[...])
            pos = (lax.broadcasted_iota(jnp.int32, (tm, 1), 0) + i * tm + 1).astype(F32)
            outs = []
            for g, w in enumerate(POOL_WINDOWS):
                sl = slice(g * POOL_GW, (g + 1) * POOL_GW)
                a = jnp.concatenate([halo[:, sl], u[:, sl]], axis=0)
                span = 1
                while span < w:
                    n = a.shape[0] - span
                    a = a[span:, :] + a[:n, :]
                    span *= 2
                off = a.shape[0] - tm
                win_sum = a[off:, :]
                outs.append(win_sum / jnp.minimum(pos, float(w)) - u[:, sl])
            pooled = jnp.concatenate(outs, axis=-1)
        pool_parts = []
        for g in range(len(POOL_WINDOWS)):
            sl = slice(g * POOL_GW, (g + 1) * POOL_GW)
            pool_parts.append(jnp.dot(pooled[:, sl].astype(BF16), wpm_ref[g],
                                      preferred_element_type=F32))
        pool_o = jnp.concatenate(pool_parts, axis=-1) * psc_ref[...]

        up_a = jnp.dot(attn_o.astype(BF16), wua_ref[...], preferred_element_type=F32)
        up_p = jnp.dot(pool_o.astype(BF16), wup_ref[...], preferred_element_type=F32)
        merged = sga_ref[...].astype(F32) * up_a + sgp_ref[...].astype(F32) * up_p
        mo = jnp.dot(merged.astype(BF16), wout_ref[...], preferred_element_type=F32)
        x1 = x_ref[...] + gt1_ref[...] * mo
        x1_ref[...] = x1

        ms = jnp.mean(x1 * x1, axis=-1, keepdims=True)
        h2 = x1 * lax.rsqrt(ms + EPS) * g2_ref[...] * (1.0 + sc2_ref[...]) + sh2_ref[...]
        h2_hi = h2.astype(BF16)
        h2_ref[...] = h2_hi
        h2_lo = (h2 - h2_hi.astype(F32)).astype(BF16)
        logits = (jnp.dot(h2_hi, wrh_ref[...], preferred_element_type=F32)
                  + jnp.dot(h2_lo, wrh_ref[...], preferred_element_type=F32)
                  + jnp.dot(h2_hi, wrl_ref[...], preferred_element_type=F32)
                  + br_ref[...])
        lane = lax.broadcasted_iota(jnp.int32, (tm, LANES), 1).astype(F32)
        work = logits
        vals, ids = [], []
        for _ in range(TOP_K):
            m = jnp.max(work, axis=-1, keepdims=True)
            ik = jnp.min(jnp.where(work == m, lane, float(LANES)), axis=-1, keepdims=True)
            vals.append(m)
            ids.append(ik)
            work = jnp.where(lane == ik, -3e38, work)
        ex = [jnp.exp(v - vals[0]) for v in vals]
        den_k = ex[0] + ex[1] + ex[2] + ex[3]
        a = jnp.zeros((tm, LANES), F32)
        idx = jnp.zeros((tm, LANES), F32)
        gk = jnp.zeros((tm, LANES), F32)
        for kk in range(TOP_K):
            gate = ex[kk] / den_k
            a = a + jnp.where(lane == ids[kk], gate, 0.0)
            idx = jnp.where(lane == float(kk), ids[kk], idx)
            gk = jnp.where(lane == float(kk), gate, gk)
        a_ref[...] = a
        idx_ref[...] = idx
        gk_ref[...] = gk
        cnt = jnp.sum((a > 0.0).astype(F32), axis=0, keepdims=True)
        row = lax.broadcasted_iota(jnp.int32, (8, LANES), 0)
        cnt_ref[0] = jnp.where(row == 0, jnp.broadcast_to(cnt, (8, LANES)), 0.0)

    if n_valid_steps is None:
        compute()
    else:
        pl.when(i < n_valid_steps)(compute)

        @pl.when(i >= n_valid_steps)
        def _():
            x1_ref[...] = jnp.zeros(x1_ref.shape, x1_ref.dtype)
            h2_ref[...] = jnp.zeros(h2_ref.shape, h2_ref.dtype)
            a_ref[...] = jnp.zeros(a_ref.shape, a_ref.dtype)
            idx_ref[...] = jnp.zeros(idx_ref.shape, idx_ref.dtype)
            gk_ref[...] = jnp.zeros(gk_ref.shape, gk_ref.dtype)


def _post(x, o_list, lse_list, pool_in, sga, sgp, wts, mods, *, tm, dils, per_row,
          rows_total, row_block0, cnt_tiles, cnt_block, grid, n_valid_steps, alias_bufs):
    pooled_given = not isinstance(pool_in, tuple)
    nv = grid if n_valid_steps is None else n_valid_steps

    def clamp(i):
        return jnp.minimum(i, nv - 1)

    def tile_spec(width):
        return pl.BlockSpec((tm, width), lambda i: (clamp(i), 0))

    in_specs = [tile_spec(D_MODEL)]
    in_specs += [pl.BlockSpec((tm // d, d * GROUP_W), lambda i: (clamp(i), 0)) for d in dils]
    in_specs += [pl.BlockSpec((tm // d, d * LANES), lambda i: (clamp(i), 0)) for d in dils]
    args = [x, *o_list, *lse_list]
    if pooled_given:
        in_specs.append(tile_spec(POOL_W))
        args.append(pool_in)
    else:
        u = pool_in[0]
        in_specs += [tile_spec(POOL_W),
                     pl.BlockSpec((16, POOL_W),
                                  lambda i: (jnp.maximum(i * (tm // 16) - 1, 0), 0))]
        args += [u, u]
    in_specs += [tile_spec(D_MODEL), tile_spec(D_MODEL)]
    args += [sga, sgp]
    wpm, psc, wua, wup, wout, expand, g2, wrh, wrl, br = wts
    gt1, sc2, sh2 = mods

    def mspec():
        if per_row:
            return pl.BlockSpec((tm, D_MODEL), lambda i: (clamp(i), 0))
        return _const_spec((1, D_MODEL))

    in_specs += [_const_spec(wpm.shape), _const_spec(psc.shape), _const_spec(wua.shape),
                 _const_spec(wup.shape), _const_spec(wout.shape), _const_spec(expand.shape),
                 mspec(), _const_spec(g2.shape), mspec(), mspec(),
                 _const_spec(wrh.shape), _const_spec(wrl.shape), _const_spec(br.shape)]
    args += [wpm, psc, wua, wup, wout, expand, gt1, g2, sc2, sh2, wrh, wrl, br]
    aliases = {}
    if alias_bufs is not None:
        base = len(args)
        in_specs += [pl.BlockSpec(memory_space=pl.ANY)] * 6
        args += list(alias_bufs)
        aliases = {base + j: j for j in range(6)}

    def out_spec(width):
        return pl.BlockSpec((tm, width), lambda i: (row_block0 + i, 0))

    out_specs = [out_spec(D_MODEL), out_spec(D_MODEL), out_spec(LANES), out_spec(LANES),
                 out_spec(LANES),
                 pl.BlockSpec((1, 8, LANES),
                              lambda i: (cnt_block if cnt_block is not None else i, 0, 0))]
    out_shape = [jax.ShapeDtypeStruct((rows_total, D_MODEL), F32),
                 jax.ShapeDtypeStruct((rows_total, D_MODEL), BF16),
                 jax.ShapeDtypeStruct((rows_total, LANES), F32),
                 jax.ShapeDtypeStruct((rows_total, LANES), F32),
                 jax.ShapeDtypeStruct((rows_total, LANES), F32),
                 jax.ShapeDtypeStruct((cnt_tiles, 8, LANES), F32)]
    return pl.pallas_call(
        functools.partial(_post_kernel, tm=tm, dils=dils, pooled_given=pooled_given,
                          n_valid_steps=n_valid_steps, aliased=alias_bufs is not None),
        grid=(grid,),
        in_specs=in_specs,
        out_specs=out_specs,
        out_shape=out_shape,
        scratch_shapes=[pltpu.VMEM((GROUP_W // LANES, tm, LANES), F32),
                        pltpu.VMEM((tm, LANES), F32)],
        input_output_aliases=aliases,
        compiler_params=_cparams(1),
        name="post_sample" if per_row else "post",
    )(*args)


def _sort_rows(tm):
    return -(-(TOP_K * tm + N_EXPERTS * (ROW_CHUNK - 1)) // SEL_CHUNK) * SEL_CHUNK


def _moe_sort_kernel(seg_s, goff_s, nch_s, ntot_s, tstart_s, tnch_s,
                     a_ref, idx_ref, h2_ref, segv_ref, lt_ref,
                     xb_hbm, dst_ref, xs_scr, zero_scr, sem, *, tm, n_rows):
    i = pl.program_id(0)
    nt = pl.num_programs(0)
    sel = a_ref[...] > 0.0
    ahead = jnp.dot(lt_ref[...], sel.astype(BF16), preferred_element_type=F32)
    slot1 = jnp.where(sel, segv_ref[0] + ahead + 1.0, 0.0)
    lane = lax.broadcasted_iota(jnp.int32, (tm, LANES), 1).astype(F32)
    idx = idx_ref[...]
    dst = jnp.full((tm, LANES), -1.0, F32)
    for kk in range(TOP_K):
        hit = lane == idx[:, kk:kk + 1]
        dk = jnp.sum(jnp.where(hit, slot1, 0.0), axis=-1, keepdims=True) - 1.0
        dst = jnp.where(lane == float(kk), dk, dst)
    dst_ref[...] = dst
    dst_t = dst.T
    h2 = h2_ref[...]
    for c in range(n_rows // SEL_CHUNK):
        rows = (lax.broadcasted_iota(jnp.int32, (SEL_CHUNK, tm), 0) + c * SEL_CHUNK).astype(F32)
        p = rows == dst_t[0:1, :]
        for kk in range(1, TOP_K):
            p = p | (rows == dst_t[kk:kk + 1, :])
        xs = jnp.dot(jnp.where(p, 1.0, 0.0).astype(BF16), h2, preferred_element_type=F32)
        xs_scr[c * SEL_CHUNK:(c + 1) * SEL_CHUNK, :] = xs.astype(BF16)

    def chunk_copy(src_row, dst_row):
        return pltpu.make_async_copy(
            xs_scr.at[pl.ds(pl.multiple_of(src_row, ROW_CHUNK), ROW_CHUNK)],
            xb_hbm.at[pl.ds(pl.multiple_of(dst_row, ROW_CHUNK), ROW_CHUNK)], sem)

    def per_expert(e, carry):
        so = seg_s[i * N_EXPERTS + e]
        go = goff_s[i * N_EXPERTS + e]

        def per_chunk(c, carry2):
            chunk_copy(so + c * ROW_CHUNK, go + c * ROW_CHUNK).start()
            return carry2

        return lax.fori_loop(0, nch_s[i * N_EXPERTS + e], per_chunk, carry)

    lax.fori_loop(0, N_EXPERTS, per_expert, 0)

    def wait_one(c, carry):
        chunk_copy(0, 0).wait()
        return carry

    lax.fori_loop(0, ntot_s[i], wait_one, 0)

    @pl.when(i == nt - 1)
    def _():
        zero_scr[...] = jnp.zeros(zero_scr.shape, zero_scr.dtype)

        def tail_copy(dst_row):
            return pltpu.make_async_copy(
                zero_scr, xb_hbm.at[pl.ds(pl.multiple_of(dst_row, ROW_CHUNK), ROW_CHUNK)], sem)

        def per_expert_tail(e, carry):
            def per_chunk(c, carry2):
                tail_copy(tstart_s[e] + c * ROW_CHUNK).start()
                return carry2

            lax.fori_loop(0, tnch_s[e], per_chunk, 0)

            def wait_chunk(c, carry2):
                tail_copy(0).wait()
                return carry2

            return lax.fori_loop(0, tnch_s[e], wait_chunk, carry)

        lax.fori_loop(0, N_EXPERTS, per_expert_tail, 0)


def _moe_sort(meta, a_all, idx_all, h2_all, *, tm, cap):
    t_all = a_all.shape[0]
    nt = t_all // tm
    n_rows = _sort_rows(tm)
    lt = jnp.tril(jnp.ones((tm, tm), BF16), -1)
    grid_spec = pltpu.PrefetchScalarGridSpec(
        num_scalar_prefetch=6,
        grid=(nt,),
        in_specs=[pl.BlockSpec((tm, LANES), lambda i, *_: (i, 0)),
                  pl.BlockSpec((tm, LANES), lambda i, *_: (i, 0)),
                  pl.BlockSpec((tm, D_MODEL), lambda i, *_: (i, 0)),
                  pl.BlockSpec((1, 1, LANES), lambda i, *_: (i, 0, 0)),
                  pl.BlockSpec((tm, tm), lambda i, *_: (0, 0))],
        out_specs=[pl.BlockSpec(memory_space=pl.ANY),
                   pl.BlockSpec((tm, LANES), lambda i, *_: (i, 0))],
        scratch_shapes=[pltpu.VMEM((n_rows, D_MODEL), BF16),
                        pltpu.VMEM((ROW_CHUNK, D_MODEL), BF16),
                        pltpu.SemaphoreType.DMA(())],
    )
    return pl.pallas_call(
        functools.partial(_moe_sort_kernel, tm=tm, n_rows=n_rows),
        grid_spec=grid_spec,
        out_shape=[jax.ShapeDtypeStruct((cap, D_MODEL), BF16),
                   jax.ShapeDtypeStruct((t_all, LANES), F32)],
        compiler_params=_cparams(1),
        name="moe_sort",
    )(meta["seg"], meta["goff"], meta["nch"], meta["ntot"], meta["tstart"], meta["tnch"],
      a_all, idx_all, h2_all, meta["segv"], lt)


def _moe_ffn_kernel(be_s, nused_s, x_ref, wgu_ref, bgu_ref, wd_ref, bd_ref, y_ref,
                    wgu_bf, wd_bf):
    b = pl.program_id(0)

    @pl.when(b < nused_s[0])
    def _():
        e = be_s[b]
        e_prev = be_s[jnp.maximum(b - 1, 0)]

        @pl.when((b == 0) | (e != e_prev))
        def _():
            wgu_bf[...] = wgu_ref[0].astype(BF16)
            wd_bf[...] = wd_ref[0].astype(BF16)

        hgu = jnp.dot(x_ref[...], wgu_bf[...], preferred_element_type=F32) + bgu_ref[0]
        d_ff = hgu.shape[1] // 2
        hg = jnp.minimum(hgu[:, :d_ff], SWIGLU_LIMIT)
        hu = jnp.clip(hgu[:, d_ff:], -SWIGLU_LIMIT, SWIGLU_LIMIT)
        act = hg * jax.nn.sigmoid(SWIGLU_ALPHA * hg) * (hu + 1.0)
        y = jnp.dot(act.astype(BF16), wd_bf[...], preferred_element_type=F32) + bd_ref[0]
        y_ref[...] = y.astype(y_ref.dtype)


def _moe_ffn(meta, xb, w_gate_up, b_gate_up, w_down, b_down):
    cap = xb.shape[0]
    nb = cap // FFN_BLOCK
    d_ff2 = w_gate_up.shape[2]

    def blk(b, be, nu):
        return jnp.minimum(b, nu[0] - 1)

    grid_spec = pltpu.PrefetchScalarGridSpec(
        num_scalar_prefetch=2,
        grid=(nb,),
        in_specs=[pl.BlockSpec((FFN_BLOCK, D_MODEL), lambda b, be, nu: (blk(b, be, nu), 0)),
                  pl.BlockSpec((1, D_MODEL, d_ff2), lambda b, be, nu: (be[blk(b, be, nu)], 0, 0)),
                  pl.BlockSpec((1, 1, d_ff2), lambda b, be, nu: (be[blk(b, be, nu)], 0, 0)),
                  pl.BlockSpec((1, d_ff2 // 2, D_MODEL),
                               lambda b, be, nu: (be[blk(b, be, nu)], 0, 0)),
                  pl.BlockSpec((1, 1, D_MODEL), lambda b, be, nu: (be[blk(b, be, nu)], 0, 0))],
        out_specs=pl.BlockSpec((FFN_BLOCK, D_MODEL), lambda b, be, nu: (blk(b, be, nu), 0)),
        scratch_shapes=[pltpu.VMEM((D_MODEL, d_ff2), BF16),
                        pltpu.VMEM((d_ff2 // 2, D_MODEL), BF16)],
    )
    return pl.pallas_call(
        _moe_ffn_kernel,
        grid_spec=grid_spec,
        out_shape=jax.ShapeDtypeStruct((cap, D_MODEL), BF16),
        compiler_params=_cparams(1),
        name="moe_ffn",
    )(meta["block_expert"], meta["n_used"], xb, w_gate_up,
      b_gate_up.reshape(N_EXPERTS, 1, d_ff2), w_down, b_down.reshape(N_EXPERTS, 1, D_MODEL))


def _moe_unsort_kernel(seg_s, goff_s, nch_s, ntot_s,
                       dst_ref, gk_ref, x1_ref, g2p_ref, g2s_ref, yb_hbm,
                       yp_ref, ys_ref, ybuf, sem, *, tm, n_rows, n_prompt_tiles):
    i = pl.program_id(0)

    @pl.when(i == 0)
    def _():
        ybuf[...] = jnp.zeros(ybuf.shape, ybuf.dtype)

    def chunk_copy(src_row, dst_row):
        return pltpu.make_async_copy(
            yb_hbm.at[pl.ds(pl.multiple_of(src_row, ROW_CHUNK), ROW_CHUNK)],
            ybuf.at[pl.ds(pl.multiple_of(dst_row, ROW_CHUNK), ROW_CHUNK)], sem)

    def per_expert(e, carry):
        so = seg_s[i * N_EXPERTS + e]
        go = goff_s[i * N_EXPERTS + e]

        def per_chunk(c, carry2):
            chunk_copy(go + c * ROW_CHUNK, so + c * ROW_CHUNK).start()
            return carry2

        return lax.fori_loop(0, nch_s[i * N_EXPERTS + e], per_chunk, carry)

    lax.fori_loop(0, N_EXPERTS, per_expert, 0)

    def wait_one(c, carry):
        chunk_copy(0, 0).wait()
        return carry

    lax.fori_loop(0, ntot_s[i], wait_one, 0)

    dst = dst_ref[...]
    gk = gk_ref[...]
    acc = jnp.zeros((tm, D_MODEL), F32)
    for c in range(n_rows // SEL_CHUNK):
        cols = (lax.broadcasted_iota(jnp.int32, (tm, SEL_CHUNK), 1) + c * SEL_CHUNK).astype(F32)
        q = jnp.zeros((tm, SEL_CHUNK), F32)
        for kk in range(TOP_K):
            q = q + jnp.where(cols == dst[:, kk:kk + 1], gk[:, kk:kk + 1], 0.0)
        acc = acc + jnp.dot(q.astype(BF16), ybuf[c * SEL_CHUNK:(c + 1) * SEL_CHUNK, :],
                            preferred_element_type=F32)

    @pl.when(i < n_prompt_tiles)
    def _():
        yp_ref[...] = x1_ref[...] + g2p_ref[...] * acc

    @pl.when(i >= n_prompt_tiles)
    def _():
        ys_ref[...] = x1_ref[...] + g2s_ref[...] * acc


def _moe_unsort(meta, dst_all, gk_all, x1_all, gt2_p, gt2_s, yb, *, tm, n_prompt_tiles):
    t_all = dst_all.shape[0]
    nt = t_all // tm
    n_rows = _sort_rows(tm)
    last_p = n_prompt_tiles - 1
    grid_spec = pltpu.PrefetchScalarGridSpec(
        num_scalar_prefetch=4,
        grid=(nt,),
        in_specs=[pl.BlockSpec((tm, LANES), lambda i, *_: (i, 0)),
                  pl.BlockSpec((tm, LANES), lambda i, *_: (i, 0)),
                  pl.BlockSpec((tm, D_MODEL), lambda i, *_: (i, 0)),
                  pl.BlockSpec((1, D_MODEL), lambda i, *_: (0, 0)),
                  pl.BlockSpec((tm, D_MODEL), lambda i, *_: (0, 0)),
                  pl.BlockSpec(memory_space=pl.ANY)],
        out_specs=[pl.BlockSpec((tm, D_MODEL), lambda i, *_: (jnp.minimum(i, last_p), 0)),
                   pl.BlockSpec((tm, D_MODEL), lambda i, *_: (0, 0))],
        scratch_shapes=[pltpu.VMEM((n_rows, D_MODEL), BF16),
                        pltpu.SemaphoreType.DMA(())],
    )
    return pl.pallas_call(
        functools.partial(_moe_unsort_kernel, tm=tm, n_rows=n_rows,
                          n_prompt_tiles=n_prompt_tiles),
        grid_spec=grid_spec,
        out_shape=[jax.ShapeDtypeStruct((n_prompt_tiles * tm, D_MODEL), F32),
                   jax.ShapeDtypeStruct((tm, D_MODEL), F32)],
        compiler_params=_cparams(1),
        name="moe_unsort",
    )(meta["seg"], meta["goff"], meta["nch"], meta["ntot"],
      dst_all, gk_all, x1_all, gt2_p, gt2_s, yb)


def _moe_meta(cnt, tm):
    nt = cnt.shape[0]
    cnt = cnt.astype(jnp.int32)
    cnt_pad = (cnt + ROW_CHUNK - 1) // ROW_CHUNK * ROW_CHUNK
    seg = jnp.cumsum(cnt_pad, axis=1) - cnt_pad
    rows_e = jnp.sum(cnt_pad, axis=0)
    region = (rows_e + FFN_BLOCK - 1) // FFN_BLOCK * FFN_BLOCK
    gstart = jnp.cumsum(region) - region
    goff = gstart[None, :] + jnp.cumsum(cnt_pad, axis=0) - cnt_pad
    nblk_e = region // FFN_BLOCK
    blk_end = jnp.cumsum(nblk_e)
    cap = _moe_cap(nt * tm, tm)
    blocks = jnp.arange(cap // FFN_BLOCK, dtype=jnp.int32)
    block_expert = jnp.minimum(
        jnp.sum((blk_end[None, :] <= blocks[:, None]).astype(jnp.int32), axis=1), N_EXPERTS - 1)
    segv = jnp.zeros((nt, 1, LANES), F32).at[:, 0, :N_EXPERTS].set(seg.astype(F32))
    return {
        "seg": seg.reshape(-1), "goff": goff.reshape(-1).astype(jnp.int32),
        "nch": (cnt_pad // ROW_CHUNK).reshape(-1),
        "ntot": jnp.sum(cnt_pad, axis=1) // ROW_CHUNK,
        "tstart": (gstart + rows_e).astype(jnp.int32),
        "tnch": (region - rows_e) // ROW_CHUNK,
        "block_expert": block_expert,
        "n_used": blk_end[-1:].astype(jnp.int32),
        "segv": segv,
    }


def _moe_cap(t_all, tm):
    nt = t_all // tm
    worst = TOP_K * t_all + nt * N_EXPERTS * (ROW_CHUNK - 1) + N_EXPERTS * (FFN_BLOCK - ROW_CHUNK)
    return -(-worst // FFN_BLOCK) * FFN_BLOCK


def _t5_bucket(dist):
    max_exact = NUM_BUCKETS // 2
    d = dist.astype(jnp.int32)
    ratio = (jnp.log(jnp.maximum(d, 1).astype(F32) / max_exact)
             / math.log(MAX_DISTANCE / max_exact))
    large = jnp.minimum(max_exact + (ratio * (NUM_BUCKETS - max_exact)).astype(jnp.int32),
                        NUM_BUCKETS - 1)
    return jnp.where(d < max_exact, d, large)


def _step_bias(tab, dil):
    return tab[_t5_bucket(dil * jnp.arange(ATT_BLK + 1))].astype(F32).T


def _band_table(sb):
    return jnp.concatenate([sb[:, ::-1], jnp.full((HEADS, ATT_BLK - 1), NEG_INF, F32)], axis=1)


def _cache_table(sb, dil):
    on_grid = sb[:, :0:-1]
    if dil == 1:
        return on_grid
    off = jnp.full((HEADS, ATT_BLK, dil - 1), NEG_INF, F32)
    return jnp.concatenate([on_grid[:, :, None], off], axis=2).reshape(HEADS, ATT_BLK * dil)


def kernel(x_prompt, x_sample, cache_kv_w128, cache_kv_w512, cache_kv_w2048, state_pool, c_prompt,
           c_sample, w_ada, b_ada, norm_mix_g, norm_ffn_g, w_in, q_norm_g, k_norm_g, rel_bias,
           w_pool_mix, pool_scale, w_up_attn, w_up_pool, w_out, w_router, b_router, w_gate_up,
           b_gate_up, w_down, b_down):
    assert w_ada.shape[0] == 1, "one layer"
    seq = x_prompt.shape[1]
    n_s = x_sample.shape[0]
    assert x_prompt.shape[0] == 1 and x_sample.shape[1] == 1
    assert seq % (DIL_GROUPS[-1][1] * ATT_BLK) == 0 and seq % TM_PROMPT == 0
    assert n_s == TM_SAMPLE
    dils = tuple(d for _, d in DIL_GROUPS)
    caches = (cache_kv_w128, cache_kv_w512, cache_kv_w2048)

    w_in_bf = w_in[0].astype(BF16)
    heads_of = jnp.arange(GROUP_W) // HEAD_DIM
    bdiag = (heads_of[:, None] == heads_of[None, :]).astype(BF16)
    qg = (jnp.tile(q_norm_g[0], HEADS) * SCALE).reshape(1, GROUP_W)
    kg = jnp.tile(k_norm_g[0], HEADS).reshape(1, GROUP_W)
    expand = (jnp.arange(LANES)[:, None] == heads_of[None, :]).astype(BF16)
    wr = jnp.zeros((D_MODEL, LANES), F32).at[:, :N_EXPERTS].set(w_router[0])
    wr_hi = wr.astype(BF16)
    wr_lo = (wr - wr_hi.astype(F32)).astype(BF16)
    br = jnp.full((1, LANES), NEG_INF, F32).at[0, :N_EXPERTS].set(b_router[0])
    wts = (w_pool_mix[0].astype(BF16), pool_scale[0].reshape(1, POOL_W),
           w_up_attn[0].astype(BF16), w_up_pool[0].astype(BF16), w_out[0].astype(BF16), expand,
           norm_ffn_g[0].reshape(1, D_MODEL), wr_hi, wr_lo, br)
    g1 = norm_mix_g[0].reshape(1, D_MODEL)

    n_c = 1 + n_s
    c_all = jnp.zeros((-(-n_c // 8) * 8, D_MODEL), F32).at[0:1].set(c_prompt).at[1:n_c].set(c_sample)
    mod = _ada(c_all, w_ada[0], b_ada[0])
    sh1, sc1, gt1, sh2, sc2, gt2 = jnp.split(mod, N_ADA, axis=-1)

    def prow(m):
        return m[0:1]

    def srows(m):
        return m[1:n_c]

    xp = x_prompt[0]
    q_p, k_p, v_p, u_p, sga_p, sgp_p, st_p = _proj(
        xp, g1, prow(sc1), prow(sh1), w_in_bf, bdiag, qg, kg,
        tm=TM_PROMPT, dils=dils, per_row=False)
    step_bias = [_step_bias(rel_bias[:, g * HEADS:(g + 1) * HEADS], d)
                 for g, (_, d) in enumerate(DIL_GROUPS)]
    o_p, lse_p = [], []
    for g, (_, d) in enumerate(DIL_GROUPS):
        o, lse = _attn_prompt(q_p[g], k_p[g], v_p[g], _band_table(step_bias[g]), d)
        o_p.append(o)
        lse_p.append(lse)

    xs = x_sample[:, 0]
    ones = (1, 1, 1)
    q_s, _, _, u_s, sga_s, sgp_s, st_s = _proj(
        xs, g1, srows(sc1), srows(sh1), w_in_bf, bdiag, qg, kg,
        tm=TM_SAMPLE, dils=ones, per_row=True)
    def slots(parts):
        return jnp.concatenate([p.astype(F32).reshape(n_s, HEADS, HEAD_DIM) for p in parts], axis=1)

    x_rows = jnp.concatenate([slots(q_s), slots([st[:, :GROUP_W] for st in st_s]),
                              slots([st[:, GROUP_W:] for st in st_s])], axis=1)
    x_pack = jnp.zeros((n_s, LANES, LANES), F32).at[:, :3 * N_SLOTS, :HEAD_DIM].set(x_rows)
    bself = jnp.broadcast_to(jnp.concatenate([sb[:, 0] for sb in step_bias])[:, None],
                             (N_SLOTS, LANES))
    btabs = [_cache_table(sb, d) for sb, (_, d) in zip(step_bias, DIL_GROUPS)]
    caches_t = [jnp.transpose(c, (0, 1, 3, 4, 5, 2)) for c in caches]
    o3, lse3 = _attn_sample(x_pack, bself, btabs, caches_t)
    o_s = [o3[:, g].reshape(n_s, GROUP_W) for g in range(N_GROUPS)]
    lse_s = [lse3[:, g, 0, :] for g in range(N_GROUPS)]
    pooled_s, pool_state_t = _pool_sample(jnp.transpose(state_pool, (0, 2, 1, 3)), u_s)
    pool_state_s = jnp.transpose(pool_state_t, (0, 2, 1, 3))

    nt_p = seq // TM_PROMPT
    t_all = seq + TM_PROMPT
    bufs = _post(xp, o_p, lse_p, (u_p,), sga_p, sgp_p, wts, (prow(gt1), prow(sc2), prow(sh2)),
                 tm=TM_PROMPT, dils=dils, per_row=False, rows_total=t_all, row_block0=0,
                 cnt_tiles=nt_p + 1, cnt_block=None, grid=nt_p, n_valid_steps=None,
                 alias_bufs=None)
    bufs = _post(xs, o_s, lse_s, pooled_s, sga_s, sgp_s, wts,
                 (srows(gt1), srows(sc2), srows(sh2)),
                 tm=TM_SAMPLE, dils=ones, per_row=True, rows_total=t_all,
                 row_block0=seq // TM_SAMPLE, cnt_tiles=nt_p + 1, cnt_block=nt_p,
                 grid=TM_PROMPT // TM_SAMPLE, n_valid_steps=1, alias_bufs=bufs)
    x1_all, h2_all, a_all, idx_all, gk_all, cnt = bufs

    meta = _moe_meta(cnt[:, 0, :N_EXPERTS], TM_PROMPT)
    cap = _moe_cap(t_all, TM_PROMPT)
    xb, dst_all = _moe_sort(meta, a_all, idx_all, h2_all, tm=TM_PROMPT, cap=cap)
    yb = _moe_ffn(meta, xb, w_gate_up[0], b_gate_up[0], w_down[0], b_down[0])
    gt2_s = jnp.zeros((TM_PROMPT, D_MODEL), F32).at[:n_s].set(srows(gt2))
    y_p, y_s = _moe_unsort(meta, dst_all, gk_all, x1_all, prow(gt2), gt2_s, yb,
                           tm=TM_PROMPT, n_prompt_tiles=nt_p)

    def kv_state(st, rows):
        return st.reshape(1, 1, rows, 2, HEADS, HEAD_DIM)

    kv_p = [kv_state(st, st.shape[0]) for st in st_p]
    kv_s = [st.reshape(1, n_s, 1, 2, HEADS, HEAD_DIM) for st in st_s]
    pool_p = u_p[seq - POOL_BUF:].reshape(1, 1, POOL_BUF, POOL_W)
    return (y_p.reshape(1, seq, D_MODEL), y_s[:n_s].reshape(n_s, 1, D_MODEL),
            kv_p[0], kv_p[1], kv_p[2], pool_p, kv_s[0], kv_s[1], kv_s[2], pool_state_s)
```

```python
import functools
import math

import jax
import jax.numpy as jnp
from jax import lax
from jax.experimental import pallas as pl
from jax.experimental.pallas import tpu as pltpu

F32 = jnp.float32
BF16 = jnp.bfloat16

D_MODEL = 1024
HEAD_DIM = 64
HEADS = 8
GROUP_W = HEADS * HEAD_DIM
DIL_GROUPS = ((128, 1), (512, 4), (2048, 16))
N_GROUPS = len(DIL_GROUPS)
QKV_W = N_GROUPS * GROUP_W
ATT_BLK = 128
POOL_WINDOWS = (2, 4, 8, 16)
POOL_W = 512
POOL_GW = 128
POOL_BUF = 15
OFF_K, OFF_V = QKV_W, 2 * QKV_W
OFF_U = 3 * QKV_W
OFF_GA = OFF_U + POOL_W
OFF_GP = OFF_GA + D_MODEL
IN_W = OFF_GP + D_MODEL
NUM_BUCKETS = 32
MAX_DISTANCE = 2048
N_EXPERTS = 32
TOP_K = 4
SWIGLU_LIMIT = 7.0
SWIGLU_ALPHA = 1.702
N_ADA = 6
EPS = 1e-6
NEG_INF = -1e30
PAST_LEN = 8192
SCALE = HEAD_DIM ** -0.5

LANES = 128
ROW_CHUNK = 16
TM_PROMPT = 512
TM_SAMPLE = 128
FFN_BLOCK = 512
SEL_CHUNK = 512
VMEM_LIMIT = 56 * 1024 * 1024


def _cparams(n_axes):
    return pltpu.CompilerParams(dimension_semantics=("arbitrary",) * n_axes,
                                vmem_limit_bytes=VMEM_LIMIT)


def _const_spec(shape):
    nd = len(shape)
    return pl.BlockSpec(shape, lambda *_: (0,) * nd)


def _ada_kernel(c_ref, w_ref, b_ref, o_ref):
    c = c_ref[...]
    s = c * jax.nn.sigmoid(c)
    o_ref[...] = jnp.dot(s.astype(BF16), w_ref[...].astype(BF16),
                         preferred_element_type=F32) + b_ref[...]


def _ada(c_all, w_ada, b_ada):
    rows = c_all.shape[0]
    n = w_ada.shape[1]
    tn = 1536
    return pl.pallas_call(
        _ada_kernel,
        grid=(n // tn,),
        in_specs=[pl.BlockSpec((rows, D_MODEL), lambda j: (0, 0)),
                  pl.BlockSpec((D_MODEL, tn), lambda j: (0, j)),
                  pl.BlockSpec((1, tn), lambda j: (0, j))],
        out_specs=pl.BlockSpec((rows, tn), lambda j: (0, j)),
        out_shape=jax.ShapeDtypeStruct((rows, n), F32),
        compiler_params=_cparams(1),
        name="ada",
    )(c_all, w_ada, b_ada.reshape(1, n))


def _proj_kernel(x_ref, g_ref, sc_ref, sh_ref, w_ref, bd_ref, qg_ref, kg_ref,
                 *refs, tm, dils, st_rows):
    q_refs, k_refs, v_refs = refs[0:3], refs[3:6], refs[6:9]
    u_ref, sga_ref, sgp_ref = refs[9:12]
    st_refs = refs[12:15]
    scr = refs[15]

    x = x_ref[...]
    ms = jnp.mean(x * x, axis=-1, keepdims=True)
    h = x * lax.rsqrt(ms + EPS) * g_ref[...] * (1.0 + sc_ref[...]) + sh_ref[...]
    hb = h.astype(BF16)

    def proj(off, width):
        return jnp.dot(hb, w_ref[:, off:off + width], preferred_element_type=F32)

    def head_norm(z, gain_ref):
        ss = jnp.dot((z * z).astype(BF16), bd_ref[...], preferred_element_type=F32)
        return z * lax.rsqrt(ss * (1.0 / HEAD_DIM) + EPS) * gain_ref[...]

    def put(out_ref, val, d):
        if d == 1:
            out_ref[...] = val.astype(out_ref.dtype)
        else:
            for c in range(GROUP_W // LANES):
                scr[c] = val[:, c * LANES:(c + 1) * LANES]
            for r in range(d):
                for c in range(GROUP_W // LANES):
                    col = r * GROUP_W + c * LANES
                    out_ref[:, col:col + LANES] = (
                        scr[c, pl.ds(r, tm // d, stride=d), :].astype(out_ref.dtype))

    for g, d in enumerate(dils):
        qn = head_norm(proj(g * GROUP_W, GROUP_W), qg_ref)
        put(q_refs[g], qn, d)
        kn = head_norm(proj(OFF_K + g * GROUP_W, GROUP_W), kg_ref)
        put(k_refs[g], kn, d)
        v = proj(OFF_V + g * GROUP_W, GROUP_W)
        put(v_refs[g], v, d)
        rb = st_rows[g]
        st_refs[g][:, 0:GROUP_W] = kn[tm - rb:, :]
        st_refs[g][:, GROUP_W:2 * GROUP_W] = v[tm - rb:, :]

    u_ref[...] = proj(OFF_U, POOL_W)
    sga_ref[...] = jax.nn.sigmoid(proj(OFF_GA, D_MODEL)).astype(BF16)
    sgp_ref[...] = jax.nn.sigmoid(proj(OFF_GP, D_MODEL)).astype(BF16)


def _mod_spec(per_row, tm):
    if per_row:
        return pl.BlockSpec((tm, D_MODEL), lambda i: (i, 0))
    return pl.BlockSpec((1, D_MODEL), lambda i: (0, 0))


def _proj(x, g1, sc1, sh1, w_in_bf, bdiag, qg, kg, *, tm, dils, per_row):
    s = x.shape[0]
    nt = s // tm
    wins = tuple(min(w, s) for w, _ in DIL_GROUPS)
    st_rows = tuple(min(tm, w) for w in wins)

    def res_spec(d):
        return pl.BlockSpec((tm // d, d * GROUP_W), lambda i: (i, 0))

    def st_spec(w, rb):
        first = nt - w // rb
        return pl.BlockSpec((rb, 2 * GROUP_W), lambda i: (jnp.maximum(i - first, 0), 0))

    qkv_shapes = [jax.ShapeDtypeStruct((s // d, d * GROUP_W), BF16) for d in dils]
    out_shape = (qkv_shapes * 3
                 + [jax.ShapeDtypeStruct((s, POOL_W), F32),
                    jax.ShapeDtypeStruct((s, D_MODEL), BF16),
                    jax.ShapeDtypeStruct((s, D_MODEL), BF16)]
                 + [jax.ShapeDtypeStruct((w, 2 * GROUP_W), F32) for w in wins])
    out_specs = ([res_spec(d) for d in dils] * 3
                 + [pl.BlockSpec((tm, POOL_W), lambda i: (i, 0)),
                    pl.BlockSpec((tm, D_MODEL), lambda i: (i, 0)),
                    pl.BlockSpec((tm, D_MODEL), lambda i: (i, 0))]
                 + [st_spec(w, rb) for w, rb in zip(wins, st_rows)])
    in_specs = [pl.BlockSpec((tm, D_MODEL), lambda i: (i, 0)),
                _const_spec((1, D_MODEL)),
                _mod_spec(per_row, tm), _mod_spec(per_row, tm),
                pl.BlockSpec((D_MODEL, IN_W), lambda i: (0, 0), pipeline_mode=pl.Buffered(1)),
                _const_spec((GROUP_W, GROUP_W)),
                _const_spec((1, GROUP_W)), _const_spec((1, GROUP_W))]
    outs = pl.pallas_call(
        functools.partial(_proj_kernel, tm=tm, dils=dils, st_rows=st_rows),
        grid=(nt,),
        in_specs=in_specs,
        out_specs=out_specs,
        out_shape=out_shape,
        scratch_shapes=[pltpu.VMEM((GROUP_W // LANES, tm, LANES), F32)],
        compiler_params=_cparams(1),
        name="proj",
    )(x, g1, sc1, sh1, w_in_bf, bdiag, qg, kg)
    return outs[0:3], outs[3:6], outs[6:9], outs[9], outs[10], outs[11], outs[12:15]


def _attn_kernel(q_ref, kp_ref, kc_ref, vp_ref, vc_ref, r_ref, o_ref, lse_ref, bias_ref):
    i = pl.program_id(1)

    @pl.when((pl.program_id(0) == 0) & (i == 0))
    def _():
        for h in range(HEADS):
            row = jnp.broadcast_to(r_ref[h:h + 1, :], (ATT_BLK, 2 * ATT_BLK))
            bias_ref[h] = pltpu.roll(row, 0, 1, stride=1, stride_axis=0)

    q = q_ref[...]
    k = jnp.concatenate([kp_ref[...], kc_ref[...]], axis=0)
    v = jnp.concatenate([vp_ref[...], vc_ref[...]], axis=0)
    col = lax.broadcasted_iota(jnp.int32, (ATT_BLK, 2 * ATT_BLK), 1)
    no_prev = jnp.where((col < ATT_BLK) & (i == 0), NEG_INF, 0.0)
    outs, lses = [], []
    for h in range(HEADS):
        sl = slice(h * HEAD_DIM, (h + 1) * HEAD_DIM)
        s = lax.dot_general(q[:, sl], k[:, sl], (((1,), (1,)), ((), ())),
                            preferred_element_type=F32)
        s = s + bias_ref[h] + no_prev
        m = jnp.max(s, axis=-1, keepdims=True)
        p = jnp.exp(s - m)
        l = jnp.sum(p, axis=-1, keepdims=True)
        o = jnp.dot(p.astype(BF16), v[:, sl], preferred_element_type=F32)
        outs.append(o / l)
        lses.append(m + jnp.log(l))
    o_ref[...] = jnp.concatenate(outs, axis=-1).astype(o_ref.dtype)
    lse_ref[...] = jnp.concatenate(
        lses + [jnp.zeros((ATT_BLK, LANES - HEADS), F32)], axis=-1)


def _attn_prompt(q, k, v, r_tab, d):
    rows = q.shape[0]
    nblk = rows // ATT_BLK
    cur = pl.BlockSpec((ATT_BLK, GROUP_W), lambda r, i: (i, r))
    prev = pl.BlockSpec((ATT_BLK, GROUP_W), lambda r, i: (jnp.maximum(i - 1, 0), r))
    return pl.pallas_call(
        _attn_kernel,
        grid=(d, nblk),
        in_specs=[cur, prev, cur, prev, cur,
                  pl.BlockSpec((HEADS, 2 * ATT_BLK), lambda r, i: (0, 0))],
        out_specs=[pl.BlockSpec((ATT_BLK, GROUP_W), lambda r, i: (i, r)),
                   pl.BlockSpec((ATT_BLK, LANES), lambda r, i: (i, r))],
        out_shape=[jax.ShapeDtypeStruct((rows, d * GROUP_W), BF16),
                   jax.ShapeDtypeStruct((rows, d * LANES), F32)],
        scratch_shapes=[pltpu.VMEM((HEADS, ATT_BLK, 2 * ATT_BLK), F32)],
        compiler_params=_cparams(2),
        name=f"attn_d{d}",
    )(q, k, k, v, v, r_tab)


def _attn_sample_kernel(q_ref, kn_ref, vn_ref, bself_ref, b0_ref, b1_ref, b2_ref,
                        c0_ref, c1_ref, c2_ref, o_ref, lse_ref):
    row_e = lax.broadcasted_iota(jnp.int32, (HEADS, HEAD_DIM), 0)
    for g, (c_ref, b_ref) in enumerate(((c0_ref, b0_ref), (c1_ref, b1_ref), (c2_ref, b2_ref))):
        win = b_ref.shape[1]
        row_w = lax.broadcasted_iota(jnp.int32, (HEADS, win), 0)
        q = q_ref[0, g]
        qb = q.astype(BF16)
        s = jnp.zeros((HEADS, win), F32)
        for h in range(HEADS):
            sh = jnp.dot(qb, c_ref[0, 0, 0, h].astype(BF16), preferred_element_type=F32)
            s = jnp.where(row_w == h, sh, s)
        s = s + b_ref[...]
        s0 = jnp.sum(q * kn_ref[0, g], axis=-1, keepdims=True) + bself_ref[g]
        m = jnp.maximum(jnp.max(s, axis=-1, keepdims=True), s0)
        p = jnp.exp(s - m)
        p0 = jnp.exp(s0 - m)
        l = jnp.sum(p, axis=-1, keepdims=True) + p0
        pb = p.astype(BF16)
        o = jnp.zeros((HEADS, HEAD_DIM), F32)
        for h in range(HEADS):
            oh = lax.dot_general(pb, c_ref[0, 0, 1, h].astype(BF16), (((1,), (1,)), ((), ())),
                                 preferred_element_type=F32)
            o = jnp.where(row_e == h, oh, o)
        o_ref[0, g] = (o + p0 * vn_ref[0, g]) / l
        lse_ref[0, g] = m + jnp.log(l)


def _attn_sample(q3, kn3, vn3, bself, btabs, caches_t):
    n = q3.shape[0]
    tok = pl.BlockSpec((1, N_GROUPS, HEADS, HEAD_DIM), lambda t: (t, 0, 0, 0))

    def cache_spec(c):
        return pl.BlockSpec((1, 1) + c.shape[2:], lambda t: (0, t, 0, 0, 0, 0))

    return pl.pallas_call(
        _attn_sample_kernel,
        grid=(n,),
        in_specs=[tok, tok, tok, _const_spec(bself.shape)]
                 + [_const_spec(b.shape) for b in btabs]
                 + [cache_spec(c) for c in caches_t],
        out_specs=[tok, pl.BlockSpec((1, N_GROUPS, HEADS, 1), lambda t: (t, 0, 0, 0))],
        out_shape=[jax.ShapeDtypeStruct((n, N_GROUPS, HEADS, HEAD_DIM), F32),
                   jax.ShapeDtypeStruct((n, N_GROUPS, HEADS, 1), F32)],
        compiler_params=_cparams(1),
        name="attn_sample",
    )(q3, kn3, vn3, bself, *btabs, *caches_t)


def _pool_sample_kernel(st_ref, u_ref, pooled_ref, new_ref):
    u = u_ref[...]
    rows = [st_ref[0, j] for j in range(POOL_BUF)]
    outs = []
    for g, w in enumerate(POOL_WINDOWS):
        sl = slice(g * POOL_GW, (g + 1) * POOL_GW)
        acc = u[:, sl]
        for j in range(POOL_BUF - (w - 1), POOL_BUF):
            acc = acc + rows[j][:, sl]
        outs.append(acc / float(w) - u[:, sl])
    pooled_ref[...] = jnp.concatenate(outs, axis=-1)
    for j in range(POOL_BUF - 1):
        new_ref[0, j] = rows[j + 1]
    new_ref[0, POOL_BUF - 1] = u


def _pool_sample(state, u):
    n = u.shape[0]
    return pl.pallas_call(
        _pool_sample_kernel,
        grid=(1,),
        in_specs=[_const_spec(state.shape), _const_spec(u.shape)],
        out_specs=[_const_spec(u.shape), _const_spec(state.shape)],
        out_shape=[jax.ShapeDtypeStruct((n, POOL_W), F32),
                   jax.ShapeDtypeStruct(state.shape, F32)],
        compiler_params=_cparams(1),
        name="pool_sample",
    )(state, u)


def _post_kernel(*refs, tm, dils, pooled_given, n_valid_steps, aliased):
    it = iter(refs)
    x_ref = next(it)
    o_refs = [next(it) for _ in range(N_GROUPS)]
    lse_refs = [next(it) for _ in range(N_GROUPS)]
    if pooled_given:
        pooled_ref = next(it)
    else:
        u_ref, uh_ref = next(it), next(it)
    sga_ref, sgp_ref = next(it), next(it)
    wpm_ref, psc_ref, wua_ref, wup_ref, wout_ref, exp_ref = (next(it) for _ in range(6))
    gt1_ref, g2_ref, sc2_ref, sh2_ref = (next(it) for _ in range(4))
    wrh_ref, wrl_ref, br_ref = (next(it) for _ in range(3))
    if aliased:
        for _ in range(6):
            next(it)
    x1_ref, h2_ref, a_ref, idx_ref, gk_ref, cnt_ref = (next(it) for _ in range(6))
    ob_scr, ls_scr = next(it), next(it)

    i = pl.program_id(0)

    def compute():
        obs, lss = [], []
        for g, d in enumerate(dils):
            if d == 1:
                obs.append(o_refs[g][...].astype(F32))
                lss.append(lse_refs[g][...])
            else:
                for r in range(d):
                    for c in range(GROUP_W // LANES):
                        col = r * GROUP_W + c * LANES
                        ob_scr[c, pl.ds(r, tm // d, stride=d), :] = (
                            o_refs[g][:, col:col + LANES].astype(F32))
                    ls_scr[pl.ds(r, tm // d, stride=d), :] = (
                        lse_refs[g][:, r * LANES:(r + 1) * LANES])
                obs.append(jnp.concatenate([ob_scr[c] for c in range(GROUP_W // LANES)],
                                           axis=-1))
                lss.append(ls_scr[...])
        mx = jnp.maximum(jnp.maximum(lss[0], lss[1]), lss[2])
        es = [jnp.exp(l - mx) for l in lss]
        den = es[0] + es[1] + es[2]
        attn_o = jnp.zeros((tm, GROUP_W), F32)
        for g in range(N_GROUPS):
            w = es[g] / den
            w_hi = w.astype(BF16)
            w_lo = (w - w_hi.astype(F32)).astype(BF16)
            wexp = (jnp.dot(w_hi, exp_ref[...], preferred_element_type=F32)
                    + jnp.dot(w_lo, exp_ref[...], preferred_element_type=F32))
            attn_o = attn_o + wexp * obs[g]

        if pooled_given:
            pooled = pooled_ref[...]
        else:
            u = u_ref[...]
            halo = jnp.where(i == 0, 0.0, uh_ref[...])
            pos = (lax.broadcasted_iota(jnp.int32, (tm, 1), 0) + i * tm + 1).astype(F32)
            outs = []
            for g, w in enumerate(POOL_WINDOWS):
                sl = slice(g * POOL_GW, (g + 1) * POOL_GW)
                a = jnp.concatenate([halo[:, sl], u[:, sl]], axis=0)
                span = 1
                while span < w:
                    n = a.shape[0] - span
                    a = a[span:, :] + a[:n, :]
                    span *= 2
                off = a.shape[0] - tm
                win_sum = a[off:, :]
                outs.append(win_sum / jnp.minimum(pos, float(w)) - u[:, sl])
            pooled = jnp.concatenate(outs, axis=-1)
        pool_parts = []
        for g in range(len(POOL_WINDOWS)):
            sl = slice(g * POOL_GW, (g + 1) * POOL_GW)
            pool_parts.append(jnp.dot(pooled[:, sl].astype(BF16), wpm_ref[g],
                                      preferred_element_type=F32))
        pool_o = jnp.concatenate(pool_parts, axis=-1) * psc_ref[...]

        up_a = jnp.dot(attn_o.astype(BF16), wua_ref[...], preferred_element_type=F32)
        up_p = jnp.dot(pool_o.astype(BF16), wup_ref[...], preferred_element_type=F32)
        merged = sga_ref[...].astype(F32) * up_a + sgp_ref[...].astype(F32) * up_p
        mo = jnp.dot(merged.astype(BF16), wout_ref[...], preferred_element_type=F32)
        x1 = x_ref[...] + gt1_ref[...] * mo
        x1_ref[...] = x1

        ms = jnp.mean(x1 * x1, axis=-1, keepdims=True)
        h2 = x1 * lax.rsqrt(ms + EPS) * g2_ref[...] * (1.0 + sc2_ref[...]) + sh2_ref[...]
        h2_hi = h2.astype(BF16)
        h2_ref[...] = h2_hi
        h2_lo = (h2 - h2_hi.astype(F32)).astype(BF16)
        logits = (jnp.dot(h2_hi, wrh_ref[...], preferred_element_type=F32)
                  + jnp.dot(h2_lo, wrh_ref[...], preferred_element_type=F32)
                  + jnp.dot(h2_hi, wrl_ref[...], preferred_element_type=F32)
                  + br_ref[...])
        lane = lax.broadcasted_iota(jnp.int32, (tm, LANES), 1).astype(F32)
        work = logits
        vals, ids = [], []
        for _ in range(TOP_K):
            m = jnp.max(work, axis=-1, keepdims=True)
            ik = jnp.min(jnp.where(work == m, lane, float(LANES)), axis=-1, keepdims=True)
            vals.append(m)
            ids.append(ik)
            work = jnp.where(lane == ik, -3e38, work)
        ex = [jnp.exp(v - vals[0]) for v in vals]
        den_k = ex[0] + ex[1] + ex[2] + ex[3]
        a = jnp.zeros((tm, LANES), F32)
        idx = jnp.zeros((tm, LANES), F32)
        gk = jnp.zeros((tm, LANES), F32)
        for kk in range(TOP_K):
            gate = ex[kk] / den_k
            a = a + jnp.where(lane == ids[kk], gate, 0.0)
            idx = jnp.where(lane == float(kk), ids[kk], idx)
            gk = jnp.where(lane == float(kk), gate, gk)
        a_ref[...] = a
        idx_ref[...] = idx
        gk_ref[...] = gk
        cnt = jnp.sum((a > 0.0).astype(F32), axis=0, keepdims=True)
        row = lax.broadcasted_iota(jnp.int32, (8, LANES), 0)
        cnt_ref[0] = jnp.where(row == 0, jnp.broadcast_to(cnt, (8, LANES)), 0.0)

    if n_valid_steps is None:
        compute()
    else:
        pl.when(i < n_valid_steps)(compute)

        @pl.when(i >= n_valid_steps)
        def _():
            x1_ref[...] = jnp.zeros(x1_ref.shape, x1_ref.dtype)
            h2_ref[...] = jnp.zeros(h2_ref.shape, h2_ref.dtype)
            a_ref[...] = jnp.zeros(a_ref.shape, a_ref.dtype)
            idx_ref[...] = jnp.zeros(idx_ref.shape, idx_ref.dtype)
            gk_ref[...] = jnp.zeros(gk_ref.shape, gk_ref.dtype)


def _post(x, o_list, lse_list, pool_in, sga, sgp, wts, mods, *, tm, dils, per_row,
          rows_total, row_block0, cnt_tiles, cnt_block, grid, n_valid_steps, alias_bufs):
    pooled_given = not isinstance(pool_in, tuple)
    nv = grid if n_valid_steps is None else n_valid_steps

    def clamp(i):
        return jnp.minimum(i, nv - 1)

    def tile_spec(width):
        return pl.BlockSpec((tm, width), lambda i: (clamp(i), 0))

    in_specs = [tile_spec(D_MODEL)]
    in_specs += [pl.BlockSpec((tm // d, d * GROUP_W), lambda i: (clamp(i), 0)) for d in dils]
    in_specs += [pl.BlockSpec((tm // d, d * LANES), lambda i: (clamp(i), 0)) for d in dils]
    args = [x, *o_list, *lse_list]
    if pooled_given:
        in_specs.append(tile_spec(POOL_W))
        args.append(pool_in)
    else:
        u = pool_in[0]
        in_specs += [tile_spec(POOL_W),
                     pl.BlockSpec((16, POOL_W),
                                  lambda i: (jnp.maximum(i * (tm // 16) - 1, 0), 0))]
        args += [u, u]
    in_specs += [tile_spec(D_MODEL), tile_spec(D_MODEL)]
    args += [sga, sgp]
    wpm, psc, wua, wup, wout, expand, g2, wrh, wrl, br = wts
    gt1, sc2, sh2 = mods

    def mspec():
        if per_row:
            return pl.BlockSpec((tm, D_MODEL), lambda i: (clamp(i), 0))
        return _const_spec((1, D_MODEL))

    in_specs += [_const_spec(wpm.shape), _const_spec(psc.shape), _const_spec(wua.shape),
                 _const_spec(wup.shape), _const_spec(wout.shape), _const_spec(expand.shape),
                 mspec(), _const_spec(g2.shape), mspec(), mspec(),
                 _const_spec(wrh.shape), _const_spec(wrl.shape), _const_spec(br.shape)]
    args += [wpm, psc, wua, wup, wout, expand, gt1, g2, sc2, sh2, wrh, wrl, br]
    aliases = {}
    if alias_bufs is not None:
        base = len(args)
        in_specs += [pl.BlockSpec(memory_space=pl.ANY)] * 6
        args += list(alias_bufs)
        aliases = {base + j: j for j in range(6)}

    def out_spec(width):
        return pl.BlockSpec((tm, width), lambda i: (row_block0 + i, 0))

    out_specs = [out_spec(D_MODEL), out_spec(D_MODEL), out_spec(LANES), out_spec(LANES),
                 out_spec(LANES),
                 pl.BlockSpec((1, 8, LANES),
                              lambda i: (cnt_block if cnt_block is not None else i, 0, 0))]
    out_shape = [jax.ShapeDtypeStruct((rows_total, D_MODEL), F32),
                 jax.ShapeDtypeStruct((rows_total, D_MODEL), BF16),
                 jax.ShapeDtypeStruct((rows_total, LANES), F32),
                 jax.ShapeDtypeStruct((rows_total, LANES), F32),
                 jax.ShapeDtypeStruct((rows_total, LANES), F32),
                 jax.ShapeDtypeStruct((cnt_tiles, 8, LANES), F32)]
    return pl.pallas_call(
        functools.partial(_post_kernel, tm=tm, dils=dils, pooled_given=pooled_given,
                          n_valid_steps=n_valid_steps, aliased=alias_bufs is not None),
        grid=(grid,),
        in_specs=in_specs,
        out_specs=out_specs,
        out_shape=out_shape,
        scratch_shapes=[pltpu.VMEM((GROUP_W // LANES, tm, LANES), F32),
                        pltpu.VMEM((tm, LANES), F32)],
        input_output_aliases=aliases,
        compiler_params=_cparams(1),
        name="post_sample" if per_row else "post",
    )(*args)


def _sort_rows(tm):
    return -(-(TOP_K * tm + N_EXPERTS * (ROW_CHUNK - 1)) // SEL_CHUNK) * SEL_CHUNK


def _moe_sort_kernel(seg_s, goff_s, nch_s, ntot_s, tstart_s, tnch_s,
                     a_ref, idx_ref, h2_ref, segv_ref, lt_ref,
                     xb_hbm, dst_ref, xs_scr, zero_scr, sem, *, tm, n_rows):
    i = pl.program_id(0)
    nt = pl.num_programs(0)
    sel = a_ref[...] > 0.0
    ahead = jnp.dot(lt_ref[...], sel.astype(BF16), preferred_element_type=F32)
    slot1 = jnp.where(sel, segv_ref[0] + ahead + 1.0, 0.0)
    lane = lax.broadcasted_iota(jnp.int32, (tm, LANES), 1).astype(F32)
    idx = idx_ref[...]
    dst = jnp.full((tm, LANES), -1.0, F32)
    for kk in range(TOP_K):
        hit = lane == idx[:, kk:kk + 1]
        dk = jnp.sum(jnp.where(hit, slot1, 0.0), axis=-1, keepdims=True) - 1.0
        dst = jnp.where(lane == float(kk), dk, dst)
    dst_ref[...] = dst
    dst_t = dst.T
    h2 = h2_ref[...]
    for c in range(n_rows // SEL_CHUNK):
        rows = (lax.broadcasted_iota(jnp.int32, (SEL_CHUNK, tm), 0) + c * SEL_CHUNK).astype(F32)
        p = rows == dst_t[0:1, :]
        for kk in range(1, TOP_K):
            p = p | (rows == dst_t[kk:kk + 1, :])
        xs = jnp.dot(jnp.where(p, 1.0, 0.0).astype(BF16), h2, preferred_element_type=F32)
        xs_scr[c * SEL_CHUNK:(c + 1) * SEL_CHUNK, :] = xs.astype(BF16)

    def chunk_copy(src_row, dst_row):
        return pltpu.make_async_copy(
            xs_scr.at[pl.ds(pl.multiple_of(src_row, ROW_CHUNK), ROW_CHUNK)],
            xb_hbm.at[pl.ds(pl.multiple_of(dst_row, ROW_CHUNK), ROW_CHUNK)], sem)

    def per_expert(e, carry):
        so = seg_s[i * N_EXPERTS + e]
        go = goff_s[i * N_EXPERTS + e]

        def per_chunk(c, carry2):
            chunk_copy(so + c * ROW_CHUNK, go + c * ROW_CHUNK).start()
            return carry2

        return lax.fori_loop(0, nch_s[i * N_EXPERTS + e], per_chunk, carry)

    lax.fori_loop(0, N_EXPERTS, per_expert, 0)

    def wait_one(c, carry):
        chunk_copy(0, 0).wait()
        return carry

    lax.fori_loop(0, ntot_s[i], wait_one, 0)

    @pl.when(i == nt - 1)
    def _():
        zero_scr[...] = jnp.zeros(zero_scr.shape, zero_scr.dtype)

        def tail_copy(dst_row):
            return pltpu.make_async_copy(
                zero_scr, xb_hbm.at[pl.ds(pl.multiple_of(dst_row, ROW_CHUNK), ROW_CHUNK)], sem)

        def per_expert_tail(e, carry):
            def per_chunk(c, carry2):
                tail_copy(tstart_s[e] + c * ROW_CHUNK).start()
                return carry2

            lax.fori_loop(0, tnch_s[e], per_chunk, 0)

            def wait_chunk(c, carry2):
                tail_copy(0).wait()
                return carry2

            return lax.fori_loop(0, tnch_s[e], wait_chunk, carry)

        lax.fori_loop(0, N_EXPERTS, per_expert_tail, 0)


def _moe_sort(meta, a_all, idx_all, h2_all, *, tm, cap):
    t_all = a_all.shape[0]
    nt = t_all // tm
    n_rows = _sort_rows(tm)
    lt = jnp.tril(jnp.ones((tm, tm), BF16), -1)
    grid_spec = pltpu.PrefetchScalarGridSpec(
        num_scalar_prefetch=6,
        grid=(nt,),
        in_specs=[pl.BlockSpec((tm, LANES), lambda i, *_: (i, 0)),
                  pl.BlockSpec((tm, LANES), lambda i, *_: (i, 0)),
                  pl.BlockSpec((tm, D_MODEL), lambda i, *_: (i, 0)),
                  pl.BlockSpec((1, 1, LANES), lambda i, *_: (i, 0, 0)),
                  pl.BlockSpec((tm, tm), lambda i, *_: (0, 0))],
        out_specs=[pl.BlockSpec(memory_space=pl.ANY),
                   pl.BlockSpec((tm, LANES), lambda i, *_: (i, 0))],
        scratch_shapes=[pltpu.VMEM((n_rows, D_MODEL), BF16),
                        pltpu.VMEM((ROW_CHUNK, D_MODEL), BF16),
                        pltpu.SemaphoreType.DMA(())],
    )
    return pl.pallas_call(
        functools.partial(_moe_sort_kernel, tm=tm, n_rows=n_rows),
        grid_spec=grid_spec,
        out_shape=[jax.ShapeDtypeStruct((cap, D_MODEL), BF16),
                   jax.ShapeDtypeStruct((t_all, LANES), F32)],
        compiler_params=_cparams(1),
        name="moe_sort",
    )(meta["seg"], meta["goff"], meta["nch"], meta["ntot"], meta["tstart"], meta["tnch"],
      a_all, idx_all, h2_all, meta["segv"], lt)


def _moe_ffn_kernel(be_s, nused_s, x_ref, wgu_ref, bgu_ref, wd_ref, bd_ref, y_ref,
                    wgu_bf, wd_bf):
    b = pl.program_id(0)

    @pl.when(b < nused_s[0])
    def _():
        e = be_s[b]
        e_prev = be_s[jnp.maximum(b - 1, 0)]

        @pl.when((b == 0) | (e != e_prev))
        def _():
            wgu_bf[...] = wgu_ref[0].astype(BF16)
            wd_bf[...] = wd_ref[0].astype(BF16)

        hgu = jnp.dot(x_ref[...], wgu_bf[...], preferred_element_type=F32) + bgu_ref[0]
        d_ff = hgu.shape[1] // 2
        hg = jnp.minimum(hgu[:, :d_ff], SWIGLU_LIMIT)
        hu = jnp.clip(hgu[:, d_ff:], -SWIGLU_LIMIT, SWIGLU_LIMIT)
        act = hg * jax.nn.sigmoid(SWIGLU_ALPHA * hg) * (hu + 1.0)
        y = jnp.dot(act.astype(BF16), wd_bf[...], preferred_element_type=F32) + bd_ref[0]
        y_ref[...] = y.astype(y_ref.dtype)


def _moe_ffn(meta, xb, w_gate_up, b_gate_up, w_down, b_down):
    cap = xb.shape[0]
    nb = cap // FFN_BLOCK
    d_ff2 = w_gate_up.shape[2]

    def blk(b, be, nu):
        return jnp.minimum(b, jnp.maximum(nu[0] - 1, 0))

    grid_spec = pltpu.PrefetchScalarGridSpec(
        num_scalar_prefetch=2,
        grid=(nb,),
        in_specs=[pl.BlockSpec((FFN_BLOCK, D_MODEL), lambda b, be, nu: (blk(b, be, nu), 0)),
                  pl.BlockSpec((1, D_MODEL, d_ff2), lambda b, be, nu: (be[blk(b, be, nu)], 0, 0)),
                  pl.BlockSpec((1, 1, d_ff2), lambda b, be, nu: (be[blk(b, be, nu)], 0, 0)),
                  pl.BlockSpec((1, d_ff2 // 2, D_MODEL),
                               lambda b, be, nu: (be[blk(b, be, nu)], 0, 0)),
                  pl.BlockSpec((1, 1, D_MODEL), lambda b, be, nu: (be[blk(b, be, nu)], 0, 0))],
        out_specs=pl.BlockSpec((FFN_BLOCK, D_MODEL), lambda b, be, nu: (blk(b, be, nu), 0)),
        scratch_shapes=[pltpu.VMEM((D_MODEL, d_ff2), BF16),
                        pltpu.VMEM((d_ff2 // 2, D_MODEL), BF16)],
    )
    return pl.pallas_call(
        _moe_ffn_kernel,
        grid_spec=grid_spec,
        out_shape=jax.ShapeDtypeStruct((cap, D_MODEL), BF16),
        compiler_params=_cparams(1),
        name="moe_ffn",
    )(meta["block_expert"], meta["n_used"], xb, w_gate_up,
      b_gate_up.reshape(N_EXPERTS, 1, d_ff2), w_down, b_down.reshape(N_EXPERTS, 1, D_MODEL))


def _moe_unsort_kernel(seg_s, goff_s, nch_s, ntot_s,
                       dst_ref, gk_ref, x1_ref, g2p_ref, g2s_ref, yb_hbm,
                       yp_ref, ys_ref, ybuf, sem, *, tm, n_rows, n_prompt_tiles):
    i = pl.program_id(0)

    @pl.when(i == 0)
    def _():
        ybuf[...] = jnp.zeros(ybuf.shape, ybuf.dtype)

    def chunk_copy(src_row, dst_row):
        return pltpu.make_async_copy(
            yb_hbm.at[pl.ds(pl.multiple_of(src_row, ROW_CHUNK), ROW_CHUNK)],
            ybuf.at[pl.ds(pl.multiple_of(dst_row, ROW_CHUNK), ROW_CHUNK)], sem)

    def per_expert(e, carry):
        so = seg_s[i * N_EXPERTS + e]
        go = goff_s[i * N_EXPERTS + e]

        def per_chunk(c, carry2):
            chunk_copy(go + c * ROW_CHUNK, so + c * ROW_CHUNK).start()
            return carry2

        return lax.fori_loop(0, nch_s[i * N_EXPERTS + e], per_chunk, carry)

    lax.fori_loop(0, N_EXPERTS, per_expert, 0)

    def wait_one(c, carry):
        chunk_copy(0, 0).wait()
        return carry

    lax.fori_loop(0, ntot_s[i], wait_one, 0)

    dst = dst_ref[...]
    gk = gk_ref[...]
    acc = jnp.zeros((tm, D_MODEL), F32)
    for c in range(n_rows // SEL_CHUNK):
        cols = (lax.broadcasted_iota(jnp.int32, (tm, SEL_CHUNK), 1) + c * SEL_CHUNK).astype(F32)
        q = jnp.zeros((tm, SEL_CHUNK), F32)
        for kk in range(TOP_K):
            q = q + jnp.where(cols == dst[:, kk:kk + 1], gk[:, kk:kk + 1], 0.0)
        acc = acc + jnp.dot(q.astype(BF16), ybuf[c * SEL_CHUNK:(c + 1) * SEL_CHUNK, :],
                            preferred_element_type=F32)

    @pl.when(i < n_prompt_tiles)
    def _():
        yp_ref[...] = x1_ref[...] + g2p_ref[...] * acc

    @pl.when(i >= n_prompt_tiles)
    def _():
        ys_ref[...] = x1_ref[...] + g2s_ref[...] * acc


def _moe_unsort(meta, dst_all, gk_all, x1_all, gt2_p, gt2_s, yb, *, tm, n_prompt_tiles):
    t_all = dst_all.shape[0]
    nt = t_all // tm
    n_rows = _sort_rows(tm)
    last_p = n_prompt_tiles - 1
    grid_spec = pltpu.PrefetchScalarGridSpec(
        num_scalar_prefetch=4,
        grid=(nt,),
        in_specs=[pl.BlockSpec((tm, LANES), lambda i, *_: (i, 0)),
                  pl.BlockSpec((tm, LANES), lambda i, *_: (i, 0)),
                  pl.BlockSpec((tm, D_MODEL), lambda i, *_: (i, 0)),
                  pl.BlockSpec((1, D_MODEL), lambda i, *_: (0, 0)),
                  pl.BlockSpec((tm, D_MODEL), lambda i, *_: (0, 0)),
                  pl.BlockSpec(memory_space=pl.ANY)],
        out_specs=[pl.BlockSpec((tm, D_MODEL), lambda i, *_: (jnp.minimum(i, last_p), 0)),
                   pl.BlockSpec((tm, D_MODEL), lambda i, *_: (0, 0))],
        scratch_shapes=[pltpu.VMEM((n_rows, D_MODEL), BF16),
                        pltpu.SemaphoreType.DMA(())],
    )
    return pl.pallas_call(
        functools.partial(_moe_unsort_kernel, tm=tm, n_rows=n_rows,
                          n_prompt_tiles=n_prompt_tiles),
        grid_spec=grid_spec,
        out_shape=[jax.ShapeDtypeStruct((n_prompt_tiles * tm, D_MODEL), F32),
                   jax.ShapeDtypeStruct((tm, D_MODEL), F32)],
        compiler_params=_cparams(1),
        name="moe_unsort",
    )(meta["seg"], meta["goff"], meta["nch"], meta["ntot"],
      dst_all, gk_all, x1_all, gt2_p, gt2_s, yb)


def _moe_meta(cnt, tm):
    nt = cnt.shape[0]
    cnt = cnt.astype(jnp.int32)
    cnt_pad = (cnt + ROW_CHUNK - 1) // ROW_CHUNK * ROW_CHUNK
    seg = jnp.cumsum(cnt_pad, axis=1) - cnt_pad
    rows_e = jnp.sum(cnt_pad, axis=0)
    region = (rows_e + FFN_BLOCK - 1) // FFN_BLOCK * FFN_BLOCK
    gstart = jnp.cumsum(region) - region
    goff = gstart[None, :] + jnp.cumsum(cnt_pad, axis=0) - cnt_pad
    nblk_e = region // FFN_BLOCK
    blk_end = jnp.cumsum(nblk_e)
    cap = _moe_cap(nt * tm, tm)
    blocks = jnp.arange(cap // FFN_BLOCK, dtype=jnp.int32)
    block_expert = jnp.minimum(
        jnp.sum((blk_end[None, :] <= blocks[:, None]).astype(jnp.int32), axis=1), N_EXPERTS - 1)
    segv = jnp.zeros((nt, 1, LANES), F32).at[:, 0, :N_EXPERTS].set(seg.astype(F32))
    return {
        "seg": seg.reshape(-1), "goff": goff.reshape(-1).astype(jnp.int32),
        "nch": (cnt_pad // ROW_CHUNK).reshape(-1),
        "ntot": jnp.sum(cnt_pad, axis=1) // ROW_CHUNK,
        "tstart": (gstart + rows_e).astype(jnp.int32),
        "tnch": (region - rows_e) // ROW_CHUNK,
        "block_expert": block_expert,
        "n_used": blk_end[-1:].astype(jnp.int32),
        "segv": segv,
    }


def _moe_cap(t_all, tm):
    nt = t_all // tm
    worst = TOP_K * t_all + nt * N_EXPERTS * (ROW_CHUNK - 1) + N_EXPERTS * (FFN_BLOCK - ROW_CHUNK)
    return -(-worst // FFN_BLOCK) * FFN_BLOCK


def _t5_bucket(dist):
    max_exact = NUM_BUCKETS // 2
    d = dist.astype(jnp.int32)
    ratio = (jnp.log(jnp.maximum(d, 1).astype(F32) / max_exact)
             / math.log(MAX_DISTANCE / max_exact))
    large = jnp.minimum(max_exact + (ratio * (NUM_BUCKETS - max_exact)).astype(jnp.int32),
                        NUM_BUCKETS - 1)
    return jnp.where(d < max_exact, d, large)


def _step_bias(tab, dil):
    return tab[_t5_bucket(dil * jnp.arange(ATT_BLK + 1))].astype(F32).T


def _band_table(sb):
    return jnp.concatenate([sb[:, ::-1], jnp.full((HEADS, ATT_BLK - 1), NEG_INF, F32)], axis=1)


def _cache_table(sb, dil):
    on_grid = sb[:, :0:-1]
    if dil == 1:
        return on_grid
    off = jnp.full((HEADS, ATT_BLK, dil - 1), NEG_INF, F32)
    return jnp.concatenate([on_grid[:, :, None], off], axis=2).reshape(HEADS, ATT_BLK * dil)


def kernel(x_prompt, x_sample, cache_kv_w128, cache_kv_w512, cache_kv_w2048, state_pool, c_prompt,
           c_sample, w_ada, b_ada, norm_mix_g, norm_ffn_g, w_in, q_norm_g, k_norm_g, rel_bias,
           w_pool_mix, pool_scale, w_up_attn, w_up_pool, w_out, w_router, b_router, w_gate_up,
           b_gate_up, w_down, b_down):
    assert w_ada.shape[0] == 1, "one layer"
    seq = x_prompt.shape[1]
    n_s = x_sample.shape[0]
    assert x_prompt.shape[0] == 1 and x_sample.shape[1] == 1
    assert seq % (DIL_GROUPS[-1][1] * ATT_BLK) == 0 and seq % TM_PROMPT == 0
    assert n_s == TM_SAMPLE
    dils = tuple(d for _, d in DIL_GROUPS)
    caches = (cache_kv_w128, cache_kv_w512, cache_kv_w2048)

    w_in_bf = w_in[0].astype(BF16)
    heads_of = jnp.arange(GROUP_W) // HEAD_DIM
    bdiag = (heads_of[:, None] == heads_of[None, :]).astype(BF16)
    qg = (jnp.tile(q_norm_g[0], HEADS) * SCALE).reshape(1, GROUP_W)
    kg = jnp.tile(k_norm_g[0], HEADS).reshape(1, GROUP_W)
    expand = (jnp.arange(LANES)[:, None] == heads_of[None, :]).astype(BF16)
    wr = jnp.zeros((D_MODEL, LANES), F32).at[:, :N_EXPERTS].set(w_router[0])
    wr_hi = wr.astype(BF16)
    wr_lo = (wr - wr_hi.astype(F32)).astype(BF16)
    br = jnp.full((1, LANES), NEG_INF, F32).at[0, :N_EXPERTS].set(b_router[0])
    wts = (w_pool_mix[0].astype(BF16), pool_scale[0].reshape(1, POOL_W),
           w_up_attn[0].astype(BF16), w_up_pool[0].astype(BF16), w_out[0].astype(BF16), expand,
           norm_ffn_g[0].reshape(1, D_MODEL), wr_hi, wr_lo, br)
    g1 = norm_mix_g[0].reshape(1, D_MODEL)

    n_c = 1 + n_s
    c_all = jnp.zeros((-(-n_c // 8) * 8, D_MODEL), F32).at[0:1].set(c_prompt).at[1:n_c].set(c_sample)
    mod = _ada(c_all, w_ada[0], b_ada[0])
    sh1, sc1, gt1, sh2, sc2, gt2 = jnp.split(mod, N_ADA, axis=-1)

    def prow(m):
        return m[0:1]

    def srows(m):
        return m[1:n_c]

    xp = x_prompt[0]
    q_p, k_p, v_p, u_p, sga_p, sgp_p, st_p = _proj(
        xp, g1, prow(sc1), prow(sh1), w_in_bf, bdiag, qg, kg,
        tm=TM_PROMPT, dils=dils, per_row=False)
    step_bias = [_step_bias(rel_bias[:, g * HEADS:(g + 1) * HEADS], d)
                 for g, (_, d) in enumerate(DIL_GROUPS)]
    o_p, lse_p = [], []
    for g, (_, d) in enumerate(DIL_GROUPS):
        o, lse = _attn_prompt(q_p[g], k_p[g], v_p[g], _band_table(step_bias[g]), d)
        o_p.append(o)
        lse_p.append(lse)

    xs = x_sample[:, 0]
    ones = (1, 1, 1)
    q_s, _, _, u_s, sga_s, sgp_s, st_s = _proj(
        xs, g1, srows(sc1), srows(sh1), w_in_bf, bdiag, qg, kg,
        tm=TM_SAMPLE, dils=ones, per_row=True)
    def heads(parts):
        return jnp.stack([p.astype(F32).reshape(n_s, HEADS, HEAD_DIM) for p in parts], axis=1)

    bself = jnp.stack([sb[:, 0:1] for sb in step_bias])
    btabs = [_cache_table(sb, d) for sb, (_, d) in zip(step_bias, DIL_GROUPS)]
    caches_t = [jnp.transpose(c, (0, 1, 3, 4, 5, 2)) for c in caches]
    o3, lse3 = _attn_sample(heads(q_s), heads([st[:, :GROUP_W] for st in st_s]),
                            heads([st[:, GROUP_W:] for st in st_s]), bself, btabs, caches_t)
    o_s = [o3[:, g].reshape(n_s, GROUP_W) for g in range(N_GROUPS)]
    lse_s = [jnp.zeros((n_s, LANES), F32).at[:, :HEADS].set(lse3[:, g, :, 0])
             for g in range(N_GROUPS)]
    pooled_s, pool_state_t = _pool_sample(jnp.transpose(state_pool, (0, 2, 1, 3)), u_s)
    pool_state_s = jnp.transpose(pool_state_t, (0, 2, 1, 3))

    nt_p = seq // TM_PROMPT
    t_all = seq + TM_PROMPT
    bufs = _post(xp, o_p, lse_p, (u_p,), sga_p, sgp_p, wts, (prow(gt1), prow(sc2), prow(sh2)),
                 tm=TM_PROMPT, dils=dils, per_row=False, rows_total=t_all, row_block0=0,
                 cnt_tiles=nt_p + 1, cnt_block=None, grid=nt_p, n_valid_steps=None,
                 alias_bufs=None)
    bufs = _post(xs, o_s, lse_s, pooled_s, sga_s, sgp_s, wts,
                 (srows(gt1), srows(sc2), srows(sh2)),
                 tm=TM_SAMPLE, dils=ones, per_row=True, rows_total=t_all,
                 row_block0=seq // TM_SAMPLE, cnt_tiles=nt_p + 1, cnt_block=nt_p,
                 grid=TM_PROMPT // TM_SAMPLE, n_valid_steps=1, alias_bufs=bufs)
    x1_all, h2_all, a_all, idx_all, gk_all, cnt = bufs

    meta = _moe_meta(cnt[:, 0, :N_EXPERTS], TM_PROMPT)
    cap = _moe_cap(t_all, TM_PROMPT)
    xb, dst_all = _moe_sort(meta, a_all, idx_all, h2_all, tm=TM_PROMPT, cap=cap)
    yb = _moe_ffn(meta, xb, w_gate_up[0], b_gate_up[0], w_down[0], b_down[0])
    gt2_s = jnp.zeros((TM_PROMPT, D_MODEL), F32).at[:n_s].set(srows(gt2))
    y_p, y_s = _moe_unsort(meta, dst_all, gk_all, x1_all, prow(gt2), gt2_s, yb,
                           tm=TM_PROMPT, n_prompt_tiles=nt_p)

    def kv_state(st, rows):
        return st.reshape(1, 1, rows, 2, HEADS, HEAD_DIM)

    kv_p = [kv_state(st, st.shape[0]) for st in st_p]
    kv_s = [st.reshape(1, n_s, 1, 2, HEADS, HEAD_DIM) for st in st_s]
    pool_p = u_p[seq - POOL_BUF:].reshape(1, 1, POOL_BUF, POOL_W)
    return (y_p.reshape(1, seq, D_MODEL), y_s[:n_s].reshape(n_s, 1, D_MODEL),
            kv_p[0], kv_p[1], kv_p[2], pool_p, kv_s[0], kv_s[1], kv_s[2], pool_state_s)
```

```python
import functools
import math

import jax
import jax.numpy as jnp
from jax import lax
from jax.experimental import pallas as pl
from jax.experimental.pallas import tpu as pltpu

F32 = jnp.float32
BF16 = jnp.bfloat16

D_MODEL = 1024
HEAD_DIM = 64
HEADS = 8
GROUP_W = HEADS * HEAD_DIM
DIL_GROUPS = ((128, 1), (512, 4), (2048, 16))
N_GROUPS = len(DIL_GROUPS)
QKV_W = N_GROUPS * GROUP_W
ATT_BLK = 128
POOL_WINDOWS = (2, 4, 8, 16)
POOL_W = 512
POOL_GW = 128
POOL_BUF = 15
OFF_K, OFF_V = QKV_W, 2 * QKV_W
OFF_U = 3 * QKV_W
OFF_GA = OFF_U + POOL_W
OFF_GP = OFF_GA + D_MODEL
IN_W = OFF_GP + D_MODEL
NUM_BUCKETS = 32
MAX_DISTANCE = 2048
N_EXPERTS = 32
TOP_K = 4
SWIGLU_LIMIT = 7.0
SWIGLU_ALPHA = 1.702
N_ADA = 6
EPS = 1e-6
NEG_INF = -1e30
PAST_LEN = 8192
SCALE = HEAD_DIM ** -0.5

LANES = 128
ROW_CHUNK = 16
TM_PROMPT = 512
TM_SAMPLE = 128
FFN_BLOCK = 512
SEL_CHUNK = 512
VMEM_LIMIT = 56 * 1024 * 1024


def _cparams(n_axes):
    return pltpu.CompilerParams(dimension_semantics=("arbitrary",) * n_axes,
                                vmem_limit_bytes=VMEM_LIMIT)


def _const_spec(shape):
    nd = len(shape)
    return pl.BlockSpec(shape, lambda *_: (0,) * nd)


def _ada_kernel(c_ref, w_ref, b_ref, o_ref):
    c = c_ref[...]
    s = c * jax.nn.sigmoid(c)
    o_ref[...] = jnp.dot(s.astype(BF16), w_ref[...].astype(BF16),
                         preferred_element_type=F32) + b_ref[...]


def _ada(c_all, w_ada, b_ada):
    rows = c_all.shape[0]
    n = w_ada.shape[1]
    tn = 1536
    return pl.pallas_call(
        _ada_kernel,
        grid=(n // tn,),
        in_specs=[pl.BlockSpec((rows, D_MODEL), lambda j: (0, 0)),
                  pl.BlockSpec((D_MODEL, tn), lambda j: (0, j)),
                  pl.BlockSpec((1, tn), lambda j: (0, j))],
        out_specs=pl.BlockSpec((rows, tn), lambda j: (0, j)),
        out_shape=jax.ShapeDtypeStruct((rows, n), F32),
        compiler_params=_cparams(1),
        name="ada",
    )(c_all, w_ada, b_ada.reshape(1, n))


def _proj_kernel(x_ref, g_ref, sc_ref, sh_ref, w_ref, bd_ref, qg_ref, kg_ref,
                 *refs, tm, dils, st_rows):
    q_refs, k_refs, v_refs = refs[0:3], refs[3:6], refs[6:9]
    u_ref, sga_ref, sgp_ref = refs[9:12]
    st_refs = refs[12:15]
    scr = refs[15]

    x = x_ref[...]
    ms = jnp.mean(x * x, axis=-1, keepdims=True)
    h = x * lax.rsqrt(ms + EPS) * g_ref[...] * (1.0 + sc_ref[...]) + sh_ref[...]
    hb = h.astype(BF16)

    def proj(off, width):
        return jnp.dot(hb, w_ref[:, off:off + width], preferred_element_type=F32)

    def head_norm(z, gain_ref):
        ss = jnp.dot((z * z).astype(BF16), bd_ref[...], preferred_element_type=F32)
        return z * lax.rsqrt(ss * (1.0 / HEAD_DIM) + EPS) * gain_ref[...]

    def put(out_ref, val, d):
        if d == 1:
            out_ref[...] = val.astype(out_ref.dtype)
        else:
            for c in range(GROUP_W // LANES):
                scr[c] = val[:, c * LANES:(c + 1) * LANES]
            for r in range(d):
                for c in range(GROUP_W // LANES):
                    col = r * GROUP_W + c * LANES
                    out_ref[:, col:col + LANES] = (
                        scr[c, pl.ds(r, tm // d, stride=d), :].astype(out_ref.dtype))

    for g, d in enumerate(dils):
        qn = head_norm(proj(g * GROUP_W, GROUP_W), qg_ref)
        put(q_refs[g], qn, d)
        kn = head_norm(proj(OFF_K + g * GROUP_W, GROUP_W), kg_ref)
        put(k_refs[g], kn, d)
        v = proj(OFF_V + g * GROUP_W, GROUP_W)
        put(v_refs[g], v, d)
        rb = st_rows[g]
        st_refs[g][:, 0:GROUP_W] = kn[tm - rb:, :]
        st_refs[g][:, GROUP_W:2 * GROUP_W] = v[tm - rb:, :]

    u_ref[...] = proj(OFF_U, POOL_W)
    sga_ref[...] = jax.nn.sigmoid(proj(OFF_GA, D_MODEL)).astype(BF16)
    sgp_ref[...] = jax.nn.sigmoid(proj(OFF_GP, D_MODEL)).astype(BF16)


def _mod_spec(per_row, tm):
    if per_row:
        return pl.BlockSpec((tm, D_MODEL), lambda i: (i, 0))
    return pl.BlockSpec((1, D_MODEL), lambda i: (0, 0))


def _proj(x, g1, sc1, sh1, w_in_bf, bdiag, qg, kg, *, tm, dils, per_row):
    s = x.shape[0]
    nt = s // tm
    wins = tuple(min(w, s) for w, _ in DIL_GROUPS)
    st_rows = tuple(min(tm, w) for w in wins)

    def res_spec(d):
        return pl.BlockSpec((tm // d, d * GROUP_W), lambda i: (i, 0))

    def st_spec(w, rb):
        first = nt - w // rb
        return pl.BlockSpec((rb, 2 * GROUP_W), lambda i: (jnp.maximum(i - first, 0), 0))

    qkv_shapes = [jax.ShapeDtypeStruct((s // d, d * GROUP_W), BF16) for d in dils]
    out_shape = (qkv_shapes * 3
                 + [jax.ShapeDtypeStruct((s, POOL_W), F32),
                    jax.ShapeDtypeStruct((s, D_MODEL), BF16),
                    jax.ShapeDtypeStruct((s, D_MODEL), BF16)]
                 + [jax.ShapeDtypeStruct((w, 2 * GROUP_W), F32) for w in wins])
    out_specs = ([res_spec(d) for d in dils] * 3
                 + [pl.BlockSpec((tm, POOL_W), lambda i: (i, 0)),
                    pl.BlockSpec((tm, D_MODEL), lambda i: (i, 0)),
                    pl.BlockSpec((tm, D_MODEL), lambda i: (i, 0))]
                 + [st_spec(w, rb) for w, rb in zip(wins, st_rows)])
    in_specs = [pl.BlockSpec((tm, D_MODEL), lambda i: (i, 0)),
                _const_spec((1, D_MODEL)),
                _mod_spec(per_row, tm), _mod_spec(per_row, tm),
                pl.BlockSpec((D_MODEL, IN_W), lambda i: (0, 0), pipeline_mode=pl.Buffered(1)),
                _const_spec((GROUP_W, GROUP_W)),
                _const_spec((1, GROUP_W)), _const_spec((1, GROUP_W))]
    outs = pl.pallas_call(
        functools.partial(_proj_kernel, tm=tm, dils=dils, st_rows=st_rows),
        grid=(nt,),
        in_specs=in_specs,
        out_specs=out_specs,
        out_shape=out_shape,
        scratch_shapes=[pltpu.VMEM((GROUP_W // LANES, tm, LANES), F32)],
        compiler_params=_cparams(1),
        name="proj",
    )(x, g1, sc1, sh1, w_in_bf, bdiag, qg, kg)
    return outs[0:3], outs[3:6], outs[6:9], outs[9], outs[10], outs[11], outs[12:15]


def _attn_kernel(q_ref, kp_ref, kc_ref, vp_ref, vc_ref, r_ref, o_ref, lse_ref, bias_ref):
    i = pl.program_id(1)

    @pl.when((pl.program_id(0) == 0) & (i == 0))
    def _():
        for h in range(HEADS):
            row = jnp.broadcast_to(r_ref[h:h + 1, :], (ATT_BLK, 2 * ATT_BLK))
            bias_ref[h] = pltpu.roll(row, 0, 1, stride=1, stride_axis=0)

    q = q_ref[...]
    k = jnp.concatenate([kp_ref[...], kc_ref[...]], axis=0)
    v = jnp.concatenate([vp_ref[...], vc_ref[...]], axis=0)
    col = lax.broadcasted_iota(jnp.int32, (ATT_BLK, 2 * ATT_BLK), 1)
    no_prev = jnp.where((col < ATT_BLK) & (i == 0), NEG_INF, 0.0)
    lane_q = lax.broadcasted_iota(jnp.int32, (ATT_BLK, LANES), 1)
    lane_v = lax.broadcasted_iota(jnp.int32, (2 * ATT_BLK, LANES), 1)
    def pair(h):
        return slice((h // 2) * LANES, (h // 2 + 1) * LANES)

    def mine(lane, h):
        return (lane < HEAD_DIM) == (h % 2 == 0)

    ss = []
    for h in range(HEADS):
        q2 = q[:, pair(h)]
        s = lax.dot_general(jnp.where(mine(lane_q, h), q2, jnp.zeros_like(q2)), k[:, pair(h)],
                            (((1,), (1,)), ((), ())), preferred_element_type=F32)
        ss.append(s + bias_ref[h] + no_prev)
    s = jnp.concatenate(ss, axis=0)
    m = jnp.max(s, axis=-1, keepdims=True)
    p = jnp.exp(s - m)
    l = jnp.sum(p, axis=-1, keepdims=True)
    pb = p.astype(BF16)
    lse = m + jnp.log(l)
    inv_l = 1.0 / l
    outs, lses = [], []
    for h in range(HEADS):
        rows = slice(h * ATT_BLK, (h + 1) * ATT_BLK)
        v2 = v[:, pair(h)]
        o = jnp.dot(pb[rows], jnp.where(mine(lane_v, h), v2, jnp.zeros_like(v2)),
                    preferred_element_type=F32) * inv_l[rows]
        if h % 2 == 0:
            outs.append(o)
        else:
            outs[-1] = outs[-1] + o
        lses.append(lse[rows])
    o_ref[...] = jnp.concatenate(outs, axis=-1).astype(o_ref.dtype)
    lse_ref[...] = jnp.concatenate(
        lses + [jnp.zeros((ATT_BLK, LANES - HEADS), F32)], axis=-1)


def _attn_prompt(q, k, v, r_tab, d):
    rows = q.shape[0]
    nblk = rows // ATT_BLK
    cur = pl.BlockSpec((ATT_BLK, GROUP_W), lambda r, i: (i, r))
    prev = pl.BlockSpec((ATT_BLK, GROUP_W), lambda r, i: (jnp.maximum(i - 1, 0), r))
    return pl.pallas_call(
        _attn_kernel,
        grid=(d, nblk),
        in_specs=[cur, prev, cur, prev, cur,
                  pl.BlockSpec((HEADS, 2 * ATT_BLK), lambda r, i: (0, 0))],
        out_specs=[pl.BlockSpec((ATT_BLK, GROUP_W), lambda r, i: (i, r)),
                   pl.BlockSpec((ATT_BLK, LANES), lambda r, i: (i, r))],
        out_shape=[jax.ShapeDtypeStruct((rows, d * GROUP_W), BF16),
                   jax.ShapeDtypeStruct((rows, d * LANES), F32)],
        scratch_shapes=[pltpu.VMEM((HEADS, ATT_BLK, 2 * ATT_BLK), F32)],
        compiler_params=_cparams(2),
        name=f"attn_d{d}",
    )(q, k, k, v, v, r_tab)


def _attn_sample_kernel(q_ref, kn_ref, vn_ref, bself_ref, b0_ref, b1_ref, b2_ref,
                        c0_ref, c1_ref, c2_ref, o_ref, lse_ref):
    row_e = lax.broadcasted_iota(jnp.int32, (HEADS, HEAD_DIM), 0)
    for g, (c_ref, b_ref) in enumerate(((c0_ref, b0_ref), (c1_ref, b1_ref), (c2_ref, b2_ref))):
        win = b_ref.shape[1]
        row_w = lax.broadcasted_iota(jnp.int32, (HEADS, win), 0)
        q = q_ref[0, g]
        qb = q.astype(BF16)
        s = jnp.zeros((HEADS, win), F32)
        for h in range(HEADS):
            sh = jnp.dot(qb, c_ref[0, 0, 0, h].astype(BF16), preferred_element_type=F32)
            s = jnp.where(row_w == h, sh, s)
        s = s + b_ref[...]
        s0 = jnp.sum(q * kn_ref[0, g], axis=-1, keepdims=True) + bself_ref[g]
        m = jnp.maximum(jnp.max(s, axis=-1, keepdims=True), s0)
        p = jnp.exp(s - m)
        p0 = jnp.exp(s0 - m)
        l = jnp.sum(p, axis=-1, keepdims=True) + p0
        pb = p.astype(BF16)
        o = jnp.zeros((HEADS, HEAD_DIM), F32)
        for h in range(HEADS):
            oh = lax.dot_general(pb, c_ref[0, 0, 1, h].astype(BF16), (((1,), (1,)), ((), ())),
                                 preferred_element_type=F32)
            o = jnp.where(row_e == h, oh, o)
        o_ref[0, g] = (o + p0 * vn_ref[0, g]) / l
        lse_ref[0, g] = m + jnp.log(l)


def _attn_sample(q3, kn3, vn3, bself, btabs, caches_t):
    n = q3.shape[0]
    tok = pl.BlockSpec((1, N_GROUPS, HEADS, HEAD_DIM), lambda t: (t, 0, 0, 0))

    def cache_spec(c):
        return pl.BlockSpec((1, 1) + c.shape[2:], lambda t: (0, t, 0, 0, 0, 0))

    return pl.pallas_call(
        _attn_sample_kernel,
        grid=(n,),
        in_specs=[tok, tok, tok, _const_spec(bself.shape)]
                 + [_const_spec(b.shape) for b in btabs]
                 + [cache_spec(c) for c in caches_t],
        out_specs=[tok, pl.BlockSpec((1, N_GROUPS, HEADS, 1), lambda t: (t, 0, 0, 0))],
        out_shape=[jax.ShapeDtypeStruct((n, N_GROUPS, HEADS, HEAD_DIM), F32),
                   jax.ShapeDtypeStruct((n, N_GROUPS, HEADS, 1), F32)],
        compiler_params=_cparams(1),
        name="attn_sample",
    )(q3, kn3, vn3, bself, *btabs, *caches_t)


def _pool_sample_kernel(st_ref, u_ref, pooled_ref, new_ref):
    u = u_ref[...]
    rows = [st_ref[0, j] for j in range(POOL_BUF)]
    outs = []
    for g, w in enumerate(POOL_WINDOWS):
        sl = slice(g * POOL_GW, (g + 1) * POOL_GW)
        acc = u[:, sl]
        for j in range(POOL_BUF - (w - 1), POOL_BUF):
            acc = acc + rows[j][:, sl]
        outs.append(acc / float(w) - u[:, sl])
    pooled_ref[...] = jnp.concatenate(outs, axis=-1)
    for j in range(POOL_BUF - 1):
        new_ref[0, j] = rows[j + 1]
    new_ref[0, POOL_BUF - 1] = u


def _pool_sample(state, u):
    n = u.shape[0]
    return pl.pallas_call(
        _pool_sample_kernel,
        grid=(1,),
        in_specs=[_const_spec(state.shape), _const_spec(u.shape)],
        out_specs=[_const_spec(u.shape), _const_spec(state.shape)],
        out_shape=[jax.ShapeDtypeStruct((n, POOL_W), F32),
                   jax.ShapeDtypeStruct(state.shape, F32)],
        compiler_params=_cparams(1),
        name="pool_sample",
    )(state, u)


def _post_kernel(*refs, tm, dils, pooled_given, n_valid_steps, aliased):
    it = iter(refs)
    x_ref = next(it)
    o_refs = [next(it) for _ in range(N_GROUPS)]
    lse_refs = [next(it) for _ in range(N_GROUPS)]
    if pooled_given:
        pooled_ref = next(it)
    else:
        u_ref, uh_ref = next(it), next(it)
    sga_ref, sgp_ref = next(it), next(it)
    wpm_ref, psc_ref, wua_ref, wup_ref, wout_ref, exp_ref = (next(it) for _ in range(6))
    gt1_ref, g2_ref, sc2_ref, sh2_ref = (next(it) for _ in range(4))
    wrh_ref, wrl_ref, br_ref = (next(it) for _ in range(3))
    if aliased:
        for _ in range(6):
            next(it)
    x1_ref, h2_ref, a_ref, idx_ref, gk_ref, cnt_ref = (next(it) for _ in range(6))
    ob_scr, ls_scr = next(it), next(it)

    i = pl.program_id(0)

    def compute():
        obs, lss = [], []
        for g, d in enumerate(dils):
            if d == 1:
                obs.append(o_refs[g][...].astype(F32))
                lss.append(lse_refs[g][...])
            else:
                for r in range(d):
                    for c in range(GROUP_W // LANES):
                        col = r * GROUP_W + c * LANES
                        ob_scr[c, pl.ds(r, tm // d, stride=d), :] = (
                            o_refs[g][:, col:col + LANES].astype(F32))
                    ls_scr[pl.ds(r, tm // d, stride=d), :] = (
                        lse_refs[g][:, r * LANES:(r + 1) * LANES])
                obs.append(jnp.concatenate([ob_scr[c] for c in range(GROUP_W // LANES)],
                                           axis=-1))
                lss.append(ls_scr[...])
        mx = jnp.maximum(jnp.maximum(lss[0], lss[1]), lss[2])
        es = [jnp.exp(l - mx) for l in lss]
        den = es[0] + es[1] + es[2]
        attn_o = jnp.zeros((tm, GROUP_W), F32)
        for g in range(N_GROUPS):
            w = es[g] / den
            w_hi = w.astype(BF16)
            w_lo = (w - w_hi.astype(F32)).astype(BF16)
            wexp = (jnp.dot(w_hi, exp_ref[...], preferred_element_type=F32)
                    + jnp.dot(w_lo, exp_ref[...], preferred_element_type=F32))
            attn_o = attn_o + wexp * obs[g]

        if pooled_given:
            pooled = pooled_ref[...]
        else:
            u = u_ref[...]
            halo = jnp.where(i == 0, 0.0, uh_ref[...])
            pos = (lax.broadcasted_iota(jnp.int32, (tm, 1), 0) + i * tm + 1).astype(F32)
            outs = []
            for g, w in enumerate(POOL_WINDOWS):
                sl = slice(g * POOL_GW, (g + 1) * POOL_GW)
                a = jnp.concatenate([halo[:, sl], u[:, sl]], axis=0)
                span = 1
                while span < w:
                    n = a.shape[0] - span
                    a = a[span:, :] + a[:n, :]
                    span *= 2
                off = a.shape[0] - tm
                win_sum = a[off:, :]
                outs.append(win_sum / jnp.minimum(pos, float(w)) - u[:, sl])
            pooled = jnp.concatenate(outs, axis=-1)
        pool_parts = []
        for g in range(len(POOL_WINDOWS)):
            sl = slice(g * POOL_GW, (g + 1) * POOL_GW)
            pool_parts.append(jnp.dot(pooled[:, sl].astype(BF16), wpm_ref[g],
                                      preferred_element_type=F32))
        pool_o = jnp.concatenate(pool_parts, axis=-1) * psc_ref[...]

        up_a = jnp.dot(attn_o.astype(BF16), wua_ref[...], preferred_element_type=F32)
        up_p = jnp.dot(pool_o.astype(BF16), wup_ref[...], preferred_element_type=F32)
        merged = sga_ref[...].astype(F32) * up_a + sgp_ref[...].astype(F32) * up_p
        mo = jnp.dot(merged.astype(BF16), wout_ref[...], preferred_element_type=F32)
        x1 = x_ref[...] + gt1_ref[...] * mo
        x1_ref[...] = x1

        ms = jnp.mean(x1 * x1, axis=-1, keepdims=True)
        h2 = x1 * lax.rsqrt(ms + EPS) * g2_ref[...] * (1.0 + sc2_ref[...]) + sh2_ref[...]
        h2_hi = h2.astype(BF16)
        h2_ref[...] = h2_hi
        h2_lo = (h2 - h2_hi.astype(F32)).astype(BF16)
        logits = (jnp.dot(h2_hi, wrh_ref[...], preferred_element_type=F32)
                  + jnp.dot(h2_lo, wrh_ref[...], preferred_element_type=F32)
                  + jnp.dot(h2_hi, wrl_ref[...], preferred_element_type=F32)
                  + br_ref[...])
        lane = lax.broadcasted_iota(jnp.int32, (tm, LANES), 1).astype(F32)
        work = logits
        vals, ids = [], []
        for _ in range(TOP_K):
            m = jnp.max(work, axis=-1, keepdims=True)
            ik = jnp.min(jnp.where(work == m, lane, float(LANES)), axis=-1, keepdims=True)
            vals.append(m)
            ids.append(ik)
            work = jnp.where(lane == ik, -3e38, work)
        ex = [jnp.exp(v - vals[0]) for v in vals]
        den_k = ex[0] + ex[1] + ex[2] + ex[3]
        a = jnp.zeros((tm, LANES), F32)
        idx = jnp.zeros((tm, LANES), F32)
        gk = jnp.zeros((tm, LANES), F32)
        for kk in range(TOP_K):
            gate = ex[kk] / den_k
            a = a + jnp.where(lane == ids[kk], gate, 0.0)
            idx = jnp.where(lane == float(kk), ids[kk], idx)
            gk = jnp.where(lane == float(kk), gate, gk)
        a_ref[...] = a
        idx_ref[...] = idx
        gk_ref[...] = gk
        cnt = jnp.sum((a > 0.0).astype(F32), axis=0, keepdims=True)
        row = lax.broadcasted_iota(jnp.int32, (8, LANES), 0)
        cnt_ref[0] = jnp.where(row == 0, jnp.broadcast_to(cnt, (8, LANES)), 0.0)

    if n_valid_steps is None:
        compute()
    else:
        pl.when(i < n_valid_steps)(compute)

        @pl.when(i >= n_valid_steps)
        def _():
            x1_ref[...] = jnp.zeros(x1_ref.shape, x1_ref.dtype)
            h2_ref[...] = jnp.zeros(h2_ref.shape, h2_ref.dtype)
            a_ref[...] = jnp.zeros(a_ref.shape, a_ref.dtype)
            idx_ref[...] = jnp.zeros(idx_ref.shape, idx_ref.dtype)
            gk_ref[...] = jnp.zeros(gk_ref.shape, gk_ref.dtype)


def _post(x, o_list, lse_list, pool_in, sga, sgp, wts, mods, *, tm, dils, per_row,
          rows_total, row_block0, cnt_tiles, cnt_block, grid, n_valid_steps, alias_bufs):
    pooled_given = not isinstance(pool_in, tuple)
    nv = grid if n_valid_steps is None else n_valid_steps

    def clamp(i):
        return jnp.minimum(i, nv - 1)

    def tile_spec(width):
        return pl.BlockSpec((tm, width), lambda i: (clamp(i), 0))

    in_specs = [tile_spec(D_MODEL)]
    in_specs += [pl.BlockSpec((tm // d, d * GROUP_W), lambda i: (clamp(i), 0)) for d in dils]
    in_specs += [pl.BlockSpec((tm // d, d * LANES), lambda i: (clamp(i), 0)) for d in dils]
    args = [x, *o_list, *lse_list]
    if pooled_given:
        in_specs.append(tile_spec(POOL_W))
        args.append(pool_in)
    else:
        u = pool_in[0]
        in_specs += [tile_spec(POOL_W),
                     pl.BlockSpec((16, POOL_W),
                                  lambda i: (jnp.maximum(i * (tm // 16) - 1, 0), 0))]
        args += [u, u]
    in_specs += [tile_spec(D_MODEL), tile_spec(D_MODEL)]
    args += [sga, sgp]
    wpm, psc, wua, wup, wout, expand, g2, wrh, wrl, br = wts
    gt1, sc2, sh2 = mods

    def mspec():
        if per_row:
            return pl.BlockSpec((tm, D_MODEL), lambda i: (clamp(i), 0))
        return _const_spec((1, D_MODEL))

    in_specs += [_const_spec(wpm.shape), _const_spec(psc.shape), _const_spec(wua.shape),
                 _const_spec(wup.shape), _const_spec(wout.shape), _const_spec(expand.shape),
                 mspec(), _const_spec(g2.shape), mspec(), mspec(),
                 _const_spec(wrh.shape), _const_spec(wrl.shape), _const_spec(br.shape)]
    args += [wpm, psc, wua, wup, wout, expand, gt1, g2, sc2, sh2, wrh, wrl, br]
    aliases = {}
    if alias_bufs is not None:
        base = len(args)
        in_specs += [pl.BlockSpec(memory_space=pl.ANY)] * 6
        args += list(alias_bufs)
        aliases = {base + j: j for j in range(6)}

    def out_spec(width):
        return pl.BlockSpec((tm, width), lambda i: (row_block0 + i, 0))

    out_specs = [out_spec(D_MODEL), out_spec(D_MODEL), out_spec(LANES), out_spec(LANES),
                 out_spec(LANES),
                 pl.BlockSpec((1, 8, LANES),
                              lambda i: (cnt_block if cnt_block is not None else i, 0, 0))]
    out_shape = [jax.ShapeDtypeStruct((rows_total, D_MODEL), F32),
                 jax.ShapeDtypeStruct((rows_total, D_MODEL), BF16),
                 jax.ShapeDtypeStruct((rows_total, LANES), F32),
                 jax.ShapeDtypeStruct((rows_total, LANES), F32),
                 jax.ShapeDtypeStruct((rows_total, LANES), F32),
                 jax.ShapeDtypeStruct((cnt_tiles, 8, LANES), F32)]
    return pl.pallas_call(
        functools.partial(_post_kernel, tm=tm, dils=dils, pooled_given=pooled_given,
                          n_valid_steps=n_valid_steps, aliased=alias_bufs is not None),
        grid=(grid,),
        in_specs=in_specs,
        out_specs=out_specs,
        out_shape=out_shape,
        scratch_shapes=[pltpu.VMEM((GROUP_W // LANES, tm, LANES), F32),
                        pltpu.VMEM((tm, LANES), F32)],
        input_output_aliases=aliases,
        compiler_params=_cparams(1),
        name="post_sample" if per_row else "post",
    )(*args)


def _sort_rows(tm):
    return -(-(TOP_K * tm + N_EXPERTS * (ROW_CHUNK - 1)) // SEL_CHUNK) * SEL_CHUNK


def _moe_sort_kernel(seg_s, goff_s, nch_s, ntot_s, tstart_s, tnch_s,
                     a_ref, idx_ref, h2_ref, segv_ref, lt_ref,
                     xb_hbm, dst_ref, xs_scr, zero_scr, sem, *, tm, n_rows):
    i = pl.program_id(0)
    nt = pl.num_programs(0)
    slot = i % 2
    sel = a_ref[...] > 0.0
    ahead = jnp.dot(lt_ref[...], sel.astype(BF16), preferred_element_type=F32)
    slot1 = jnp.where(sel, segv_ref[0] + ahead + 1.0, 0.0)
    lane = lax.broadcasted_iota(jnp.int32, (tm, LANES), 1).astype(F32)
    idx = idx_ref[...]
    dst = jnp.full((tm, LANES), -1.0, F32)
    for kk in range(TOP_K):
        hit = lane == idx[:, kk:kk + 1]
        dk = jnp.sum(jnp.where(hit, slot1, 0.0), axis=-1, keepdims=True) - 1.0
        dst = jnp.where(lane == float(kk), dk, dst)
    dst_ref[...] = dst
    dst_t = dst.T
    h2 = h2_ref[...]
    for c in range(n_rows // SEL_CHUNK):
        rows = (lax.broadcasted_iota(jnp.int32, (SEL_CHUNK, tm), 0) + c * SEL_CHUNK).astype(F32)
        p = rows == dst_t[0:1, :]
        for kk in range(1, TOP_K):
            p = p | (rows == dst_t[kk:kk + 1, :])
        xs = jnp.dot(jnp.where(p, 1.0, 0.0).astype(BF16), h2, preferred_element_type=F32)
        xs_scr[slot, c * SEL_CHUNK:(c + 1) * SEL_CHUNK, :] = xs.astype(BF16)

    def chunk_copy(buf, src_row, dst_row):
        return pltpu.make_async_copy(
            xs_scr.at[buf, pl.ds(pl.multiple_of(src_row, ROW_CHUNK), ROW_CHUNK)],
            xb_hbm.at[pl.ds(pl.multiple_of(dst_row, ROW_CHUNK), ROW_CHUNK)], sem.at[buf])

    def per_expert(e, carry):
        so = seg_s[i * N_EXPERTS + e]
        go = goff_s[i * N_EXPERTS + e]

        def per_chunk(c, carry2):
            chunk_copy(slot, so + c * ROW_CHUNK, go + c * ROW_CHUNK).start()
            return carry2

        return lax.fori_loop(0, nch_s[i * N_EXPERTS + e], per_chunk, carry)

    lax.fori_loop(0, N_EXPERTS, per_expert, 0)

    def drain(buf, tile):
        def wait_one(c, carry):
            chunk_copy(buf, 0, 0).wait()
            return carry

        lax.fori_loop(0, ntot_s[tile], wait_one, 0)

    @pl.when(i > 0)
    def _():
        drain(1 - slot, i - 1)

    @pl.when(i == nt - 1)
    def _():
        drain(slot, i)
        zero_scr[...] = jnp.zeros(zero_scr.shape, zero_scr.dtype)

        def tail_copy(dst_row):
            return pltpu.make_async_copy(
                zero_scr, xb_hbm.at[pl.ds(pl.multiple_of(dst_row, ROW_CHUNK), ROW_CHUNK)],
                sem.at[2])

        def per_expert_tail(e, carry):
            def per_chunk(c, carry2):
                tail_copy(tstart_s[e] + c * ROW_CHUNK).start()
                return carry2

            lax.fori_loop(0, tnch_s[e], per_chunk, 0)

            def wait_chunk(c, carry2):
                tail_copy(0).wait()
                return carry2

            return lax.fori_loop(0, tnch_s[e], wait_chunk, carry)

        lax.fori_loop(0, N_EXPERTS, per_expert_tail, 0)


def _moe_sort(meta, a_all, idx_all, h2_all, *, tm, cap):
    t_all = a_all.shape[0]
    nt = t_all // tm
    n_rows = _sort_rows(tm)
    lt = jnp.tril(jnp.ones((tm, tm), BF16), -1)
    grid_spec = pltpu.PrefetchScalarGridSpec(
        num_scalar_prefetch=6,
        grid=(nt,),
        in_specs=[pl.BlockSpec((tm, LANES), lambda i, *_: (i, 0)),
                  pl.BlockSpec((tm, LANES), lambda i, *_: (i, 0)),
                  pl.BlockSpec((tm, D_MODEL), lambda i, *_: (i, 0)),
                  pl.BlockSpec((1, 1, LANES), lambda i, *_: (i, 0, 0)),
                  pl.BlockSpec((tm, tm), lambda i, *_: (0, 0))],
        out_specs=[pl.BlockSpec(memory_space=pl.ANY),
                   pl.BlockSpec((tm, LANES), lambda i, *_: (i, 0))],
        scratch_shapes=[pltpu.VMEM((2, n_rows, D_MODEL), BF16),
                        pltpu.VMEM((ROW_CHUNK, D_MODEL), BF16),
                        pltpu.SemaphoreType.DMA((3,))],
    )
    return pl.pallas_call(
        functools.partial(_moe_sort_kernel, tm=tm, n_rows=n_rows),
        grid_spec=grid_spec,
        out_shape=[jax.ShapeDtypeStruct((cap, D_MODEL), BF16),
                   jax.ShapeDtypeStruct((t_all, LANES), F32)],
        compiler_params=_cparams(1),
        name="moe_sort",
    )(meta["seg"], meta["goff"], meta["nch"], meta["ntot"], meta["tstart"], meta["tnch"],
      a_all, idx_all, h2_all, meta["segv"], lt)


def _moe_ffn_kernel(be_s, nused_s, x_ref, wgu_ref, bgu_ref, wd_ref, bd_ref, y_ref,
                    wgu_bf, wd_bf):
    b = pl.program_id(0)

    @pl.when(b < nused_s[0])
    def _():
        e = be_s[b]
        e_prev = be_s[jnp.maximum(b - 1, 0)]

        @pl.when((b == 0) | (e != e_prev))
        def _():
            wgu_bf[...] = wgu_ref[0].astype(BF16)
            wd_bf[...] = wd_ref[0].astype(BF16)

        hgu = jnp.dot(x_ref[...], wgu_bf[...], preferred_element_type=F32) + bgu_ref[0]
        d_ff = hgu.shape[1] // 2
        hg = jnp.minimum(hgu[:, :d_ff], SWIGLU_LIMIT)
        hu = jnp.clip(hgu[:, d_ff:], -SWIGLU_LIMIT, SWIGLU_LIMIT)
        act = hg * jax.nn.sigmoid(SWIGLU_ALPHA * hg) * (hu + 1.0)
        y = jnp.dot(act.astype(BF16), wd_bf[...], preferred_element_type=F32) + bd_ref[0]
        y_ref[...] = y.astype(y_ref.dtype)


def _moe_ffn(meta, xb, w_gate_up, b_gate_up, w_down, b_down):
    cap = xb.shape[0]
    nb = cap // FFN_BLOCK
    d_ff2 = w_gate_up.shape[2]

    def blk(b, be, nu):
        return jnp.minimum(b, jnp.maximum(nu[0] - 1, 0))

    grid_spec = pltpu.PrefetchScalarGridSpec(
        num_scalar_prefetch=2,
        grid=(nb,),
        in_specs=[pl.BlockSpec((FFN_BLOCK, D_MODEL), lambda b, be, nu: (blk(b, be, nu), 0)),
                  pl.BlockSpec((1, D_MODEL, d_ff2), lambda b, be, nu: (be[blk(b, be, nu)], 0, 0)),
                  pl.BlockSpec((1, 1, d_ff2), lambda b, be, nu: (be[blk(b, be, nu)], 0, 0)),
                  pl.BlockSpec((1, d_ff2 // 2, D_MODEL),
                               lambda b, be, nu: (be[blk(b, be, nu)], 0, 0)),
                  pl.BlockSpec((1, 1, D_MODEL), lambda b, be, nu: (be[blk(b, be, nu)], 0, 0))],
        out_specs=pl.BlockSpec((FFN_BLOCK, D_MODEL), lambda b, be, nu: (blk(b, be, nu), 0)),
        scratch_shapes=[pltpu.VMEM((D_MODEL, d_ff2), BF16),
                        pltpu.VMEM((d_ff2 // 2, D_MODEL), BF16)],
    )
    return pl.pallas_call(
        _moe_ffn_kernel,
        grid_spec=grid_spec,
        out_shape=jax.ShapeDtypeStruct((cap, D_MODEL), BF16),
        compiler_params=_cparams(1),
        name="moe_ffn",
    )(meta["block_expert"], meta["n_used"], xb, w_gate_up,
      b_gate_up.reshape(N_EXPERTS, 1, d_ff2), w_down, b_down.reshape(N_EXPERTS, 1, D_MODEL))


def _moe_unsort_kernel(seg_s, goff_s, nch_s, ntot_s,
                       dst_ref, gk_ref, x1_ref, g2p_ref, g2s_ref, yb_hbm,
                       yp_ref, ys_ref, ybuf, sem, *, tm, n_rows, n_prompt_tiles):
    i = pl.program_id(0)
    nt = pl.num_programs(0)
    slot = i % 2

    def chunk_copy(buf, src_row, dst_row):
        return pltpu.make_async_copy(
            yb_hbm.at[pl.ds(pl.multiple_of(src_row, ROW_CHUNK), ROW_CHUNK)],
            ybuf.at[buf, pl.ds(pl.multiple_of(dst_row, ROW_CHUNK), ROW_CHUNK)], sem.at[buf])

    def fetch(tile, buf):
        def per_expert(e, carry):
            so = seg_s[tile * N_EXPERTS + e]
            go = goff_s[tile * N_EXPERTS + e]

            def per_chunk(c, carry2):
                chunk_copy(buf, go + c * ROW_CHUNK, so + c * ROW_CHUNK).start()
                return carry2

            return lax.fori_loop(0, nch_s[tile * N_EXPERTS + e], per_chunk, carry)

        lax.fori_loop(0, N_EXPERTS, per_expert, 0)

    @pl.when(i == 0)
    def _():
        ybuf[...] = jnp.zeros(ybuf.shape, ybuf.dtype)
        fetch(0, 0)

    @pl.when(i + 1 < nt)
    def _():
        fetch(i + 1, 1 - slot)

    def wait_one(c, carry):
        chunk_copy(slot, 0, 0).wait()
        return carry

    lax.fori_loop(0, ntot_s[i], wait_one, 0)

    dst = dst_ref[...]
    gk = gk_ref[...]
    acc = jnp.zeros((tm, D_MODEL), F32)
    for c in range(n_rows // SEL_CHUNK):
        cols = (lax.broadcasted_iota(jnp.int32, (tm, SEL_CHUNK), 1) + c * SEL_CHUNK).astype(F32)
        q = jnp.zeros((tm, SEL_CHUNK), F32)
        for kk in range(TOP_K):
            q = q + jnp.where(cols == dst[:, kk:kk + 1], gk[:, kk:kk + 1], 0.0)
        acc = acc + jnp.dot(q.astype(BF16), ybuf[slot, c * SEL_CHUNK:(c + 1) * SEL_CHUNK, :],
                            preferred_element_type=F32)

    @pl.when(i < n_prompt_tiles)
    def _():
        yp_ref[...] = x1_ref[...] + g2p_ref[...] * acc

    @pl.when(i >= n_prompt_tiles)
    def _():
        ys_ref[...] = x1_ref[...] + g2s_ref[...] * acc


def _moe_unsort(meta, dst_all, gk_all, x1_all, gt2_p, gt2_s, yb, *, tm, n_prompt_tiles):
    t_all = dst_all.shape[0]
    nt = t_all // tm
    n_rows = _sort_rows(tm)
    last_p = n_prompt_tiles - 1
    grid_spec = pltpu.PrefetchScalarGridSpec(
        num_scalar_prefetch=4,
        grid=(nt,),
        in_specs=[pl.BlockSpec((tm, LANES), lambda i, *_: (i, 0)),
                  pl.BlockSpec((tm, LANES), lambda i, *_: (i, 0)),
                  pl.BlockSpec((tm, D_MODEL), lambda i, *_: (i, 0)),
                  pl.BlockSpec((1, D_MODEL), lambda i, *_: (0, 0)),
                  pl.BlockSpec((tm, D_MODEL), lambda i, *_: (0, 0)),
                  pl.BlockSpec(memory_space=pl.ANY)],
        out_specs=[pl.BlockSpec((tm, D_MODEL), lambda i, *_: (jnp.minimum(i, last_p), 0)),
                   pl.BlockSpec((tm, D_MODEL), lambda i, *_: (0, 0))],
        scratch_shapes=[pltpu.VMEM((2, n_rows, D_MODEL), BF16),
                        pltpu.SemaphoreType.DMA((2,))],
    )
    return pl.pallas_call(
        functools.partial(_moe_unsort_kernel, tm=tm, n_rows=n_rows,
                          n_prompt_tiles=n_prompt_tiles),
        grid_spec=grid_spec,
        out_shape=[jax.ShapeDtypeStruct((n_prompt_tiles * tm, D_MODEL), F32),
                   jax.ShapeDtypeStruct((tm, D_MODEL), F32)],
        compiler_params=_cparams(1),
        name="moe_unsort",
    )(meta["seg"], meta["goff"], meta["nch"], meta["ntot"],
      dst_all, gk_all, x1_all, gt2_p, gt2_s, yb)


def _moe_meta(cnt, tm):
    nt = cnt.shape[0]
    cnt = cnt.astype(jnp.int32)
    cnt_pad = (cnt + ROW_CHUNK - 1) // ROW_CHUNK * ROW_CHUNK
    seg = jnp.cumsum(cnt_pad, axis=1) - cnt_pad
    rows_e = jnp.sum(cnt_pad, axis=0)
    region = (rows_e + FFN_BLOCK - 1) // FFN_BLOCK * FFN_BLOCK
    gstart = jnp.cumsum(region) - region
    goff = gstart[None, :] + jnp.cumsum(cnt_pad, axis=0) - cnt_pad
    nblk_e = region // FFN_BLOCK
    blk_end = jnp.cumsum(nblk_e)
    cap = _moe_cap(nt * tm, tm)
    blocks = jnp.arange(cap // FFN_BLOCK, dtype=jnp.int32)
    block_expert = jnp.minimum(
        jnp.sum((blk_end[None, :] <= blocks[:, None]).astype(jnp.int32), axis=1), N_EXPERTS - 1)
    segv = jnp.zeros((nt, 1, LANES), F32).at[:, 0, :N_EXPERTS].set(seg.astype(F32))
    return {
        "seg": seg.reshape(-1), "goff": goff.reshape(-1).astype(jnp.int32),
        "nch": (cnt_pad // ROW_CHUNK).reshape(-1),
        "ntot": jnp.sum(cnt_pad, axis=1) // ROW_CHUNK,
        "tstart": (gstart + rows_e).astype(jnp.int32),
        "tnch": (region - rows_e) // ROW_CHUNK,
        "block_expert": block_expert,
        "n_used": blk_end[-1:].astype(jnp.int32),
        "segv": segv,
    }


def _moe_cap(t_all, tm):
    nt = t_all // tm
    worst = TOP_K * t_all + nt * N_EXPERTS * (ROW_CHUNK - 1) + N_EXPERTS * (FFN_BLOCK - ROW_CHUNK)
    return -(-worst // FFN_BLOCK) * FFN_BLOCK


def _t5_bucket(dist):
    max_exact = NUM_BUCKETS // 2
    d = dist.astype(jnp.int32)
    ratio = (jnp.log(jnp.maximum(d, 1).astype(F32) / max_exact)
             / math.log(MAX_DISTANCE / max_exact))
    large = jnp.minimum(max_exact + (ratio * (NUM_BUCKETS - max_exact)).astype(jnp.int32),
                        NUM_BUCKETS - 1)
    return jnp.where(d < max_exact, d, large)


def _step_bias(tab, dil):
    return tab[_t5_bucket(dil * jnp.arange(ATT_BLK + 1))].astype(F32).T


def _band_table(sb):
    return jnp.concatenate([sb[:, ::-1], jnp.full((HEADS, ATT_BLK - 1), NEG_INF, F32)], axis=1)


def _cache_table(sb, dil):
    on_grid = sb[:, :0:-1]
    if dil == 1:
        return on_grid
    off = jnp.full((HEADS, ATT_BLK, dil - 1), NEG_INF, F32)
    return jnp.concatenate([on_grid[:, :, None], off], axis=2).reshape(HEADS, ATT_BLK * dil)


def kernel(x_prompt, x_sample, cache_kv_w128, cache_kv_w512, cache_kv_w2048, state_pool, c_prompt,
           c_sample, w_ada, b_ada, norm_mix_g, norm_ffn_g, w_in, q_norm_g, k_norm_g, rel_bias,
           w_pool_mix, pool_scale, w_up_attn, w_up_pool, w_out, w_router, b_router, w_gate_up,
           b_gate_up, w_down, b_down):
    assert w_ada.shape[0] == 1, "one layer"
    seq = x_prompt.shape[1]
    n_s = x_sample.shape[0]
    assert x_prompt.shape[0] == 1 and x_sample.shape[1] == 1
    assert seq % (DIL_GROUPS[-1][1] * ATT_BLK) == 0 and seq % TM_PROMPT == 0
    assert n_s == TM_SAMPLE
    dils = tuple(d for _, d in DIL_GROUPS)
    caches = (cache_kv_w128, cache_kv_w512, cache_kv_w2048)

    w_in_bf = w_in[0].astype(BF16)
    heads_of = jnp.arange(GROUP_W) // HEAD_DIM
    bdiag = (heads_of[:, None] == heads_of[None, :]).astype(BF16)
    qg = (jnp.tile(q_norm_g[0], HEADS) * SCALE).reshape(1, GROUP_W)
    kg = jnp.tile(k_norm_g[0], HEADS).reshape(1, GROUP_W)
    expand = (jnp.arange(LANES)[:, None] == heads_of[None, :]).astype(BF16)
    wr = jnp.zeros((D_MODEL, LANES), F32).at[:, :N_EXPERTS].set(w_router[0])
    wr_hi = wr.astype(BF16)
    wr_lo = (wr - wr_hi.astype(F32)).astype(BF16)
    br = jnp.full((1, LANES), NEG_INF, F32).at[0, :N_EXPERTS].set(b_router[0])
    wts = (w_pool_mix[0].astype(BF16), pool_scale[0].reshape(1, POOL_W),
           w_up_attn[0].astype(BF16), w_up_pool[0].astype(BF16), w_out[0].astype(BF16), expand,
           norm_ffn_g[0].reshape(1, D_MODEL), wr_hi, wr_lo, br)
    g1 = norm_mix_g[0].reshape(1, D_MODEL)

    n_c = 1 + n_s
    c_all = jnp.zeros((-(-n_c // 8) * 8, D_MODEL), F32).at[0:1].set(c_prompt).at[1:n_c].set(c_sample)
    mod = _ada(c_all, w_ada[0], b_ada[0])
    sh1, sc1, gt1, sh2, sc2, gt2 = jnp.split(mod, N_ADA, axis=-1)

    def prow(m):
        return m[0:1]

    def srows(m):
        return m[1:n_c]

    xp = x_prompt[0]
    q_p, k_p, v_p, u_p, sga_p, sgp_p, st_p = _proj(
        xp, g1, prow(sc1), prow(sh1), w_in_bf, bdiag, qg, kg,
        tm=TM_PROMPT, dils=dils, per_row=False)
    step_bias = [_step_bias(rel_bias[:, g * HEADS:(g + 1) * HEADS], d)
                 for g, (_, d) in enumerate(DIL_GROUPS)]
    o_p, lse_p = [], []
    for g, (_, d) in enumerate(DIL_GROUPS):
        o, lse = _attn_prompt(q_p[g], k_p[g], v_p[g], _band_table(step_bias[g]), d)
        o_p.append(o)
        lse_p.append(lse)

    xs = x_sample[:, 0]
    ones = (1, 1, 1)
    q_s, _, _, u_s, sga_s, sgp_s, st_s = _proj(
        xs, g1, srows(sc1), srows(sh1), w_in_bf, bdiag, qg, kg,
        tm=TM_SAMPLE, dils=ones, per_row=True)
    def heads(parts):
        return jnp.stack([p.astype(F32).reshape(n_s, HEADS, HEAD_DIM) for p in parts], axis=1)

    bself = jnp.stack([sb[:, 0:1] for sb in step_bias])
    btabs = [_cache_table(sb, d) for sb, (_, d) in zip(step_bias, DIL_GROUPS)]
    caches_t = [jnp.transpose(c, (0, 1, 3, 4, 5, 2)) for c in caches]
    o3, lse3 = _attn_sample(heads(q_s), heads([st[:, :GROUP_W] for st in st_s]),
                            heads([st[:, GROUP_W:] for st in st_s]), bself, btabs, caches_t)
    o_s = [o3[:, g].reshape(n_s, GROUP_W) for g in range(N_GROUPS)]
    lse_s = [jnp.zeros((n_s, LANES), F32).at[:, :HEADS].set(lse3[:, g, :, 0])
             for g in range(N_GROUPS)]
    pooled_s, pool_state_t = _pool_sample(jnp.transpose(state_pool, (0, 2, 1, 3)), u_s)
    pool_state_s = jnp.transpose(pool_state_t, (0, 2, 1, 3))

    nt_p = seq // TM_PROMPT
    t_all = seq + TM_PROMPT
    bufs = _post(xp, o_p, lse_p, (u_p,), sga_p, sgp_p, wts, (prow(gt1), prow(sc2), prow(sh2)),
                 tm=TM_PROMPT, dils=dils, per_row=False, rows_total=t_all, row_block0=0,
                 cnt_tiles=nt_p + 1, cnt_block=None, grid=nt_p, n_valid_steps=None,
                 alias_bufs=None)
    bufs = _post(xs, o_s, lse_s, pooled_s, sga_s, sgp_s, wts,
                 (srows(gt1), srows(sc2), srows(sh2)),
                 tm=TM_SAMPLE, dils=ones, per_row=True, rows_total=t_all,
                 row_block0=seq // TM_SAMPLE, cnt_tiles=nt_p + 1, cnt_block=nt_p,
                 grid=TM_PROMPT // TM_SAMPLE, n_valid_steps=1, alias_bufs=bufs)
    x1_all, h2_all, a_all, idx_all, gk_all, cnt = bufs

    meta = _moe_meta(cnt[:, 0, :N_EXPERTS], TM_PROMPT)
    cap = _moe_cap(t_all, TM_PROMPT)
    xb, dst_all = _moe_sort(meta, a_all, idx_all, h2_all, tm=TM_PROMPT, cap=cap)
    yb = _moe_ffn(meta, xb, w_gate_up[0], b_gate_up[0], w_down[0], b_down[0])
    gt2_s = jnp.zeros((TM_PROMPT, D_MODEL), F32).at[:n_s].set(srows(gt2))
    y_p, y_s = _moe_unsort(meta, dst_all, gk_all, x1_all, prow(gt2), gt2_s, yb,
                           tm=TM_PROMPT, n_prompt_tiles=nt_p)

    def kv_state(st, rows):
        return st.reshape(1, 1, rows, 2, HEADS, HEAD_DIM)

    kv_p = [kv_state(st, st.shape[0]) for st in st_p]
    kv_s = [st.reshape(1, n_s, 1, 2, HEADS, HEAD_DIM) for st in st_s]
    pool_p = u_p[seq - POOL_BUF:].reshape(1, 1, POOL_BUF, POOL_W)
    return (y_p.reshape(1, seq, D_MODEL), y_s[:n_s].reshape(n_s, 1, D_MODEL),
            kv_p[0], kv_p[1], kv_p[2], pool_p, kv_s[0], kv_s[1], kv_s[2], pool_state_s)
```

```python
import functools
import math

import jax
import jax.numpy as jnp
from jax import lax
from jax.experimental import pallas as pl
from jax.experimental.pallas import tpu as pltpu

F32 = jnp.float32
BF16 = jnp.bfloat16

D_MODEL = 1024
HEAD_DIM = 64
HEADS = 8
GROUP_W = HEADS * HEAD_DIM
DIL_GROUPS = ((128, 1), (512, 4), (2048, 16))
N_GROUPS = len(DIL_GROUPS)
QKV_W = N_GROUPS * GROUP_W
ATT_BLK = 128
ATT_SUB = 2
POOL_WINDOWS = (2, 4, 8, 16)
POOL_W = 512
POOL_GW = 128
POOL_BUF = 15
OFF_K, OFF_V = QKV_W, 2 * QKV_W
OFF_U = 3 * QKV_W
OFF_GA = OFF_U + POOL_W
OFF_GP = OFF_GA + D_MODEL
IN_W = OFF_GP + D_MODEL
NUM_BUCKETS = 32
MAX_DISTANCE = 2048
N_EXPERTS = 32
TOP_K = 4
SWIGLU_LIMIT = 7.0
SWIGLU_ALPHA = 1.702
N_ADA = 6
EPS = 1e-6
NEG_INF = -1e30
PAST_LEN = 8192
SCALE = HEAD_DIM ** -0.5

LANES = 128
ROW_CHUNK = 16
TM_PROMPT = 512
TM_SAMPLE = 128
FFN_BLOCK = 512
SEL_CHUNK = 512
VMEM_LIMIT = 56 * 1024 * 1024


def _cparams(n_axes):
    return pltpu.CompilerParams(dimension_semantics=("arbitrary",) * n_axes,
                                vmem_limit_bytes=VMEM_LIMIT)


def _const_spec(shape):
    nd = len(shape)
    return pl.BlockSpec(shape, lambda *_: (0,) * nd)


def _ada_kernel(c_ref, w_ref, b_ref, o_ref):
    c = c_ref[...]
    s = c * jax.nn.sigmoid(c)
    o_ref[...] = jnp.dot(s.astype(BF16), w_ref[...].astype(BF16),
                         preferred_element_type=F32) + b_ref[...]


def _ada(c_all, w_ada, b_ada):
    rows = c_all.shape[0]
    n = w_ada.shape[1]
    tn = 1536
    return pl.pallas_call(
        _ada_kernel,
        grid=(n // tn,),
        in_specs=[pl.BlockSpec((rows, D_MODEL), lambda j: (0, 0)),
                  pl.BlockSpec((D_MODEL, tn), lambda j: (0, j)),
                  pl.BlockSpec((1, tn), lambda j: (0, j))],
        out_specs=pl.BlockSpec((rows, tn), lambda j: (0, j)),
        out_shape=jax.ShapeDtypeStruct((rows, n), F32),
        compiler_params=_cparams(1),
        name="ada",
    )(c_all, w_ada, b_ada.reshape(1, n))


def _proj_kernel(x_ref, g_ref, sc_ref, sh_ref, w_ref, bd_ref, qg_ref, kg_ref,
                 *refs, tm, dils, st_rows):
    q_refs, k_refs, v_refs = refs[0:3], refs[3:6], refs[6:9]
    u_ref, sga_ref, sgp_ref = refs[9:12]
    st_refs = refs[12:15]
    scr = refs[15]

    x = x_ref[...]
    ms = jnp.mean(x * x, axis=-1, keepdims=True)
    h = x * lax.rsqrt(ms + EPS) * g_ref[...] * (1.0 + sc_ref[...]) + sh_ref[...]
    hb = h.astype(BF16)

    def proj(off, width):
        return jnp.dot(hb, w_ref[:, off:off + width], preferred_element_type=F32)

    def head_norm(z, gain_ref):
        zz = (z * z).astype(BF16)
        half = GROUP_W // 2
        ss = jnp.concatenate(
            [jnp.dot(zz[:, :half], bd_ref[...], preferred_element_type=F32),
             jnp.dot(zz[:, half:], bd_ref[...], preferred_element_type=F32)], axis=1)
        return z * lax.rsqrt(ss * (1.0 / HEAD_DIM) + EPS) * gain_ref[...]

    def put(out_ref, val, d):
        if d == 1:
            out_ref[...] = val.astype(out_ref.dtype)
        else:
            for c in range(GROUP_W // LANES):
                scr[c] = val[:, c * LANES:(c + 1) * LANES]
            for r in range(d):
                for c in range(GROUP_W // LANES):
                    col = r * GROUP_W + c * LANES
                    out_ref[:, col:col + LANES] = (
                        scr[c, pl.ds(r, tm // d, stride=d), :].astype(out_ref.dtype))

    for g, d in enumerate(dils):
        qn = head_norm(proj(g * GROUP_W, GROUP_W), qg_ref)
        put(q_refs[g], qn, d)
        kn = head_norm(proj(OFF_K + g * GROUP_W, GROUP_W), kg_ref)
        put(k_refs[g], kn, d)
        v = proj(OFF_V + g * GROUP_W, GROUP_W)
        put(v_refs[g], v, d)
        rb = st_rows[g]
        st_refs[g][:, 0:GROUP_W] = kn[tm - rb:, :]
        st_refs[g][:, GROUP_W:2 * GROUP_W] = v[tm - rb:, :]

    u_ref[...] = proj(OFF_U, POOL_W)
    sga_ref[...] = jax.nn.sigmoid(proj(OFF_GA, D_MODEL)).astype(BF16)
    sgp_ref[...] = jax.nn.sigmoid(proj(OFF_GP, D_MODEL)).astype(BF16)


def _mod_spec(per_row, tm):
    if per_row:
        return pl.BlockSpec((tm, D_MODEL), lambda i: (i, 0))
    return pl.BlockSpec((1, D_MODEL), lambda i: (0, 0))


def _proj(x, g1, sc1, sh1, w_in_bf, bdiag, qg, kg, *, tm, dils, per_row):
    s = x.shape[0]
    nt = s // tm
    wins = tuple(min(w, s) for w, _ in DIL_GROUPS)
    st_rows = tuple(min(tm, w) for w in wins)

    def res_spec(d):
        return pl.BlockSpec((tm // d, d * GROUP_W), lambda i: (i, 0))

    def st_spec(w, rb):
        first = nt - w // rb
        return pl.BlockSpec((rb, 2 * GROUP_W), lambda i: (jnp.maximum(i - first, 0), 0))

    qkv_shapes = [jax.ShapeDtypeStruct((s // d, d * GROUP_W), BF16) for d in dils]
    out_shape = (qkv_shapes * 3
                 + [jax.ShapeDtypeStruct((s, POOL_W), F32),
                    jax.ShapeDtypeStruct((s, D_MODEL), BF16),
                    jax.ShapeDtypeStruct((s, D_MODEL), BF16)]
                 + [jax.ShapeDtypeStruct((w, 2 * GROUP_W), F32) for w in wins])
    out_specs = ([res_spec(d) for d in dils] * 3
                 + [pl.BlockSpec((tm, POOL_W), lambda i: (i, 0)),
                    pl.BlockSpec((tm, D_MODEL), lambda i: (i, 0)),
                    pl.BlockSpec((tm, D_MODEL), lambda i: (i, 0))]
                 + [st_spec(w, rb) for w, rb in zip(wins, st_rows)])
    in_specs = [pl.BlockSpec((tm, D_MODEL), lambda i: (i, 0)),
                _const_spec((1, D_MODEL)),
                _mod_spec(per_row, tm), _mod_spec(per_row, tm),
                pl.BlockSpec((D_MODEL, IN_W), lambda i: (0, 0), pipeline_mode=pl.Buffered(1)),
                _const_spec((GROUP_W // 2, GROUP_W // 2)),
                _const_spec((1, GROUP_W)), _const_spec((1, GROUP_W))]
    outs = pl.pallas_call(
        functools.partial(_proj_kernel, tm=tm, dils=dils, st_rows=st_rows),
        grid=(nt,),
        in_specs=in_specs,
        out_specs=out_specs,
        out_shape=out_shape,
        scratch_shapes=[pltpu.VMEM((GROUP_W // LANES, tm, LANES), F32)],
        compiler_params=_cparams(1),
        name="proj",
    )(x, g1, sc1, sh1, w_in_bf, bdiag, qg, kg)
    return outs[0:3], outs[3:6], outs[6:9], outs[9], outs[10], outs[11], outs[12:15]


def _attn_kernel(q_ref, kp_ref, kc_ref, vp_ref, vc_ref, r_ref, o_ref, lse_ref, bias_ref):
    i = pl.program_id(1)

    @pl.when((pl.program_id(0) == 0) & (i == 0))
    def _():
        for h in range(HEADS):
            row = jnp.broadcast_to(r_ref[h:h + 1, :], (ATT_BLK, 2 * ATT_BLK))
            bias_ref[h] = pltpu.roll(row, 0, 1, stride=1, stride_axis=0)

    q = q_ref[...]
    k = jnp.concatenate([kp_ref[...], kc_ref[...]], axis=0)
    v = jnp.concatenate([vp_ref[...], vc_ref[...]], axis=0)
    col = lax.broadcasted_iota(jnp.int32, (ATT_BLK, 2 * ATT_BLK), 1)
    no_prev = jnp.where((col < ATT_BLK) & (i == 0), NEG_INF, 0.0)
    lane_q = lax.broadcasted_iota(jnp.int32, (ATT_SUB * ATT_BLK, LANES), 1)
    lane_v = lax.broadcasted_iota(jnp.int32, ((ATT_SUB + 1) * ATT_BLK, LANES), 1)

    def pair(h):
        return slice((h // 2) * LANES, (h // 2 + 1) * LANES)

    def mine(lane, h):
        return (lane < HEAD_DIM) == (h % 2 == 0)

    ss = []
    for h in range(HEADS):
        q2 = q[:, pair(h)]
        qm = jnp.where(mine(lane_q, h), q2, jnp.zeros_like(q2))
        k2 = k[:, pair(h)]
        for j in range(ATT_SUB):
            s = lax.dot_general(qm[j * ATT_BLK:(j + 1) * ATT_BLK], k2[j * ATT_BLK:(j + 2) * ATT_BLK],
                                (((1,), (1,)), ((), ())), preferred_element_type=F32)
            s = s + bias_ref[h]
            ss.append(s + no_prev if j == 0 else s)
    s = jnp.concatenate(ss, axis=0)
    m = jnp.max(s, axis=-1, keepdims=True)
    p = jnp.exp(s - m)
    l = jnp.sum(p, axis=-1, keepdims=True)
    pb = p.astype(BF16)
    lse = m + jnp.log(l)
    inv_l = 1.0 / l
    outs, lses = [], []
    for h in range(HEADS):
        v2 = v[:, pair(h)]
        vm = jnp.where(mine(lane_v, h), v2, jnp.zeros_like(v2))
        o_sub, lse_sub = [], []
        for j in range(ATT_SUB):
            rows = slice((h * ATT_SUB + j) * ATT_BLK, (h * ATT_SUB + j + 1) * ATT_BLK)
            o_sub.append(jnp.dot(pb[rows], vm[j * ATT_BLK:(j + 2) * ATT_BLK],
                                 preferred_element_type=F32) * inv_l[rows])
            lse_sub.append(lse[rows])
        o = jnp.concatenate(o_sub, axis=0)
        if h % 2 == 0:
            outs.append(o)
        else:
            outs[-1] = outs[-1] + o
        lses.append(jnp.concatenate(lse_sub, axis=0))
    o_ref[...] = jnp.concatenate(outs, axis=-1).astype(o_ref.dtype)
    lse_ref[...] = jnp.concatenate(
        lses + [jnp.zeros((ATT_SUB * ATT_BLK, LANES - HEADS), F32)], axis=-1)


def _attn_prompt(q, k, v, r_tab, d):
    rows = q.shape[0]
    step = ATT_SUB * ATT_BLK
    nblk = rows // step
    cur = pl.BlockSpec((step, GROUP_W), lambda r, i: (i, r))
    prev = pl.BlockSpec((ATT_BLK, GROUP_W), lambda r, i: (jnp.maximum(i * ATT_SUB - 1, 0), r))
    return pl.pallas_call(
        _attn_kernel,
        grid=(d, nblk),
        in_specs=[cur, prev, cur, prev, cur,
                  pl.BlockSpec((HEADS, 2 * ATT_BLK), lambda r, i: (0, 0))],
        out_specs=[pl.BlockSpec((step, GROUP_W), lambda r, i: (i, r)),
                   pl.BlockSpec((step, LANES), lambda r, i: (i, r))],
        out_shape=[jax.ShapeDtypeStruct((rows, d * GROUP_W), BF16),
                   jax.ShapeDtypeStruct((rows, d * LANES), F32)],
        scratch_shapes=[pltpu.VMEM((HEADS, ATT_BLK, 2 * ATT_BLK), F32)],
        compiler_params=_cparams(2),
        name=f"attn_d{d}",
    )(q, k, k, v, v, r_tab)


def _attn_sample_kernel(q_ref, kn_ref, vn_ref, bself_ref, b0_ref, b1_ref, b2_ref,
                        c0_ref, c1_ref, c2_ref, o_ref, lse_ref):
    row_e = lax.broadcasted_iota(jnp.int32, (HEADS, HEAD_DIM), 0)
    for g, (c_ref, b_ref) in enumerate(((c0_ref, b0_ref), (c1_ref, b1_ref), (c2_ref, b2_ref))):
        win = b_ref.shape[1]
        row_w = lax.broadcasted_iota(jnp.int32, (HEADS, win), 0)
        q = q_ref[0, g]
        qb = q.astype(BF16)
        s = jnp.zeros((HEADS, win), F32)
        for h in range(HEADS):
            sh = jnp.dot(qb, c_ref[0, 0, 0, h].astype(BF16), preferred_element_type=F32)
            s = jnp.where(row_w == h, sh, s)
        s = s + b_ref[...]
        s0 = jnp.sum(q * kn_ref[0, g], axis=-1, keepdims=True) + bself_ref[g]
        m = jnp.maximum(jnp.max(s, axis=-1, keepdims=True), s0)
        p = jnp.exp(s - m)
        p0 = jnp.exp(s0 - m)
        l = jnp.sum(p, axis=-1, keepdims=True) + p0
        pb = p.astype(BF16)
        o = jnp.zeros((HEADS, HEAD_DIM), F32)
        for h in range(HEADS):
            oh = lax.dot_general(pb, c_ref[0, 0, 1, h].astype(BF16), (((1,), (1,)), ((), ())),
                                 preferred_element_type=F32)
            o = jnp.where(row_e == h, oh, o)
        o_ref[0, g] = (o + p0 * vn_ref[0, g]) / l
        lse_ref[0, g] = m + jnp.log(l)


def _attn_sample(q3, kn3, vn3, bself, btabs, caches_t):
    n = q3.shape[0]
    tok = pl.BlockSpec((1, N_GROUPS, HEADS, HEAD_DIM), lambda t: (t, 0, 0, 0))

    def cache_spec(c):
        return pl.BlockSpec((1, 1) + c.shape[2:], lambda t: (0, t, 0, 0, 0, 0))

    return pl.pallas_call(
        _attn_sample_kernel,
        grid=(n,),
        in_specs=[tok, tok, tok, _const_spec(bself.shape)]
                 + [_const_spec(b.shape) for b in btabs]
                 + [cache_spec(c) for c in caches_t],
        out_specs=[tok, pl.BlockSpec((1, N_GROUPS, HEADS, 1), lambda t: (t, 0, 0, 0))],
        out_shape=[jax.ShapeDtypeStruct((n, N_GROUPS, HEADS, HEAD_DIM), F32),
                   jax.ShapeDtypeStruct((n, N_GROUPS, HEADS, 1), F32)],
        compiler_params=_cparams(1),
        name="attn_sample",
    )(q3, kn3, vn3, bself, *btabs, *caches_t)


def _pool_sample_kernel(st_ref, u_ref, pooled_ref, new_ref):
    u = u_ref[...]
    rows = [st_ref[0, j] for j in range(POOL_BUF)]
    outs = []
    for g, w in enumerate(POOL_WINDOWS):
        sl = slice(g * POOL_GW, (g + 1) * POOL_GW)
        acc = u[:, sl]
        for j in range(POOL_BUF - (w - 1), POOL_BUF):
            acc = acc + rows[j][:, sl]
        outs.append(acc / float(w) - u[:, sl])
    pooled_ref[...] = jnp.concatenate(outs, axis=-1)
    for j in range(POOL_BUF - 1):
        new_ref[0, j] = rows[j + 1]
    new_ref[0, POOL_BUF - 1] = u


def _pool_sample(state, u):
    n = u.shape[0]
    return pl.pallas_call(
        _pool_sample_kernel,
        grid=(1,),
        in_specs=[_const_spec(state.shape), _const_spec(u.shape)],
        out_specs=[_const_spec(u.shape), _const_spec(state.shape)],
        out_shape=[jax.ShapeDtypeStruct((n, POOL_W), F32),
                   jax.ShapeDtypeStruct(state.shape, F32)],
        compiler_params=_cparams(1),
        name="pool_sample",
    )(state, u)


def _post_kernel(*refs, tm, dils, pooled_given, n_valid_steps, aliased):
    it = iter(refs)
    x_ref = next(it)
    o_refs = [next(it) for _ in range(N_GROUPS)]
    lse_refs = [next(it) for _ in range(N_GROUPS)]
    if pooled_given:
        pooled_ref = next(it)
    else:
        u_ref, uh_ref = next(it), next(it)
    sga_ref, sgp_ref = next(it), next(it)
    wpm_ref, psc_ref, wua_ref, wup_ref, wout_ref, exp_ref = (next(it) for _ in range(6))
    gt1_ref, g2_ref, sc2_ref, sh2_ref = (next(it) for _ in range(4))
    wrh_ref, wrl_ref, br_ref = (next(it) for _ in range(3))
    if aliased:
        for _ in range(6):
            next(it)
    x1_ref, h2_ref, a_ref, idx_ref, gk_ref, cnt_ref = (next(it) for _ in range(6))
    ob_scr, ls_scr = next(it), next(it)

    i = pl.program_id(0)

    def compute():
        obs, lss = [], []
        for g, d in enumerate(dils):
            if d == 1:
                obs.append(o_refs[g][...].astype(F32))
                lss.append(lse_refs[g][...])
            else:
                for r in range(d):
                    for c in range(GROUP_W // LANES):
                        col = r * GROUP_W + c * LANES
                        ob_scr[c, pl.ds(r, tm // d, stride=d), :] = (
                            o_refs[g][:, col:col + LANES].astype(F32))
                    ls_scr[pl.ds(r, tm // d, stride=d), :] = (
                        lse_refs[g][:, r * LANES:(r + 1) * LANES])
                obs.append(jnp.concatenate([ob_scr[c] for c in range(GROUP_W // LANES)],
                                           axis=-1))
                lss.append(ls_scr[...])
        mx = jnp.maximum(jnp.maximum(lss[0], lss[1]), lss[2])
        es = [jnp.exp(l - mx) for l in lss]
        den = es[0] + es[1] + es[2]
        attn_o = jnp.zeros((tm, GROUP_W), F32)
        head_lane = lax.broadcasted_iota(jnp.int32, (tm, LANES), 1) < HEADS
        for g in range(N_GROUPS):
            w = jnp.where(head_lane, es[g] / den, 0.0)
            w_hi = w.astype(BF16).astype(F32)
            w_lo = (w - w_hi).astype(BF16).astype(F32)
            lhs = (w_hi + pltpu.roll(w_lo, HEADS, 1)).astype(BF16)
            wexp = jnp.dot(lhs, exp_ref[...], preferred_element_type=F32)
            attn_o = attn_o + wexp * obs[g]

        if pooled_given:
            pooled = pooled_ref[...]
        else:
            u = u_ref[...]
            halo = jnp.where(i == 0, 0.0, uh_ref[...])
            pos = (lax.broadcasted_iota(jnp.int32, (tm, 1), 0) + i * tm + 1).astype(F32)
            outs = []
            for g, w in enumerate(POOL_WINDOWS):
                sl = slice(g * POOL_GW, (g + 1) * POOL_GW)
                a = jnp.concatenate([halo[:, sl], u[:, sl]], axis=0)
                span = 1
                while span < w:
                    n = a.shape[0] - span
                    a = a[span:, :] + a[:n, :]
                    span *= 2
                off = a.shape[0] - tm
                win_sum = a[off:, :]
                outs.append(win_sum / jnp.minimum(pos, float(w)) - u[:, sl])
            pooled = jnp.concatenate(outs, axis=-1)
        pool_parts = []
        for g in range(len(POOL_WINDOWS)):
            sl = slice(g * POOL_GW, (g + 1) * POOL_GW)
            pool_parts.append(jnp.dot(pooled[:, sl].astype(BF16), wpm_ref[g],
                                      preferred_element_type=F32))
        pool_o = jnp.concatenate(pool_parts, axis=-1) * psc_ref[...]

        up_a = jnp.dot(attn_o.astype(BF16), wua_ref[...], preferred_element_type=F32)
        up_p = jnp.dot(pool_o.astype(BF16), wup_ref[...], preferred_element_type=F32)
        merged = sga_ref[...].astype(F32) * up_a + sgp_ref[...].astype(F32) * up_p
        mo = jnp.dot(merged.astype(BF16), wout_ref[...], preferred_element_type=F32)
        x1 = x_ref[...] + gt1_ref[...] * mo
        x1_ref[...] = x1

        ms = jnp.mean(x1 * x1, axis=-1, keepdims=True)
        h2 = x1 * lax.rsqrt(ms + EPS) * g2_ref[...] * (1.0 + sc2_ref[...]) + sh2_ref[...]
        h2_hi = h2.astype(BF16)
        h2_ref[...] = h2_hi
        h2_lo = (h2 - h2_hi.astype(F32)).astype(BF16)
        logits = (jnp.dot(h2_hi, wrh_ref[...], preferred_element_type=F32)
                  + jnp.dot(h2_lo, wrh_ref[...], preferred_element_type=F32)
                  + jnp.dot(h2_hi, wrl_ref[...], preferred_element_type=F32)
                  + br_ref[...])
        lane = lax.broadcasted_iota(jnp.int32, (tm, LANES), 1).astype(F32)
        work = logits
        vals, ids = [], []
        for _ in range(TOP_K):
            m = jnp.max(work, axis=-1, keepdims=True)
            ik = jnp.min(jnp.where(work == m, lane, float(LANES)), axis=-1, keepdims=True)
            vals.append(m)
            ids.append(ik)
            work = jnp.where(lane == ik, -3e38, work)
        ex = [jnp.exp(v - vals[0]) for v in vals]
        den_k = ex[0] + ex[1] + ex[2] + ex[3]
        a = jnp.zeros((tm, LANES), F32)
        idx = jnp.zeros((tm, LANES), F32)
        gk = jnp.zeros((tm, LANES), F32)
        for kk in range(TOP_K):
            gate = ex[kk] / den_k
            a = a + jnp.where(lane == ids[kk], gate, 0.0)
            idx = jnp.where(lane == float(kk), ids[kk], idx)
            gk = jnp.where(lane == float(kk), gate, gk)
        a_ref[...] = a
        idx_ref[...] = idx
        gk_ref[...] = gk
        cnt = jnp.sum((a > 0.0).astype(F32), axis=0, keepdims=True)
        row = lax.broadcasted_iota(jnp.int32, (8, LANES), 0)
        cnt_ref[0] = jnp.where(row == 0, jnp.broadcast_to(cnt, (8, LANES)), 0.0)

    if n_valid_steps is None:
        compute()
    else:
        pl.when(i < n_valid_steps)(compute)

        @pl.when(i >= n_valid_steps)
        def _():
            x1_ref[...] = jnp.zeros(x1_ref.shape, x1_ref.dtype)
            h2_ref[...] = jnp.zeros(h2_ref.shape, h2_ref.dtype)
            a_ref[...] = jnp.zeros(a_ref.shape, a_ref.dtype)
            idx_ref[...] = jnp.zeros(idx_ref.shape, idx_ref.dtype)
            gk_ref[...] = jnp.zeros(gk_ref.shape, gk_ref.dtype)


def _post(x, o_list, lse_list, pool_in, sga, sgp, wts, mods, *, tm, dils, per_row,
          rows_total, row_block0, cnt_tiles, cnt_block, grid, n_valid_steps, alias_bufs):
    pooled_given = not isinstance(pool_in, tuple)
    nv = grid if n_valid_steps is None else n_valid_steps

    def clamp(i):
        return jnp.minimum(i, nv - 1)

    def tile_spec(width):
        return pl.BlockSpec((tm, width), lambda i: (clamp(i), 0))

    in_specs = [tile_spec(D_MODEL)]
    in_specs += [pl.BlockSpec((tm // d, d * GROUP_W), lambda i: (clamp(i), 0)) for d in dils]
    in_specs += [pl.BlockSpec((tm // d, d * LANES), lambda i: (clamp(i), 0)) for d in dils]
    args = [x, *o_list, *lse_list]
    if pooled_given:
        in_specs.append(tile_spec(POOL_W))
        args.append(pool_in)
    else:
        u = pool_in[0]
        in_specs += [tile_spec(POOL_W),
                     pl.BlockSpec((16, POOL_W),
                                  lambda i: (jnp.maximum(i * (tm // 16) - 1, 0), 0))]
        args += [u, u]
    in_specs += [tile_spec(D_MODEL), tile_spec(D_MODEL)]
    args += [sga, sgp]
    wpm, psc, wua, wup, wout, expand, g2, wrh, wrl, br = wts
    gt1, sc2, sh2 = mods

    def mspec():
        if per_row:
            return pl.BlockSpec((tm, D_MODEL), lambda i: (clamp(i), 0))
        return _const_spec((1, D_MODEL))

    in_specs += [_const_spec(wpm.shape), _const_spec(psc.shape), _const_spec(wua.shape),
                 _const_spec(wup.shape), _const_spec(wout.shape), _const_spec(expand.shape),
                 mspec(), _const_spec(g2.shape), mspec(), mspec(),
                 _const_spec(wrh.shape), _const_spec(wrl.shape), _const_spec(br.shape)]
    args += [wpm, psc, wua, wup, wout, expand, gt1, g2, sc2, sh2, wrh, wrl, br]
    aliases = {}
    if alias_bufs is not None:
        base = len(args)
        in_specs += [pl.BlockSpec(memory_space=pl.ANY)] * 6
        args += list(alias_bufs)
        aliases = {base + j: j for j in range(6)}

    def out_spec(width):
        return pl.BlockSpec((tm, width), lambda i: (row_block0 + i, 0))

    out_specs = [out_spec(D_MODEL), out_spec(D_MODEL), out_spec(LANES), out_spec(LANES),
                 out_spec(LANES),
                 pl.BlockSpec((1, 8, LANES),
                              lambda i: (cnt_block if cnt_block is not None else i, 0, 0))]
    out_shape = [jax.ShapeDtypeStruct((rows_total, D_MODEL), F32),
                 jax.ShapeDtypeStruct((rows_total, D_MODEL), BF16),
                 jax.ShapeDtypeStruct((rows_total, LANES), F32),
                 jax.ShapeDtypeStruct((rows_total, LANES), F32),
                 jax.ShapeDtypeStruct((rows_total, LANES), F32),
                 jax.ShapeDtypeStruct((cnt_tiles, 8, LANES), F32)]
    return pl.pallas_call(
        functools.partial(_post_kernel, tm=tm, dils=dils, pooled_given=pooled_given,
                          n_valid_steps=n_valid_steps, aliased=alias_bufs is not None),
        grid=(grid,),
        in_specs=in_specs,
        out_specs=out_specs,
        out_shape=out_shape,
        scratch_shapes=[pltpu.VMEM((GROUP_W // LANES, tm, LANES), F32),
                        pltpu.VMEM((tm, LANES), F32)],
        input_output_aliases=aliases,
        compiler_params=_cparams(1),
        name="post_sample" if per_row else "post",
    )(*args)


def _sort_rows(tm):
    return -(-(TOP_K * tm + N_EXPERTS * (ROW_CHUNK - 1)) // SEL_CHUNK) * SEL_CHUNK


def _for_row_pieces(n_chunks, max_pow, fn):
    big = 1 << max_pow

    def body(c, carry):
        fn(c * big, big)
        return carry

    lax.fori_loop(0, n_chunks >> max_pow, body, 0)
    for pw in range(max_pow - 1, -1, -1):
        @pl.when(((n_chunks >> pw) & 1) == 1)
        def _(pw=pw):
            fn((n_chunks >> (pw + 1)) << (pw + 1), 1 << pw)


SEG_MAX_POW = 3
TILE_MAX_POW = 5


def _moe_sort_kernel(seg_s, goff_s, nch_s, ntot_s, tstart_s, tnch_s,
                     a_ref, idx_ref, h2_ref, segv_ref, lt_ref,
                     xb_hbm, dst_ref, xs_scr, zero_scr, sem, *, tm, n_rows):
    i = pl.program_id(0)
    nt = pl.num_programs(0)
    slot = i % 2
    sel = a_ref[...] > 0.0
    ahead = jnp.dot(lt_ref[...], sel.astype(BF16), preferred_element_type=F32)
    slot1 = jnp.where(sel, segv_ref[0] + ahead + 1.0, 0.0)
    lane = lax.broadcasted_iota(jnp.int32, (tm, LANES), 1).astype(F32)
    idx = idx_ref[...]
    dst = jnp.full((tm, LANES), -1.0, F32)
    for kk in range(TOP_K):
        hit = lane == idx[:, kk:kk + 1]
        dk = jnp.sum(jnp.where(hit, slot1, 0.0), axis=-1, keepdims=True) - 1.0
        dst = jnp.where(lane == float(kk), dk, dst)
    dst_ref[...] = dst
    dst_t = dst.T
    h2 = h2_ref[...]
    for c in range(n_rows // SEL_CHUNK):
        rows = (lax.broadcasted_iota(jnp.int32, (SEL_CHUNK, tm), 0) + c * SEL_CHUNK).astype(F32)
        p = rows == dst_t[0:1, :]
        for kk in range(1, TOP_K):
            p = p | (rows == dst_t[kk:kk + 1, :])
        xs = jnp.dot(jnp.where(p, 1.0, 0.0).astype(BF16), h2, preferred_element_type=F32)
        xs_scr[slot, c * SEL_CHUNK:(c + 1) * SEL_CHUNK, :] = xs.astype(BF16)

    def rows_copy(buf, src_row, dst_row, n_chunks):
        return pltpu.make_async_copy(
            xs_scr.at[buf, pl.ds(pl.multiple_of(src_row, ROW_CHUNK), n_chunks * ROW_CHUNK)],
            xb_hbm.at[pl.ds(pl.multiple_of(dst_row, ROW_CHUNK), n_chunks * ROW_CHUNK)],
            sem.at[buf])

    def per_expert(e, carry):
        so = seg_s[i * N_EXPERTS + e]
        go = goff_s[i * N_EXPERTS + e]
        _for_row_pieces(
            nch_s[i * N_EXPERTS + e], SEG_MAX_POW,
            lambda off, n: rows_copy(slot, so + off * ROW_CHUNK, go + off * ROW_CHUNK, n).start())
        return carry

    lax.fori_loop(0, N_EXPERTS, per_expert, 0)

    def drain(buf, tile):
        _for_row_pieces(ntot_s[tile], TILE_MAX_POW, lambda off, n: rows_copy(buf, 0, 0, n).wait())

    @pl.when(i > 0)
    def _():
        drain(1 - slot, i - 1)

    @pl.when(i == nt - 1)
    def _():
        drain(slot, i)
        zero_scr[...] = jnp.zeros(zero_scr.shape, zero_scr.dtype)

        def tail_copy(dst_row):
            return pltpu.make_async_copy(
                zero_scr, xb_hbm.at[pl.ds(pl.multiple_of(dst_row, ROW_CHUNK), ROW_CHUNK)],
                sem.at[2])

        def per_expert_tail(e, carry):
            def per_chunk(c, carry2):
                tail_copy(tstart_s[e] + c * ROW_CHUNK).start()
                return carry2

            lax.fori_loop(0, tnch_s[e], per_chunk, 0)

            def wait_chunk(c, carry2):
                tail_copy(0).wait()
                return carry2

            return lax.fori_loop(0, tnch_s[e], wait_chunk, carry)

        lax.fori_loop(0, N_EXPERTS, per_expert_tail, 0)


def _moe_sort(meta, a_all, idx_all, h2_all, *, tm, cap):
    t_all = a_all.shape[0]
    nt = t_all // tm
    n_rows = _sort_rows(tm)
    lt = jnp.tril(jnp.ones((tm, tm), BF16), -1)
    grid_spec = pltpu.PrefetchScalarGridSpec(
        num_scalar_prefetch=6,
        grid=(nt,),
        in_specs=[pl.BlockSpec((tm, LANES), lambda i, *_: (i, 0)),
                  pl.BlockSpec((tm, LANES), lambda i, *_: (i, 0)),
                  pl.BlockSpec((tm, D_MODEL), lambda i, *_: (i, 0)),
                  pl.BlockSpec((1, 1, LANES), lambda i, *_: (i, 0, 0)),
                  pl.BlockSpec((tm, tm), lambda i, *_: (0, 0))],
        out_specs=[pl.BlockSpec(memory_space=pl.ANY),
                   pl.BlockSpec((tm, LANES), lambda i, *_: (i, 0))],
        scratch_shapes=[pltpu.VMEM((2, n_rows, D_MODEL), BF16),
                        pltpu.VMEM((ROW_CHUNK, D_MODEL), BF16),
                        pltpu.SemaphoreType.DMA((3,))],
    )
    return pl.pallas_call(
        functools.partial(_moe_sort_kernel, tm=tm, n_rows=n_rows),
        grid_spec=grid_spec,
        out_shape=[jax.ShapeDtypeStruct((cap, D_MODEL), BF16),
                   jax.ShapeDtypeStruct((t_all, LANES), F32)],
        compiler_params=_cparams(1),
        name="moe_sort",
    )(meta["seg"], meta["goff"], meta["nch"], meta["ntot"], meta["tstart"], meta["tnch"],
      a_all, idx_all, h2_all, meta["segv"], lt)


def _moe_ffn_kernel(be_s, nused_s, x_ref, wgu_ref, bgu_ref, wd_ref, bd_ref, y_ref,
                    wgu_bf, wd_bf):
    b = pl.program_id(0)

    @pl.when(b < nused_s[0])
    def _():
        e = be_s[b]
        e_prev = be_s[jnp.maximum(b - 1, 0)]

        @pl.when((b == 0) | (e != e_prev))
        def _():
            wgu_bf[...] = wgu_ref[0].astype(BF16)
            wd_bf[...] = wd_ref[0].astype(BF16)

        hgu = jnp.dot(x_ref[...], wgu_bf[...], preferred_element_type=F32) + bgu_ref[0]
        d_ff = hgu.shape[1] // 2
        hg = jnp.minimum(hgu[:, :d_ff], SWIGLU_LIMIT)
        hu = jnp.clip(hgu[:, d_ff:], -SWIGLU_LIMIT, SWIGLU_LIMIT)
        act = hg * jax.nn.sigmoid(SWIGLU_ALPHA * hg) * (hu + 1.0)
        y = jnp.dot(act.astype(BF16), wd_bf[...], preferred_element_type=F32) + bd_ref[0]
        y_ref[...] = y.astype(y_ref.dtype)


def _moe_ffn(meta, xb, w_gate_up, b_gate_up, w_down, b_down):
    cap = xb.shape[0]
    nb = cap // FFN_BLOCK
    d_ff2 = w_gate_up.shape[2]

    def blk(b, be, nu):
        return jnp.minimum(b, jnp.maximum(nu[0] - 1, 0))

    grid_spec = pltpu.PrefetchScalarGridSpec(
        num_scalar_prefetch=2,
        grid=(nb,),
        in_specs=[pl.BlockSpec((FFN_BLOCK, D_MODEL), lambda b, be, nu: (blk(b, be, nu), 0)),
                  pl.BlockSpec((1, D_MODEL, d_ff2), lambda b, be, nu: (be[blk(b, be, nu)], 0, 0)),
                  pl.BlockSpec((1, 1, d_ff2), lambda b, be, nu: (be[blk(b, be, nu)], 0, 0)),
                  pl.BlockSpec((1, d_ff2 // 2, D_MODEL),
                               lambda b, be, nu: (be[blk(b, be, nu)], 0, 0)),
                  pl.BlockSpec((1, 1, D_MODEL), lambda b, be, nu: (be[blk(b, be, nu)], 0, 0))],
        out_specs=pl.BlockSpec((FFN_BLOCK, D_MODEL), lambda b, be, nu: (blk(b, be, nu), 0)),
        scratch_shapes=[pltpu.VMEM((D_MODEL, d_ff2), BF16),
                        pltpu.VMEM((d_ff2 // 2, D_MODEL), BF16)],
    )
    return pl.pallas_call(
        _moe_ffn_kernel,
        grid_spec=grid_spec,
        out_shape=jax.ShapeDtypeStruct((cap, D_MODEL), BF16),
        compiler_params=_cparams(1),
        name="moe_ffn",
    )(meta["block_expert"], meta["n_used"], xb, w_gate_up,
      b_gate_up.reshape(N_EXPERTS, 1, d_ff2), w_down, b_down.reshape(N_EXPERTS, 1, D_MODEL))


def _moe_unsort_kernel(seg_s, goff_s, nch_s, ntot_s,
                       dst_ref, gk_ref, x1_ref, g2p_ref, g2s_ref, yb_hbm,
                       yp_ref, ys_ref, ybuf, sem, *, tm, n_rows, n_prompt_tiles):
    i = pl.program_id(0)
    nt = pl.num_programs(0)
    slot = i % 2

    def rows_copy(buf, src_row, dst_row, n_chunks):
        return pltpu.make_async_copy(
            yb_hbm.at[pl.ds(pl.multiple_of(src_row, ROW_CHUNK), n_chunks * ROW_CHUNK)],
            ybuf.at[buf, pl.ds(pl.multiple_of(dst_row, ROW_CHUNK), n_chunks * ROW_CHUNK)],
            sem.at[buf])

    def fetch(tile, buf):
        def per_expert(e, carry):
            so = seg_s[tile * N_EXPERTS + e]
            go = goff_s[tile * N_EXPERTS + e]
            _for_row_pieces(
                nch_s[tile * N_EXPERTS + e], SEG_MAX_POW,
                lambda off, n: rows_copy(buf, go + off * ROW_CHUNK, so + off * ROW_CHUNK,
                                         n).start())
            return carry

        lax.fori_loop(0, N_EXPERTS, per_expert, 0)

    @pl.when(i == 0)
    def _():
        ybuf[...] = jnp.zeros(ybuf.shape, ybuf.dtype)
        fetch(0, 0)

    @pl.when(i + 1 < nt)
    def _():
        fetch(i + 1, 1 - slot)

    _for_row_pieces(ntot_s[i], TILE_MAX_POW, lambda off, n: rows_copy(slot, 0, 0, n).wait())

    dst = dst_ref[...]
    gk = gk_ref[...]
    acc = jnp.zeros((tm, D_MODEL), F32)
    for c in range(n_rows // SEL_CHUNK):
        cols = (lax.broadcasted_iota(jnp.int32, (tm, SEL_CHUNK), 1) + c * SEL_CHUNK).astype(F32)
        q = jnp.zeros((tm, SEL_CHUNK), F32)
        for kk in range(TOP_K):
            q = q + jnp.where(cols == dst[:, kk:kk + 1], gk[:, kk:kk + 1], 0.0)
        acc = acc + jnp.dot(q.astype(BF16), ybuf[slot, c * SEL_CHUNK:(c + 1) * SEL_CHUNK, :],
                            preferred_element_type=F32)

    @pl.when(i < n_prompt_tiles)
    def _():
        yp_ref[...] = x1_ref[...] + g2p_ref[...] * acc

    @pl.when(i >= n_prompt_tiles)
    def _():
        ys_ref[...] = x1_ref[...] + g2s_ref[...] * acc


def _moe_unsort(meta, dst_all, gk_all, x1_all, gt2_p, gt2_s, yb, *, tm, n_prompt_tiles):
    t_all = dst_all.shape[0]
    nt = t_all // tm
    n_rows = _sort_rows(tm)
    last_p = n_prompt_tiles - 1
    grid_spec = pltpu.PrefetchScalarGridSpec(
        num_scalar_prefetch=4,
        grid=(nt,),
        in_specs=[pl.BlockSpec((tm, LANES), lambda i, *_: (i, 0)),
                  pl.BlockSpec((tm, LANES), lambda i, *_: (i, 0)),
                  pl.BlockSpec((tm, D_MODEL), lambda i, *_: (i, 0)),
                  pl.BlockSpec((1, D_MODEL), lambda i, *_: (0, 0)),
                  pl.BlockSpec((tm, D_MODEL), lambda i, *_: (0, 0)),
                  pl.BlockSpec(memory_space=pl.ANY)],
        out_specs=[pl.BlockSpec((tm, D_MODEL), lambda i, *_: (jnp.minimum(i, last_p), 0)),
                   pl.BlockSpec((tm, D_MODEL), lambda i, *_: (0, 0))],
        scratch_shapes=[pltpu.VMEM((2, n_rows, D_MODEL), BF16),
                        pltpu.SemaphoreType.DMA((2,))],
    )
    return pl.pallas_call(
        functools.partial(_moe_unsort_kernel, tm=tm, n_rows=n_rows,
                          n_prompt_tiles=n_prompt_tiles),
        grid_spec=grid_spec,
        out_shape=[jax.ShapeDtypeStruct((n_prompt_tiles * tm, D_MODEL), F32),
                   jax.ShapeDtypeStruct((tm, D_MODEL), F32)],
        compiler_params=_cparams(1),
        name="moe_unsort",
    )(meta["seg"], meta["goff"], meta["nch"], meta["ntot"],
      dst_all, gk_all, x1_all, gt2_p, gt2_s, yb)


def _moe_meta(cnt, tm):
    nt = cnt.shape[0]
    cnt = cnt.astype(jnp.int32)
    cnt_pad = (cnt + ROW_CHUNK - 1) // ROW_CHUNK * ROW_CHUNK
    seg = jnp.cumsum(cnt_pad, axis=1) - cnt_pad
    rows_e = jnp.sum(cnt_pad, axis=0)
    region = (rows_e + FFN_BLOCK - 1) // FFN_BLOCK * FFN_BLOCK
    gstart = jnp.cumsum(region) - region
    goff = gstart[None, :] + jnp.cumsum(cnt_pad, axis=0) - cnt_pad
    nblk_e = region // FFN_BLOCK
    blk_end = jnp.cumsum(nblk_e)
    cap = _moe_cap(nt * tm, tm)
    blocks = jnp.arange(cap // FFN_BLOCK, dtype=jnp.int32)
    block_expert = jnp.minimum(
        jnp.sum((blk_end[None, :] <= blocks[:, None]).astype(jnp.int32), axis=1), N_EXPERTS - 1)
    segv = jnp.zeros((nt, 1, LANES), F32).at[:, 0, :N_EXPERTS].set(seg.astype(F32))
    return {
        "seg": seg.reshape(-1), "goff": goff.reshape(-1).astype(jnp.int32),
        "nch": (cnt_pad // ROW_CHUNK).reshape(-1),
        "ntot": jnp.sum(cnt_pad, axis=1) // ROW_CHUNK,
        "tstart": (gstart + rows_e).astype(jnp.int32),
        "tnch": (region - rows_e) // ROW_CHUNK,
        "block_expert": block_expert,
        "n_used": blk_end[-1:].astype(jnp.int32),
        "segv": segv,
    }


def _moe_cap(t_all, tm):
    nt = t_all // tm
    worst = TOP_K * t_all + nt * N_EXPERTS * (ROW_CHUNK - 1) + N_EXPERTS * (FFN_BLOCK - ROW_CHUNK)
    return -(-worst // FFN_BLOCK) * FFN_BLOCK


def _t5_bucket(dist):
    max_exact = NUM_BUCKETS // 2
    d = dist.astype(jnp.int32)
    ratio = (jnp.log(jnp.maximum(d, 1).astype(F32) / max_exact)
             / math.log(MAX_DISTANCE / max_exact))
    large = jnp.minimum(max_exact + (ratio * (NUM_BUCKETS - max_exact)).astype(jnp.int32),
                        NUM_BUCKETS - 1)
    return jnp.where(d < max_exact, d, large)


def _step_bias(tab, dil):
    return tab[_t5_bucket(dil * jnp.arange(ATT_BLK + 1))].astype(F32).T


def _band_table(sb):
    return jnp.concatenate([sb[:, ::-1], jnp.full((HEADS, ATT_BLK - 1), NEG_INF, F32)], axis=1)


def _cache_table(sb, dil):
    on_grid = sb[:, :0:-1]
    if dil == 1:
        return on_grid
    off = jnp.full((HEADS, ATT_BLK, dil - 1), NEG_INF, F32)
    return jnp.concatenate([on_grid[:, :, None], off], axis=2).reshape(HEADS, ATT_BLK * dil)


def kernel(x_prompt, x_sample, cache_kv_w128, cache_kv_w512, cache_kv_w2048, state_pool, c_prompt,
           c_sample, w_ada, b_ada, norm_mix_g, norm_ffn_g, w_in, q_norm_g, k_norm_g, rel_bias,
           w_pool_mix, pool_scale, w_up_attn, w_up_pool, w_out, w_router, b_router, w_gate_up,
           b_gate_up, w_down, b_down):
    assert w_ada.shape[0] == 1, "one layer"
    seq = x_prompt.shape[1]
    n_s = x_sample.shape[0]
    assert x_prompt.shape[0] == 1 and x_sample.shape[1] == 1
    assert seq % (DIL_GROUPS[-1][1] * ATT_BLK * ATT_SUB) == 0 and seq % TM_PROMPT == 0
    assert n_s == TM_SAMPLE
    dils = tuple(d for _, d in DIL_GROUPS)
    caches = (cache_kv_w128, cache_kv_w512, cache_kv_w2048)

    w_in_bf = w_in[0].astype(BF16)
    heads_of = jnp.arange(GROUP_W) // HEAD_DIM
    half_heads = heads_of[:GROUP_W // 2]
    bdiag = (half_heads[:, None] == half_heads[None, :]).astype(BF16)
    qg = (jnp.tile(q_norm_g[0], HEADS) * SCALE).reshape(1, GROUP_W)
    kg = jnp.tile(k_norm_g[0], HEADS).reshape(1, GROUP_W)
    expand = ((jnp.arange(LANES)[:, None] % HEADS == heads_of[None, :])
              & (jnp.arange(LANES)[:, None] < 2 * HEADS)).astype(BF16)
    wr = jnp.zeros((D_MODEL, LANES), F32).at[:, :N_EXPERTS].set(w_router[0])
    wr_hi = wr.astype(BF16)
    wr_lo = (wr - wr_hi.astype(F32)).astype(BF16)
    br = jnp.full((1, LANES), NEG_INF, F32).at[0, :N_EXPERTS].set(b_router[0])
    wts = (w_pool_mix[0].astype(BF16), pool_scale[0].reshape(1, POOL_W),
           w_up_attn[0].astype(BF16), w_up_pool[0].astype(BF16), w_out[0].astype(BF16), expand,
           norm_ffn_g[0].reshape(1, D_MODEL), wr_hi, wr_lo, br)
    g1 = norm_mix_g[0].reshape(1, D_MODEL)

    n_c = 1 + n_s
    c_all = jnp.zeros((-(-n_c // 8) * 8, D_MODEL), F32).at[0:1].set(c_prompt).at[1:n_c].set(c_sample)
    mod = _ada(c_all, w_ada[0], b_ada[0])
    sh1, sc1, gt1, sh2, sc2, gt2 = jnp.split(mod, N_ADA, axis=-1)

    def prow(m):
        return m[0:1]

    def srows(m):
        return m[1:n_c]

    xp = x_prompt[0]
    q_p, k_p, v_p, u_p, sga_p, sgp_p, st_p = _proj(
        xp, g1, prow(sc1), prow(sh1), w_in_bf, bdiag, qg, kg,
        tm=TM_PROMPT, dils=dils, per_row=False)
    step_bias = [_step_bias(rel_bias[:, g * HEADS:(g + 1) * HEADS], d)
                 for g, (_, d) in enumerate(DIL_GROUPS)]
    o_p, lse_p = [], []
    for g, (_, d) in enumerate(DIL_GROUPS):
        o, lse = _attn_prompt(q_p[g], k_p[g], v_p[g], _band_table(step_bias[g]), d)
        o_p.append(o)
        lse_p.append(lse)

    xs = x_sample[:, 0]
    ones = (1, 1, 1)
    q_s, _, _, u_s, sga_s, sgp_s, st_s = _proj(
        xs, g1, srows(sc1), srows(sh1), w_in_bf, bdiag, qg, kg,
        tm=TM_SAMPLE, dils=ones, per_row=True)
    def heads(parts):
        return jnp.stack([p.astype(F32).reshape(n_s, HEADS, HEAD_DIM) for p in parts], axis=1)

    bself = jnp.stack([sb[:, 0:1] for sb in step_bias])
    btabs = [_cache_table(sb, d) for sb, (_, d) in zip(step_bias, DIL_GROUPS)]
    caches_t = [jnp.transpose(c, (0, 1, 3, 4, 5, 2)) for c in caches]
    o3, lse3 = _attn_sample(heads(q_s), heads([st[:, :GROUP_W] for st in st_s]),
                            heads([st[:, GROUP_W:] for st in st_s]), bself, btabs, caches_t)
    o_s = [o3[:, g].reshape(n_s, GROUP_W) for g in range(N_GROUPS)]
    lse_s = [jnp.zeros((n_s, LANES), F32).at[:, :HEADS].set(lse3[:, g, :, 0])
             for g in range(N_GROUPS)]
    pooled_s, pool_state_t = _pool_sample(jnp.transpose(state_pool, (0, 2, 1, 3)), u_s)
    pool_state_s = jnp.transpose(pool_state_t, (0, 2, 1, 3))

    nt_p = seq // TM_PROMPT
    t_all = seq + TM_PROMPT
    bufs = _post(xp, o_p, lse_p, (u_p,), sga_p, sgp_p, wts, (prow(gt1), prow(sc2), prow(sh2)),
                 tm=TM_PROMPT, dils=dils, per_row=False, rows_total=t_all, row_block0=0,
                 cnt_tiles=nt_p + 1, cnt_block=None, grid=nt_p, n_valid_steps=None,
                 alias_bufs=None)
    bufs = _post(xs, o_s, lse_s, pooled_s, sga_s, sgp_s, wts,
                 (srows(gt1), srows(sc2), srows(sh2)),
                 tm=TM_SAMPLE, dils=ones, per_row=True, rows_total=t_all,
                 row_block0=seq // TM_SAMPLE, cnt_tiles=nt_p + 1, cnt_block=nt_p,
                 grid=TM_PROMPT // TM_SAMPLE, n_valid_steps=1, alias_bufs=bufs)
    x1_all, h2_all, a_all, idx_all, gk_all, cnt = bufs

    meta = _moe_meta(cnt[:, 0, :N_EXPERTS], TM_PROMPT)
    cap = _moe_cap(t_all, TM_PROMPT)
    xb, dst_all = _moe_sort(meta, a_all, idx_all, h2_all, tm=TM_PROMPT, cap=cap)
    yb = _moe_ffn(meta, xb, w_gate_up[0], b_gate_up[0], w_down[0], b_down[0])
    gt2_s = jnp.zeros((TM_PROMPT, D_MODEL), F32).at[:n_s].set(srows(gt2))
    y_p, y_s = _moe_unsort(meta, dst_all, gk_all, x1_all, prow(gt2), gt2_s, yb,
                           tm=TM_PROMPT, n_prompt_tiles=nt_p)

    def kv_state(st, rows):
        return st.reshape(1, 1, rows, 2, HEADS, HEAD_DIM)

    kv_p = [kv_state(st, st.shape[0]) for st in st_p]
    kv_s = [st.reshape(1, n_s, 1, 2, HEADS, HEAD_DIM) for st in st_s]
    pool_p = u_p[seq - POOL_BUF:].reshape(1, 1, POOL_BUF, POOL_W)
    return (y_p.reshape(1, seq, D_MODEL), y_s[:n_s].reshape(n_s, 1, D_MODEL),
            kv_p[0], kv_p[1], kv_p[2], pool_p, kv_s[0], kv_s[1], kv_s[2], pool_state_s)
```

```python
import functools
import math

import jax
import jax.numpy as jnp
from jax import lax
from jax.experimental import pallas as pl
from jax.experimental.pallas import tpu as pltpu

F32 = jnp.float32
BF16 = jnp.bfloat16

D_MODEL = 1024
HEAD_DIM = 64
HEADS = 8
GROUP_W = HEADS * HEAD_DIM
DIL_GROUPS = ((128, 1), (512, 4), (2048, 16))
N_GROUPS = len(DIL_GROUPS)
QKV_W = N_GROUPS * GROUP_W
ATT_BLK = 128
ATT_SUB = 2
POOL_WINDOWS = (2, 4, 8, 16)
POOL_W = 512
POOL_GW = 128
POOL_BUF = 15
OFF_K, OFF_V = QKV_W, 2 * QKV_W
OFF_U = 3 * QKV_W
OFF_GA = OFF_U + POOL_W
OFF_GP = OFF_GA + D_MODEL
IN_W = OFF_GP + D_MODEL
NUM_BUCKETS = 32
MAX_DISTANCE = 2048
N_EXPERTS = 32
TOP_K = 4
SWIGLU_LIMIT = 7.0
SWIGLU_ALPHA = 1.702
N_ADA = 6
EPS = 1e-6
NEG_INF = -1e30
PAST_LEN = 8192
SCALE = HEAD_DIM ** -0.5

LANES = 128
ROW_CHUNK = 16
TM_PROMPT = 512
TM_SAMPLE = 128
FFN_BLOCK = 512
SEL_CHUNK = 512
VMEM_LIMIT = 56 * 1024 * 1024


def _cparams(n_axes):
    return pltpu.CompilerParams(dimension_semantics=("arbitrary",) * n_axes,
                                vmem_limit_bytes=VMEM_LIMIT)


def _const_spec(shape):
    nd = len(shape)
    return pl.BlockSpec(shape, lambda *_: (0,) * nd)


def _ada_kernel(c_ref, w_ref, b_ref, o_ref):
    c = c_ref[...]
    s = c * jax.nn.sigmoid(c)
    o_ref[...] = jnp.dot(s.astype(BF16), w_ref[...].astype(BF16),
                         preferred_element_type=F32) + b_ref[...]


def _ada(c_all, w_ada, b_ada):
    rows = c_all.shape[0]
    n = w_ada.shape[1]
    tn = 1536
    return pl.pallas_call(
        _ada_kernel,
        grid=(n // tn,),
        in_specs=[pl.BlockSpec((rows, D_MODEL), lambda j: (0, 0)),
                  pl.BlockSpec((D_MODEL, tn), lambda j: (0, j)),
                  pl.BlockSpec((1, tn), lambda j: (0, j))],
        out_specs=pl.BlockSpec((rows, tn), lambda j: (0, j)),
        out_shape=jax.ShapeDtypeStruct((rows, n), F32),
        compiler_params=_cparams(1),
        name="ada",
    )(c_all, w_ada, b_ada.reshape(1, n))


def _proj_kernel(x_ref, g_ref, sc_ref, sh_ref, w_ref, bd_ref, qg_ref, kg_ref,
                 *refs, tm, dils, st_rows):
    q_refs, k_refs, v_refs = refs[0:3], refs[3:6], refs[6:9]
    u_ref, sga_ref, sgp_ref = refs[9:12]
    st_refs = refs[12:15]
    scr = refs[15]

    x = x_ref[...]
    ms = jnp.mean(x * x, axis=-1, keepdims=True)
    h = x * lax.rsqrt(ms + EPS) * g_ref[...] * (1.0 + sc_ref[...]) + sh_ref[...]
    hb = h.astype(BF16)

    def proj(off, width):
        return jnp.dot(hb, w_ref[:, off:off + width], preferred_element_type=F32)

    def head_norm(z, gain_ref):
        zz = (z * z).astype(BF16)
        half = GROUP_W // 2
        ss = jnp.concatenate(
            [jnp.dot(zz[:, :half], bd_ref[...], preferred_element_type=F32),
             jnp.dot(zz[:, half:], bd_ref[...], preferred_element_type=F32)], axis=1)
        return z * lax.rsqrt(ss * (1.0 / HEAD_DIM) + EPS) * gain_ref[...]

    def put(out_ref, val, d):
        if d == 1:
            out_ref[...] = val.astype(out_ref.dtype)
        else:
            for c in range(GROUP_W // LANES):
                scr[c] = val[:, c * LANES:(c + 1) * LANES]
            for r in range(d):
                for c in range(GROUP_W // LANES):
                    col = r * GROUP_W + c * LANES
                    out_ref[:, col:col + LANES] = (
                        scr[c, pl.ds(r, tm // d, stride=d), :].astype(out_ref.dtype))

    for g, d in enumerate(dils):
        qn = head_norm(proj(g * GROUP_W, GROUP_W), qg_ref)
        put(q_refs[g], qn, d)
        kn = head_norm(proj(OFF_K + g * GROUP_W, GROUP_W), kg_ref)
        put(k_refs[g], kn, d)
        v = proj(OFF_V + g * GROUP_W, GROUP_W)
        put(v_refs[g], v, d)
        rb = st_rows[g]
        st_refs[g][:, 0:GROUP_W] = kn[tm - rb:, :]
        st_refs[g][:, GROUP_W:2 * GROUP_W] = v[tm - rb:, :]

    u_ref[...] = proj(OFF_U, POOL_W)
    sga_ref[...] = jax.nn.sigmoid(proj(OFF_GA, D_MODEL)).astype(BF16)
    sgp_ref[...] = jax.nn.sigmoid(proj(OFF_GP, D_MODEL)).astype(BF16)


def _mod_spec(per_row, tm):
    if per_row:
        return pl.BlockSpec((tm, D_MODEL), lambda i: (i, 0))
    return pl.BlockSpec((1, D_MODEL), lambda i: (0, 0))


def _proj(x, g1, sc1, sh1, w_in_bf, bdiag, qg, kg, *, tm, dils, per_row):
    s = x.shape[0]
    nt = s // tm
    wins = tuple(min(w, s) for w, _ in DIL_GROUPS)
    st_rows = tuple(min(tm, w) for w in wins)

    def res_spec(d):
        return pl.BlockSpec((tm // d, d * GROUP_W), lambda i: (i, 0))

    def st_spec(w, rb):
        first = nt - w // rb
        return pl.BlockSpec((rb, 2 * GROUP_W), lambda i: (jnp.maximum(i - first, 0), 0))

    qkv_shapes = [jax.ShapeDtypeStruct((s // d, d * GROUP_W), BF16) for d in dils]
    out_shape = (qkv_shapes * 3
                 + [jax.ShapeDtypeStruct((s, POOL_W), F32),
                    jax.ShapeDtypeStruct((s, D_MODEL), BF16),
                    jax.ShapeDtypeStruct((s, D_MODEL), BF16)]
                 + [jax.ShapeDtypeStruct((w, 2 * GROUP_W), F32) for w in wins])
    out_specs = ([res_spec(d) for d in dils] * 3
                 + [pl.BlockSpec((tm, POOL_W), lambda i: (i, 0)),
                    pl.BlockSpec((tm, D_MODEL), lambda i: (i, 0)),
                    pl.BlockSpec((tm, D_MODEL), lambda i: (i, 0))]
                 + [st_spec(w, rb) for w, rb in zip(wins, st_rows)])
    in_specs = [pl.BlockSpec((tm, D_MODEL), lambda i: (i, 0)),
                _const_spec((1, D_MODEL)),
                _mod_spec(per_row, tm), _mod_spec(per_row, tm),
                pl.BlockSpec((D_MODEL, IN_W), lambda i: (0, 0), pipeline_mode=pl.Buffered(1)),
                _const_spec((GROUP_W // 2, GROUP_W // 2)),
                _const_spec((1, GROUP_W)), _const_spec((1, GROUP_W))]
    outs = pl.pallas_call(
        functools.partial(_proj_kernel, tm=tm, dils=dils, st_rows=st_rows),
        grid=(nt,),
        in_specs=in_specs,
        out_specs=out_specs,
        out_shape=out_shape,
        scratch_shapes=[pltpu.VMEM((GROUP_W // LANES, tm, LANES), F32)],
        compiler_params=_cparams(1),
        name="proj",
    )(x, g1, sc1, sh1, w_in_bf, bdiag, qg, kg)
    return outs[0:3], outs[3:6], outs[6:9], outs[9], outs[10], outs[11], outs[12:15]


def _sample_token_attn(q, kn, vn, bself, btab, k_slab, v_slab):
    win = btab.shape[1]
    row_w = lax.broadcasted_iota(jnp.int32, (HEADS, win), 0)
    row_e = lax.broadcasted_iota(jnp.int32, (HEADS, HEAD_DIM), 0)
    qb = q.astype(BF16)
    s = jnp.zeros((HEADS, win), F32)
    for h in range(HEADS):
        sh = jnp.dot(qb, k_slab(h).astype(BF16), preferred_element_type=F32)
        s = jnp.where(row_w == h, sh, s)
    s = s + btab
    s0 = jnp.sum(q * kn, axis=-1, keepdims=True) + bself
    m = jnp.maximum(jnp.max(s, axis=-1, keepdims=True), s0)
    p = jnp.exp(s - m)
    p0 = jnp.exp(s0 - m)
    l = jnp.sum(p, axis=-1, keepdims=True) + p0
    pb = p.astype(BF16)
    o = jnp.zeros((HEADS, HEAD_DIM), F32)
    for h in range(HEADS):
        oh = lax.dot_general(pb, v_slab(h).astype(BF16), (((1,), (1,)), ((), ())),
                             preferred_element_type=F32)
        o = jnp.where(row_e == h, oh, o)
    return (o + p0 * vn) / l, m + jnp.log(l)


def _attn_kernel(q_ref, kp_ref, kc_ref, vp_ref, vc_ref, r_ref,
                 qs_ref, kns_ref, vns_ref, bself_ref, btab_ref, c_ref,
                 o_ref, lse_ref, os_ref, lses_ref, bias_ref):
    i = pl.program_id(1)

    for t in range(qs_ref.shape[0]):
        o_t, lse_t = _sample_token_attn(
            qs_ref[t], kns_ref[t], vns_ref[t], bself_ref[...], btab_ref[...],
            lambda h, t=t: c_ref[0, t, 0, h], lambda h, t=t: c_ref[0, t, 1, h])
        os_ref[t] = o_t
        lses_ref[t] = lse_t

    @pl.when((pl.program_id(0) == 0) & (i == 0))
    def _():
        for h in range(HEADS):
            row = jnp.broadcast_to(r_ref[h:h + 1, :], (ATT_BLK, 2 * ATT_BLK))
            bias_ref[h] = pltpu.roll(row, 0, 1, stride=1, stride_axis=0)

    q = q_ref[...]
    k = jnp.concatenate([kp_ref[...], kc_ref[...]], axis=0)
    v = jnp.concatenate([vp_ref[...], vc_ref[...]], axis=0)
    col = lax.broadcasted_iota(jnp.int32, (ATT_BLK, 2 * ATT_BLK), 1)
    no_prev = jnp.where((col < ATT_BLK) & (i == 0), NEG_INF, 0.0)
    lane_q = lax.broadcasted_iota(jnp.int32, (ATT_SUB * ATT_BLK, LANES), 1)
    lane_v = lax.broadcasted_iota(jnp.int32, ((ATT_SUB + 1) * ATT_BLK, LANES), 1)

    def pair(h):
        return slice((h // 2) * LANES, (h // 2 + 1) * LANES)

    def mine(lane, h):
        return (lane < HEAD_DIM) == (h % 2 == 0)

    ss = []
    for h in range(HEADS):
        q2 = q[:, pair(h)]
        qm = jnp.where(mine(lane_q, h), q2, jnp.zeros_like(q2))
        k2 = k[:, pair(h)]
        for j in range(ATT_SUB):
            s = lax.dot_general(qm[j * ATT_BLK:(j + 1) * ATT_BLK], k2[j * ATT_BLK:(j + 2) * ATT_BLK],
                                (((1,), (1,)), ((), ())), preferred_element_type=F32)
            s = s + bias_ref[h]
            ss.append(s + no_prev if j == 0 else s)
    s = jnp.concatenate(ss, axis=0)
    m = jnp.max(s, axis=-1, keepdims=True)
    p = jnp.exp(s - m)
    l = jnp.sum(p, axis=-1, keepdims=True)
    pb = p.astype(BF16)
    lse = m + jnp.log(l)
    inv_l = 1.0 / l
    outs, lses = [], []
    for h in range(HEADS):
        v2 = v[:, pair(h)]
        vm = jnp.where(mine(lane_v, h), v2, jnp.zeros_like(v2))
        o_sub, lse_sub = [], []
        for j in range(ATT_SUB):
            rows = slice((h * ATT_SUB + j) * ATT_BLK, (h * ATT_SUB + j + 1) * ATT_BLK)
            o_sub.append(jnp.dot(pb[rows], vm[j * ATT_BLK:(j + 2) * ATT_BLK],
                                 preferred_element_type=F32) * inv_l[rows])
            lse_sub.append(lse[rows])
        o = jnp.concatenate(o_sub, axis=0)
        if h % 2 == 0:
            outs.append(o)
        else:
            outs[-1] = outs[-1] + o
        lses.append(jnp.concatenate(lse_sub, axis=0))
    o_ref[...] = jnp.concatenate(outs, axis=-1).astype(o_ref.dtype)
    lse_ref[...] = jnp.concatenate(
        lses + [jnp.zeros((ATT_SUB * ATT_BLK, LANES - HEADS), F32)], axis=-1)


def _attn(q, k, v, r_tab, d, qs, kns, vns, bself, btab, cache_t):
    rows = q.shape[0]
    step = ATT_SUB * ATT_BLK
    nblk = rows // step
    n_tok = qs.shape[0]
    assert n_tok % (d * nblk) == 0
    tok = n_tok // (d * nblk)
    cur = pl.BlockSpec((step, GROUP_W), lambda r, i: (i, r))
    prev = pl.BlockSpec((ATT_BLK, GROUP_W), lambda r, i: (jnp.maximum(i * ATT_SUB - 1, 0), r))
    tok_spec = pl.BlockSpec((tok, HEADS, HEAD_DIM), lambda r, i: (r * nblk + i, 0, 0))
    return pl.pallas_call(
        _attn_kernel,
        grid=(d, nblk),
        in_specs=[cur, prev, cur, prev, cur,
                  pl.BlockSpec((HEADS, 2 * ATT_BLK), lambda r, i: (0, 0)),
                  tok_spec, tok_spec, tok_spec,
                  pl.BlockSpec(bself.shape, lambda r, i: (0, 0)),
                  pl.BlockSpec(btab.shape, lambda r, i: (0, 0)),
                  pl.BlockSpec((1, tok) + cache_t.shape[2:],
                               lambda r, i: (0, r * nblk + i, 0, 0, 0, 0))],
        out_specs=[pl.BlockSpec((step, GROUP_W), lambda r, i: (i, r)),
                   pl.BlockSpec((step, LANES), lambda r, i: (i, r)),
                   tok_spec,
                   pl.BlockSpec((tok, HEADS, 1), lambda r, i: (r * nblk + i, 0, 0))],
        out_shape=[jax.ShapeDtypeStruct((rows, d * GROUP_W), BF16),
                   jax.ShapeDtypeStruct((rows, d * LANES), F32),
                   jax.ShapeDtypeStruct((n_tok, HEADS, HEAD_DIM), F32),
                   jax.ShapeDtypeStruct((n_tok, HEADS, 1), F32)],
        scratch_shapes=[pltpu.VMEM((HEADS, ATT_BLK, 2 * ATT_BLK), F32)],
        compiler_params=_cparams(2),
        name=f"attn_d{d}",
    )(q, k, k, v, v, r_tab, qs, kns, vns, bself, btab, cache_t)


def _pool_sample_kernel(st_ref, u_ref, pooled_ref, new_ref):
    u = u_ref[...]
    rows = [st_ref[0, j] for j in range(POOL_BUF)]
    outs = []
    for g, w in enumerate(POOL_WINDOWS):
        sl = slice(g * POOL_GW, (g + 1) * POOL_GW)
        acc = u[:, sl]
        for j in range(POOL_BUF - (w - 1), POOL_BUF):
            acc = acc + rows[j][:, sl]
        outs.append(acc / float(w) - u[:, sl])
    pooled_ref[...] = jnp.concatenate(outs, axis=-1)
    for j in range(POOL_BUF - 1):
        new_ref[0, j] = rows[j + 1]
    new_ref[0, POOL_BUF - 1] = u


def _pool_sample(state, u):
    n = u.shape[0]
    return pl.pallas_call(
        _pool_sample_kernel,
        grid=(1,),
        in_specs=[_const_spec(state.shape), _const_spec(u.shape)],
        out_specs=[_const_spec(u.shape), _const_spec(state.shape)],
        out_shape=[jax.ShapeDtypeStruct((n, POOL_W), F32),
                   jax.ShapeDtypeStruct(state.shape, F32)],
        compiler_params=_cparams(1),
        name="pool_sample",
    )(state, u)


def _post_kernel(*refs, tm, dils, pooled_given, n_valid_steps, aliased):
    it = iter(refs)
    x_ref = next(it)
    o_refs = [next(it) for _ in range(N_GROUPS)]
    lse_refs = [next(it) for _ in range(N_GROUPS)]
    if pooled_given:
        pooled_ref = next(it)
    else:
        u_ref, uh_ref = next(it), next(it)
    sga_ref, sgp_ref = next(it), next(it)
    wpm_ref, psc_ref, wua_ref, wup_ref, wout_ref, exp_ref = (next(it) for _ in range(6))
    gt1_ref, g2_ref, sc2_ref, sh2_ref = (next(it) for _ in range(4))
    wrh_ref, wrl_ref, br_ref = (next(it) for _ in range(3))
    if aliased:
        for _ in range(6):
            next(it)
    x1_ref, h2_ref, a_ref, idx_ref, gk_ref, cnt_ref = (next(it) for _ in range(6))
    ob_scr, ls_scr = next(it), next(it)

    i = pl.program_id(0)

    def compute():
        obs, lss = [], []
        for g, d in enumerate(dils):
            if d == 1:
                obs.append(o_refs[g][...].astype(F32))
                lss.append(lse_refs[g][...])
            else:
                for r in range(d):
                    for c in range(GROUP_W // LANES):
                        col = r * GROUP_W + c * LANES
                        ob_scr[c, pl.ds(r, tm // d, stride=d), :] = (
                            o_refs[g][:, col:col + LANES].astype(F32))
                    ls_scr[pl.ds(r, tm // d, stride=d), :] = (
                        lse_refs[g][:, r * LANES:(r + 1) * LANES])
                obs.append(jnp.concatenate([ob_scr[c] for c in range(GROUP_W // LANES)],
                                           axis=-1))
                lss.append(ls_scr[...])
        mx = jnp.maximum(jnp.maximum(lss[0], lss[1]), lss[2])
        es = [jnp.exp(l - mx) for l in lss]
        den = es[0] + es[1] + es[2]
        attn_o = jnp.zeros((tm, GROUP_W), F32)
        head_lane = lax.broadcasted_iota(jnp.int32, (tm, LANES), 1) < HEADS
        for g in range(N_GROUPS):
            w = jnp.where(head_lane, es[g] / den, 0.0)
            w_hi = w.astype(BF16).astype(F32)
            w_lo = (w - w_hi).astype(BF16).astype(F32)
            lhs = (w_hi + pltpu.roll(w_lo, HEADS, 1)).astype(BF16)
            wexp = jnp.dot(lhs, exp_ref[...], preferred_element_type=F32)
            attn_o = attn_o + wexp * obs[g]

        if pooled_given:
            pooled = pooled_ref[...]
        else:
            u = u_ref[...]
            halo = jnp.where(i == 0, 0.0, uh_ref[...])
            pos = (lax.broadcasted_iota(jnp.int32, (tm, 1), 0) + i * tm + 1).astype(F32)
            outs = []
            for g, w in enumerate(POOL_WINDOWS):
                sl = slice(g * POOL_GW, (g + 1) * POOL_GW)
                a = jnp.concatenate([halo[:, sl], u[:, sl]], axis=0)
                span = 1
                while span < w:
                    n = a.shape[0] - span
                    a = a[span:, :] + a[:n, :]
                    span *= 2
                off = a.shape[0] - tm
                win_sum = a[off:, :]
                outs.append(win_sum / jnp.minimum(pos, float(w)) - u[:, sl])
            pooled = jnp.concatenate(outs, axis=-1)
        pool_parts = []
        for g in range(len(POOL_WINDOWS)):
            sl = slice(g * POOL_GW, (g + 1) * POOL_GW)
            pool_parts.append(jnp.dot(pooled[:, sl].astype(BF16), wpm_ref[g],
                                      preferred_element_type=F32))
        pool_o = jnp.concatenate(pool_parts, axis=-1) * psc_ref[...]

        up_a = jnp.dot(attn_o.astype(BF16), wua_ref[...], preferred_element_type=F32)
        up_p = jnp.dot(pool_o.astype(BF16), wup_ref[...], preferred_element_type=F32)
        merged = sga_ref[...].astype(F32) * up_a + sgp_ref[...].astype(F32) * up_p
        mo = jnp.dot(merged.astype(BF16), wout_ref[...], preferred_element_type=F32)
        x1 = x_ref[...] + gt1_ref[...] * mo
        x1_ref[...] = x1

        ms = jnp.mean(x1 * x1, axis=-1, keepdims=True)
        h2 = x1 * lax.rsqrt(ms + EPS) * g2_ref[...] * (1.0 + sc2_ref[...]) + sh2_ref[...]
        h2_hi = h2.astype(BF16)
        h2_ref[...] = h2_hi
        h2_lo = (h2 - h2_hi.astype(F32)).astype(BF16)
        logits = (jnp.dot(h2_hi, wrh_ref[...], preferred_element_type=F32)
                  + jnp.dot(h2_lo, wrh_ref[...], preferred_element_type=F32)
                  + jnp.dot(h2_hi, wrl_ref[...], preferred_element_type=F32)
                  + br_ref[...])
        lane = lax.broadcasted_iota(jnp.int32, (tm, LANES), 1).astype(F32)
        work = logits
        vals, ids = [], []
        for _ in range(TOP_K):
            m = jnp.max(work, axis=-1, keepdims=True)
            ik = jnp.min(jnp.where(work == m, lane, float(LANES)), axis=-1, keepdims=True)
            vals.append(m)
            ids.append(ik)
            work = jnp.where(lane == ik, -3e38, work)
        ex = [jnp.exp(v - vals[0]) for v in vals]
        den_k = ex[0] + ex[1] + ex[2] + ex[3]
        a = jnp.zeros((tm, LANES), F32)
        idx = jnp.zeros((tm, LANES), F32)
        gk = jnp.zeros((tm, LANES), F32)
        for kk in range(TOP_K):
            gate = ex[kk] / den_k
            a = a + jnp.where(lane == ids[kk], gate, 0.0)
            idx = jnp.where(lane == float(kk), ids[kk], idx)
            gk = jnp.where(lane == float(kk), gate, gk)
        a_ref[...] = a
        idx_ref[...] = idx
        gk_ref[...] = gk
        cnt = jnp.sum((a > 0.0).astype(F32), axis=0, keepdims=True)
        row = lax.broadcasted_iota(jnp.int32, (8, LANES), 0)
        cnt_ref[0] = jnp.where(row == 0, jnp.broadcast_to(cnt, (8, LANES)), 0.0)

    if n_valid_steps is None:
        compute()
    else:
        pl.when(i < n_valid_steps)(compute)

        @pl.when(i >= n_valid_steps)
        def _():
            x1_ref[...] = jnp.zeros(x1_ref.shape, x1_ref.dtype)
            h2_ref[...] = jnp.zeros(h2_ref.shape, h2_ref.dtype)
            a_ref[...] = jnp.zeros(a_ref.shape, a_ref.dtype)
            idx_ref[...] = jnp.zeros(idx_ref.shape, idx_ref.dtype)
            gk_ref[...] = jnp.zeros(gk_ref.shape, gk_ref.dtype)


def _post(x, o_list, lse_list, pool_in, sga, sgp, wts, mods, *, tm, dils, per_row,
          rows_total, row_block0, cnt_tiles, cnt_block, grid, n_valid_steps, alias_bufs):
    pooled_given = not isinstance(pool_in, tuple)
    nv = grid if n_valid_steps is None else n_valid_steps

    def clamp(i):
        return jnp.minimum(i, nv - 1)

    def tile_spec(width):
        return pl.BlockSpec((tm, width), lambda i: (clamp(i), 0))

    in_specs = [tile_spec(D_MODEL)]
    in_specs += [pl.BlockSpec((tm // d, d * GROUP_W), lambda i: (clamp(i), 0)) for d in dils]
    in_specs += [pl.BlockSpec((tm // d, d * LANES), lambda i: (clamp(i), 0)) for d in dils]
    args = [x, *o_list, *lse_list]
    if pooled_given:
        in_specs.append(tile_spec(POOL_W))
        args.append(pool_in)
    else:
        u = pool_in[0]
        in_specs += [tile_spec(POOL_W),
                     pl.BlockSpec((16, POOL_W),
                                  lambda i: (jnp.maximum(i * (tm // 16) - 1, 0), 0))]
        args += [u, u]
    in_specs += [tile_spec(D_MODEL), tile_spec(D_MODEL)]
    args += [sga, sgp]
    wpm, psc, wua, wup, wout, expand, g2, wrh, wrl, br = wts
    gt1, sc2, sh2 = mods

    def mspec():
        if per_row:
            return pl.BlockSpec((tm, D_MODEL), lambda i: (clamp(i), 0))
        return _const_spec((1, D_MODEL))

    in_specs += [_const_spec(wpm.shape), _const_spec(psc.shape), _const_spec(wua.shape),
                 _const_spec(wup.shape), _const_spec(wout.shape), _const_spec(expand.shape),
                 mspec(), _const_spec(g2.shape), mspec(), mspec(),
                 _const_spec(wrh.shape), _const_spec(wrl.shape), _const_spec(br.shape)]
    args += [wpm, psc, wua, wup, wout, expand, gt1, g2, sc2, sh2, wrh, wrl, br]
    aliases = {}
    if alias_bufs is not None:
        base = len(args)
        in_specs += [pl.BlockSpec(memory_space=pl.ANY)] * 6
        args += list(alias_bufs)
        aliases = {base + j: j for j in range(6)}

    def out_spec(width):
        return pl.BlockSpec((tm, width), lambda i: (row_block0 + i, 0))

    out_specs = [out_spec(D_MODEL), out_spec(D_MODEL), out_spec(LANES), out_spec(LANES),
                 out_spec(LANES),
                 pl.BlockSpec((1, 8, LANES),
                              lambda i: (cnt_block if cnt_block is not None else i, 0, 0))]
    out_shape = [jax.ShapeDtypeStruct((rows_total, D_MODEL), F32),
                 jax.ShapeDtypeStruct((rows_total, D_MODEL), BF16),
                 jax.ShapeDtypeStruct((rows_total, LANES), F32),
                 jax.ShapeDtypeStruct((rows_total, LANES), F32),
                 jax.ShapeDtypeStruct((rows_total, LANES), F32),
                 jax.ShapeDtypeStruct((cnt_tiles, 8, LANES), F32)]
    return pl.pallas_call(
        functools.partial(_post_kernel, tm=tm, dils=dils, pooled_given=pooled_given,
                          n_valid_steps=n_valid_steps, aliased=alias_bufs is not None),
        grid=(grid,),
        in_specs=in_specs,
        out_specs=out_specs,
        out_shape=out_shape,
        scratch_shapes=[pltpu.VMEM((GROUP_W // LANES, tm, LANES), F32),
                        pltpu.VMEM((tm, LANES), F32)],
        input_output_aliases=aliases,
        compiler_params=_cparams(1),
        name="post_sample" if per_row else "post",
    )(*args)


def _sort_rows(tm):
    return -(-(TOP_K * tm + N_EXPERTS * (ROW_CHUNK - 1)) // SEL_CHUNK) * SEL_CHUNK


def _for_row_pieces(n_chunks, max_pow, fn):
    big = 1 << max_pow

    def body(c, carry):
        fn(c * big, big)
        return carry

    lax.fori_loop(0, n_chunks >> max_pow, body, 0)
    for pw in range(max_pow - 1, -1, -1):
        @pl.when(((n_chunks >> pw) & 1) == 1)
        def _(pw=pw):
            fn((n_chunks >> (pw + 1)) << (pw + 1), 1 << pw)


SEG_MAX_POW = 3
TILE_MAX_POW = 5


def _moe_sort_kernel(seg_s, goff_s, nch_s, ntot_s, tstart_s, tnch_s,
                     a_ref, idx_ref, h2_ref, segv_ref, lt_ref,
                     xb_hbm, dst_ref, xs_scr, zero_scr, sem, *, tm, n_rows):
    i = pl.program_id(0)
    nt = pl.num_programs(0)
    slot = i % 2
    sel = a_ref[...] > 0.0
    ahead = jnp.dot(lt_ref[...], sel.astype(BF16), preferred_element_type=F32)
    slot1 = jnp.where(sel, segv_ref[0] + ahead + 1.0, 0.0)
    lane = lax.broadcasted_iota(jnp.int32, (tm, LANES), 1).astype(F32)
    idx = idx_ref[...]
    dst = jnp.full((tm, LANES), -1.0, F32)
    for kk in range(TOP_K):
        hit = lane == idx[:, kk:kk + 1]
        dk = jnp.sum(jnp.where(hit, slot1, 0.0), axis=-1, keepdims=True) - 1.0
        dst = jnp.where(lane == float(kk), dk, dst)
    dst_ref[...] = dst
    dst_t = dst.T
    h2 = h2_ref[...]
    for c in range(n_rows // SEL_CHUNK):
        rows = (lax.broadcasted_iota(jnp.int32, (SEL_CHUNK, tm), 0) + c * SEL_CHUNK).astype(F32)
        p = rows == dst_t[0:1, :]
        for kk in range(1, TOP_K):
            p = p | (rows == dst_t[kk:kk + 1, :])
        xs = jnp.dot(jnp.where(p, 1.0, 0.0).astype(BF16), h2, preferred_element_type=F32)
        xs_scr[slot, c * SEL_CHUNK:(c + 1) * SEL_CHUNK, :] = xs.astype(BF16)

    def rows_copy(buf, src_row, dst_row, n_chunks):
        return pltpu.make_async_copy(
            xs_scr.at[buf, pl.ds(pl.multiple_of(src_row, ROW_CHUNK), n_chunks * ROW_CHUNK)],
            xb_hbm.at[pl.ds(pl.multiple_of(dst_row, ROW_CHUNK), n_chunks * ROW_CHUNK)],
            sem.at[buf])

    def per_expert(e, carry):
        so = seg_s[i * N_EXPERTS + e]
        go = goff_s[i * N_EXPERTS + e]
        _for_row_pieces(
            nch_s[i * N_EXPERTS + e], SEG_MAX_POW,
            lambda off, n: rows_copy(slot, so + off * ROW_CHUNK, go + off * ROW_CHUNK, n).start())
        return carry

    lax.fori_loop(0, N_EXPERTS, per_expert, 0)

    def drain(buf, tile):
        _for_row_pieces(ntot_s[tile], TILE_MAX_POW, lambda off, n: rows_copy(buf, 0, 0, n).wait())

    @pl.when(i > 0)
    def _():
        drain(1 - slot, i - 1)

    @pl.when(i == nt - 1)
    def _():
        drain(slot, i)
        zero_scr[...] = jnp.zeros(zero_scr.shape, zero_scr.dtype)

        def tail_copy(dst_row):
            return pltpu.make_async_copy(
                zero_scr, xb_hbm.at[pl.ds(pl.multiple_of(dst_row, ROW_CHUNK), ROW_CHUNK)],
                sem.at[2])

        def per_expert_tail(e, carry):
            def per_chunk(c, carry2):
                tail_copy(tstart_s[e] + c * ROW_CHUNK).start()
                return carry2

            lax.fori_loop(0, tnch_s[e], per_chunk, 0)

            def wait_chunk(c, carry2):
                tail_copy(0).wait()
                return carry2

            return lax.fori_loop(0, tnch_s[e], wait_chunk, carry)

        lax.fori_loop(0, N_EXPERTS, per_expert_tail, 0)


def _moe_sort(meta, a_all, idx_all, h2_all, *, tm, cap):
    t_all = a_all.shape[0]
    nt = t_all // tm
    n_rows = _sort_rows(tm)
    lt = jnp.tril(jnp.ones((tm, tm), BF16), -1)
    grid_spec = pltpu.PrefetchScalarGridSpec(
        num_scalar_prefetch=6,
        grid=(nt,),
        in_specs=[pl.BlockSpec((tm, LANES), lambda i, *_: (i, 0)),
                  pl.BlockSpec((tm, LANES), lambda i, *_: (i, 0)),
                  pl.BlockSpec((tm, D_MODEL), lambda i, *_: (i, 0)),
                  pl.BlockSpec((1, 1, LANES), lambda i, *_: (i, 0, 0)),
                  pl.BlockSpec((tm, tm), lambda i, *_: (0, 0))],
        out_specs=[pl.BlockSpec(memory_space=pl.ANY),
                   pl.BlockSpec((tm, LANES), lambda i, *_: (i, 0))],
        scratch_shapes=[pltpu.VMEM((2, n_rows, D_MODEL), BF16),
                        pltpu.VMEM((ROW_CHUNK, D_MODEL), BF16),
                        pltpu.SemaphoreType.DMA((3,))],
    )
    return pl.pallas_call(
        functools.partial(_moe_sort_kernel, tm=tm, n_rows=n_rows),
        grid_spec=grid_spec,
        out_shape=[jax.ShapeDtypeStruct((cap, D_MODEL), BF16),
                   jax.ShapeDtypeStruct((t_all, LANES), F32)],
        compiler_params=_cparams(1),
        name="moe_sort",
    )(meta["seg"], meta["goff"], meta["nch"], meta["ntot"], meta["tstart"], meta["tnch"],
      a_all, idx_all, h2_all, meta["segv"], lt)


def _moe_ffn_kernel(be_s, nused_s, x_ref, wgu_ref, bgu_ref, wd_ref, bd_ref, y_ref,
                    wgu_bf, wd_bf):
    b = pl.program_id(0)

    @pl.when(b < nused_s[0])
    def _():
        e = be_s[b]
        e_prev = be_s[jnp.maximum(b - 1, 0)]

        @pl.when((b == 0) | (e != e_prev))
        def _():
            wgu_bf[...] = wgu_ref[0].astype(BF16)
            wd_bf[...] = wd_ref[0].astype(BF16)

        hgu = jnp.dot(x_ref[...], wgu_bf[...], preferred_element_type=F32) + bgu_ref[0]
        d_ff = hgu.shape[1] // 2
        hg = jnp.minimum(hgu[:, :d_ff], SWIGLU_LIMIT)
        hu = jnp.clip(hgu[:, d_ff:], -SWIGLU_LIMIT, SWIGLU_LIMIT)
        act = hg * jax.nn.sigmoid(SWIGLU_ALPHA * hg) * (hu + 1.0)
        y = jnp.dot(act.astype(BF16), wd_bf[...], preferred_element_type=F32) + bd_ref[0]
        y_ref[...] = y.astype(y_ref.dtype)


def _moe_ffn(meta, xb, w_gate_up, b_gate_up, w_down, b_down):
    cap = xb.shape[0]
    nb = cap // FFN_BLOCK
    d_ff2 = w_gate_up.shape[2]

    def blk(b, be, nu):
        return jnp.minimum(b, jnp.maximum(nu[0] - 1, 0))

    grid_spec = pltpu.PrefetchScalarGridSpec(
        num_scalar_prefetch=2,
        grid=(nb,),
        in_specs=[pl.BlockSpec((FFN_BLOCK, D_MODEL), lambda b, be, nu: (blk(b, be, nu), 0)),
                  pl.BlockSpec((1, D_MODEL, d_ff2), lambda b, be, nu: (be[blk(b, be, nu)], 0, 0)),
                  pl.BlockSpec((1, 1, d_ff2), lambda b, be, nu: (be[blk(b, be, nu)], 0, 0)),
                  pl.BlockSpec((1, d_ff2 // 2, D_MODEL),
                               lambda b, be, nu: (be[blk(b, be, nu)], 0, 0)),
                  pl.BlockSpec((1, 1, D_MODEL), lambda b, be, nu: (be[blk(b, be, nu)], 0, 0))],
        out_specs=pl.BlockSpec((FFN_BLOCK, D_MODEL), lambda b, be, nu: (blk(b, be, nu), 0)),
        scratch_shapes=[pltpu.VMEM((D_MODEL, d_ff2), BF16),
                        pltpu.VMEM((d_ff2 // 2, D_MODEL), BF16)],
    )
    return pl.pallas_call(
        _moe_ffn_kernel,
        grid_spec=grid_spec,
        out_shape=jax.ShapeDtypeStruct((cap, D_MODEL), BF16),
        compiler_params=_cparams(1),
        name="moe_ffn",
    )(meta["block_expert"], meta["n_used"], xb, w_gate_up,
      b_gate_up.reshape(N_EXPERTS, 1, d_ff2), w_down, b_down.reshape(N_EXPERTS, 1, D_MODEL))


def _moe_unsort_kernel(seg_s, goff_s, nch_s, ntot_s,
                       dst_ref, gk_ref, x1_ref, g2p_ref, g2s_ref, yb_hbm,
                       yp_ref, ys_ref, ybuf, sem, *, tm, n_rows, n_prompt_tiles):
    i = pl.program_id(0)
    nt = pl.num_programs(0)
    slot = i % 2

    def rows_copy(buf, src_row, dst_row, n_chunks):
        return pltpu.make_async_copy(
            yb_hbm.at[pl.ds(pl.multiple_of(src_row, ROW_CHUNK), n_chunks * ROW_CHUNK)],
            ybuf.at[buf, pl.ds(pl.multiple_of(dst_row, ROW_CHUNK), n_chunks * ROW_CHUNK)],
            sem.at[buf])

    def fetch(tile, buf):
        def per_expert(e, carry):
            so = seg_s[tile * N_EXPERTS + e]
            go = goff_s[tile * N_EXPERTS + e]
            _for_row_pieces(
                nch_s[tile * N_EXPERTS + e], SEG_MAX_POW,
                lambda off, n: rows_copy(buf, go + off * ROW_CHUNK, so + off * ROW_CHUNK,
                                         n).start())
            return carry

        lax.fori_loop(0, N_EXPERTS, per_expert, 0)

    @pl.when(i == 0)
    def _():
        ybuf[...] = jnp.zeros(ybuf.shape, ybuf.dtype)
        fetch(0, 0)

    @pl.when(i + 1 < nt)
    def _():
        fetch(i + 1, 1 - slot)

    _for_row_pieces(ntot_s[i], TILE_MAX_POW, lambda off, n: rows_copy(slot, 0, 0, n).wait())

    dst = dst_ref[...]
    gk = gk_ref[...]
    acc = jnp.zeros((tm, D_MODEL), F32)
    for c in range(n_rows // SEL_CHUNK):
        cols = (lax.broadcasted_iota(jnp.int32, (tm, SEL_CHUNK), 1) + c * SEL_CHUNK).astype(F32)
        q = jnp.zeros((tm, SEL_CHUNK), F32)
        for kk in range(TOP_K):
            q = q + jnp.where(cols == dst[:, kk:kk + 1], gk[:, kk:kk + 1], 0.0)
        acc = acc + jnp.dot(q.astype(BF16), ybuf[slot, c * SEL_CHUNK:(c + 1) * SEL_CHUNK, :],
                            preferred_element_type=F32)

    @pl.when(i < n_prompt_tiles)
    def _():
        yp_ref[...] = x1_ref[...] + g2p_ref[...] * acc

    @pl.when(i >= n_prompt_tiles)
    def _():
        ys_ref[...] = x1_ref[...] + g2s_ref[...] * acc


def _moe_unsort(meta, dst_all, gk_all, x1_all, gt2_p, gt2_s, yb, *, tm, n_prompt_tiles):
    t_all = dst_all.shape[0]
    nt = t_all // tm
    n_rows = _sort_rows(tm)
    last_p = n_prompt_tiles - 1
    grid_spec = pltpu.PrefetchScalarGridSpec(
        num_scalar_prefetch=4,
        grid=(nt,),
        in_specs=[pl.BlockSpec((tm, LANES), lambda i, *_: (i, 0)),
                  pl.BlockSpec((tm, LANES), lambda i, *_: (i, 0)),
                  pl.BlockSpec((tm, D_MODEL), lambda i, *_: (i, 0)),
                  pl.BlockSpec((1, D_MODEL), lambda i, *_: (0, 0)),
                  pl.BlockSpec((tm, D_MODEL), lambda i, *_: (0, 0)),
                  pl.BlockSpec(memory_space=pl.ANY)],
        out_specs=[pl.BlockSpec((tm, D_MODEL), lambda i, *_: (jnp.minimum(i, last_p), 0)),
                   pl.BlockSpec((tm, D_MODEL), lambda i, *_: (0, 0))],
        scratch_shapes=[pltpu.VMEM((2, n_rows, D_MODEL), BF16),
                        pltpu.SemaphoreType.DMA((2,))],
    )
    return pl.pallas_call(
        functools.partial(_moe_unsort_kernel, tm=tm, n_rows=n_rows,
                          n_prompt_tiles=n_prompt_tiles),
        grid_spec=grid_spec,
        out_shape=[jax.ShapeDtypeStruct((n_prompt_tiles * tm, D_MODEL), F32),
                   jax.ShapeDtypeStruct((tm, D_MODEL), F32)],
        compiler_params=_cparams(1),
        name="moe_unsort",
    )(meta["seg"], meta["goff"], meta["nch"], meta["ntot"],
      dst_all, gk_all, x1_all, gt2_p, gt2_s, yb)


def _moe_meta(cnt, tm):
    nt = cnt.shape[0]
    cnt = cnt.astype(jnp.int32)
    cnt_pad = (cnt + ROW_CHUNK - 1) // ROW_CHUNK * ROW_CHUNK
    seg = jnp.cumsum(cnt_pad, axis=1) - cnt_pad
    rows_e = jnp.sum(cnt_pad, axis=0)
    region = (rows_e + FFN_BLOCK - 1) // FFN_BLOCK * FFN_BLOCK
    gstart = jnp.cumsum(region) - region
    goff = gstart[None, :] + jnp.cumsum(cnt_pad, axis=0) - cnt_pad
    nblk_e = region // FFN_BLOCK
    blk_end = jnp.cumsum(nblk_e)
    cap = _moe_cap(nt * tm, tm)
    blocks = jnp.arange(cap // FFN_BLOCK, dtype=jnp.int32)
    block_expert = jnp.minimum(
        jnp.sum((blk_end[None, :] <= blocks[:, None]).astype(jnp.int32), axis=1), N_EXPERTS - 1)
    segv = jnp.zeros((nt, 1, LANES), F32).at[:, 0, :N_EXPERTS].set(seg.astype(F32))
    return {
        "seg": seg.reshape(-1), "goff": goff.reshape(-1).astype(jnp.int32),
        "nch": (cnt_pad // ROW_CHUNK).reshape(-1),
        "ntot": jnp.sum(cnt_pad, axis=1) // ROW_CHUNK,
        "tstart": (gstart + rows_e).astype(jnp.int32),
        "tnch": (region - rows_e) // ROW_CHUNK,
        "block_expert": block_expert,
        "n_used": blk_end[-1:].astype(jnp.int32),
        "segv": segv,
    }


def _moe_cap(t_all, tm):
    nt = t_all // tm
    worst = TOP_K * t_all + nt * N_EXPERTS * (ROW_CHUNK - 1) + N_EXPERTS * (FFN_BLOCK - ROW_CHUNK)
    return -(-worst // FFN_BLOCK) * FFN_BLOCK


def _t5_bucket(dist):
    max_exact = NUM_BUCKETS // 2
    d = dist.astype(jnp.int32)
    ratio = (jnp.log(jnp.maximum(d, 1).astype(F32) / max_exact)
             / math.log(MAX_DISTANCE / max_exact))
    large = jnp.minimum(max_exact + (ratio * (NUM_BUCKETS - max_exact)).astype(jnp.int32),
                        NUM_BUCKETS - 1)
    return jnp.where(d < max_exact, d, large)


def _step_bias(tab, dil):
    return tab[_t5_bucket(dil * jnp.arange(ATT_BLK + 1))].astype(F32).T


def _band_table(sb):
    return jnp.concatenate([sb[:, ::-1], jnp.full((HEADS, ATT_BLK - 1), NEG_INF, F32)], axis=1)


def _cache_table(sb, dil):
    on_grid = sb[:, :0:-1]
    if dil == 1:
        return on_grid
    off = jnp.full((HEADS, ATT_BLK, dil - 1), NEG_INF, F32)
    return jnp.concatenate([on_grid[:, :, None], off], axis=2).reshape(HEADS, ATT_BLK * dil)


def kernel(x_prompt, x_sample, cache_kv_w128, cache_kv_w512, cache_kv_w2048, state_pool, c_prompt,
           c_sample, w_ada, b_ada, norm_mix_g, norm_ffn_g, w_in, q_norm_g, k_norm_g, rel_bias,
           w_pool_mix, pool_scale, w_up_attn, w_up_pool, w_out, w_router, b_router, w_gate_up,
           b_gate_up, w_down, b_down):
    assert w_ada.shape[0] == 1, "one layer"
    seq = x_prompt.shape[1]
    n_s = x_sample.shape[0]
    assert x_prompt.shape[0] == 1 and x_sample.shape[1] == 1
    assert seq % (DIL_GROUPS[-1][1] * ATT_BLK * ATT_SUB) == 0 and seq % TM_PROMPT == 0
    assert n_s == TM_SAMPLE
    dils = tuple(d for _, d in DIL_GROUPS)
    caches = (cache_kv_w128, cache_kv_w512, cache_kv_w2048)

    w_in_bf = w_in[0].astype(BF16)
    heads_of = jnp.arange(GROUP_W) // HEAD_DIM
    half_heads = heads_of[:GROUP_W // 2]
    bdiag = (half_heads[:, None] == half_heads[None, :]).astype(BF16)
    qg = (jnp.tile(q_norm_g[0], HEADS) * SCALE).reshape(1, GROUP_W)
    kg = jnp.tile(k_norm_g[0], HEADS).reshape(1, GROUP_W)
    expand = ((jnp.arange(LANES)[:, None] % HEADS == heads_of[None, :])
              & (jnp.arange(LANES)[:, None] < 2 * HEADS)).astype(BF16)
    wr = jnp.zeros((D_MODEL, LANES), F32).at[:, :N_EXPERTS].set(w_router[0])
    wr_hi = wr.astype(BF16)
    wr_lo = (wr - wr_hi.astype(F32)).astype(BF16)
    br = jnp.full((1, LANES), NEG_INF, F32).at[0, :N_EXPERTS].set(b_router[0])
    wts = (w_pool_mix[0].astype(BF16), pool_scale[0].reshape(1, POOL_W),
           w_up_attn[0].astype(BF16), w_up_pool[0].astype(BF16), w_out[0].astype(BF16), expand,
           norm_ffn_g[0].reshape(1, D_MODEL), wr_hi, wr_lo, br)
    g1 = norm_mix_g[0].reshape(1, D_MODEL)

    n_c = 1 + n_s
    c_all = jnp.zeros((-(-n_c // 8) * 8, D_MODEL), F32).at[0:1].set(c_prompt).at[1:n_c].set(c_sample)
    mod = _ada(c_all, w_ada[0], b_ada[0])
    sh1, sc1, gt1, sh2, sc2, gt2 = jnp.split(mod, N_ADA, axis=-1)

    def prow(m):
        return m[0:1]

    def srows(m):
        return m[1:n_c]

    xp = x_prompt[0]
    q_p, k_p, v_p, u_p, sga_p, sgp_p, st_p = _proj(
        xp, g1, prow(sc1), prow(sh1), w_in_bf, bdiag, qg, kg,
        tm=TM_PROMPT, dils=dils, per_row=False)
    xs = x_sample[:, 0]
    ones = (1, 1, 1)
    q_s, _, _, u_s, sga_s, sgp_s, st_s = _proj(
        xs, g1, srows(sc1), srows(sh1), w_in_bf, bdiag, qg, kg,
        tm=TM_SAMPLE, dils=ones, per_row=True)
    pooled_s, pool_state_t = _pool_sample(jnp.transpose(state_pool, (0, 2, 1, 3)), u_s)
    pool_state_s = jnp.transpose(pool_state_t, (0, 2, 1, 3))

    def heads(a):
        return a.astype(F32).reshape(n_s, HEADS, HEAD_DIM)

    o_p, lse_p, o_s, lse_s = [], [], [], []
    for g, (_, d) in enumerate(DIL_GROUPS):
        sb = _step_bias(rel_bias[:, g * HEADS:(g + 1) * HEADS], d)
        cache_t = jnp.transpose(caches[g], (0, 1, 3, 4, 5, 2))
        o, lse, os_g, lses_g = _attn(
            q_p[g], k_p[g], v_p[g], _band_table(sb), d,
            heads(q_s[g]), heads(st_s[g][:, :GROUP_W]), heads(st_s[g][:, GROUP_W:]),
            sb[:, 0:1], _cache_table(sb, d), cache_t)
        o_p.append(o)
        lse_p.append(lse)
        o_s.append(os_g.reshape(n_s, GROUP_W))
        lse_s.append(jnp.zeros((n_s, LANES), F32).at[:, :HEADS].set(lses_g[:, :, 0]))

    nt_p = seq // TM_PROMPT
    t_all = seq + TM_PROMPT
    bufs = _post(xp, o_p, lse_p, (u_p,), sga_p, sgp_p, wts, (prow(gt1), prow(sc2), prow(sh2)),
                 tm=TM_PROMPT, dils=dils, per_row=False, rows_total=t_all, row_block0=0,
                 cnt_tiles=nt_p + 1, cnt_block=None, grid=nt_p, n_valid_steps=None,
                 alias_bufs=None)
    bufs = _post(xs, o_s, lse_s, pooled_s, sga_s, sgp_s, wts,
                 (srows(gt1), srows(sc2), srows(sh2)),
                 tm=TM_SAMPLE, dils=ones, per_row=True, rows_total=t_all,
                 row_block0=seq // TM_SAMPLE, cnt_tiles=nt_p + 1, cnt_block=nt_p,
                 grid=TM_PROMPT // TM_SAMPLE, n_valid_steps=1, alias_bufs=bufs)
    x1_all, h2_all, a_all, idx_all, gk_all, cnt = bufs

    meta = _moe_meta(cnt[:, 0, :N_EXPERTS], TM_PROMPT)
    cap = _moe_cap(t_all, TM_PROMPT)
    xb, dst_all = _moe_sort(meta, a_all, idx_all, h2_all, tm=TM_PROMPT, cap=cap)
    yb = _moe_ffn(meta, xb, w_gate_up[0], b_gate_up[0], w_down[0], b_down[0])
    gt2_s = jnp.zeros((TM_PROMPT, D_MODEL), F32).at[:n_s].set(srows(gt2))
    y_p, y_s = _moe_unsort(meta, dst_all, gk_all, x1_all, prow(gt2), gt2_s, yb,
                           tm=TM_PROMPT, n_prompt_tiles=nt_p)

    def kv_state(st, rows):
        return st.reshape(1, 1, rows, 2, HEADS, HEAD_DIM)

    kv_p = [kv_state(st, st.shape[0]) for st in st_p]
    kv_s = [st.reshape(1, n_s, 1, 2, HEADS, HEAD_DIM) for st in st_s]
    pool_p = u_p[seq - POOL_BUF:].reshape(1, 1, POOL_BUF, POOL_W)
    return (y_p.reshape(1, seq, D_MODEL), y_s[:n_s].reshape(n_s, 1, D_MODEL),
            kv_p[0], kv_p[1], kv_p[2], pool_p, kv_s[0], kv_s[1], kv_s[2], pool_state_s)
```

```python
import functools
import math

import jax
import jax.numpy as jnp
from jax import lax
from jax.experimental import pallas as pl
from jax.experimental.pallas import tpu as pltpu

F32 = jnp.float32
BF16 = jnp.bfloat16

D_MODEL = 1024
HEAD_DIM = 64
HEADS = 8
GROUP_W = HEADS * HEAD_DIM
DIL_GROUPS = ((128, 1), (512, 4), (2048, 16))
N_GROUPS = len(DIL_GROUPS)
QKV_W = N_GROUPS * GROUP_W
ATT_BLK = 128
ATT_SUB = 2
POOL_WINDOWS = (2, 4, 8, 16)
POOL_W = 512
POOL_GW = 128
POOL_BUF = 15
OFF_K, OFF_V = QKV_W, 2 * QKV_W
OFF_U = 3 * QKV_W
OFF_GA = OFF_U + POOL_W
OFF_GP = OFF_GA + D_MODEL
IN_W = OFF_GP + D_MODEL
NUM_BUCKETS = 32
MAX_DISTANCE = 2048
N_EXPERTS = 32
TOP_K = 4
SWIGLU_LIMIT = 7.0
SWIGLU_ALPHA = 1.702
N_ADA = 6
EPS = 1e-6
NEG_INF = -1e30
PAST_LEN = 8192
SCALE = HEAD_DIM ** -0.5

LANES = 128
ROW_CHUNK = 16
TM_PROMPT = 512
TM_SAMPLE = 128
FFN_BLOCK = 512
SEL_CHUNK = 512
VMEM_LIMIT = 56 * 1024 * 1024


def _cparams(n_axes):
    return pltpu.CompilerParams(dimension_semantics=("arbitrary",) * n_axes,
                                vmem_limit_bytes=VMEM_LIMIT)


def _const_spec(shape):
    nd = len(shape)
    return pl.BlockSpec(shape, lambda *_: (0,) * nd)


def _ada_kernel(c_ref, w_ref, b_ref, o_ref):
    c = c_ref[...]
    s = c * jax.nn.sigmoid(c)
    o_ref[...] = jnp.dot(s.astype(BF16), w_ref[...].astype(BF16),
                         preferred_element_type=F32) + b_ref[...]


def _ada(c_all, w_ada, b_ada):
    rows = c_all.shape[0]
    n = w_ada.shape[1]
    tn = 1536
    return pl.pallas_call(
        _ada_kernel,
        grid=(n // tn,),
        in_specs=[pl.BlockSpec((rows, D_MODEL), lambda j: (0, 0)),
                  pl.BlockSpec((D_MODEL, tn), lambda j: (0, j)),
                  pl.BlockSpec((1, tn), lambda j: (0, j))],
        out_specs=pl.BlockSpec((rows, tn), lambda j: (0, j)),
        out_shape=jax.ShapeDtypeStruct((rows, n), F32),
        compiler_params=_cparams(1),
        name="ada",
    )(c_all, w_ada, b_ada.reshape(1, n))


def _proj_kernel(x_ref, g_ref, sc_ref, sh_ref, w_ref, bd_ref, qg_ref, kg_ref,
                 *refs, tm, dils, st_rows):
    q_refs, k_refs, v_refs = refs[0:3], refs[3:6], refs[6:9]
    u_ref, sga_ref, sgp_ref = refs[9:12]
    st_refs = refs[12:15]
    scr = refs[15]

    x = x_ref[...]
    ms = jnp.mean(x * x, axis=-1, keepdims=True)
    h = x * lax.rsqrt(ms + EPS) * g_ref[...] * (1.0 + sc_ref[...]) + sh_ref[...]
    hb = h.astype(BF16)

    def proj(off, width):
        return jnp.dot(hb, w_ref[:, off:off + width], preferred_element_type=F32)

    def head_norm(z, gain_ref):
        zz = (z * z).astype(BF16)
        half = GROUP_W // 2
        ss = jnp.concatenate(
            [jnp.dot(zz[:, :half], bd_ref[...], preferred_element_type=F32),
             jnp.dot(zz[:, half:], bd_ref[...], preferred_element_type=F32)], axis=1)
        return z * lax.rsqrt(ss * (1.0 / HEAD_DIM) + EPS) * gain_ref[...]

    def put(out_ref, val, d):
        if d == 1:
            out_ref[...] = val.astype(out_ref.dtype)
        else:
            for c in range(GROUP_W // LANES):
                scr[c] = val[:, c * LANES:(c + 1) * LANES]
            for r in range(d):
                for c in range(GROUP_W // LANES):
                    col = r * GROUP_W + c * LANES
                    out_ref[:, col:col + LANES] = (
                        scr[c, pl.ds(r, tm // d, stride=d), :].astype(out_ref.dtype))

    for g, d in enumerate(dils):
        qn = head_norm(proj(g * GROUP_W, GROUP_W), qg_ref)
        put(q_refs[g], qn, d)
        kn = head_norm(proj(OFF_K + g * GROUP_W, GROUP_W), kg_ref)
        put(k_refs[g], kn, d)
        v = proj(OFF_V + g * GROUP_W, GROUP_W)
        put(v_refs[g], v, d)
        rb = st_rows[g]
        st_refs[g][:, 0:GROUP_W] = kn[tm - rb:, :]
        st_refs[g][:, GROUP_W:2 * GROUP_W] = v[tm - rb:, :]

    u_ref[...] = proj(OFF_U, POOL_W)
    sga_ref[...] = jax.nn.sigmoid(proj(OFF_GA, D_MODEL)).astype(BF16)
    sgp_ref[...] = jax.nn.sigmoid(proj(OFF_GP, D_MODEL)).astype(BF16)


def _mod_spec(per_row, tm):
    if per_row:
        return pl.BlockSpec((tm, D_MODEL), lambda i: (i, 0))
    return pl.BlockSpec((1, D_MODEL), lambda i: (0, 0))


def _proj(x, g1, sc1, sh1, w_in_bf, bdiag, qg, kg, *, tm, dils, per_row):
    s = x.shape[0]
    nt = s // tm
    wins = tuple(min(w, s) for w, _ in DIL_GROUPS)
    st_rows = tuple(min(tm, w) for w in wins)

    def res_spec(d):
        return pl.BlockSpec((tm // d, d * GROUP_W), lambda i: (i, 0))

    def st_spec(w, rb):
        first = nt - w // rb
        return pl.BlockSpec((rb, 2 * GROUP_W), lambda i: (jnp.maximum(i - first, 0), 0))

    qkv_shapes = [jax.ShapeDtypeStruct((s // d, d * GROUP_W), BF16) for d in dils]
    out_shape = (qkv_shapes * 3
                 + [jax.ShapeDtypeStruct((s, POOL_W), F32),
                    jax.ShapeDtypeStruct((s, D_MODEL), BF16),
                    jax.ShapeDtypeStruct((s, D_MODEL), BF16)]
                 + [jax.ShapeDtypeStruct((w, 2 * GROUP_W), F32) for w in wins])
    out_specs = ([res_spec(d) for d in dils] * 3
                 + [pl.BlockSpec((tm, POOL_W), lambda i: (i, 0)),
                    pl.BlockSpec((tm, D_MODEL), lambda i: (i, 0)),
                    pl.BlockSpec((tm, D_MODEL), lambda i: (i, 0))]
                 + [st_spec(w, rb) for w, rb in zip(wins, st_rows)])
    in_specs = [pl.BlockSpec((tm, D_MODEL), lambda i: (i, 0)),
                _const_spec((1, D_MODEL)),
                _mod_spec(per_row, tm), _mod_spec(per_row, tm),
                pl.BlockSpec((D_MODEL, IN_W), lambda i: (0, 0), pipeline_mode=pl.Buffered(1)),
                _const_spec((GROUP_W // 2, GROUP_W // 2)),
                _const_spec((1, GROUP_W)), _const_spec((1, GROUP_W))]
    outs = pl.pallas_call(
        functools.partial(_proj_kernel, tm=tm, dils=dils, st_rows=st_rows),
        grid=(nt,),
        in_specs=in_specs,
        out_specs=out_specs,
        out_shape=out_shape,
        scratch_shapes=[pltpu.VMEM((GROUP_W // LANES, tm, LANES), F32)],
        compiler_params=_cparams(1),
        name="proj",
    )(x, g1, sc1, sh1, w_in_bf, bdiag, qg, kg)
    return outs[0:3], outs[3:6], outs[6:9], outs[9], outs[10], outs[11], outs[12:15]


def _sample_group_attn(qs_ref, kns_ref, vns_ref, bself_ref, btab_ref, c_ref, os_ref, lses_ref):
    n_tok = qs_ref.shape[0]
    win = btab_ref.shape[1]
    row_w = lax.broadcasted_iota(jnp.int32, (HEADS, win), 0)
    row_e = lax.broadcasted_iota(jnp.int32, (HEADS, HEAD_DIM), 0)
    ss, s0s = [], []
    for t in range(n_tok):
        q = qs_ref[t]
        qb = q.astype(BF16)
        s = jnp.zeros((HEADS, win), F32)
        for h in range(HEADS):
            sh = jnp.dot(qb, c_ref[0, t, 0, h].astype(BF16), preferred_element_type=F32)
            s = jnp.where(row_w == h, sh, s)
        ss.append(s + btab_ref[...])
        s0s.append(jnp.sum(q * kns_ref[t], axis=-1, keepdims=True) + bself_ref[...])
    s = jnp.concatenate(ss, axis=0)
    s0 = jnp.concatenate(s0s, axis=0)
    m = jnp.maximum(jnp.max(s, axis=-1, keepdims=True), s0)
    p = jnp.exp(s - m)
    p0 = jnp.exp(s0 - m)
    l = jnp.sum(p, axis=-1, keepdims=True) + p0
    pb = p.astype(BF16)
    lse = m + jnp.log(l)
    for t in range(n_tok):
        rows = slice(t * HEADS, (t + 1) * HEADS)
        o = jnp.zeros((HEADS, HEAD_DIM), F32)
        for h in range(HEADS):
            oh = lax.dot_general(pb[rows], c_ref[0, t, 1, h].astype(BF16),
                                 (((1,), (1,)), ((), ())), preferred_element_type=F32)
            o = jnp.where(row_e == h, oh, o)
        os_ref[t] = (o + p0[rows] * vns_ref[t]) / l[rows]
        lses_ref[t] = lse[rows]


SAMPLE_IN = 6
SAMPLE_HOST = ((), (1,), (2, 0))


def _attn_kernel(q_ref, kp_ref, kc_ref, vp_ref, vc_ref, r_ref, *refs, n_sample_groups):
    n_in = SAMPLE_IN * n_sample_groups
    o_ref, lse_ref = refs[n_in], refs[n_in + 1]
    bias_ref = refs[-1]
    i = pl.program_id(1)

    for sg in range(n_sample_groups):
        _sample_group_attn(*refs[SAMPLE_IN * sg:SAMPLE_IN * (sg + 1)],
                           *refs[n_in + 2 + 2 * sg:n_in + 4 + 2 * sg])

    @pl.when((pl.program_id(0) == 0) & (i == 0))
    def _():
        for h in range(HEADS):
            row = jnp.broadcast_to(r_ref[h:h + 1, :], (ATT_BLK, 2 * ATT_BLK))
            bias_ref[h] = pltpu.roll(row, 0, 1, stride=1, stride_axis=0)

    q = q_ref[...]
    k = jnp.concatenate([kp_ref[...], kc_ref[...]], axis=0)
    v = jnp.concatenate([vp_ref[...], vc_ref[...]], axis=0)
    col = lax.broadcasted_iota(jnp.int32, (ATT_BLK, 2 * ATT_BLK), 1)
    no_prev = jnp.where((col < ATT_BLK) & (i == 0), NEG_INF, 0.0)
    lane_q = lax.broadcasted_iota(jnp.int32, (ATT_SUB * ATT_BLK, LANES), 1)
    lane_v = lax.broadcasted_iota(jnp.int32, ((ATT_SUB + 1) * ATT_BLK, LANES), 1)

    def pair(h):
        return slice((h // 2) * LANES, (h // 2 + 1) * LANES)

    def mine(lane, h):
        return (lane < HEAD_DIM) == (h % 2 == 0)

    ss = []
    for h in range(HEADS):
        q2 = q[:, pair(h)]
        qm = jnp.where(mine(lane_q, h), q2, jnp.zeros_like(q2))
        k2 = k[:, pair(h)]
        for j in range(ATT_SUB):
            s = lax.dot_general(qm[j * ATT_BLK:(j + 1) * ATT_BLK], k2[j * ATT_BLK:(j + 2) * ATT_BLK],
                                (((1,), (1,)), ((), ())), preferred_element_type=F32)
            s = s + bias_ref[h]
            ss.append(s + no_prev if j == 0 else s)
    s = jnp.concatenate(ss, axis=0)
    m = jnp.max(s, axis=-1, keepdims=True)
    p = jnp.exp(s - m)
    l = jnp.sum(p, axis=-1, keepdims=True)
    pb = p.astype(BF16)
    lse = m + jnp.log(l)
    inv_l = 1.0 / l
    outs, lses = [], []
    for h in range(HEADS):
        v2 = v[:, pair(h)]
        vm = jnp.where(mine(lane_v, h), v2, jnp.zeros_like(v2))
        o_sub, lse_sub = [], []
        for j in range(ATT_SUB):
            rows = slice((h * ATT_SUB + j) * ATT_BLK, (h * ATT_SUB + j + 1) * ATT_BLK)
            o_sub.append(jnp.dot(pb[rows], vm[j * ATT_BLK:(j + 2) * ATT_BLK],
                                 preferred_element_type=F32) * inv_l[rows])
            lse_sub.append(lse[rows])
        o = jnp.concatenate(o_sub, axis=0)
        if h % 2 == 0:
            outs.append(o)
        else:
            outs[-1] = outs[-1] + o
        lses.append(jnp.concatenate(lse_sub, axis=0))
    o_ref[...] = jnp.concatenate(outs, axis=-1).astype(o_ref.dtype)
    lse_ref[...] = jnp.concatenate(
        lses + [jnp.zeros((ATT_SUB * ATT_BLK, LANES - HEADS), F32)], axis=-1)


def _attn(q, k, v, r_tab, d, sample_groups):
    rows = q.shape[0]
    step = ATT_SUB * ATT_BLK
    nblk = rows // step
    cur = pl.BlockSpec((step, GROUP_W), lambda r, i: (i, r))
    prev = pl.BlockSpec((ATT_BLK, GROUP_W), lambda r, i: (jnp.maximum(i * ATT_SUB - 1, 0), r))
    in_specs = [cur, prev, cur, prev, cur,
                pl.BlockSpec((HEADS, 2 * ATT_BLK), lambda r, i: (0, 0))]
    out_specs = [pl.BlockSpec((step, GROUP_W), lambda r, i: (i, r)),
                 pl.BlockSpec((step, LANES), lambda r, i: (i, r))]
    out_shape = [jax.ShapeDtypeStruct((rows, d * GROUP_W), BF16),
                 jax.ShapeDtypeStruct((rows, d * LANES), F32)]
    args = [q, k, k, v, v, r_tab]
    for qs, kns, vns, bself, btab, cache_t in sample_groups:
        n_tok = qs.shape[0]
        assert n_tok % (d * nblk) == 0
        tok = n_tok // (d * nblk)
        tok_spec = pl.BlockSpec((tok, HEADS, HEAD_DIM), lambda r, i: (r * nblk + i, 0, 0))
        in_specs += [tok_spec, tok_spec, tok_spec,
                     pl.BlockSpec(bself.shape, lambda r, i: (0, 0)),
                     pl.BlockSpec(btab.shape, lambda r, i: (0, 0)),
                     pl.BlockSpec((1, tok) + cache_t.shape[2:],
                                  lambda r, i: (0, r * nblk + i, 0, 0, 0, 0))]
        out_specs += [tok_spec, pl.BlockSpec((tok, HEADS, 1), lambda r, i: (r * nblk + i, 0, 0))]
        out_shape += [jax.ShapeDtypeStruct((n_tok, HEADS, HEAD_DIM), F32),
                      jax.ShapeDtypeStruct((n_tok, HEADS, 1), F32)]
        args += [qs, kns, vns, bself, btab, cache_t]
    outs = pl.pallas_call(
        functools.partial(_attn_kernel, n_sample_groups=len(sample_groups)),
        grid=(d, nblk),
        in_specs=in_specs,
        out_specs=out_specs,
        out_shape=out_shape,
        scratch_shapes=[pltpu.VMEM((HEADS, ATT_BLK, 2 * ATT_BLK), F32)],
        compiler_params=_cparams(2),
        name=f"attn_d{d}",
    )(*args)
    return outs[0], outs[1], [(outs[2 + 2 * j], outs[3 + 2 * j]) for j in range(len(sample_groups))]


def _pool_sample_kernel(st_ref, u_ref, pooled_ref, new_ref):
    u = u_ref[...]
    rows = [st_ref[0, j] for j in range(POOL_BUF)]
    outs = []
    for g, w in enumerate(POOL_WINDOWS):
        sl = slice(g * POOL_GW, (g + 1) * POOL_GW)
        acc = u[:, sl]
        for j in range(POOL_BUF - (w - 1), POOL_BUF):
            acc = acc + rows[j][:, sl]
        outs.append(acc / float(w) - u[:, sl])
    pooled_ref[...] = jnp.concatenate(outs, axis=-1)
    for j in range(POOL_BUF - 1):
        new_ref[0, j] = rows[j + 1]
    new_ref[0, POOL_BUF - 1] = u


def _pool_sample(state, u):
    n = u.shape[0]
    return pl.pallas_call(
        _pool_sample_kernel,
        grid=(1,),
        in_specs=[_const_spec(state.shape), _const_spec(u.shape)],
        out_specs=[_const_spec(u.shape), _const_spec(state.shape)],
        out_shape=[jax.ShapeDtypeStruct((n, POOL_W), F32),
                   jax.ShapeDtypeStruct(state.shape, F32)],
        compiler_params=_cparams(1),
        name="pool_sample",
    )(state, u)


def _post_kernel(*refs, tm, dils, pooled_given, n_valid_steps, aliased):
    it = iter(refs)
    x_ref = next(it)
    o_refs = [next(it) for _ in range(N_GROUPS)]
    lse_refs = [next(it) for _ in range(N_GROUPS)]
    if pooled_given:
        pooled_ref = next(it)
    else:
        u_ref, uh_ref = next(it), next(it)
    sga_ref, sgp_ref = next(it), next(it)
    wpm_ref, psc_ref, wua_ref, wup_ref, wout_ref, exp_ref = (next(it) for _ in range(6))
    gt1_ref, g2_ref, sc2_ref, sh2_ref = (next(it) for _ in range(4))
    wrh_ref, wrl_ref, br_ref = (next(it) for _ in range(3))
    if aliased:
        for _ in range(6):
            next(it)
    x1_ref, h2_ref, a_ref, idx_ref, gk_ref, cnt_ref = (next(it) for _ in range(6))
    ob_scr, ls_scr = next(it), next(it)

    i = pl.program_id(0)

    def compute():
        obs, lss = [], []
        for g, d in enumerate(dils):
            if d == 1:
                obs.append(o_refs[g][...].astype(F32))
                lss.append(lse_refs[g][...])
            else:
                for r in range(d):
                    for c in range(GROUP_W // LANES):
                        col = r * GROUP_W + c * LANES
                        ob_scr[c, pl.ds(r, tm // d, stride=d), :] = (
                            o_refs[g][:, col:col + LANES].astype(F32))
                    ls_scr[pl.ds(r, tm // d, stride=d), :] = (
                        lse_refs[g][:, r * LANES:(r + 1) * LANES])
                obs.append(jnp.concatenate([ob_scr[c] for c in range(GROUP_W // LANES)],
                                           axis=-1))
                lss.append(ls_scr[...])
        mx = jnp.maximum(jnp.maximum(lss[0], lss[1]), lss[2])
        es = [jnp.exp(l - mx) for l in lss]
        den = es[0] + es[1] + es[2]
        attn_o = jnp.zeros((tm, GROUP_W), F32)
        head_lane = lax.broadcasted_iota(jnp.int32, (tm, LANES), 1) < HEADS
        for g in range(N_GROUPS):
            w = jnp.where(head_lane, es[g] / den, 0.0)
            w_hi = w.astype(BF16).astype(F32)
            w_lo = (w - w_hi).astype(BF16).astype(F32)
            lhs = (w_hi + pltpu.roll(w_lo, HEADS, 1)).astype(BF16)
            wexp = jnp.dot(lhs, exp_ref[...], preferred_element_type=F32)
            attn_o = attn_o + wexp * obs[g]

        if pooled_given:
            pooled = pooled_ref[...]
        else:
            u = u_ref[...]
            halo = jnp.where(i == 0, 0.0, uh_ref[...])
            pos = (lax.broadcasted_iota(jnp.int32, (tm, 1), 0) + i * tm + 1).astype(F32)
            outs = []
            for g, w in enumerate(POOL_WINDOWS):
                sl = slice(g * POOL_GW, (g + 1) * POOL_GW)
                a = jnp.concatenate([halo[:, sl], u[:, sl]], axis=0)
                span = 1
                while span < w:
                    n = a.shape[0] - span
                    a = a[span:, :] + a[:n, :]
                    span *= 2
                off = a.shape[0] - tm
                win_sum = a[off:, :]
                outs.append(win_sum / jnp.minimum(pos, float(w)) - u[:, sl])
            pooled = jnp.concatenate(outs, axis=-1)
        pool_parts = []
        for g in range(len(POOL_WINDOWS)):
            sl = slice(g * POOL_GW, (g + 1) * POOL_GW)
            pool_parts.append(jnp.dot(pooled[:, sl].astype(BF16), wpm_ref[g],
                                      preferred_element_type=F32))
        pool_o = jnp.concatenate(pool_parts, axis=-1) * psc_ref[...]

        up_a = jnp.dot(attn_o.astype(BF16), wua_ref[...], preferred_element_type=F32)
        up_p = jnp.dot(pool_o.astype(BF16), wup_ref[...], preferred_element_type=F32)
        merged = sga_ref[...].astype(F32) * up_a + sgp_ref[...].astype(F32) * up_p
        mo = jnp.dot(merged.astype(BF16), wout_ref[...], preferred_element_type=F32)
        x1 = x_ref[...] + gt1_ref[...] * mo
        x1_ref[...] = x1

        ms = jnp.mean(x1 * x1, axis=-1, keepdims=True)
        h2 = x1 * lax.rsqrt(ms + EPS) * g2_ref[...] * (1.0 + sc2_ref[...]) + sh2_ref[...]
        h2_hi = h2.astype(BF16)
        h2_ref[...] = h2_hi
        h2_lo = (h2 - h2_hi.astype(F32)).astype(BF16)
        logits = (jnp.dot(h2_hi, wrh_ref[...], preferred_element_type=F32)
                  + jnp.dot(h2_lo, wrh_ref[...], preferred_element_type=F32)
                  + jnp.dot(h2_hi, wrl_ref[...], preferred_element_type=F32)
                  + br_ref[...])
        lane = lax.broadcasted_iota(jnp.int32, (tm, LANES), 1).astype(F32)
        work = logits
        vals, ids = [], []
        for _ in range(TOP_K):
            m = jnp.max(work, axis=-1, keepdims=True)
            ik = jnp.min(jnp.where(work == m, lane, float(LANES)), axis=-1, keepdims=True)
            vals.append(m)
            ids.append(ik)
            work = jnp.where(lane == ik, -3e38, work)
        ex = [jnp.exp(v - vals[0]) for v in vals]
        den_k = ex[0] + ex[1] + ex[2] + ex[3]
        a = jnp.zeros((tm, LANES), F32)
        idx = jnp.zeros((tm, LANES), F32)
        gk = jnp.zeros((tm, LANES), F32)
        for kk in range(TOP_K):
            gate = ex[kk] / den_k
            a = a + jnp.where(lane == ids[kk], gate, 0.0)
            idx = jnp.where(lane == float(kk), ids[kk], idx)
            gk = jnp.where(lane == float(kk), gate, gk)
        a_ref[...] = a
        idx_ref[...] = idx
        gk_ref[...] = gk
        cnt = jnp.sum((a > 0.0).astype(F32), axis=0, keepdims=True)
        row = lax.broadcasted_iota(jnp.int32, (8, LANES), 0)
        cnt_ref[0] = jnp.where(row == 0, jnp.broadcast_to(cnt, (8, LANES)), 0.0)

    if n_valid_steps is None:
        compute()
    else:
        pl.when(i < n_valid_steps)(compute)

        @pl.when(i >= n_valid_steps)
        def _():
            x1_ref[...] = jnp.zeros(x1_ref.shape, x1_ref.dtype)
            h2_ref[...] = jnp.zeros(h2_ref.shape, h2_ref.dtype)
            a_ref[...] = jnp.zeros(a_ref.shape, a_ref.dtype)
            idx_ref[...] = jnp.zeros(idx_ref.shape, idx_ref.dtype)
            gk_ref[...] = jnp.zeros(gk_ref.shape, gk_ref.dtype)


def _post(x, o_list, lse_list, pool_in, sga, sgp, wts, mods, *, tm, dils, per_row,
          rows_total, row_block0, cnt_tiles, cnt_block, grid, n_valid_steps, alias_bufs):
    pooled_given = not isinstance(pool_in, tuple)
    nv = grid if n_valid_steps is None else n_valid_steps

    def clamp(i):
        return jnp.minimum(i, nv - 1)

    def tile_spec(width):
        return pl.BlockSpec((tm, width), lambda i: (clamp(i), 0))

    in_specs = [tile_spec(D_MODEL)]
    in_specs += [pl.BlockSpec((tm // d, d * GROUP_W), lambda i: (clamp(i), 0)) for d in dils]
    in_specs += [pl.BlockSpec((tm // d, d * LANES), lambda i: (clamp(i), 0)) for d in dils]
    args = [x, *o_list, *lse_list]
    if pooled_given:
        in_specs.append(tile_spec(POOL_W))
        args.append(pool_in)
    else:
        u = pool_in[0]
        in_specs += [tile_spec(POOL_W),
                     pl.BlockSpec((16, POOL_W),
                                  lambda i: (jnp.maximum(i * (tm // 16) - 1, 0), 0))]
        args += [u, u]
    in_specs += [tile_spec(D_MODEL), tile_spec(D_MODEL)]
    args += [sga, sgp]
    wpm, psc, wua, wup, wout, expand, g2, wrh, wrl, br = wts
    gt1, sc2, sh2 = mods

    def mspec():
        if per_row:
            return pl.BlockSpec((tm, D_MODEL), lambda i: (clamp(i), 0))
        return _const_spec((1, D_MODEL))

    in_specs += [_const_spec(wpm.shape), _const_spec(psc.shape), _const_spec(wua.shape),
                 _const_spec(wup.shape), _const_spec(wout.shape), _const_spec(expand.shape),
                 mspec(), _const_spec(g2.shape), mspec(), mspec(),
                 _const_spec(wrh.shape), _const_spec(wrl.shape), _const_spec(br.shape)]
    args += [wpm, psc, wua, wup, wout, expand, gt1, g2, sc2, sh2, wrh, wrl, br]
    aliases = {}
    if alias_bufs is not None:
        base = len(args)
        in_specs += [pl.BlockSpec(memory_space=pl.ANY)] * 6
        args += list(alias_bufs)
        aliases = {base + j: j for j in range(6)}

    def out_spec(width):
        return pl.BlockSpec((tm, width), lambda i: (row_block0 + i, 0))

    out_specs = [out_spec(D_MODEL), out_spec(D_MODEL), out_spec(LANES), out_spec(LANES),
                 out_spec(LANES),
                 pl.BlockSpec((1, 8, LANES),
                              lambda i: (cnt_block if cnt_block is not None else i, 0, 0))]
    out_shape = [jax.ShapeDtypeStruct((rows_total, D_MODEL), F32),
                 jax.ShapeDtypeStruct((rows_total, D_MODEL), BF16),
                 jax.ShapeDtypeStruct((rows_total, LANES), F32),
                 jax.ShapeDtypeStruct((rows_total, LANES), F32),
                 jax.ShapeDtypeStruct((rows_total, LANES), F32),
                 jax.ShapeDtypeStruct((cnt_tiles, 8, LANES), F32)]
    return pl.pallas_call(
        functools.partial(_post_kernel, tm=tm, dils=dils, pooled_given=pooled_given,
                          n_valid_steps=n_valid_steps, aliased=alias_bufs is not None),
        grid=(grid,),
        in_specs=in_specs,
        out_specs=out_specs,
        out_shape=out_shape,
        scratch_shapes=[pltpu.VMEM((GROUP_W // LANES, tm, LANES), F32),
                        pltpu.VMEM((tm, LANES), F32)],
        input_output_aliases=aliases,
        compiler_params=_cparams(1),
        name="post_sample" if per_row else "post",
    )(*args)


def _sort_rows(tm):
    return -(-(TOP_K * tm + N_EXPERTS * (ROW_CHUNK - 1)) // SEL_CHUNK) * SEL_CHUNK


def _for_row_pieces(n_chunks, max_pow, fn):
    big = 1 << max_pow

    def body(c, carry):
        fn(c * big, big)
        return carry

    lax.fori_loop(0, n_chunks >> max_pow, body, 0)
    for pw in range(max_pow - 1, -1, -1):
        @pl.when(((n_chunks >> pw) & 1) == 1)
        def _(pw=pw):
            fn((n_chunks >> (pw + 1)) << (pw + 1), 1 << pw)


SEG_MAX_POW = 3
TILE_MAX_POW = 5


def _moe_sort_kernel(seg_s, goff_s, nch_s, ntot_s, tstart_s, tnch_s,
                     a_ref, idx_ref, h2_ref, segv_ref, lt_ref,
                     xb_hbm, dst_ref, xs_scr, zero_scr, sem, *, tm, n_rows):
    i = pl.program_id(0)
    nt = pl.num_programs(0)
    slot = i % 2
    sel = a_ref[...] > 0.0
    ahead = jnp.dot(lt_ref[...], sel.astype(BF16), preferred_element_type=F32)
    slot1 = jnp.where(sel, segv_ref[0] + ahead + 1.0, 0.0)
    lane = lax.broadcasted_iota(jnp.int32, (tm, LANES), 1).astype(F32)
    idx = idx_ref[...]
    dst = jnp.full((tm, LANES), -1.0, F32)
    for kk in range(TOP_K):
        hit = lane == idx[:, kk:kk + 1]
        dk = jnp.sum(jnp.where(hit, slot1, 0.0), axis=-1, keepdims=True) - 1.0
        dst = jnp.where(lane == float(kk), dk, dst)
    dst_ref[...] = dst
    dst_t = dst.T
    h2 = h2_ref[...]
    for c in range(n_rows // SEL_CHUNK):
        rows = (lax.broadcasted_iota(jnp.int32, (SEL_CHUNK, tm), 0) + c * SEL_CHUNK).astype(F32)
        p = rows == dst_t[0:1, :]
        for kk in range(1, TOP_K):
            p = p | (rows == dst_t[kk:kk + 1, :])
        xs = jnp.dot(jnp.where(p, 1.0, 0.0).astype(BF16), h2, preferred_element_type=F32)
        xs_scr[slot, c * SEL_CHUNK:(c + 1) * SEL_CHUNK, :] = xs.astype(BF16)

    def rows_copy(buf, src_row, dst_row, n_chunks):
        return pltpu.make_async_copy(
            xs_scr.at[buf, pl.ds(pl.multiple_of(src_row, ROW_CHUNK), n_chunks * ROW_CHUNK)],
            xb_hbm.at[pl.ds(pl.multiple_of(dst_row, ROW_CHUNK), n_chunks * ROW_CHUNK)],
            sem.at[buf])

    def per_expert(e, carry):
        so = seg_s[i * N_EXPERTS + e]
        go = goff_s[i * N_EXPERTS + e]
        _for_row_pieces(
            nch_s[i * N_EXPERTS + e], SEG_MAX_POW,
            lambda off, n: rows_copy(slot, so + off * ROW_CHUNK, go + off * ROW_CHUNK, n).start())
        return carry

    lax.fori_loop(0, N_EXPERTS, per_expert, 0)

    def drain(buf, tile):
        _for_row_pieces(ntot_s[tile], TILE_MAX_POW, lambda off, n: rows_copy(buf, 0, 0, n).wait())

    @pl.when(i > 0)
    def _():
        drain(1 - slot, i - 1)

    @pl.when(i == nt - 1)
    def _():
        drain(slot, i)
        zero_scr[...] = jnp.zeros(zero_scr.shape, zero_scr.dtype)

        def tail_copy(dst_row):
            return pltpu.make_async_copy(
                zero_scr, xb_hbm.at[pl.ds(pl.multiple_of(dst_row, ROW_CHUNK), ROW_CHUNK)],
                sem.at[2])

        def per_expert_tail(e, carry):
            def per_chunk(c, carry2):
                tail_copy(tstart_s[e] + c * ROW_CHUNK).start()
                return carry2

            lax.fori_loop(0, tnch_s[e], per_chunk, 0)

            def wait_chunk(c, carry2):
                tail_copy(0).wait()
                return carry2

            return lax.fori_loop(0, tnch_s[e], wait_chunk, carry)

        lax.fori_loop(0, N_EXPERTS, per_expert_tail, 0)


def _moe_sort(meta, a_all, idx_all, h2_all, *, tm, cap):
    t_all = a_all.shape[0]
    nt = t_all // tm
    n_rows = _sort_rows(tm)
    lt = jnp.tril(jnp.ones((tm, tm), BF16), -1)
    grid_spec = pltpu.PrefetchScalarGridSpec(
        num_scalar_prefetch=6,
        grid=(nt,),
        in_specs=[pl.BlockSpec((tm, LANES), lambda i, *_: (i, 0)),
                  pl.BlockSpec((tm, LANES), lambda i, *_: (i, 0)),
                  pl.BlockSpec((tm, D_MODEL), lambda i, *_: (i, 0)),
                  pl.BlockSpec((1, 1, LANES), lambda i, *_: (i, 0, 0)),
                  pl.BlockSpec((tm, tm), lambda i, *_: (0, 0))],
        out_specs=[pl.BlockSpec(memory_space=pl.ANY),
                   pl.BlockSpec((tm, LANES), lambda i, *_: (i, 0))],
        scratch_shapes=[pltpu.VMEM((2, n_rows, D_MODEL), BF16),
                        pltpu.VMEM((ROW_CHUNK, D_MODEL), BF16),
                        pltpu.SemaphoreType.DMA((3,))],
    )
    return pl.pallas_call(
        functools.partial(_moe_sort_kernel, tm=tm, n_rows=n_rows),
        grid_spec=grid_spec,
        out_shape=[jax.ShapeDtypeStruct((cap, D_MODEL), BF16),
                   jax.ShapeDtypeStruct((t_all, LANES), F32)],
        compiler_params=_cparams(1),
        name="moe_sort",
    )(meta["seg"], meta["goff"], meta["nch"], meta["ntot"], meta["tstart"], meta["tnch"],
      a_all, idx_all, h2_all, meta["segv"], lt)


def _moe_ffn_kernel(be_s, nused_s, x_ref, wgu_ref, bgu_ref, wd_ref, bd_ref, y_ref,
                    wgu_bf, wd_bf):
    b = pl.program_id(0)

    @pl.when(b < nused_s[0])
    def _():
        e = be_s[b]
        e_prev = be_s[jnp.maximum(b - 1, 0)]

        @pl.when((b == 0) | (e != e_prev))
        def _():
            wgu_bf[...] = wgu_ref[0].astype(BF16)
            wd_bf[...] = wd_ref[0].astype(BF16)

        hgu = jnp.dot(x_ref[...], wgu_bf[...], preferred_element_type=F32) + bgu_ref[0]
        d_ff = hgu.shape[1] // 2
        hg = jnp.minimum(hgu[:, :d_ff], SWIGLU_LIMIT)
        hu = jnp.clip(hgu[:, d_ff:], -SWIGLU_LIMIT, SWIGLU_LIMIT)
        act = hg * jax.nn.sigmoid(SWIGLU_ALPHA * hg) * (hu + 1.0)
        y = jnp.dot(act.astype(BF16), wd_bf[...], preferred_element_type=F32) + bd_ref[0]
        y_ref[...] = y.astype(y_ref.dtype)


def _moe_ffn(meta, xb, w_gate_up, b_gate_up, w_down, b_down):
    cap = xb.shape[0]
    nb = cap // FFN_BLOCK
    d_ff2 = w_gate_up.shape[2]

    def blk(b, be, nu):
        return jnp.minimum(b, jnp.maximum(nu[0] - 1, 0))

    grid_spec = pltpu.PrefetchScalarGridSpec(
        num_scalar_prefetch=2,
        grid=(nb,),
        in_specs=[pl.BlockSpec((FFN_BLOCK, D_MODEL), lambda b, be, nu: (blk(b, be, nu), 0)),
                  pl.BlockSpec((1, D_MODEL, d_ff2), lambda b, be, nu: (be[blk(b, be, nu)], 0, 0)),
                  pl.BlockSpec((1, 1, d_ff2), lambda b, be, nu: (be[blk(b, be, nu)], 0, 0)),
                  pl.BlockSpec((1, d_ff2 // 2, D_MODEL),
                               lambda b, be, nu: (be[blk(b, be, nu)], 0, 0)),
                  pl.BlockSpec((1, 1, D_MODEL), lambda b, be, nu: (be[blk(b, be, nu)], 0, 0))],
        out_specs=pl.BlockSpec((FFN_BLOCK, D_MODEL), lambda b, be, nu: (blk(b, be, nu), 0)),
        scratch_shapes=[pltpu.VMEM((D_MODEL, d_ff2), BF16),
                        pltpu.VMEM((d_ff2 // 2, D_MODEL), BF16)],
    )
    return pl.pallas_call(
        _moe_ffn_kernel,
        grid_spec=grid_spec,
        out_shape=jax.ShapeDtypeStruct((cap, D_MODEL), BF16),
        compiler_params=_cparams(1),
        name="moe_ffn",
    )(meta["block_expert"], meta["n_used"], xb, w_gate_up,
      b_gate_up.reshape(N_EXPERTS, 1, d_ff2), w_down, b_down.reshape(N_EXPERTS, 1, D_MODEL))


def _moe_unsort_kernel(seg_s, goff_s, nch_s, ntot_s,
                       dst_ref, gk_ref, x1_ref, g2p_ref, g2s_ref, yb_hbm,
                       yp_ref, ys_ref, ybuf, sem, *, tm, n_rows, n_prompt_tiles):
    i = pl.program_id(0)
    nt = pl.num_programs(0)
    slot = i % 2

    def rows_copy(buf, src_row, dst_row, n_chunks):
        return pltpu.make_async_copy(
            yb_hbm.at[pl.ds(pl.multiple_of(src_row, ROW_CHUNK), n_chunks * ROW_CHUNK)],
            ybuf.at[buf, pl.ds(pl.multiple_of(dst_row, ROW_CHUNK), n_chunks * ROW_CHUNK)],
            sem.at[buf])

    def fetch(tile, buf):
        def per_expert(e, carry):
            so = seg_s[tile * N_EXPERTS + e]
            go = goff_s[tile * N_EXPERTS + e]
            _for_row_pieces(
                nch_s[tile * N_EXPERTS + e], SEG_MAX_POW,
                lambda off, n: rows_copy(buf, go + off * ROW_CHUNK, so + off * ROW_CHUNK,
                                         n).start())
            return carry

        lax.fori_loop(0, N_EXPERTS, per_expert, 0)

    @pl.when(i == 0)
    def _():
        ybuf[...] = jnp.zeros(ybuf.shape, ybuf.dtype)
        fetch(0, 0)

    @pl.when(i + 1 < nt)
    def _():
        fetch(i + 1, 1 - slot)

    _for_row_pieces(ntot_s[i], TILE_MAX_POW, lambda off, n: rows_copy(slot, 0, 0, n).wait())

    dst = dst_ref[...]
    gk = gk_ref[...]
    acc = jnp.zeros((tm, D_MODEL), F32)
    for c in range(n_rows // SEL_CHUNK):
        cols = (lax.broadcasted_iota(jnp.int32, (tm, SEL_CHUNK), 1) + c * SEL_CHUNK).astype(F32)
        q = jnp.zeros((tm, SEL_CHUNK), F32)
        for kk in range(TOP_K):
            q = jnp.where(cols == dst[:, kk:kk + 1], gk[:, kk:kk + 1], q)
        acc = acc + jnp.dot(q.astype(BF16), ybuf[slot, c * SEL_CHUNK:(c + 1) * SEL_CHUNK, :],
                            preferred_element_type=F32)

    @pl.when(i < n_prompt_tiles)
    def _():
        yp_ref[...] = x1_ref[...] + g2p_ref[...] * acc

    @pl.when(i >= n_prompt_tiles)
    def _():
        ys_ref[...] = x1_ref[...] + g2s_ref[...] * acc


def _moe_unsort(meta, dst_all, gk_all, x1_all, gt2_p, gt2_s, yb, *, tm, n_prompt_tiles):
    t_all = dst_all.shape[0]
    nt = t_all // tm
    n_rows = _sort_rows(tm)
    last_p = n_prompt_tiles - 1
    grid_spec = pltpu.PrefetchScalarGridSpec(
        num_scalar_prefetch=4,
        grid=(nt,),
        in_specs=[pl.BlockSpec((tm, LANES), lambda i, *_: (i, 0)),
                  pl.BlockSpec((tm, LANES), lambda i, *_: (i, 0)),
                  pl.BlockSpec((tm, D_MODEL), lambda i, *_: (i, 0)),
                  pl.BlockSpec((1, D_MODEL), lambda i, *_: (0, 0)),
                  pl.BlockSpec((tm, D_MODEL), lambda i, *_: (0, 0)),
                  pl.BlockSpec(memory_space=pl.ANY)],
        out_specs=[pl.BlockSpec((tm, D_MODEL), lambda i, *_: (jnp.minimum(i, last_p), 0)),
                   pl.BlockSpec((tm, D_MODEL), lambda i, *_: (0, 0))],
        scratch_shapes=[pltpu.VMEM((2, n_rows, D_MODEL), BF16),
                        pltpu.SemaphoreType.DMA((2,))],
    )
    return pl.pallas_call(
        functools.partial(_moe_unsort_kernel, tm=tm, n_rows=n_rows,
                          n_prompt_tiles=n_prompt_tiles),
        grid_spec=grid_spec,
        out_shape=[jax.ShapeDtypeStruct((n_prompt_tiles * tm, D_MODEL), F32),
                   jax.ShapeDtypeStruct((tm, D_MODEL), F32)],
        compiler_params=_cparams(1),
        name="moe_unsort",
    )(meta["seg"], meta["goff"], meta["nch"], meta["ntot"],
      dst_all, gk_all, x1_all, gt2_p, gt2_s, yb)


def _moe_meta(cnt, tm):
    nt = cnt.shape[0]
    cnt = cnt.astype(jnp.int32)
    cnt_pad = (cnt + ROW_CHUNK - 1) // ROW_CHUNK * ROW_CHUNK
    seg = jnp.cumsum(cnt_pad, axis=1) - cnt_pad
    rows_e = jnp.sum(cnt_pad, axis=0)
    region = (rows_e + FFN_BLOCK - 1) // FFN_BLOCK * FFN_BLOCK
    gstart = jnp.cumsum(region) - region
    goff = gstart[None, :] + jnp.cumsum(cnt_pad, axis=0) - cnt_pad
    nblk_e = region // FFN_BLOCK
    blk_end = jnp.cumsum(nblk_e)
    cap = _moe_cap(nt * tm, tm)
    blocks = jnp.arange(cap // FFN_BLOCK, dtype=jnp.int32)
    block_expert = jnp.minimum(
        jnp.sum((blk_end[None, :] <= blocks[:, None]).astype(jnp.int32), axis=1), N_EXPERTS - 1)
    segv = jnp.zeros((nt, 1, LANES), F32).at[:, 0, :N_EXPERTS].set(seg.astype(F32))
    return {
        "seg": seg.reshape(-1), "goff": goff.reshape(-1).astype(jnp.int32),
        "nch": (cnt_pad // ROW_CHUNK).reshape(-1),
        "ntot": jnp.sum(cnt_pad, axis=1) // ROW_CHUNK,
        "tstart": (gstart + rows_e).astype(jnp.int32),
        "tnch": (region - rows_e) // ROW_CHUNK,
        "block_expert": block_expert,
        "n_used": blk_end[-1:].astype(jnp.int32),
        "segv": segv,
    }


def _moe_cap(t_all, tm):
    nt = t_all // tm
    worst = TOP_K * t_all + nt * N_EXPERTS * (ROW_CHUNK - 1) + N_EXPERTS * (FFN_BLOCK - ROW_CHUNK)
    return -(-worst // FFN_BLOCK) * FFN_BLOCK


def _t5_bucket(dist):
    max_exact = NUM_BUCKETS // 2
    d = dist.astype(jnp.int32)
    ratio = (jnp.log(jnp.maximum(d, 1).astype(F32) / max_exact)
             / math.log(MAX_DISTANCE / max_exact))
    large = jnp.minimum(max_exact + (ratio * (NUM_BUCKETS - max_exact)).astype(jnp.int32),
                        NUM_BUCKETS - 1)
    return jnp.where(d < max_exact, d, large)


def _step_bias(tab, dil):
    return tab[_t5_bucket(dil * jnp.arange(ATT_BLK + 1))].astype(F32).T


def _band_table(sb):
    return jnp.concatenate([sb[:, ::-1], jnp.full((HEADS, ATT_BLK - 1), NEG_INF, F32)], axis=1)


def _cache_table(sb, dil):
    on_grid = sb[:, :0:-1]
    if dil == 1:
        return on_grid
    off = jnp.full((HEADS, ATT_BLK, dil - 1), NEG_INF, F32)
    return jnp.concatenate([on_grid[:, :, None], off], axis=2).reshape(HEADS, ATT_BLK * dil)


def kernel(x_prompt, x_sample, cache_kv_w128, cache_kv_w512, cache_kv_w2048, state_pool, c_prompt,
           c_sample, w_ada, b_ada, norm_mix_g, norm_ffn_g, w_in, q_norm_g, k_norm_g, rel_bias,
           w_pool_mix, pool_scale, w_up_attn, w_up_pool, w_out, w_router, b_router, w_gate_up,
           b_gate_up, w_down, b_down):
    assert w_ada.shape[0] == 1, "one layer"
    seq = x_prompt.shape[1]
    n_s = x_sample.shape[0]
    assert x_prompt.shape[0] == 1 and x_sample.shape[1] == 1
    assert seq % (DIL_GROUPS[-1][1] * ATT_BLK * ATT_SUB) == 0 and seq % TM_PROMPT == 0
    assert n_s == TM_SAMPLE
    dils = tuple(d for _, d in DIL_GROUPS)
    caches = (cache_kv_w128, cache_kv_w512, cache_kv_w2048)

    w_in_bf = w_in[0].astype(BF16)
    heads_of = jnp.arange(GROUP_W) // HEAD_DIM
    half_heads = heads_of[:GROUP_W // 2]
    bdiag = (half_heads[:, None] == half_heads[None, :]).astype(BF16)
    qg = (jnp.tile(q_norm_g[0], HEADS) * SCALE).reshape(1, GROUP_W)
    kg = jnp.tile(k_norm_g[0], HEADS).reshape(1, GROUP_W)
    expand = ((jnp.arange(LANES)[:, None] % HEADS == heads_of[None, :])
              & (jnp.arange(LANES)[:, None] < 2 * HEADS)).astype(BF16)
    wr = jnp.zeros((D_MODEL, LANES), F32).at[:, :N_EXPERTS].set(w_router[0])
    wr_hi = wr.astype(BF16)
    wr_lo = (wr - wr_hi.astype(F32)).astype(BF16)
    br = jnp.full((1, LANES), NEG_INF, F32).at[0, :N_EXPERTS].set(b_router[0])
    wts = (w_pool_mix[0].astype(BF16), pool_scale[0].reshape(1, POOL_W),
           w_up_attn[0].astype(BF16), w_up_pool[0].astype(BF16), w_out[0].astype(BF16), expand,
           norm_ffn_g[0].reshape(1, D_MODEL), wr_hi, wr_lo, br)
    g1 = norm_mix_g[0].reshape(1, D_MODEL)

    n_c = 1 + n_s
    c_all = jnp.zeros((-(-n_c // 8) * 8, D_MODEL), F32).at[0:1].set(c_prompt).at[1:n_c].set(c_sample)
    mod = _ada(c_all, w_ada[0], b_ada[0])
    sh1, sc1, gt1, sh2, sc2, gt2 = jnp.split(mod, N_ADA, axis=-1)

    def prow(m):
        return m[0:1]

    def srows(m):
        return m[1:n_c]

    xp = x_prompt[0]
    q_p, k_p, v_p, u_p, sga_p, sgp_p, st_p = _proj(
        xp, g1, prow(sc1), prow(sh1), w_in_bf, bdiag, qg, kg,
        tm=TM_PROMPT, dils=dils, per_row=False)
    xs = x_sample[:, 0]
    ones = (1, 1, 1)
    q_s, _, _, u_s, sga_s, sgp_s, st_s = _proj(
        xs, g1, srows(sc1), srows(sh1), w_in_bf, bdiag, qg, kg,
        tm=TM_SAMPLE, dils=ones, per_row=True)
    pooled_s, pool_state_t = _pool_sample(jnp.transpose(state_pool, (0, 2, 1, 3)), u_s)
    pool_state_s = jnp.transpose(pool_state_t, (0, 2, 1, 3))

    def heads(a):
        return a.astype(F32).reshape(n_s, HEADS, HEAD_DIM)

    step_bias = [_step_bias(rel_bias[:, g * HEADS:(g + 1) * HEADS], d)
                 for g, (_, d) in enumerate(DIL_GROUPS)]

    def sample_group(g):
        cache_t = jnp.transpose(caches[g], (0, 1, 3, 4, 5, 2))
        return (heads(q_s[g]), heads(st_s[g][:, :GROUP_W]), heads(st_s[g][:, GROUP_W:]),
                step_bias[g][:, 0:1], _cache_table(step_bias[g], DIL_GROUPS[g][1]), cache_t)

    o_p, lse_p = [], []
    o_s, lse_s = [None] * N_GROUPS, [None] * N_GROUPS
    for g, (_, d) in enumerate(DIL_GROUPS):
        hosted = SAMPLE_HOST[g]
        o, lse, sample_outs = _attn(q_p[g], k_p[g], v_p[g], _band_table(step_bias[g]), d,
                                    [sample_group(sg) for sg in hosted])
        o_p.append(o)
        lse_p.append(lse)
        for sg, (os_g, lses_g) in zip(hosted, sample_outs):
            o_s[sg] = os_g.reshape(n_s, GROUP_W)
            lse_s[sg] = jnp.zeros((n_s, LANES), F32).at[:, :HEADS].set(lses_g[:, :, 0])

    nt_p = seq // TM_PROMPT
    t_all = seq + TM_PROMPT
    bufs = _post(xp, o_p, lse_p, (u_p,), sga_p, sgp_p, wts, (prow(gt1), prow(sc2), prow(sh2)),
                 tm=TM_PROMPT, dils=dils, per_row=False, rows_total=t_all, row_block0=0,
                 cnt_tiles=nt_p + 1, cnt_block=None, grid=nt_p, n_valid_steps=None,
                 alias_bufs=None)
    bufs = _post(xs, o_s, lse_s, pooled_s, sga_s, sgp_s, wts,
                 (srows(gt1), srows(sc2), srows(sh2)),
                 tm=TM_SAMPLE, dils=ones, per_row=True, rows_total=t_all,
                 row_block0=seq // TM_SAMPLE, cnt_tiles=nt_p + 1, cnt_block=nt_p,
                 grid=TM_PROMPT // TM_SAMPLE, n_valid_steps=1, alias_bufs=bufs)
    x1_all, h2_all, a_all, idx_all, gk_all, cnt = bufs

    meta = _moe_meta(cnt[:, 0, :N_EXPERTS], TM_PROMPT)
    cap = _moe_cap(t_all, TM_PROMPT)
    xb, dst_all = _moe_sort(meta, a_all, idx_all, h2_all, tm=TM_PROMPT, cap=cap)
    yb = _moe_ffn(meta, xb, w_gate_up[0], b_gate_up[0], w_down[0], b_down[0])
    gt2_s = jnp.zeros((TM_PROMPT, D_MODEL), F32).at[:n_s].set(srows(gt2))
    y_p, y_s = _moe_unsort(meta, dst_all, gk_all, x1_all, prow(gt2), gt2_s, yb,
                           tm=TM_PROMPT, n_prompt_tiles=nt_p)

    def kv_state(st, rows):
        return st.reshape(1, 1, rows, 2, HEADS, HEAD_DIM)

    kv_p = [kv_state(st, st.shape[0]) for st in st_p]
    kv_s = [st.reshape(1, n_s, 1, 2, HEADS, HEAD_DIM) for st in st_s]
    pool_p = u_p[seq - POOL_BUF:].reshape(1, 1, POOL_BUF, POOL_W)
    return (y_p.reshape(1, seq, D_MODEL), y_s[:n_s].reshape(n_s, 1, D_MODEL),
            kv_p[0], kv_p[1], kv_p[2], pool_p, kv_s[0], kv_s[1], kv_s[2], pool_state_s)
```

```python
import functools
import math

import jax
import jax.numpy as jnp
from jax import lax
from jax.experimental import pallas as pl
from jax.experimental.pallas import tpu as pltpu

F32 = jnp.float32
BF16 = jnp.bfloat16

D_MODEL = 1024
HEAD_DIM = 64
HEADS = 8
GROUP_W = HEADS * HEAD_DIM
DIL_GROUPS = ((128, 1), (512, 4), (2048, 16))
N_GROUPS = len(DIL_GROUPS)
QKV_W = N_GROUPS * GROUP_W
ATT_BLK = 128
ATT_SUB = 2
POOL_WINDOWS = (2, 4, 8, 16)
POOL_W = 512
POOL_GW = 128
POOL_BUF = 15
OFF_K, OFF_V = QKV_W, 2 * QKV_W
OFF_U = 3 * QKV_W
OFF_GA = OFF_U + POOL_W
OFF_GP = OFF_GA + D_MODEL
IN_W = OFF_GP + D_MODEL
NUM_BUCKETS = 32
MAX_DISTANCE = 2048
N_EXPERTS = 32
TOP_K = 4
SWIGLU_LIMIT = 7.0
SWIGLU_ALPHA = 1.702
N_ADA = 6
EPS = 1e-6
NEG_INF = -1e30
PAST_LEN = 8192
SCALE = HEAD_DIM ** -0.5

LANES = 128
ROW_CHUNK = 16
TM_PROMPT = 512
TM_SAMPLE = 128
FFN_BLOCK = 512
SEL_CHUNK = 512
VMEM_LIMIT = 56 * 1024 * 1024


def _cparams(n_axes):
    return pltpu.CompilerParams(dimension_semantics=("arbitrary",) * n_axes,
                                vmem_limit_bytes=VMEM_LIMIT)


def _const_spec(shape):
    nd = len(shape)
    return pl.BlockSpec(shape, lambda *_: (0,) * nd)


def _ada_kernel(c_ref, w_ref, b_ref, o_ref):
    c = c_ref[...]
    s = c * jax.nn.sigmoid(c)
    o_ref[...] = jnp.dot(s.astype(BF16), w_ref[...].astype(BF16),
                         preferred_element_type=F32) + b_ref[...]


def _ada(c_all, w_ada, b_ada):
    rows = c_all.shape[0]
    n = w_ada.shape[1]
    tn = 1536
    return pl.pallas_call(
        _ada_kernel,
        grid=(n // tn,),
        in_specs=[pl.BlockSpec((rows, D_MODEL), lambda j: (0, 0)),
                  pl.BlockSpec((D_MODEL, tn), lambda j: (0, j)),
                  pl.BlockSpec((1, tn), lambda j: (0, j))],
        out_specs=pl.BlockSpec((rows, tn), lambda j: (0, j)),
        out_shape=jax.ShapeDtypeStruct((rows, n), F32),
        compiler_params=_cparams(1),
        name="ada",
    )(c_all, w_ada, b_ada.reshape(1, n))


def _proj_kernel(x_ref, g_ref, sc_ref, sh_ref, w_ref, bd_ref, qg_ref, kg_ref,
                 *refs, tm, dils, st_rows):
    q_refs, k_refs, v_refs = refs[0:3], refs[3:6], refs[6:9]
    u_ref, sga_ref, sgp_ref = refs[9:12]
    st_refs = refs[12:15]
    scr = refs[15]

    x = x_ref[...]
    ms = jnp.mean(x * x, axis=-1, keepdims=True)
    h = x * lax.rsqrt(ms + EPS) * g_ref[...] * (1.0 + sc_ref[...]) + sh_ref[...]
    hb = h.astype(BF16)

    def proj(off, width):
        return jnp.dot(hb, w_ref[:, off:off + width], preferred_element_type=F32)

    def head_norm(z, gain_ref):
        zz = (z * z).astype(BF16)
        half = GROUP_W // 2
        ss = jnp.concatenate(
            [jnp.dot(zz[:, :half], bd_ref[...], preferred_element_type=F32),
             jnp.dot(zz[:, half:], bd_ref[...], preferred_element_type=F32)], axis=1)
        return z * lax.rsqrt(ss * (1.0 / HEAD_DIM) + EPS) * gain_ref[...]

    def put(out_ref, val, d):
        if d == 1:
            out_ref[...] = val.astype(out_ref.dtype)
        else:
            for c in range(GROUP_W // LANES):
                scr[c] = val[:, c * LANES:(c + 1) * LANES]
            for r in range(d):
                for c in range(GROUP_W // LANES):
                    col = r * GROUP_W + c * LANES
                    out_ref[:, col:col + LANES] = (
                        scr[c, pl.ds(r, tm // d, stride=d), :].astype(out_ref.dtype))

    for g, d in enumerate(dils):
        qn = head_norm(proj(g * GROUP_W, GROUP_W), qg_ref)
        put(q_refs[g], qn, d)
        kn = head_norm(proj(OFF_K + g * GROUP_W, GROUP_W), kg_ref)
        put(k_refs[g], kn, d)
        v = proj(OFF_V + g * GROUP_W, GROUP_W)
        put(v_refs[g], v, d)
        rb = st_rows[g]
        st_refs[g][:, 0:GROUP_W] = kn[tm - rb:, :]
        st_refs[g][:, GROUP_W:2 * GROUP_W] = v[tm - rb:, :]

    u_ref[...] = proj(OFF_U, POOL_W)
    sga_ref[...] = jax.nn.sigmoid(proj(OFF_GA, D_MODEL)).astype(BF16)
    sgp_ref[...] = jax.nn.sigmoid(proj(OFF_GP, D_MODEL)).astype(BF16)


def _mod_spec(per_row, tm):
    if per_row:
        return pl.BlockSpec((tm, D_MODEL), lambda i: (i, 0))
    return pl.BlockSpec((1, D_MODEL), lambda i: (0, 0))


def _proj(x, g1, sc1, sh1, w_in_bf, bdiag, qg, kg, *, tm, dils, per_row):
    s = x.shape[0]
    nt = s // tm
    wins = tuple(min(w, s) for w, _ in DIL_GROUPS)
    st_rows = tuple(min(tm, w) for w in wins)

    def res_spec(d):
        return pl.BlockSpec((tm // d, d * GROUP_W), lambda i: (i, 0))

    def st_spec(w, rb):
        first = nt - w // rb
        return pl.BlockSpec((rb, 2 * GROUP_W), lambda i: (jnp.maximum(i - first, 0), 0))

    qkv_shapes = [jax.ShapeDtypeStruct((s // d, d * GROUP_W), BF16) for d in dils]
    out_shape = (qkv_shapes * 3
                 + [jax.ShapeDtypeStruct((s, POOL_W), F32),
                    jax.ShapeDtypeStruct((s, D_MODEL), BF16),
                    jax.ShapeDtypeStruct((s, D_MODEL), BF16)]
                 + [jax.ShapeDtypeStruct((w, 2 * GROUP_W), F32) for w in wins])
    out_specs = ([res_spec(d) for d in dils] * 3
                 + [pl.BlockSpec((tm, POOL_W), lambda i: (i, 0)),
                    pl.BlockSpec((tm, D_MODEL), lambda i: (i, 0)),
                    pl.BlockSpec((tm, D_MODEL), lambda i: (i, 0))]
                 + [st_spec(w, rb) for w, rb in zip(wins, st_rows)])
    in_specs = [pl.BlockSpec((tm, D_MODEL), lambda i: (i, 0)),
                _const_spec((1, D_MODEL)),
                _mod_spec(per_row, tm), _mod_spec(per_row, tm),
                pl.BlockSpec((D_MODEL, IN_W), lambda i: (0, 0), pipeline_mode=pl.Buffered(1)),
                _const_spec((GROUP_W // 2, GROUP_W // 2)),
                _const_spec((1, GROUP_W)), _const_spec((1, GROUP_W))]
    outs = pl.pallas_call(
        functools.partial(_proj_kernel, tm=tm, dils=dils, st_rows=st_rows),
        grid=(nt,),
        in_specs=in_specs,
        out_specs=out_specs,
        out_shape=out_shape,
        scratch_shapes=[pltpu.VMEM((GROUP_W // LANES, tm, LANES), F32)],
        compiler_params=_cparams(1),
        name="proj",
    )(x, g1, sc1, sh1, w_in_bf, bdiag, qg, kg)
    return outs[0:3], outs[3:6], outs[6:9], outs[9], outs[10], outs[11], outs[12:15]


def _sample_group_attn(qs_ref, kns_ref, vns_ref, bself_ref, btab_ref, c_ref, os_ref, lses_ref):
    n_tok = qs_ref.shape[0]
    win = btab_ref.shape[1]
    row_w = lax.broadcasted_iota(jnp.int32, (HEADS, win), 0)
    row_e = lax.broadcasted_iota(jnp.int32, (HEADS, HEAD_DIM), 0)
    ss, s0s = [], []
    for t in range(n_tok):
        q = qs_ref[t]
        qb = q.astype(BF16)
        s = jnp.zeros((HEADS, win), F32)
        for h in range(HEADS):
            sh = jnp.dot(qb, c_ref[0, t, 0, h].astype(BF16), preferred_element_type=F32)
            s = jnp.where(row_w == h, sh, s)
        ss.append(s + btab_ref[...])
        s0s.append(jnp.sum(q * kns_ref[t], axis=-1, keepdims=True) + bself_ref[...])
    s = jnp.concatenate(ss, axis=0)
    s0 = jnp.concatenate(s0s, axis=0)
    m = jnp.maximum(jnp.max(s, axis=-1, keepdims=True), s0)
    p = jnp.exp(s - m)
    p0 = jnp.exp(s0 - m)
    l = jnp.sum(p, axis=-1, keepdims=True) + p0
    pb = p.astype(BF16)
    lse = m + jnp.log(l)
    for t in range(n_tok):
        rows = slice(t * HEADS, (t + 1) * HEADS)
        o = jnp.zeros((HEADS, HEAD_DIM), F32)
        for h in range(HEADS):
            oh = lax.dot_general(pb[rows], c_ref[0, t, 1, h].astype(BF16),
                                 (((1,), (1,)), ((), ())), preferred_element_type=F32)
            o = jnp.where(row_e == h, oh, o)
        os_ref[t] = (o + p0[rows] * vns_ref[t]) / l[rows]
        lses_ref[t] = lse[rows]


SAMPLE_IN = 6
SAMPLE_PARTS = (1, 1, 2)
SAMPLE_HOST = (((2, 1),), ((1, 0),), ((2, 0), (0, 0)))


def _attn_kernel(q_ref, kp_ref, kc_ref, vp_ref, vc_ref, r_ref, *refs, n_sample_groups):
    n_in = SAMPLE_IN * n_sample_groups
    o_ref, lse_ref = refs[n_in], refs[n_in + 1]
    bias_ref = refs[-1]
    i = pl.program_id(1)

    for sg in range(n_sample_groups):
        _sample_group_attn(*refs[SAMPLE_IN * sg:SAMPLE_IN * (sg + 1)],
                           *refs[n_in + 2 + 2 * sg:n_in + 4 + 2 * sg])

    @pl.when((pl.program_id(0) == 0) & (i == 0))
    def _():
        for h in range(HEADS):
            row = jnp.broadcast_to(r_ref[h:h + 1, :], (ATT_BLK, 2 * ATT_BLK))
            bias_ref[h] = pltpu.roll(row, 0, 1, stride=1, stride_axis=0)

    q = q_ref[...]
    k = jnp.concatenate([kp_ref[...], kc_ref[...]], axis=0)
    v = jnp.concatenate([vp_ref[...], vc_ref[...]], axis=0)
    col = lax.broadcasted_iota(jnp.int32, (ATT_BLK, 2 * ATT_BLK), 1)
    no_prev = jnp.where((col < ATT_BLK) & (i == 0), NEG_INF, 0.0)
    lane_q = lax.broadcasted_iota(jnp.int32, (ATT_SUB * ATT_BLK, LANES), 1)
    lane_v = lax.broadcasted_iota(jnp.int32, ((ATT_SUB + 1) * ATT_BLK, LANES), 1)

    def pair(h):
        return slice((h // 2) * LANES, (h // 2 + 1) * LANES)

    def mine(lane, h):
        return (lane < HEAD_DIM) == (h % 2 == 0)

    ss = []
    for h in range(HEADS):
        q2 = q[:, pair(h)]
        qm = jnp.where(mine(lane_q, h), q2, jnp.zeros_like(q2))
        k2 = k[:, pair(h)]
        for j in range(ATT_SUB):
            s = lax.dot_general(qm[j * ATT_BLK:(j + 1) * ATT_BLK], k2[j * ATT_BLK:(j + 2) * ATT_BLK],
                                (((1,), (1,)), ((), ())), preferred_element_type=F32)
            s = s + bias_ref[h]
            ss.append(s + no_prev if j == 0 else s)
    s = jnp.concatenate(ss, axis=0)
    m = jnp.max(s, axis=-1, keepdims=True)
    p = jnp.exp(s - m)
    l = jnp.sum(p, axis=-1, keepdims=True)
    pb = p.astype(BF16)
    lse = m + jnp.log(l)
    inv_l = 1.0 / l
    outs, lses = [], []
    for h in range(HEADS):
        v2 = v[:, pair(h)]
        vm = jnp.where(mine(lane_v, h), v2, jnp.zeros_like(v2))
        o_sub, lse_sub = [], []
        for j in range(ATT_SUB):
            rows = slice((h * ATT_SUB + j) * ATT_BLK, (h * ATT_SUB + j + 1) * ATT_BLK)
            o_sub.append(jnp.dot(pb[rows], vm[j * ATT_BLK:(j + 2) * ATT_BLK],
                                 preferred_element_type=F32) * inv_l[rows])
            lse_sub.append(lse[rows])
        o = jnp.concatenate(o_sub, axis=0)
        if h % 2 == 0:
            outs.append(o)
        else:
            outs[-1] = outs[-1] + o
        lses.append(jnp.concatenate(lse_sub, axis=0))
    o_ref[...] = jnp.concatenate(outs, axis=-1).astype(o_ref.dtype)
    lse_ref[...] = jnp.concatenate(
        lses + [jnp.zeros((ATT_SUB * ATT_BLK, LANES - HEADS), F32)], axis=-1)


def _attn(q, k, v, r_tab, d, sample_groups):
    rows = q.shape[0]
    step = ATT_SUB * ATT_BLK
    nblk = rows // step
    cur = pl.BlockSpec((step, GROUP_W), lambda r, i: (i, r))
    prev = pl.BlockSpec((ATT_BLK, GROUP_W), lambda r, i: (jnp.maximum(i * ATT_SUB - 1, 0), r))
    in_specs = [cur, prev, cur, prev, cur,
                pl.BlockSpec((HEADS, 2 * ATT_BLK), lambda r, i: (0, 0))]
    out_specs = [pl.BlockSpec((step, GROUP_W), lambda r, i: (i, r)),
                 pl.BlockSpec((step, LANES), lambda r, i: (i, r))]
    out_shape = [jax.ShapeDtypeStruct((rows, d * GROUP_W), BF16),
                 jax.ShapeDtypeStruct((rows, d * LANES), F32)]
    args = [q, k, k, v, v, r_tab]
    for qs, kns, vns, bself, btab, cache_t, tok_start, n_tok in sample_groups:
        assert n_tok % (d * nblk) == 0
        tok = n_tok // (d * nblk)
        assert tok_start % tok == 0
        first = tok_start // tok

        def in_idx(r, i, first=first):
            return (first + r * nblk + i, 0, 0)

        def out_idx(r, i):
            return (r * nblk + i, 0, 0)

        tok_in = pl.BlockSpec((tok, HEADS, HEAD_DIM), in_idx)
        in_specs += [tok_in, tok_in, tok_in,
                     pl.BlockSpec(bself.shape, lambda r, i: (0, 0)),
                     pl.BlockSpec(btab.shape, lambda r, i: (0, 0)),
                     pl.BlockSpec((1, tok) + cache_t.shape[2:],
                                  lambda r, i, first=first: (0, first + r * nblk + i, 0, 0, 0, 0))]
        out_specs += [pl.BlockSpec((tok, HEADS, HEAD_DIM), out_idx),
                      pl.BlockSpec((tok, HEADS, 1), out_idx)]
        out_shape += [jax.ShapeDtypeStruct((n_tok, HEADS, HEAD_DIM), F32),
                      jax.ShapeDtypeStruct((n_tok, HEADS, 1), F32)]
        args += [qs, kns, vns, bself, btab, cache_t]
    outs = pl.pallas_call(
        functools.partial(_attn_kernel, n_sample_groups=len(sample_groups)),
        grid=(d, nblk),
        in_specs=in_specs,
        out_specs=out_specs,
        out_shape=out_shape,
        scratch_shapes=[pltpu.VMEM((HEADS, ATT_BLK, 2 * ATT_BLK), F32)],
        compiler_params=_cparams(2),
        name=f"attn_d{d}",
    )(*args)
    return outs[0], outs[1], [(outs[2 + 2 * j], outs[3 + 2 * j]) for j in range(len(sample_groups))]


def _pool_sample_kernel(st_ref, u_ref, pooled_ref, new_ref):
    u = u_ref[...]
    rows = [st_ref[0, j] for j in range(POOL_BUF)]
    outs = []
    for g, w in enumerate(POOL_WINDOWS):
        sl = slice(g * POOL_GW, (g + 1) * POOL_GW)
        acc = u[:, sl]
        for j in range(POOL_BUF - (w - 1), POOL_BUF):
            acc = acc + rows[j][:, sl]
        outs.append(acc / float(w) - u[:, sl])
    pooled_ref[...] = jnp.concatenate(outs, axis=-1)
    for j in range(POOL_BUF - 1):
        new_ref[0, j] = rows[j + 1]
    new_ref[0, POOL_BUF - 1] = u


def _pool_sample(state, u):
    n = u.shape[0]
    return pl.pallas_call(
        _pool_sample_kernel,
        grid=(1,),
        in_specs=[_const_spec(state.shape), _const_spec(u.shape)],
        out_specs=[_const_spec(u.shape), _const_spec(state.shape)],
        out_shape=[jax.ShapeDtypeStruct((n, POOL_W), F32),
                   jax.ShapeDtypeStruct(state.shape, F32)],
        compiler_params=_cparams(1),
        name="pool_sample",
    )(state, u)


def _post_kernel(*refs, tm, dils, pooled_given, n_valid_steps, aliased):
    it = iter(refs)
    x_ref = next(it)
    o_refs = [next(it) for _ in range(N_GROUPS)]
    lse_refs = [next(it) for _ in range(N_GROUPS)]
    if pooled_given:
        pooled_ref = next(it)
    else:
        u_ref, uh_ref = next(it), next(it)
    sga_ref, sgp_ref = next(it), next(it)
    wpm_ref, psc_ref, wua_ref, wup_ref, wout_ref, exp_ref = (next(it) for _ in range(6))
    gt1_ref, g2_ref, sc2_ref, sh2_ref = (next(it) for _ in range(4))
    wrh_ref, wrl_ref, br_ref = (next(it) for _ in range(3))
    if aliased:
        for _ in range(6):
            next(it)
    x1_ref, h2_ref, a_ref, idx_ref, gk_ref, cnt_ref = (next(it) for _ in range(6))
    ob_scr, ls_scr = next(it), next(it)

    i = pl.program_id(0)

    def compute():
        obs, lss = [], []
        for g, d in enumerate(dils):
            if d == 1:
                obs.append(o_refs[g][...].astype(F32))
                lss.append(lse_refs[g][...])
            else:
                for r in range(d):
                    for c in range(GROUP_W // LANES):
                        col = r * GROUP_W + c * LANES
                        ob_scr[c, pl.ds(r, tm // d, stride=d), :] = (
                            o_refs[g][:, col:col + LANES].astype(F32))
                    ls_scr[pl.ds(r, tm // d, stride=d), :] = (
                        lse_refs[g][:, r * LANES:(r + 1) * LANES])
                obs.append(jnp.concatenate([ob_scr[c] for c in range(GROUP_W // LANES)],
                                           axis=-1))
                lss.append(ls_scr[...])
        mx = jnp.maximum(jnp.maximum(lss[0], lss[1]), lss[2])
        es = [jnp.exp(l - mx) for l in lss]
        den = es[0] + es[1] + es[2]
        attn_o = jnp.zeros((tm, GROUP_W), F32)
        head_lane = lax.broadcasted_iota(jnp.int32, (tm, LANES), 1) < HEADS
        for g in range(N_GROUPS):
            w = jnp.where(head_lane, es[g] / den, 0.0)
            w_hi = w.astype(BF16).astype(F32)
            w_lo = (w - w_hi).astype(BF16).astype(F32)
            lhs = (w_hi + pltpu.roll(w_lo, HEADS, 1)).astype(BF16)
            wexp = jnp.dot(lhs, exp_ref[...], preferred_element_type=F32)
            attn_o = attn_o + wexp * obs[g]

        if pooled_given:
            pooled = pooled_ref[...]
        else:
            u = u_ref[...]
            halo = jnp.where(i == 0, 0.0, uh_ref[...])
            pos = (lax.broadcasted_iota(jnp.int32, (tm, 1), 0) + i * tm + 1).astype(F32)
            outs = []
            for g, w in enumerate(POOL_WINDOWS):
                sl = slice(g * POOL_GW, (g + 1) * POOL_GW)
                a = jnp.concatenate([halo[:, sl], u[:, sl]], axis=0)
                span = 1
                while span < w:
                    n = a.shape[0] - span
                    a = a[span:, :] + a[:n, :]
                    span *= 2
                off = a.shape[0] - tm
                win_sum = a[off:, :]
                outs.append(win_sum / jnp.minimum(pos, float(w)) - u[:, sl])
            pooled = jnp.concatenate(outs, axis=-1)
        pool_parts = []
        for g in range(len(POOL_WINDOWS)):
            sl = slice(g * POOL_GW, (g + 1) * POOL_GW)
            pool_parts.append(jnp.dot(pooled[:, sl].astype(BF16), wpm_ref[g],
                                      preferred_element_type=F32))
        pool_o = jnp.concatenate(pool_parts, axis=-1) * psc_ref[...]

        up_a = jnp.dot(attn_o.astype(BF16), wua_ref[...], preferred_element_type=F32)
        up_p = jnp.dot(pool_o.astype(BF16), wup_ref[...], preferred_element_type=F32)
        merged = sga_ref[...].astype(F32) * up_a + sgp_ref[...].astype(F32) * up_p
        mo = jnp.dot(merged.astype(BF16), wout_ref[...], preferred_element_type=F32)
        x1 = x_ref[...] + gt1_ref[...] * mo
        x1_ref[...] = x1

        ms = jnp.mean(x1 * x1, axis=-1, keepdims=True)
        h2 = x1 * lax.rsqrt(ms + EPS) * g2_ref[...] * (1.0 + sc2_ref[...]) + sh2_ref[...]
        h2_hi = h2.astype(BF16)
        h2_ref[...] = h2_hi
        h2_lo = (h2 - h2_hi.astype(F32)).astype(BF16)
        logits = (jnp.dot(h2_hi, wrh_ref[...], preferred_element_type=F32)
                  + jnp.dot(h2_lo, wrh_ref[...], preferred_element_type=F32)
                  + jnp.dot(h2_hi, wrl_ref[...], preferred_element_type=F32)
                  + br_ref[...])
        lane = lax.broadcasted_iota(jnp.int32, (tm, LANES), 1).astype(F32)
        work = logits
        vals, ids = [], []
        for _ in range(TOP_K):
            m = jnp.max(work, axis=-1, keepdims=True)
            ik = jnp.min(jnp.where(work == m, lane, float(LANES)), axis=-1, keepdims=True)
            vals.append(m)
            ids.append(ik)
            work = jnp.where(lane == ik, -3e38, work)
        ex = [jnp.exp(v - vals[0]) for v in vals]
        den_k = ex[0] + ex[1] + ex[2] + ex[3]
        a = jnp.zeros((tm, LANES), F32)
        idx = jnp.zeros((tm, LANES), F32)
        gk = jnp.zeros((tm, LANES), F32)
        for kk in range(TOP_K):
            gate = ex[kk] / den_k
            a = a + jnp.where(lane == ids[kk], gate, 0.0)
            idx = jnp.where(lane == float(kk), ids[kk], idx)
            gk = jnp.where(lane == float(kk), gate, gk)
        a_ref[...] = a
        idx_ref[...] = idx
        gk_ref[...] = gk
        cnt = jnp.sum((a > 0.0).astype(F32), axis=0, keepdims=True)
        row = lax.broadcasted_iota(jnp.int32, (8, LANES), 0)
        cnt_ref[0] = jnp.where(row == 0, jnp.broadcast_to(cnt, (8, LANES)), 0.0)

    if n_valid_steps is None:
        compute()
    else:
        pl.when(i < n_valid_steps)(compute)

        @pl.when(i >= n_valid_steps)
        def _():
            x1_ref[...] = jnp.zeros(x1_ref.shape, x1_ref.dtype)
            h2_ref[...] = jnp.zeros(h2_ref.shape, h2_ref.dtype)
            a_ref[...] = jnp.zeros(a_ref.shape, a_ref.dtype)
            idx_ref[...] = jnp.zeros(idx_ref.shape, idx_ref.dtype)
            gk_ref[...] = jnp.zeros(gk_ref.shape, gk_ref.dtype)


def _post(x, o_list, lse_list, pool_in, sga, sgp, wts, mods, *, tm, dils, per_row,
          rows_total, row_block0, cnt_tiles, cnt_block, grid, n_valid_steps, alias_bufs):
    pooled_given = not isinstance(pool_in, tuple)
    nv = grid if n_valid_steps is None else n_valid_steps

    def clamp(i):
        return jnp.minimum(i, nv - 1)

    def tile_spec(width):
        return pl.BlockSpec((tm, width), lambda i: (clamp(i), 0))

    in_specs = [tile_spec(D_MODEL)]
    in_specs += [pl.BlockSpec((tm // d, d * GROUP_W), lambda i: (clamp(i), 0)) for d in dils]
    in_specs += [pl.BlockSpec((tm // d, d * LANES), lambda i: (clamp(i), 0)) for d in dils]
    args = [x, *o_list, *lse_list]
    if pooled_given:
        in_specs.append(tile_spec(POOL_W))
        args.append(pool_in)
    else:
        u = pool_in[0]
        in_specs += [tile_spec(POOL_W),
                     pl.BlockSpec((16, POOL_W),
                                  lambda i: (jnp.maximum(i * (tm // 16) - 1, 0), 0))]
        args += [u, u]
    in_specs += [tile_spec(D_MODEL), tile_spec(D_MODEL)]
    args += [sga, sgp]
    wpm, psc, wua, wup, wout, expand, g2, wrh, wrl, br = wts
    gt1, sc2, sh2 = mods

    def mspec():
        if per_row:
            return pl.BlockSpec((tm, D_MODEL), lambda i: (clamp(i), 0))
        return _const_spec((1, D_MODEL))

    in_specs += [_const_spec(wpm.shape), _const_spec(psc.shape), _const_spec(wua.shape),
                 _const_spec(wup.shape), _const_spec(wout.shape), _const_spec(expand.shape),
                 mspec(), _const_spec(g2.shape), mspec(), mspec(),
                 _const_spec(wrh.shape), _const_spec(wrl.shape), _const_spec(br.shape)]
    args += [wpm, psc, wua, wup, wout, expand, gt1, g2, sc2, sh2, wrh, wrl, br]
    aliases = {}
    if alias_bufs is not None:
        base = len(args)
        in_specs += [pl.BlockSpec(memory_space=pl.ANY)] * 6
        args += list(alias_bufs)
        aliases = {base + j: j for j in range(6)}

    def out_spec(width):
        return pl.BlockSpec((tm, width), lambda i: (row_block0 + i, 0))

    out_specs = [out_spec(D_MODEL), out_spec(D_MODEL), out_spec(LANES), out_spec(LANES),
                 out_spec(LANES),
                 pl.BlockSpec((1, 8, LANES),
                              lambda i: (cnt_block if cnt_block is not None else i, 0, 0))]
    out_shape = [jax.ShapeDtypeStruct((rows_total, D_MODEL), F32),
                 jax.ShapeDtypeStruct((rows_total, D_MODEL), BF16),
                 jax.ShapeDtypeStruct((rows_total, LANES), F32),
                 jax.ShapeDtypeStruct((rows_total, LANES), F32),
                 jax.ShapeDtypeStruct((rows_total, LANES), F32),
                 jax.ShapeDtypeStruct((cnt_tiles, 8, LANES), F32)]
    return pl.pallas_call(
        functools.partial(_post_kernel, tm=tm, dils=dils, pooled_given=pooled_given,
                          n_valid_steps=n_valid_steps, aliased=alias_bufs is not None),
        grid=(grid,),
        in_specs=in_specs,
        out_specs=out_specs,
        out_shape=out_shape,
        scratch_shapes=[pltpu.VMEM((GROUP_W // LANES, tm, LANES), F32),
                        pltpu.VMEM((tm, LANES), F32)],
        input_output_aliases=aliases,
        compiler_params=_cparams(1),
        name="post_sample" if per_row else "post",
    )(*args)


def _sort_rows(tm):
    return -(-(TOP_K * tm + N_EXPERTS * (ROW_CHUNK - 1)) // SEL_CHUNK) * SEL_CHUNK


def _for_row_pieces(n_chunks, max_pow, fn):
    big = 1 << max_pow

    def body(c, carry):
        fn(c * big, big)
        return carry

    lax.fori_loop(0, n_chunks >> max_pow, body, 0)
    for pw in range(max_pow - 1, -1, -1):
        @pl.when(((n_chunks >> pw) & 1) == 1)
        def _(pw=pw):
            fn((n_chunks >> (pw + 1)) << (pw + 1), 1 << pw)


SEG_MAX_POW = 3
TILE_MAX_POW = 5


def _moe_sort_kernel(seg_s, goff_s, nch_s, ntot_s, tstart_s, tnch_s,
                     a_ref, idx_ref, h2_ref, segv_ref, lt_ref,
                     xb_hbm, dst_ref, xs_scr, zero_scr, sem, *, tm, n_rows):
    i = pl.program_id(0)
    nt = pl.num_programs(0)
    slot = i % 2
    sel = a_ref[...] > 0.0
    ahead = jnp.dot(lt_ref[...], sel.astype(BF16), preferred_element_type=F32)
    slot1 = jnp.where(sel, segv_ref[0] + ahead + 1.0, 0.0)
    lane = lax.broadcasted_iota(jnp.int32, (tm, LANES), 1).astype(F32)
    idx = idx_ref[...]
    dst = jnp.full((tm, LANES), -1.0, F32)
    for kk in range(TOP_K):
        hit = lane == idx[:, kk:kk + 1]
        dk = jnp.sum(jnp.where(hit, slot1, 0.0), axis=-1, keepdims=True) - 1.0
        dst = jnp.where(lane == float(kk), dk, dst)
    dst_ref[...] = dst
    dst_t = dst.T
    h2 = h2_ref[...]
    for c in range(n_rows // SEL_CHUNK):
        rows = (lax.broadcasted_iota(jnp.int32, (SEL_CHUNK, tm), 0) + c * SEL_CHUNK).astype(F32)
        p = rows == dst_t[0:1, :]
        for kk in range(1, TOP_K):
            p = p | (rows == dst_t[kk:kk + 1, :])
        xs = jnp.dot(jnp.where(p, 1.0, 0.0).astype(BF16), h2, preferred_element_type=F32)
        xs_scr[slot, c * SEL_CHUNK:(c + 1) * SEL_CHUNK, :] = xs.astype(BF16)

    def rows_copy(buf, src_row, dst_row, n_chunks):
        return pltpu.make_async_copy(
            xs_scr.at[buf, pl.ds(pl.multiple_of(src_row, ROW_CHUNK), n_chunks * ROW_CHUNK)],
            xb_hbm.at[pl.ds(pl.multiple_of(dst_row, ROW_CHUNK), n_chunks * ROW_CHUNK)],
            sem.at[buf])

    def per_expert(e, carry):
        so = seg_s[i * N_EXPERTS + e]
        go = goff_s[i * N_EXPERTS + e]
        _for_row_pieces(
            nch_s[i * N_EXPERTS + e], SEG_MAX_POW,
            lambda off, n: rows_copy(slot, so + off * ROW_CHUNK, go + off * ROW_CHUNK, n).start())
        return carry

    lax.fori_loop(0, N_EXPERTS, per_expert, 0)

    def drain(buf, tile):
        _for_row_pieces(ntot_s[tile], TILE_MAX_POW, lambda off, n: rows_copy(buf, 0, 0, n).wait())

    @pl.when(i > 0)
    def _():
        drain(1 - slot, i - 1)

    @pl.when(i == nt - 1)
    def _():
        drain(slot, i)
        zero_scr[...] = jnp.zeros(zero_scr.shape, zero_scr.dtype)

        def tail_copy(dst_row):
            return pltpu.make_async_copy(
                zero_scr, xb_hbm.at[pl.ds(pl.multiple_of(dst_row, ROW_CHUNK), ROW_CHUNK)],
                sem.at[2])

        def per_expert_tail(e, carry):
            def per_chunk(c, carry2):
                tail_copy(tstart_s[e] + c * ROW_CHUNK).start()
                return carry2

            lax.fori_loop(0, tnch_s[e], per_chunk, 0)

            def wait_chunk(c, carry2):
                tail_copy(0).wait()
                return carry2

            return lax.fori_loop(0, tnch_s[e], wait_chunk, carry)

        lax.fori_loop(0, N_EXPERTS, per_expert_tail, 0)


def _moe_sort(meta, a_all, idx_all, h2_all, *, tm, cap):
    t_all = a_all.shape[0]
    nt = t_all // tm
    n_rows = _sort_rows(tm)
    lt = jnp.tril(jnp.ones((tm, tm), BF16), -1)
    grid_spec = pltpu.PrefetchScalarGridSpec(
        num_scalar_prefetch=6,
        grid=(nt,),
        in_specs=[pl.BlockSpec((tm, LANES), lambda i, *_: (i, 0)),
                  pl.BlockSpec((tm, LANES), lambda i, *_: (i, 0)),
                  pl.BlockSpec((tm, D_MODEL), lambda i, *_: (i, 0)),
                  pl.BlockSpec((1, 1, LANES), lambda i, *_: (i, 0, 0)),
                  pl.BlockSpec((tm, tm), lambda i, *_: (0, 0))],
        out_specs=[pl.BlockSpec(memory_space=pl.ANY),
                   pl.BlockSpec((tm, LANES), lambda i, *_: (i, 0))],
        scratch_shapes=[pltpu.VMEM((2, n_rows, D_MODEL), BF16),
                        pltpu.VMEM((ROW_CHUNK, D_MODEL), BF16),
                        pltpu.SemaphoreType.DMA((3,))],
    )
    return pl.pallas_call(
        functools.partial(_moe_sort_kernel, tm=tm, n_rows=n_rows),
        grid_spec=grid_spec,
        out_shape=[jax.ShapeDtypeStruct((cap, D_MODEL), BF16),
                   jax.ShapeDtypeStruct((t_all, LANES), F32)],
        compiler_params=_cparams(1),
        name="moe_sort",
    )(meta["seg"], meta["goff"], meta["nch"], meta["ntot"], meta["tstart"], meta["tnch"],
      a_all, idx_all, h2_all, meta["segv"], lt)


def _moe_ffn_kernel(be_s, nused_s, ord_s, next_s, x_ref, wgu_hbm, bgu_ref, wd_hbm, bd_ref, y_ref,
                    wgu_f32, wd_f32, wgu_bf, wd_bf, sem):
    b = pl.program_id(0)

    def weight_copies(e, slot):
        return (pltpu.make_async_copy(wgu_hbm.at[e], wgu_f32.at[slot], sem.at[0, slot]),
                pltpu.make_async_copy(wd_hbm.at[e], wd_f32.at[slot], sem.at[1, slot]))

    @pl.when(b < nused_s[0])
    def _():
        e = be_s[b]
        e_prev = be_s[jnp.maximum(b - 1, 0)]
        slot = ord_s[b] % 2

        @pl.when(b == 0)
        def _():
            for cp in weight_copies(e, slot):
                cp.start()

        @pl.when((b == 0) | (e != e_prev))
        def _():
            for cp in weight_copies(e, slot):
                cp.wait()
            e_next = next_s[b]

            @pl.when(e_next >= 0)
            def _():
                for cp in weight_copies(e_next, 1 - slot):
                    cp.start()

            wgu_bf[...] = wgu_f32[slot].astype(BF16)
            wd_bf[...] = wd_f32[slot].astype(BF16)

        hgu = jnp.dot(x_ref[...], wgu_bf[...], preferred_element_type=F32) + bgu_ref[0]
        d_ff = hgu.shape[1] // 2
        hg = jnp.minimum(hgu[:, :d_ff], SWIGLU_LIMIT)
        hu = jnp.clip(hgu[:, d_ff:], -SWIGLU_LIMIT, SWIGLU_LIMIT)
        act = hg * jax.nn.sigmoid(SWIGLU_ALPHA * hg) * (hu + 1.0)
        y = jnp.dot(act.astype(BF16), wd_bf[...], preferred_element_type=F32) + bd_ref[0]
        y_ref[...] = y.astype(y_ref.dtype)


def _moe_ffn(meta, xb, w_gate_up, b_gate_up, w_down, b_down):
    cap = xb.shape[0]
    nb = cap // FFN_BLOCK
    d_ff2 = w_gate_up.shape[2]

    def blk(b, be, nu, *_):
        return jnp.minimum(b, jnp.maximum(nu[0] - 1, 0))

    def row_blk(b, be, nu, *_):
        return (blk(b, be, nu), 0)

    def expert_blk(b, be, nu, *_):
        return (be[blk(b, be, nu)], 0, 0)

    grid_spec = pltpu.PrefetchScalarGridSpec(
        num_scalar_prefetch=4,
        grid=(nb,),
        in_specs=[pl.BlockSpec((FFN_BLOCK, D_MODEL), row_blk),
                  pl.BlockSpec(memory_space=pl.ANY),
                  pl.BlockSpec((1, 1, d_ff2), expert_blk),
                  pl.BlockSpec(memory_space=pl.ANY),
                  pl.BlockSpec((1, 1, D_MODEL), expert_blk)],
        out_specs=pl.BlockSpec((FFN_BLOCK, D_MODEL), row_blk),
        scratch_shapes=[pltpu.VMEM((2, D_MODEL, d_ff2), F32),
                        pltpu.VMEM((2, d_ff2 // 2, D_MODEL), F32),
                        pltpu.VMEM((D_MODEL, d_ff2), BF16),
                        pltpu.VMEM((d_ff2 // 2, D_MODEL), BF16),
                        pltpu.SemaphoreType.DMA((2, 2))],
    )
    return pl.pallas_call(
        _moe_ffn_kernel,
        grid_spec=grid_spec,
        out_shape=jax.ShapeDtypeStruct((cap, D_MODEL), BF16),
        compiler_params=_cparams(1),
        name="moe_ffn",
    )(meta["block_expert"], meta["n_used"], meta["block_ord"], meta["block_next"], xb, w_gate_up,
      b_gate_up.reshape(N_EXPERTS, 1, d_ff2), w_down, b_down.reshape(N_EXPERTS, 1, D_MODEL))


def _moe_unsort_kernel(seg_s, goff_s, nch_s, ntot_s,
                       dst_ref, gk_ref, x1_ref, g2p_ref, g2s_ref, yb_hbm,
                       yp_ref, ys_ref, ybuf, sem, *, tm, n_rows, n_prompt_tiles):
    i = pl.program_id(0)
    nt = pl.num_programs(0)
    slot = i % 2

    def rows_copy(buf, src_row, dst_row, n_chunks):
        return pltpu.make_async_copy(
            yb_hbm.at[pl.ds(pl.multiple_of(src_row, ROW_CHUNK), n_chunks * ROW_CHUNK)],
            ybuf.at[buf, pl.ds(pl.multiple_of(dst_row, ROW_CHUNK), n_chunks * ROW_CHUNK)],
            sem.at[buf])

    def fetch(tile, buf):
        def per_expert(e, carry):
            so = seg_s[tile * N_EXPERTS + e]
            go = goff_s[tile * N_EXPERTS + e]
            _for_row_pieces(
                nch_s[tile * N_EXPERTS + e], SEG_MAX_POW,
                lambda off, n: rows_copy(buf, go + off * ROW_CHUNK, so + off * ROW_CHUNK,
                                         n).start())
            return carry

        lax.fori_loop(0, N_EXPERTS, per_expert, 0)

    @pl.when(i == 0)
    def _():
        ybuf[...] = jnp.zeros(ybuf.shape, ybuf.dtype)
        fetch(0, 0)

    @pl.when(i + 1 < nt)
    def _():
        fetch(i + 1, 1 - slot)

    _for_row_pieces(ntot_s[i], TILE_MAX_POW, lambda off, n: rows_copy(slot, 0, 0, n).wait())

    dst = dst_ref[...]
    gk = gk_ref[...]
    acc = jnp.zeros((tm, D_MODEL), F32)
    for c in range(n_rows // SEL_CHUNK):
        cols = (lax.broadcasted_iota(jnp.int32, (tm, SEL_CHUNK), 1) + c * SEL_CHUNK).astype(F32)
        q = jnp.zeros((tm, SEL_CHUNK), F32)
        for kk in range(TOP_K):
            q = jnp.where(cols == dst[:, kk:kk + 1], gk[:, kk:kk + 1], q)
        acc = acc + jnp.dot(q.astype(BF16), ybuf[slot, c * SEL_CHUNK:(c + 1) * SEL_CHUNK, :],
                            preferred_element_type=F32)

    @pl.when(i < n_prompt_tiles)
    def _():
        yp_ref[...] = x1_ref[...] + g2p_ref[...] * acc

    @pl.when(i >= n_prompt_tiles)
    def _():
        ys_ref[...] = x1_ref[...] + g2s_ref[...] * acc


def _moe_unsort(meta, dst_all, gk_all, x1_all, gt2_p, gt2_s, yb, *, tm, n_prompt_tiles):
    t_all = dst_all.shape[0]
    nt = t_all // tm
    n_rows = _sort_rows(tm)
    last_p = n_prompt_tiles - 1
    grid_spec = pltpu.PrefetchScalarGridSpec(
        num_scalar_prefetch=4,
        grid=(nt,),
        in_specs=[pl.BlockSpec((tm, LANES), lambda i, *_: (i, 0)),
                  pl.BlockSpec((tm, LANES), lambda i, *_: (i, 0)),
                  pl.BlockSpec((tm, D_MODEL), lambda i, *_: (i, 0)),
                  pl.BlockSpec((1, D_MODEL), lambda i, *_: (0, 0)),
                  pl.BlockSpec((tm, D_MODEL), lambda i, *_: (0, 0)),
                  pl.BlockSpec(memory_space=pl.ANY)],
        out_specs=[pl.BlockSpec((tm, D_MODEL), lambda i, *_: (jnp.minimum(i, last_p), 0)),
                   pl.BlockSpec((tm, D_MODEL), lambda i, *_: (0, 0))],
        scratch_shapes=[pltpu.VMEM((2, n_rows, D_MODEL), BF16),
                        pltpu.SemaphoreType.DMA((2,))],
    )
    return pl.pallas_call(
        functools.partial(_moe_unsort_kernel, tm=tm, n_rows=n_rows,
                          n_prompt_tiles=n_prompt_tiles),
        grid_spec=grid_spec,
        out_shape=[jax.ShapeDtypeStruct((n_prompt_tiles * tm, D_MODEL), F32),
                   jax.ShapeDtypeStruct((tm, D_MODEL), F32)],
        compiler_params=_cparams(1),
        name="moe_unsort",
    )(meta["seg"], meta["goff"], meta["nch"], meta["ntot"],
      dst_all, gk_all, x1_all, gt2_p, gt2_s, yb)


def _moe_meta(cnt, tm):
    nt = cnt.shape[0]
    cnt = cnt.astype(jnp.int32)
    cnt_pad = (cnt + ROW_CHUNK - 1) // ROW_CHUNK * ROW_CHUNK
    seg = jnp.cumsum(cnt_pad, axis=1) - cnt_pad
    rows_e = jnp.sum(cnt_pad, axis=0)
    region = (rows_e + FFN_BLOCK - 1) // FFN_BLOCK * FFN_BLOCK
    gstart = jnp.cumsum(region) - region
    goff = gstart[None, :] + jnp.cumsum(cnt_pad, axis=0) - cnt_pad
    nblk_e = region // FFN_BLOCK
    blk_end = jnp.cumsum(nblk_e)
    cap = _moe_cap(nt * tm, tm)
    blocks = jnp.arange(cap // FFN_BLOCK, dtype=jnp.int32)
    block_expert = jnp.minimum(
        jnp.sum((blk_end[None, :] <= blocks[:, None]).astype(jnp.int32), axis=1), N_EXPERTS - 1)
    segv = jnp.zeros((nt, 1, LANES), F32).at[:, 0, :N_EXPERTS].set(seg.astype(F32))
    used = nblk_e > 0
    experts = jnp.arange(N_EXPERTS, dtype=jnp.int32)
    ord_e = jnp.cumsum(used.astype(jnp.int32)) - 1
    later = (experts[None, :] > experts[:, None]) & used[None, :]
    next_e = jnp.min(jnp.where(later, experts[None, :], N_EXPERTS), axis=1)
    next_e = jnp.where(next_e < N_EXPERTS, next_e, -1).astype(jnp.int32)
    return {
        "block_ord": ord_e[block_expert].astype(jnp.int32),
        "block_next": next_e[block_expert],
        "seg": seg.reshape(-1), "goff": goff.reshape(-1).astype(jnp.int32),
        "nch": (cnt_pad // ROW_CHUNK).reshape(-1),
        "ntot": jnp.sum(cnt_pad, axis=1) // ROW_CHUNK,
        "tstart": (gstart + rows_e).astype(jnp.int32),
        "tnch": (region - rows_e) // ROW_CHUNK,
        "block_expert": block_expert,
        "n_used": blk_end[-1:].astype(jnp.int32),
        "segv": segv,
    }


def _moe_cap(t_all, tm):
    nt = t_all // tm
    worst = TOP_K * t_all + nt * N_EXPERTS * (ROW_CHUNK - 1) + N_EXPERTS * (FFN_BLOCK - ROW_CHUNK)
    return -(-worst // FFN_BLOCK) * FFN_BLOCK


def _t5_bucket(dist):
    max_exact = NUM_BUCKETS // 2
    d = dist.astype(jnp.int32)
    ratio = (jnp.log(jnp.maximum(d, 1).astype(F32) / max_exact)
             / math.log(MAX_DISTANCE / max_exact))
    large = jnp.minimum(max_exact + (ratio * (NUM_BUCKETS - max_exact)).astype(jnp.int32),
                        NUM_BUCKETS - 1)
    return jnp.where(d < max_exact, d, large)


def _step_bias(tab, dil):
    return tab[_t5_bucket(dil * jnp.arange(ATT_BLK + 1))].astype(F32).T


def _band_table(sb):
    return jnp.concatenate([sb[:, ::-1], jnp.full((HEADS, ATT_BLK - 1), NEG_INF, F32)], axis=1)


def _cache_table(sb, dil):
    on_grid = sb[:, :0:-1]
    if dil == 1:
        return on_grid
    off = jnp.full((HEADS, ATT_BLK, dil - 1), NEG_INF, F32)
    return jnp.concatenate([on_grid[:, :, None], off], axis=2).reshape(HEADS, ATT_BLK * dil)


def kernel(x_prompt, x_sample, cache_kv_w128, cache_kv_w512, cache_kv_w2048, state_pool, c_prompt,
           c_sample, w_ada, b_ada, norm_mix_g, norm_ffn_g, w_in, q_norm_g, k_norm_g, rel_bias,
           w_pool_mix, pool_scale, w_up_attn, w_up_pool, w_out, w_router, b_router, w_gate_up,
           b_gate_up, w_down, b_down):
    assert w_ada.shape[0] == 1, "one layer"
    seq = x_prompt.shape[1]
    n_s = x_sample.shape[0]
    assert x_prompt.shape[0] == 1 and x_sample.shape[1] == 1
    assert seq % (DIL_GROUPS[-1][1] * ATT_BLK * ATT_SUB) == 0 and seq % TM_PROMPT == 0
    assert n_s == TM_SAMPLE
    dils = tuple(d for _, d in DIL_GROUPS)
    caches = (cache_kv_w128, cache_kv_w512, cache_kv_w2048)

    w_in_bf = w_in[0].astype(BF16)
    heads_of = jnp.arange(GROUP_W) // HEAD_DIM
    half_heads = heads_of[:GROUP_W // 2]
    bdiag = (half_heads[:, None] == half_heads[None, :]).astype(BF16)
    qg = (jnp.tile(q_norm_g[0], HEADS) * SCALE).reshape(1, GROUP_W)
    kg = jnp.tile(k_norm_g[0], HEADS).reshape(1, GROUP_W)
    expand = ((jnp.arange(LANES)[:, None] % HEADS == heads_of[None, :])
              & (jnp.arange(LANES)[:, None] < 2 * HEADS)).astype(BF16)
    wr = jnp.zeros((D_MODEL, LANES), F32).at[:, :N_EXPERTS].set(w_router[0])
    wr_hi = wr.astype(BF16)
    wr_lo = (wr - wr_hi.astype(F32)).astype(BF16)
    br = jnp.full((1, LANES), NEG_INF, F32).at[0, :N_EXPERTS].set(b_router[0])
    wts = (w_pool_mix[0].astype(BF16), pool_scale[0].reshape(1, POOL_W),
           w_up_attn[0].astype(BF16), w_up_pool[0].astype(BF16), w_out[0].astype(BF16), expand,
           norm_ffn_g[0].reshape(1, D_MODEL), wr_hi, wr_lo, br)
    g1 = norm_mix_g[0].reshape(1, D_MODEL)

    n_c = 1 + n_s
    c_all = jnp.zeros((-(-n_c // 8) * 8, D_MODEL), F32).at[0:1].set(c_prompt).at[1:n_c].set(c_sample)
    mod = _ada(c_all, w_ada[0], b_ada[0])
    sh1, sc1, gt1, sh2, sc2, gt2 = jnp.split(mod, N_ADA, axis=-1)

    def prow(m):
        return m[0:1]

    def srows(m):
        return m[1:n_c]

    xp = x_prompt[0]
    q_p, k_p, v_p, u_p, sga_p, sgp_p, st_p = _proj(
        xp, g1, prow(sc1), prow(sh1), w_in_bf, bdiag, qg, kg,
        tm=TM_PROMPT, dils=dils, per_row=False)
    xs = x_sample[:, 0]
    ones = (1, 1, 1)
    q_s, _, _, u_s, sga_s, sgp_s, st_s = _proj(
        xs, g1, srows(sc1), srows(sh1), w_in_bf, bdiag, qg, kg,
        tm=TM_SAMPLE, dils=ones, per_row=True)
    pooled_s, pool_state_t = _pool_sample(jnp.transpose(state_pool, (0, 2, 1, 3)), u_s)
    pool_state_s = jnp.transpose(pool_state_t, (0, 2, 1, 3))

    def heads(a):
        return a.astype(F32).reshape(n_s, HEADS, HEAD_DIM)

    step_bias = [_step_bias(rel_bias[:, g * HEADS:(g + 1) * HEADS], d)
                 for g, (_, d) in enumerate(DIL_GROUPS)]

    def sample_group(g, part, n_parts):
        cache_t = jnp.transpose(caches[g], (0, 1, 3, 4, 5, 2))
        n_sub = n_s // n_parts
        return (heads(q_s[g]), heads(st_s[g][:, :GROUP_W]), heads(st_s[g][:, GROUP_W:]),
                step_bias[g][:, 0:1], _cache_table(step_bias[g], DIL_GROUPS[g][1]), cache_t,
                part * n_sub, n_sub)

    o_p, lse_p = [], []
    o_parts = [[None] * n for n in SAMPLE_PARTS]
    lse_parts = [[None] * n for n in SAMPLE_PARTS]
    for g, (_, d) in enumerate(DIL_GROUPS):
        hosted = SAMPLE_HOST[g]
        o, lse, sample_outs = _attn(
            q_p[g], k_p[g], v_p[g], _band_table(step_bias[g]), d,
            [sample_group(sg, part, SAMPLE_PARTS[sg]) for sg, part in hosted])
        o_p.append(o)
        lse_p.append(lse)
        for (sg, part), (os_g, lses_g) in zip(hosted, sample_outs):
            o_parts[sg][part] = os_g.reshape(-1, GROUP_W)
            lse_parts[sg][part] = lses_g[:, :, 0]
    o_s = [jnp.concatenate(parts, axis=0) for parts in o_parts]
    lse_s = [jnp.zeros((n_s, LANES), F32).at[:, :HEADS].set(jnp.concatenate(parts, axis=0))
             for parts in lse_parts]

    nt_p = seq // TM_PROMPT
    t_all = seq + TM_PROMPT
    bufs = _post(xp, o_p, lse_p, (u_p,), sga_p, sgp_p, wts, (prow(gt1), prow(sc2), prow(sh2)),
                 tm=TM_PROMPT, dils=dils, per_row=False, rows_total=t_all, row_block0=0,
                 cnt_tiles=nt_p + 1, cnt_block=None, grid=nt_p, n_valid_steps=None,
                 alias_bufs=None)
    bufs = _post(xs, o_s, lse_s, pooled_s, sga_s, sgp_s, wts,
                 (srows(gt1), srows(sc2), srows(sh2)),
                 tm=TM_SAMPLE, dils=ones, per_row=True, rows_total=t_all,
                 row_block0=seq // TM_SAMPLE, cnt_tiles=nt_p + 1, cnt_block=nt_p,
                 grid=TM_PROMPT // TM_SAMPLE, n_valid_steps=1, alias_bufs=bufs)
    x1_all, h2_all, a_all, idx_all, gk_all, cnt = bufs

    meta = _moe_meta(cnt[:, 0, :N_EXPERTS], TM_PROMPT)
    cap = _moe_cap(t_all, TM_PROMPT)
    xb, dst_all = _moe_sort(meta, a_all, idx_all, h2_all, tm=TM_PROMPT, cap=cap)
    yb = _moe_ffn(meta, xb, w_gate_up[0], b_gate_up[0], w_down[0], b_down[0])
    gt2_s = jnp.zeros((TM_PROMPT, D_MODEL), F32).at[:n_s].set(srows(gt2))
    y_p, y_s = _moe_unsort(meta, dst_all, gk_all, x1_all, prow(gt2), gt2_s, yb,
                           tm=TM_PROMPT, n_prompt_tiles=nt_p)

    def kv_state(st, rows):
        return st.reshape(1, 1, rows, 2, HEADS, HEAD_DIM)

    kv_p = [kv_state(st, st.shape[0]) for st in st_p]
    kv_s = [st.reshape(1, n_s, 1, 2, HEADS, HEAD_DIM) for st in st_s]
    pool_p = u_p[seq - POOL_BUF:].reshape(1, 1, POOL_BUF, POOL_W)
    return (y_p.reshape(1, seq, D_MODEL), y_s[:n_s].reshape(n_s, 1, D_MODEL),
            kv_p[0], kv_p[1], kv_p[2], pool_p, kv_s[0], kv_s[1], kv_s[2], pool_state_s)
```

```python
import functools
import math

import jax
import jax.numpy as jnp
from jax import lax
from jax.experimental import pallas as pl
from jax.experimental.pallas import tpu as pltpu

F32 = jnp.float32
BF16 = jnp.bfloat16

D_MODEL = 1024
HEAD_DIM = 64
HEADS = 8
GROUP_W = HEADS * HEAD_DIM
DIL_GROUPS = ((128, 1), (512, 4), (2048, 16))
N_GROUPS = len(DIL_GROUPS)
QKV_W = N_GROUPS * GROUP_W
ATT_BLK = 128
ATT_SUB = 4
POOL_WINDOWS = (2, 4, 8, 16)
POOL_W = 512
POOL_GW = 128
POOL_BUF = 15
OFF_K, OFF_V = QKV_W, 2 * QKV_W
OFF_U = 3 * QKV_W
OFF_GA = OFF_U + POOL_W
OFF_GP = OFF_GA + D_MODEL
IN_W = OFF_GP + D_MODEL
NUM_BUCKETS = 32
MAX_DISTANCE = 2048
N_EXPERTS = 32
TOP_K = 4
SWIGLU_LIMIT = 7.0
SWIGLU_ALPHA = 1.702
N_ADA = 6
EPS = 1e-6
NEG_INF = -1e30
PAST_LEN = 8192
SCALE = HEAD_DIM ** -0.5

LANES = 128
ROW_CHUNK = 16
TM_PROMPT = 512
TM_SAMPLE = 128
FFN_BLOCK = 512
SEL_CHUNK = 512
VMEM_LIMIT = 56 * 1024 * 1024


def _cparams(n_axes):
    return pltpu.CompilerParams(dimension_semantics=("arbitrary",) * n_axes,
                                vmem_limit_bytes=VMEM_LIMIT)


def _const_spec(shape):
    nd = len(shape)
    return pl.BlockSpec(shape, lambda *_: (0,) * nd)


def _ada_kernel(c_ref, w_ref, b_ref, o_ref):
    c = c_ref[...]
    s = c * jax.nn.sigmoid(c)
    o_ref[...] = jnp.dot(s.astype(BF16), w_ref[...].astype(BF16),
                         preferred_element_type=F32) + b_ref[...]


def _ada(c_all, w_ada, b_ada):
    rows = c_all.shape[0]
    n = w_ada.shape[1]
    tn = 1536
    return pl.pallas_call(
        _ada_kernel,
        grid=(n // tn,),
        in_specs=[pl.BlockSpec((rows, D_MODEL), lambda j: (0, 0)),
                  pl.BlockSpec((D_MODEL, tn), lambda j: (0, j)),
                  pl.BlockSpec((1, tn), lambda j: (0, j))],
        out_specs=pl.BlockSpec((rows, tn), lambda j: (0, j)),
        out_shape=jax.ShapeDtypeStruct((rows, n), F32),
        compiler_params=_cparams(1),
        name="ada",
    )(c_all, w_ada, b_ada.reshape(1, n))


def _proj_kernel(x_ref, g_ref, sc_ref, sh_ref, w_ref, bd_ref, qg_ref, kg_ref,
                 *refs, tm, dils, st_rows):
    q_refs, k_refs, v_refs = refs[0:3], refs[3:6], refs[6:9]
    u_ref, sga_ref, sgp_ref = refs[9:12]
    st_refs = refs[12:15]
    scr = refs[15]

    x = x_ref[...]
    ms = jnp.mean(x * x, axis=-1, keepdims=True)
    h = x * lax.rsqrt(ms + EPS) * g_ref[...] * (1.0 + sc_ref[...]) + sh_ref[...]
    hb = h.astype(BF16)

    def proj(off, width):
        return jnp.dot(hb, w_ref[:, off:off + width], preferred_element_type=F32)

    def head_norm(z, gain_ref):
        zz = (z * z).astype(BF16)
        half = GROUP_W // 2
        ss = jnp.concatenate(
            [jnp.dot(zz[:, :half], bd_ref[...], preferred_element_type=F32),
             jnp.dot(zz[:, half:], bd_ref[...], preferred_element_type=F32)], axis=1)
        return z * lax.rsqrt(ss * (1.0 / HEAD_DIM) + EPS) * gain_ref[...]

    def put(out_ref, val, d):
        if d == 1:
            out_ref[...] = val.astype(out_ref.dtype)
        else:
            for c in range(GROUP_W // LANES):
                scr[c] = val[:, c * LANES:(c + 1) * LANES]
            for r in range(d):
                for c in range(GROUP_W // LANES):
                    col = r * GROUP_W + c * LANES
                    out_ref[:, col:col + LANES] = (
                        scr[c, pl.ds(r, tm // d, stride=d), :].astype(out_ref.dtype))

    for g, d in enumerate(dils):
        qn = head_norm(proj(g * GROUP_W, GROUP_W), qg_ref)
        put(q_refs[g], qn, d)
        kn = head_norm(proj(OFF_K + g * GROUP_W, GROUP_W), kg_ref)
        put(k_refs[g], kn, d)
        v = proj(OFF_V + g * GROUP_W, GROUP_W)
        put(v_refs[g], v, d)
        rb = st_rows[g]
        st_refs[g][:, 0:GROUP_W] = kn[tm - rb:, :]
        st_refs[g][:, GROUP_W:2 * GROUP_W] = v[tm - rb:, :]

    u_ref[...] = proj(OFF_U, POOL_W)
    sga_ref[...] = jax.nn.sigmoid(proj(OFF_GA, D_MODEL)).astype(BF16)
    sgp_ref[...] = jax.nn.sigmoid(proj(OFF_GP, D_MODEL)).astype(BF16)


def _mod_spec(per_row, tm):
    if per_row:
        return pl.BlockSpec((tm, D_MODEL), lambda i: (i, 0))
    return pl.BlockSpec((1, D_MODEL), lambda i: (0, 0))


def _proj(x, g1, sc1, sh1, w_in_bf, bdiag, qg, kg, *, tm, dils, per_row):
    s = x.shape[0]
    nt = s // tm
    wins = tuple(min(w, s) for w, _ in DIL_GROUPS)
    st_rows = tuple(min(tm, w) for w in wins)

    def res_spec(d):
        return pl.BlockSpec((tm // d, d * GROUP_W), lambda i: (i, 0))

    def st_spec(w, rb):
        first = nt - w // rb
        return pl.BlockSpec((rb, 2 * GROUP_W), lambda i: (jnp.maximum(i - first, 0), 0))

    qkv_shapes = [jax.ShapeDtypeStruct((s // d, d * GROUP_W), BF16) for d in dils]
    out_shape = (qkv_shapes * 3
                 + [jax.ShapeDtypeStruct((s, POOL_W), F32),
                    jax.ShapeDtypeStruct((s, D_MODEL), BF16),
                    jax.ShapeDtypeStruct((s, D_MODEL), BF16)]
                 + [jax.ShapeDtypeStruct((w, 2 * GROUP_W), F32) for w in wins])
    out_specs = ([res_spec(d) for d in dils] * 3
                 + [pl.BlockSpec((tm, POOL_W), lambda i: (i, 0)),
                    pl.BlockSpec((tm, D_MODEL), lambda i: (i, 0)),
                    pl.BlockSpec((tm, D_MODEL), lambda i: (i, 0))]
                 + [st_spec(w, rb) for w, rb in zip(wins, st_rows)])
    in_specs = [pl.BlockSpec((tm, D_MODEL), lambda i: (i, 0)),
                _const_spec((1, D_MODEL)),
                _mod_spec(per_row, tm), _mod_spec(per_row, tm),
                pl.BlockSpec((D_MODEL, IN_W), lambda i: (0, 0), pipeline_mode=pl.Buffered(1)),
                _const_spec((GROUP_W // 2, GROUP_W // 2)),
                _const_spec((1, GROUP_W)), _const_spec((1, GROUP_W))]
    outs = pl.pallas_call(
        functools.partial(_proj_kernel, tm=tm, dils=dils, st_rows=st_rows),
        grid=(nt,),
        in_specs=in_specs,
        out_specs=out_specs,
        out_shape=out_shape,
        scratch_shapes=[pltpu.VMEM((GROUP_W // LANES, tm, LANES), F32)],
        compiler_params=_cparams(1),
        name="proj",
    )(x, g1, sc1, sh1, w_in_bf, bdiag, qg, kg)
    return outs[0:3], outs[3:6], outs[6:9], outs[9], outs[10], outs[11], outs[12:15]


def _sample_group_attn(qs_ref, kns_ref, vns_ref, bself_ref, btab_ref, c_ref, os_ref, lses_ref):
    n_tok = qs_ref.shape[0]
    win = btab_ref.shape[1]
    row_w = lax.broadcasted_iota(jnp.int32, (HEADS, win), 0)
    row_e = lax.broadcasted_iota(jnp.int32, (HEADS, HEAD_DIM), 0)
    ss, s0s = [], []
    for t in range(n_tok):
        q = qs_ref[t]
        qb = q.astype(BF16)
        s = jnp.zeros((HEADS, win), F32)
        for h in range(HEADS):
            sh = jnp.dot(qb, c_ref[0, t, 0, h].astype(BF16), preferred_element_type=F32)
            s = jnp.where(row_w == h, sh, s)
        ss.append(s + btab_ref[...])
        s0s.append(jnp.sum(q * kns_ref[t], axis=-1, keepdims=True) + bself_ref[...])
    s = jnp.concatenate(ss, axis=0)
    s0 = jnp.concatenate(s0s, axis=0)
    m = jnp.maximum(jnp.max(s, axis=-1, keepdims=True), s0)
    p = jnp.exp(s - m)
    p0 = jnp.exp(s0 - m)
    l = jnp.sum(p, axis=-1, keepdims=True) + p0
    pb = p.astype(BF16)
    lse = m + jnp.log(l)
    for t in range(n_tok):
        rows = slice(t * HEADS, (t + 1) * HEADS)
        o = jnp.zeros((HEADS, HEAD_DIM), F32)
        for h in range(HEADS):
            oh = lax.dot_general(pb[rows], c_ref[0, t, 1, h].astype(BF16),
                                 (((1,), (1,)), ((), ())), preferred_element_type=F32)
            o = jnp.where(row_e == h, oh, o)
        os_ref[t] = (o + p0[rows] * vns_ref[t]) / l[rows]
        lses_ref[t] = lse[rows]


SAMPLE_IN = 6
SAMPLE_PARTS = (1, 1, 2)
SAMPLE_HOST = (((2, 1),), ((1, 0),), ((2, 0), (0, 0)))


def _attn_kernel(q_ref, kp_ref, kc_ref, vp_ref, vc_ref, r_ref, *refs, n_sample_groups):
    n_in = SAMPLE_IN * n_sample_groups
    o_ref, lse_ref = refs[n_in], refs[n_in + 1]
    bias_ref = refs[-1]
    i = pl.program_id(1)

    for sg in range(n_sample_groups):
        _sample_group_attn(*refs[SAMPLE_IN * sg:SAMPLE_IN * (sg + 1)],
                           *refs[n_in + 2 + 2 * sg:n_in + 4 + 2 * sg])

    @pl.when((pl.program_id(0) == 0) & (i == 0))
    def _():
        for h in range(HEADS):
            row = jnp.broadcast_to(r_ref[h:h + 1, :], (ATT_BLK, 2 * ATT_BLK))
            bias_ref[h] = pltpu.roll(row, 0, 1, stride=1, stride_axis=0)

    q = q_ref[...]
    k = jnp.concatenate([kp_ref[...], kc_ref[...]], axis=0)
    v = jnp.concatenate([vp_ref[...], vc_ref[...]], axis=0)
    col = lax.broadcasted_iota(jnp.int32, (ATT_BLK, 2 * ATT_BLK), 1)
    no_prev = jnp.where((col < ATT_BLK) & (i == 0), NEG_INF, 0.0)
    lane_q = lax.broadcasted_iota(jnp.int32, (ATT_SUB * ATT_BLK, LANES), 1)
    lane_v = lax.broadcasted_iota(jnp.int32, ((ATT_SUB + 1) * ATT_BLK, LANES), 1)

    def pair(h):
        return slice((h // 2) * LANES, (h // 2 + 1) * LANES)

    def mine(lane, h):
        return (lane < HEAD_DIM) == (h % 2 == 0)

    ss = []
    for h in range(HEADS):
        q2 = q[:, pair(h)]
        qm = jnp.where(mine(lane_q, h), q2, jnp.zeros_like(q2))
        k2 = k[:, pair(h)]
        for j in range(ATT_SUB):
            s = lax.dot_general(qm[j * ATT_BLK:(j + 1) * ATT_BLK], k2[j * ATT_BLK:(j + 2) * ATT_BLK],
                                (((1,), (1,)), ((), ())), preferred_element_type=F32)
            s = s + bias_ref[h]
            ss.append(s + no_prev if j == 0 else s)
    s = jnp.concatenate(ss, axis=0)
    m = jnp.max(s, axis=-1, keepdims=True)
    p = jnp.exp(s - m)
    l = jnp.sum(p, axis=-1, keepdims=True)
    pb = p.astype(BF16)
    lse = m + jnp.log(l)
    inv_l = 1.0 / l
    outs, lses = [], []
    for h in range(HEADS):
        v2 = v[:, pair(h)]
        vm = jnp.where(mine(lane_v, h), v2, jnp.zeros_like(v2))
        o_sub, lse_sub = [], []
        for j in range(ATT_SUB):
            rows = slice((h * ATT_SUB + j) * ATT_BLK, (h * ATT_SUB + j + 1) * ATT_BLK)
            o_sub.append(jnp.dot(pb[rows], vm[j * ATT_BLK:(j + 2) * ATT_BLK],
                                 preferred_element_type=F32) * inv_l[rows])
            lse_sub.append(lse[rows])
        o = jnp.concatenate(o_sub, axis=0)
        if h % 2 == 0:
            outs.append(o)
        else:
            outs[-1] = outs[-1] + o
        lses.append(jnp.concatenate(lse_sub, axis=0))
    o_ref[...] = jnp.concatenate(outs, axis=-1).astype(o_ref.dtype)
    lse_ref[...] = jnp.concatenate(
        lses + [jnp.zeros((ATT_SUB * ATT_BLK, LANES - HEADS), F32)], axis=-1)


def _attn(q, k, v, r_tab, d, sample_groups):
    rows = q.shape[0]
    step = ATT_SUB * ATT_BLK
    nblk = rows // step
    cur = pl.BlockSpec((step, GROUP_W), lambda r, i: (i, r))
    prev = pl.BlockSpec((ATT_BLK, GROUP_W), lambda r, i: (jnp.maximum(i * ATT_SUB - 1, 0), r))
    in_specs = [cur, prev, cur, prev, cur,
                pl.BlockSpec((HEADS, 2 * ATT_BLK), lambda r, i: (0, 0))]
    out_specs = [pl.BlockSpec((step, GROUP_W), lambda r, i: (i, r)),
                 pl.BlockSpec((step, LANES), lambda r, i: (i, r))]
    out_shape = [jax.ShapeDtypeStruct((rows, d * GROUP_W), BF16),
                 jax.ShapeDtypeStruct((rows, d * LANES), F32)]
    args = [q, k, k, v, v, r_tab]
    for qs, kns, vns, bself, btab, cache_t, tok_start, n_tok in sample_groups:
        assert n_tok % (d * nblk) == 0
        tok = n_tok // (d * nblk)
        assert tok_start % tok == 0
        first = tok_start // tok

        def in_idx(r, i, first=first):
            return (first + r * nblk + i, 0, 0)

        def out_idx(r, i):
            return (r * nblk + i, 0, 0)

        tok_in = pl.BlockSpec((tok, HEADS, HEAD_DIM), in_idx)
        in_specs += [tok_in, tok_in, tok_in,
                     pl.BlockSpec(bself.shape, lambda r, i: (0, 0)),
                     pl.BlockSpec(btab.shape, lambda r, i: (0, 0)),
                     pl.BlockSpec((1, tok) + cache_t.shape[2:],
                                  lambda r, i, first=first: (0, first + r * nblk + i, 0, 0, 0, 0))]
        out_specs += [pl.BlockSpec((tok, HEADS, HEAD_DIM), out_idx),
                      pl.BlockSpec((tok, HEADS, 1), out_idx)]
        out_shape += [jax.ShapeDtypeStruct((n_tok, HEADS, HEAD_DIM), F32),
                      jax.ShapeDtypeStruct((n_tok, HEADS, 1), F32)]
        args += [qs, kns, vns, bself, btab, cache_t]
    outs = pl.pallas_call(
        functools.partial(_attn_kernel, n_sample_groups=len(sample_groups)),
        grid=(d, nblk),
        in_specs=in_specs,
        out_specs=out_specs,
        out_shape=out_shape,
        scratch_shapes=[pltpu.VMEM((HEADS, ATT_BLK, 2 * ATT_BLK), F32)],
        compiler_params=_cparams(2),
        name=f"attn_d{d}",
    )(*args)
    return outs[0], outs[1], [(outs[2 + 2 * j], outs[3 + 2 * j]) for j in range(len(sample_groups))]


def _pool_sample_kernel(st_ref, u_ref, pooled_ref, new_ref):
    u = u_ref[...]
    rows = [st_ref[0, j] for j in range(POOL_BUF)]
    outs = []
    for g, w in enumerate(POOL_WINDOWS):
        sl = slice(g * POOL_GW, (g + 1) * POOL_GW)
        acc = u[:, sl]
        for j in range(POOL_BUF - (w - 1), POOL_BUF):
            acc = acc + rows[j][:, sl]
        outs.append(acc / float(w) - u[:, sl])
    pooled_ref[...] = jnp.concatenate(outs, axis=-1)
    for j in range(POOL_BUF - 1):
        new_ref[0, j] = rows[j + 1]
    new_ref[0, POOL_BUF - 1] = u


def _pool_sample(state, u):
    n = u.shape[0]
    return pl.pallas_call(
        _pool_sample_kernel,
        grid=(1,),
        in_specs=[_const_spec(state.shape), _const_spec(u.shape)],
        out_specs=[_const_spec(u.shape), _const_spec(state.shape)],
        out_shape=[jax.ShapeDtypeStruct((n, POOL_W), F32),
                   jax.ShapeDtypeStruct(state.shape, F32)],
        compiler_params=_cparams(1),
        name="pool_sample",
    )(state, u)


def _post_kernel(*refs, tm, dils, pooled_given, n_valid_steps, aliased):
    it = iter(refs)
    x_ref = next(it)
    o_refs = [next(it) for _ in range(N_GROUPS)]
    lse_refs = [next(it) for _ in range(N_GROUPS)]
    if pooled_given:
        pooled_ref = next(it)
    else:
        u_ref, uh_ref = next(it), next(it)
    sga_ref, sgp_ref = next(it), next(it)
    wpm_ref, psc_ref, wua_ref, wup_ref, wout_ref, exp_ref = (next(it) for _ in range(6))
    gt1_ref, g2_ref, sc2_ref, sh2_ref = (next(it) for _ in range(4))
    wrh_ref, wrl_ref, br_ref = (next(it) for _ in range(3))
    if aliased:
        for _ in range(6):
            next(it)
    x1_ref, h2_ref, a_ref, idx_ref, gk_ref, cnt_ref = (next(it) for _ in range(6))
    ob_scr, ls_scr = next(it), next(it)

    i = pl.program_id(0)

    def compute():
        obs, lss = [], []
        for g, d in enumerate(dils):
            if d == 1:
                obs.append(o_refs[g][...].astype(F32))
                lss.append(lse_refs[g][...])
            else:
                for r in range(d):
                    for c in range(GROUP_W // LANES):
                        col = r * GROUP_W + c * LANES
                        ob_scr[c, pl.ds(r, tm // d, stride=d), :] = (
                            o_refs[g][:, col:col + LANES].astype(F32))
                    ls_scr[pl.ds(r, tm // d, stride=d), :] = (
                        lse_refs[g][:, r * LANES:(r + 1) * LANES])
                obs.append(jnp.concatenate([ob_scr[c] for c in range(GROUP_W // LANES)],
                                           axis=-1))
                lss.append(ls_scr[...])
        mx = jnp.maximum(jnp.maximum(lss[0], lss[1]), lss[2])
        es = [jnp.exp(l - mx) for l in lss]
        den = es[0] + es[1] + es[2]
        attn_o = jnp.zeros((tm, GROUP_W), F32)
        head_lane = lax.broadcasted_iota(jnp.int32, (tm, LANES), 1) < HEADS
        for g in range(N_GROUPS):
            w = jnp.where(head_lane, es[g] / den, 0.0)
            w_hi = w.astype(BF16).astype(F32)
            w_lo = (w - w_hi).astype(BF16).astype(F32)
            lhs = (w_hi + pltpu.roll(w_lo, HEADS, 1)).astype(BF16)
            wexp = jnp.dot(lhs, exp_ref[...], preferred_element_type=F32)
            attn_o = attn_o + wexp * obs[g]

        if pooled_given:
            pooled = pooled_ref[...]
        else:
            u = u_ref[...]
            halo = jnp.where(i == 0, 0.0, uh_ref[...])
            pos = (lax.broadcasted_iota(jnp.int32, (tm, 1), 0) + i * tm + 1).astype(F32)
            outs = []
            for g, w in enumerate(POOL_WINDOWS):
                sl = slice(g * POOL_GW, (g + 1) * POOL_GW)
                a = jnp.concatenate([halo[:, sl], u[:, sl]], axis=0)
                span = 1
                while span < w:
                    n = a.shape[0] - span
                    a = a[span:, :] + a[:n, :]
                    span *= 2
                off = a.shape[0] - tm
                win_sum = a[off:, :]
                outs.append(win_sum / jnp.minimum(pos, float(w)) - u[:, sl])
            pooled = jnp.concatenate(outs, axis=-1)
        pool_parts = []
        for g in range(len(POOL_WINDOWS)):
            sl = slice(g * POOL_GW, (g + 1) * POOL_GW)
            pool_parts.append(jnp.dot(pooled[:, sl].astype(BF16), wpm_ref[g],
                                      preferred_element_type=F32))
        pool_o = jnp.concatenate(pool_parts, axis=-1) * psc_ref[...]

        up_a = jnp.dot(attn_o.astype(BF16), wua_ref[...], preferred_element_type=F32)
        up_p = jnp.dot(pool_o.astype(BF16), wup_ref[...], preferred_element_type=F32)
        merged = sga_ref[...].astype(F32) * up_a + sgp_ref[...].astype(F32) * up_p
        mo = jnp.dot(merged.astype(BF16), wout_ref[...], preferred_element_type=F32)
        x1 = x_ref[...] + gt1_ref[...] * mo
        x1_ref[...] = x1

        ms = jnp.mean(x1 * x1, axis=-1, keepdims=True)
        h2 = x1 * lax.rsqrt(ms + EPS) * g2_ref[...] * (1.0 + sc2_ref[...]) + sh2_ref[...]
        h2_hi = h2.astype(BF16)
        h2_ref[...] = h2_hi
        h2_lo = (h2 - h2_hi.astype(F32)).astype(BF16)
        logits = (jnp.dot(h2_hi, wrh_ref[...], preferred_element_type=F32)
                  + jnp.dot(h2_lo, wrh_ref[...], preferred_element_type=F32)
                  + jnp.dot(h2_hi, wrl_ref[...], preferred_element_type=F32)
                  + br_ref[...])
        lane = lax.broadcasted_iota(jnp.int32, (tm, LANES), 1).astype(F32)
        work = logits
        vals, ids = [], []
        for _ in range(TOP_K):
            m = jnp.max(work, axis=-1, keepdims=True)
            ik = jnp.min(jnp.where(work == m, lane, float(LANES)), axis=-1, keepdims=True)
            vals.append(m)
            ids.append(ik)
            work = jnp.where(lane == ik, -3e38, work)
        ex = [jnp.exp(v - vals[0]) for v in vals]
        den_k = ex[0] + ex[1] + ex[2] + ex[3]
        a = jnp.zeros((tm, LANES), F32)
        idx = jnp.zeros((tm, LANES), F32)
        gk = jnp.zeros((tm, LANES), F32)
        for kk in range(TOP_K):
            gate = ex[kk] / den_k
            a = a + jnp.where(lane == ids[kk], gate, 0.0)
            idx = jnp.where(lane == float(kk), ids[kk], idx)
            gk = jnp.where(lane == float(kk), gate, gk)
        a_ref[...] = a
        idx_ref[...] = idx
        gk_ref[...] = gk
        cnt = jnp.sum((a > 0.0).astype(F32), axis=0, keepdims=True)
        row = lax.broadcasted_iota(jnp.int32, (8, LANES), 0)
        cnt_ref[0] = jnp.where(row == 0, jnp.broadcast_to(cnt, (8, LANES)), 0.0)

    if n_valid_steps is None:
        compute()
    else:
        pl.when(i < n_valid_steps)(compute)

        @pl.when(i >= n_valid_steps)
        def _():
            x1_ref[...] = jnp.zeros(x1_ref.shape, x1_ref.dtype)
            h2_ref[...] = jnp.zeros(h2_ref.shape, h2_ref.dtype)
            a_ref[...] = jnp.zeros(a_ref.shape, a_ref.dtype)
            idx_ref[...] = jnp.zeros(idx_ref.shape, idx_ref.dtype)
            gk_ref[...] = jnp.zeros(gk_ref.shape, gk_ref.dtype)


def _post(x, o_list, lse_list, pool_in, sga, sgp, wts, mods, *, tm, dils, per_row,
          rows_total, row_block0, cnt_tiles, cnt_block, grid, n_valid_steps, alias_bufs):
    pooled_given = not isinstance(pool_in, tuple)
    nv = grid if n_valid_steps is None else n_valid_steps

    def clamp(i):
        return jnp.minimum(i, nv - 1)

    def tile_spec(width):
        return pl.BlockSpec((tm, width), lambda i: (clamp(i), 0))

    in_specs = [tile_spec(D_MODEL)]
    in_specs += [pl.BlockSpec((tm // d, d * GROUP_W), lambda i: (clamp(i), 0)) for d in dils]
    in_specs += [pl.BlockSpec((tm // d, d * LANES), lambda i: (clamp(i), 0)) for d in dils]
    args = [x, *o_list, *lse_list]
    if pooled_given:
        in_specs.append(tile_spec(POOL_W))
        args.append(pool_in)
    else:
        u = pool_in[0]
        in_specs += [tile_spec(POOL_W),
                     pl.BlockSpec((16, POOL_W),
                                  lambda i: (jnp.maximum(i * (tm // 16) - 1, 0), 0))]
        args += [u, u]
    in_specs += [tile_spec(D_MODEL), tile_spec(D_MODEL)]
    args += [sga, sgp]
    wpm, psc, wua, wup, wout, expand, g2, wrh, wrl, br = wts
    gt1, sc2, sh2 = mods

    def mspec():
        if per_row:
            return pl.BlockSpec((tm, D_MODEL), lambda i: (clamp(i), 0))
        return _const_spec((1, D_MODEL))

    in_specs += [_const_spec(wpm.shape), _const_spec(psc.shape), _const_spec(wua.shape),
                 _const_spec(wup.shape), _const_spec(wout.shape), _const_spec(expand.shape),
                 mspec(), _const_spec(g2.shape), mspec(), mspec(),
                 _const_spec(wrh.shape), _const_spec(wrl.shape), _const_spec(br.shape)]
    args += [wpm, psc, wua, wup, wout, expand, gt1, g2, sc2, sh2, wrh, wrl, br]
    aliases = {}
    if alias_bufs is not None:
        base = len(args)
        in_specs += [pl.BlockSpec(memory_space=pl.ANY)] * 6
        args += list(alias_bufs)
        aliases = {base + j: j for j in range(6)}

    def out_spec(width):
        return pl.BlockSpec((tm, width), lambda i: (row_block0 + i, 0))

    out_specs = [out_spec(D_MODEL), out_spec(D_MODEL), out_spec(LANES), out_spec(LANES),
                 out_spec(LANES),
                 pl.BlockSpec((1, 8, LANES),
                              lambda i: (cnt_block if cnt_block is not None else i, 0, 0))]
    out_shape = [jax.ShapeDtypeStruct((rows_total, D_MODEL), F32),
                 jax.ShapeDtypeStruct((rows_total, D_MODEL), BF16),
                 jax.ShapeDtypeStruct((rows_total, LANES), F32),
                 jax.ShapeDtypeStruct((rows_total, LANES), F32),
                 jax.ShapeDtypeStruct((rows_total, LANES), F32),
                 jax.ShapeDtypeStruct((cnt_tiles, 8, LANES), F32)]
    return pl.pallas_call(
        functools.partial(_post_kernel, tm=tm, dils=dils, pooled_given=pooled_given,
                          n_valid_steps=n_valid_steps, aliased=alias_bufs is not None),
        grid=(grid,),
        in_specs=in_specs,
        out_specs=out_specs,
        out_shape=out_shape,
        scratch_shapes=[pltpu.VMEM((GROUP_W // LANES, tm, LANES), F32),
                        pltpu.VMEM((tm, LANES), F32)],
        input_output_aliases=aliases,
        compiler_params=_cparams(1),
        name="post_sample" if per_row else "post",
    )(*args)


def _sort_rows(tm):
    return -(-(TOP_K * tm + N_EXPERTS * (ROW_CHUNK - 1)) // SEL_CHUNK) * SEL_CHUNK


def _for_row_pieces(n_chunks, max_pow, fn):
    big = 1 << max_pow

    def body(c, carry):
        fn(c * big, big)
        return carry

    lax.fori_loop(0, n_chunks >> max_pow, body, 0)
    for pw in range(max_pow - 1, -1, -1):
        @pl.when(((n_chunks >> pw) & 1) == 1)
        def _(pw=pw):
            fn((n_chunks >> (pw + 1)) << (pw + 1), 1 << pw)


SEG_MAX_POW = 3
TILE_MAX_POW = 5


def _moe_sort_kernel(seg_s, goff_s, nch_s, ntot_s, tstart_s, tnch_s,
                     a_ref, idx_ref, h2_ref, segv_ref, lt_ref,
                     xb_hbm, dst_ref, xs_scr, zero_scr, sem, *, tm, n_rows):
    i = pl.program_id(0)
    nt = pl.num_programs(0)
    slot = i % 2
    sel = a_ref[...] > 0.0
    ahead = jnp.dot(lt_ref[...], sel.astype(BF16), preferred_element_type=F32)
    slot1 = jnp.where(sel, segv_ref[0] + ahead + 1.0, 0.0)
    lane = lax.broadcasted_iota(jnp.int32, (tm, LANES), 1).astype(F32)
    idx = idx_ref[...]
    dst = jnp.full((tm, LANES), -1.0, F32)
    for kk in range(TOP_K):
        hit = lane == idx[:, kk:kk + 1]
        dk = jnp.sum(jnp.where(hit, slot1, 0.0), axis=-1, keepdims=True) - 1.0
        dst = jnp.where(lane == float(kk), dk, dst)
    dst_ref[...] = dst
    dst_t = dst.T
    h2 = h2_ref[...]
    for c in range(n_rows // SEL_CHUNK):
        rows = (lax.broadcasted_iota(jnp.int32, (SEL_CHUNK, tm), 0) + c * SEL_CHUNK).astype(F32)
        p = rows == dst_t[0:1, :]
        for kk in range(1, TOP_K):
            p = p | (rows == dst_t[kk:kk + 1, :])
        xs = jnp.dot(jnp.where(p, 1.0, 0.0).astype(BF16), h2, preferred_element_type=F32)
        xs_scr[slot, c * SEL_CHUNK:(c + 1) * SEL_CHUNK, :] = xs.astype(BF16)

    def rows_copy(buf, src_row, dst_row, n_chunks):
        return pltpu.make_async_copy(
            xs_scr.at[buf, pl.ds(pl.multiple_of(src_row, ROW_CHUNK), n_chunks * ROW_CHUNK)],
            xb_hbm.at[pl.ds(pl.multiple_of(dst_row, ROW_CHUNK), n_chunks * ROW_CHUNK)],
            sem.at[buf])

    def per_expert(e, carry):
        so = seg_s[i * N_EXPERTS + e]
        go = goff_s[i * N_EXPERTS + e]
        _for_row_pieces(
            nch_s[i * N_EXPERTS + e], SEG_MAX_POW,
            lambda off, n: rows_copy(slot, so + off * ROW_CHUNK, go + off * ROW_CHUNK, n).start())
        return carry

    lax.fori_loop(0, N_EXPERTS, per_expert, 0)

    def drain(buf, tile):
        _for_row_pieces(ntot_s[tile], TILE_MAX_POW, lambda off, n: rows_copy(buf, 0, 0, n).wait())

    @pl.when(i > 0)
    def _():
        drain(1 - slot, i - 1)

    @pl.when(i == nt - 1)
    def _():
        drain(slot, i)
        zero_scr[...] = jnp.zeros(zero_scr.shape, zero_scr.dtype)

        def tail_copy(dst_row):
            return pltpu.make_async_copy(
                zero_scr, xb_hbm.at[pl.ds(pl.multiple_of(dst_row, ROW_CHUNK), ROW_CHUNK)],
                sem.at[2])

        def per_expert_tail(e, carry):
            def per_chunk(c, carry2):
                tail_copy(tstart_s[e] + c * ROW_CHUNK).start()
                return carry2

            lax.fori_loop(0, tnch_s[e], per_chunk, 0)

            def wait_chunk(c, carry2):
                tail_copy(0).wait()
                return carry2

            return lax.fori_loop(0, tnch_s[e], wait_chunk, carry)

        lax.fori_loop(0, N_EXPERTS, per_expert_tail, 0)


def _moe_sort(meta, a_all, idx_all, h2_all, *, tm, cap):
    t_all = a_all.shape[0]
    nt = t_all // tm
    n_rows = _sort_rows(tm)
    lt = jnp.tril(jnp.ones((tm, tm), BF16), -1)
    grid_spec = pltpu.PrefetchScalarGridSpec(
        num_scalar_prefetch=6,
        grid=(nt,),
        in_specs=[pl.BlockSpec((tm, LANES), lambda i, *_: (i, 0)),
                  pl.BlockSpec((tm, LANES), lambda i, *_: (i, 0)),
                  pl.BlockSpec((tm, D_MODEL), lambda i, *_: (i, 0)),
                  pl.BlockSpec((1, 1, LANES), lambda i, *_: (i, 0, 0)),
                  pl.BlockSpec((tm, tm), lambda i, *_: (0, 0))],
        out_specs=[pl.BlockSpec(memory_space=pl.ANY),
                   pl.BlockSpec((tm, LANES), lambda i, *_: (i, 0))],
        scratch_shapes=[pltpu.VMEM((2, n_rows, D_MODEL), BF16),
                        pltpu.VMEM((ROW_CHUNK, D_MODEL), BF16),
                        pltpu.SemaphoreType.DMA((3,))],
    )
    return pl.pallas_call(
        functools.partial(_moe_sort_kernel, tm=tm, n_rows=n_rows),
        grid_spec=grid_spec,
        out_shape=[jax.ShapeDtypeStruct((cap, D_MODEL), BF16),
                   jax.ShapeDtypeStruct((t_all, LANES), F32)],
        compiler_params=_cparams(1),
        name="moe_sort",
    )(meta["seg"], meta["goff"], meta["nch"], meta["ntot"], meta["tstart"], meta["tnch"],
      a_all, idx_all, h2_all, meta["segv"], lt)


def _moe_ffn_kernel(be_s, nused_s, ord_s, next_s, x_ref, wgu_hbm, bgu_ref, wd_hbm, bd_ref, y_ref,
                    wgu_f32, wd_f32, wgu_bf, wd_bf, sem):
    b = pl.program_id(0)

    def weight_copies(e, slot):
        return (pltpu.make_async_copy(wgu_hbm.at[e], wgu_f32.at[slot], sem.at[0, slot]),
                pltpu.make_async_copy(wd_hbm.at[e], wd_f32.at[slot], sem.at[1, slot]))

    @pl.when(b < nused_s[0])
    def _():
        e = be_s[b]
        e_prev = be_s[jnp.maximum(b - 1, 0)]
        slot = ord_s[b] % 2

        @pl.when(b == 0)
        def _():
            for cp in weight_copies(e, slot):
                cp.start()

        @pl.when((b == 0) | (e != e_prev))
        def _():
            for cp in weight_copies(e, slot):
                cp.wait()
            e_next = next_s[b]

            @pl.when(e_next >= 0)
            def _():
                for cp in weight_copies(e_next, 1 - slot):
                    cp.start()

            wgu_bf[...] = wgu_f32[slot].astype(BF16)
            wd_bf[...] = wd_f32[slot].astype(BF16)

        hgu = jnp.dot(x_ref[...], wgu_bf[...], preferred_element_type=F32) + bgu_ref[0]
        d_ff = hgu.shape[1] // 2
        hg = jnp.minimum(hgu[:, :d_ff], SWIGLU_LIMIT)
        hu = jnp.clip(hgu[:, d_ff:], -SWIGLU_LIMIT, SWIGLU_LIMIT)
        act = hg * jax.nn.sigmoid(SWIGLU_ALPHA * hg) * (hu + 1.0)
        y = jnp.dot(act.astype(BF16), wd_bf[...], preferred_element_type=F32) + bd_ref[0]
        y_ref[...] = y.astype(y_ref.dtype)


def _moe_ffn(meta, xb, w_gate_up, b_gate_up, w_down, b_down):
    cap = xb.shape[0]
    nb = cap // FFN_BLOCK
    d_ff2 = w_gate_up.shape[2]

    def blk(b, be, nu, *_):
        return jnp.minimum(b, jnp.maximum(nu[0] - 1, 0))

    def row_blk(b, be, nu, *_):
        return (blk(b, be, nu), 0)

    def expert_blk(b, be, nu, *_):
        return (be[blk(b, be, nu)], 0, 0)

    grid_spec = pltpu.PrefetchScalarGridSpec(
        num_scalar_prefetch=4,
        grid=(nb,),
        in_specs=[pl.BlockSpec((FFN_BLOCK, D_MODEL), row_blk),
                  pl.BlockSpec(memory_space=pl.ANY),
                  pl.BlockSpec((1, 1, d_ff2), expert_blk),
                  pl.BlockSpec(memory_space=pl.ANY),
                  pl.BlockSpec((1, 1, D_MODEL), expert_blk)],
        out_specs=pl.BlockSpec((FFN_BLOCK, D_MODEL), row_blk),
        scratch_shapes=[pltpu.VMEM((2, D_MODEL, d_ff2), F32),
                        pltpu.VMEM((2, d_ff2 // 2, D_MODEL), F32),
                        pltpu.VMEM((D_MODEL, d_ff2), BF16),
                        pltpu.VMEM((d_ff2 // 2, D_MODEL), BF16),
                        pltpu.SemaphoreType.DMA((2, 2))],
    )
    return pl.pallas_call(
        _moe_ffn_kernel,
        grid_spec=grid_spec,
        out_shape=jax.ShapeDtypeStruct((cap, D_MODEL), BF16),
        compiler_params=_cparams(1),
        name="moe_ffn",
    )(meta["block_expert"], meta["n_used"], meta["block_ord"], meta["block_next"], xb, w_gate_up,
      b_gate_up.reshape(N_EXPERTS, 1, d_ff2), w_down, b_down.reshape(N_EXPERTS, 1, D_MODEL))


def _moe_unsort_kernel(seg_s, goff_s, nch_s, ntot_s,
                       dst_ref, gk_ref, x1_ref, g2p_ref, g2s_ref, yb_hbm,
                       yp_ref, ys_ref, ybuf, sem, *, tm, n_rows, n_prompt_tiles):
    i = pl.program_id(0)
    nt = pl.num_programs(0)
    slot = i % 2

    def rows_copy(buf, src_row, dst_row, n_chunks):
        return pltpu.make_async_copy(
            yb_hbm.at[pl.ds(pl.multiple_of(src_row, ROW_CHUNK), n_chunks * ROW_CHUNK)],
            ybuf.at[buf, pl.ds(pl.multiple_of(dst_row, ROW_CHUNK), n_chunks * ROW_CHUNK)],
            sem.at[buf])

    def fetch(tile, buf):
        def per_expert(e, carry):
            so = seg_s[tile * N_EXPERTS + e]
            go = goff_s[tile * N_EXPERTS + e]
            _for_row_pieces(
                nch_s[tile * N_EXPERTS + e], SEG_MAX_POW,
                lambda off, n: rows_copy(buf, go + off * ROW_CHUNK, so + off * ROW_CHUNK,
                                         n).start())
            return carry

        lax.fori_loop(0, N_EXPERTS, per_expert, 0)

    @pl.when(i == 0)
    def _():
        ybuf[...] = jnp.zeros(ybuf.shape, ybuf.dtype)
        fetch(0, 0)

    @pl.when(i + 1 < nt)
    def _():
        fetch(i + 1, 1 - slot)

    _for_row_pieces(ntot_s[i], TILE_MAX_POW, lambda off, n: rows_copy(slot, 0, 0, n).wait())

    dst = dst_ref[...]
    gk = gk_ref[...]
    acc = jnp.zeros((tm, D_MODEL), F32)
    for c in range(n_rows // SEL_CHUNK):
        cols = (lax.broadcasted_iota(jnp.int32, (tm, SEL_CHUNK), 1) + c * SEL_CHUNK).astype(F32)
        q = jnp.zeros((tm, SEL_CHUNK), F32)
        for kk in range(TOP_K):
            q = jnp.where(cols == dst[:, kk:kk + 1], gk[:, kk:kk + 1], q)
        acc = acc + jnp.dot(q.astype(BF16), ybuf[slot, c * SEL_CHUNK:(c + 1) * SEL_CHUNK, :],
                            preferred_element_type=F32)

    @pl.when(i < n_prompt_tiles)
    def _():
        yp_ref[...] = x1_ref[...] + g2p_ref[...] * acc

    @pl.when(i >= n_prompt_tiles)
    def _():
        ys_ref[...] = x1_ref[...] + g2s_ref[...] * acc


def _moe_unsort(meta, dst_all, gk_all, x1_all, gt2_p, gt2_s, yb, *, tm, n_prompt_tiles):
    t_all = dst_all.shape[0]
    nt = t_all // tm
    n_rows = _sort_rows(tm)
    last_p = n_prompt_tiles - 1
    grid_spec = pltpu.PrefetchScalarGridSpec(
        num_scalar_prefetch=4,
        grid=(nt,),
        in_specs=[pl.BlockSpec((tm, LANES), lambda i, *_: (i, 0)),
                  pl.BlockSpec((tm, LANES), lambda i, *_: (i, 0)),
                  pl.BlockSpec((tm, D_MODEL), lambda i, *_: (i, 0)),
                  pl.BlockSpec((1, D_MODEL), lambda i, *_: (0, 0)),
                  pl.BlockSpec((tm, D_MODEL), lambda i, *_: (0, 0)),
                  pl.BlockSpec(memory_space=pl.ANY)],
        out_specs=[pl.BlockSpec((tm, D_MODEL), lambda i, *_: (jnp.minimum(i, last_p), 0)),
                   pl.BlockSpec((tm, D_MODEL), lambda i, *_: (0, 0))],
        scratch_shapes=[pltpu.VMEM((2, n_rows, D_MODEL), BF16),
                        pltpu.SemaphoreType.DMA((2,))],
    )
    return pl.pallas_call(
        functools.partial(_moe_unsort_kernel, tm=tm, n_rows=n_rows,
                          n_prompt_tiles=n_prompt_tiles),
        grid_spec=grid_spec,
        out_shape=[jax.ShapeDtypeStruct((n_prompt_tiles * tm, D_MODEL), F32),
                   jax.ShapeDtypeStruct((tm, D_MODEL), F32)],
        compiler_params=_cparams(1),
        name="moe_unsort",
    )(meta["seg"], meta["goff"], meta["nch"], meta["ntot"],
      dst_all, gk_all, x1_all, gt2_p, gt2_s, yb)


def _moe_meta(cnt, tm):
    nt = cnt.shape[0]
    cnt = cnt.astype(jnp.int32)
    cnt_pad = (cnt + ROW_CHUNK - 1) // ROW_CHUNK * ROW_CHUNK
    seg = jnp.cumsum(cnt_pad, axis=1) - cnt_pad
    rows_e = jnp.sum(cnt_pad, axis=0)
    region = (rows_e + FFN_BLOCK - 1) // FFN_BLOCK * FFN_BLOCK
    gstart = jnp.cumsum(region) - region
    goff = gstart[None, :] + jnp.cumsum(cnt_pad, axis=0) - cnt_pad
    nblk_e = region // FFN_BLOCK
    blk_end = jnp.cumsum(nblk_e)
    cap = _moe_cap(nt * tm, tm)
    blocks = jnp.arange(cap // FFN_BLOCK, dtype=jnp.int32)
    block_expert = jnp.minimum(
        jnp.sum((blk_end[None, :] <= blocks[:, None]).astype(jnp.int32), axis=1), N_EXPERTS - 1)
    segv = jnp.zeros((nt, 1, LANES), F32).at[:, 0, :N_EXPERTS].set(seg.astype(F32))
    used = nblk_e > 0
    experts = jnp.arange(N_EXPERTS, dtype=jnp.int32)
    ord_e = jnp.cumsum(used.astype(jnp.int32)) - 1
    later = (experts[None, :] > experts[:, None]) & used[None, :]
    next_e = jnp.min(jnp.where(later, experts[None, :], N_EXPERTS), axis=1)
    next_e = jnp.where(next_e < N_EXPERTS, next_e, -1).astype(jnp.int32)
    of_block = (block_expert[:, None] == experts[None, :]).astype(jnp.int32)
    return {
        "block_ord": jnp.sum(of_block * ord_e[None, :].astype(jnp.int32), axis=1),
        "block_next": jnp.sum(of_block * next_e[None, :], axis=1),
        "seg": seg.reshape(-1), "goff": goff.reshape(-1).astype(jnp.int32),
        "nch": (cnt_pad // ROW_CHUNK).reshape(-1),
        "ntot": jnp.sum(cnt_pad, axis=1) // ROW_CHUNK,
        "tstart": (gstart + rows_e).astype(jnp.int32),
        "tnch": (region - rows_e) // ROW_CHUNK,
        "block_expert": block_expert,
        "n_used": blk_end[-1:].astype(jnp.int32),
        "segv": segv,
    }


def _moe_cap(t_all, tm):
    nt = t_all // tm
    worst = TOP_K * t_all + nt * N_EXPERTS * (ROW_CHUNK - 1) + N_EXPERTS * (FFN_BLOCK - ROW_CHUNK)
    return -(-worst // FFN_BLOCK) * FFN_BLOCK


def _t5_bucket(dist):
    max_exact = NUM_BUCKETS // 2
    d = dist.astype(jnp.int32)
    ratio = (jnp.log(jnp.maximum(d, 1).astype(F32) / max_exact)
             / math.log(MAX_DISTANCE / max_exact))
    large = jnp.minimum(max_exact + (ratio * (NUM_BUCKETS - max_exact)).astype(jnp.int32),
                        NUM_BUCKETS - 1)
    return jnp.where(d < max_exact, d, large)


def _step_bias(tab, dil):
    return tab[_t5_bucket(dil * jnp.arange(ATT_BLK + 1))].astype(F32).T


def _band_table(sb):
    return jnp.concatenate([sb[:, ::-1], jnp.full((HEADS, ATT_BLK - 1), NEG_INF, F32)], axis=1)


def _cache_table(sb, dil):
    on_grid = sb[:, :0:-1]
    if dil == 1:
        return on_grid
    off = jnp.full((HEADS, ATT_BLK, dil - 1), NEG_INF, F32)
    return jnp.concatenate([on_grid[:, :, None], off], axis=2).reshape(HEADS, ATT_BLK * dil)


def kernel(x_prompt, x_sample, cache_kv_w128, cache_kv_w512, cache_kv_w2048, state_pool, c_prompt,
           c_sample, w_ada, b_ada, norm_mix_g, norm_ffn_g, w_in, q_norm_g, k_norm_g, rel_bias,
           w_pool_mix, pool_scale, w_up_attn, w_up_pool, w_out, w_router, b_router, w_gate_up,
           b_gate_up, w_down, b_down):
    assert w_ada.shape[0] == 1, "one layer"
    seq = x_prompt.shape[1]
    n_s = x_sample.shape[0]
    assert x_prompt.shape[0] == 1 and x_sample.shape[1] == 1
    assert seq % (DIL_GROUPS[-1][1] * ATT_BLK * ATT_SUB) == 0 and seq % TM_PROMPT == 0
    assert n_s == TM_SAMPLE
    dils = tuple(d for _, d in DIL_GROUPS)
    caches = (cache_kv_w128, cache_kv_w512, cache_kv_w2048)

    w_in_bf = w_in[0].astype(BF16)
    heads_of = jnp.arange(GROUP_W) // HEAD_DIM
    half_heads = heads_of[:GROUP_W // 2]
    bdiag = (half_heads[:, None] == half_heads[None, :]).astype(BF16)
    qg = (jnp.tile(q_norm_g[0], HEADS) * SCALE).reshape(1, GROUP_W)
    kg = jnp.tile(k_norm_g[0], HEADS).reshape(1, GROUP_W)
    expand = ((jnp.arange(LANES)[:, None] % HEADS == heads_of[None, :])
              & (jnp.arange(LANES)[:, None] < 2 * HEADS)).astype(BF16)
    wr = jnp.zeros((D_MODEL, LANES), F32).at[:, :N_EXPERTS].set(w_router[0])
    wr_hi = wr.astype(BF16)
    wr_lo = (wr - wr_hi.astype(F32)).astype(BF16)
    br = jnp.full((1, LANES), NEG_INF, F32).at[0, :N_EXPERTS].set(b_router[0])
    wts = (w_pool_mix[0].astype(BF16), pool_scale[0].reshape(1, POOL_W),
           w_up_attn[0].astype(BF16), w_up_pool[0].astype(BF16), w_out[0].astype(BF16), expand,
           norm_ffn_g[0].reshape(1, D_MODEL), wr_hi, wr_lo, br)
    g1 = norm_mix_g[0].reshape(1, D_MODEL)

    n_c = 1 + n_s
    c_all = jnp.zeros((-(-n_c // 8) * 8, D_MODEL), F32).at[0:1].set(c_prompt).at[1:n_c].set(c_sample)
    mod = _ada(c_all, w_ada[0], b_ada[0])
    sh1, sc1, gt1, sh2, sc2, gt2 = jnp.split(mod, N_ADA, axis=-1)

    def prow(m):
        return m[0:1]

    def srows(m):
        return m[1:n_c]

    xp = x_prompt[0]
    q_p, k_p, v_p, u_p, sga_p, sgp_p, st_p = _proj(
        xp, g1, prow(sc1), prow(sh1), w_in_bf, bdiag, qg, kg,
        tm=TM_PROMPT, dils=dils, per_row=False)
    xs = x_sample[:, 0]
    ones = (1, 1, 1)
    q_s, _, _, u_s, sga_s, sgp_s, st_s = _proj(
        xs, g1, srows(sc1), srows(sh1), w_in_bf, bdiag, qg, kg,
        tm=TM_SAMPLE, dils=ones, per_row=True)
    pooled_s, pool_state_t = _pool_sample(jnp.transpose(state_pool, (0, 2, 1, 3)), u_s)
    pool_state_s = jnp.transpose(pool_state_t, (0, 2, 1, 3))

    def heads(a):
        return a.astype(F32).reshape(n_s, HEADS, HEAD_DIM)

    step_bias = [_step_bias(rel_bias[:, g * HEADS:(g + 1) * HEADS], d)
                 for g, (_, d) in enumerate(DIL_GROUPS)]

    def sample_group(g, part, n_parts):
        cache_t = jnp.transpose(caches[g], (0, 1, 3, 4, 5, 2))
        n_sub = n_s // n_parts
        return (heads(q_s[g]), heads(st_s[g][:, :GROUP_W]), heads(st_s[g][:, GROUP_W:]),
                step_bias[g][:, 0:1], _cache_table(step_bias[g], DIL_GROUPS[g][1]), cache_t,
                part * n_sub, n_sub)

    o_p, lse_p = [], []
    o_parts = [[None] * n for n in SAMPLE_PARTS]
    lse_parts = [[None] * n for n in SAMPLE_PARTS]
    for g, (_, d) in enumerate(DIL_GROUPS):
        hosted = SAMPLE_HOST[g]
        o, lse, sample_outs = _attn(
            q_p[g], k_p[g], v_p[g], _band_table(step_bias[g]), d,
            [sample_group(sg, part, SAMPLE_PARTS[sg]) for sg, part in hosted])
        o_p.append(o)
        lse_p.append(lse)
        for (sg, part), (os_g, lses_g) in zip(hosted, sample_outs):
            o_parts[sg][part] = os_g.reshape(-1, GROUP_W)
            lse_parts[sg][part] = lses_g[:, :, 0]
    o_s = [jnp.concatenate(parts, axis=0) for parts in o_parts]
    lse_s = [jnp.zeros((n_s, LANES), F32).at[:, :HEADS].set(jnp.concatenate(parts, axis=0))
             for parts in lse_parts]

    nt_p = seq // TM_PROMPT
    t_all = seq + TM_PROMPT
    bufs = _post(xp, o_p, lse_p, (u_p,), sga_p, sgp_p, wts, (prow(gt1), prow(sc2), prow(sh2)),
                 tm=TM_PROMPT, dils=dils, per_row=False, rows_total=t_all, row_block0=0,
                 cnt_tiles=nt_p + 1, cnt_block=None, grid=nt_p, n_valid_steps=None,
                 alias_bufs=None)
    bufs = _post(xs, o_s, lse_s, pooled_s, sga_s, sgp_s, wts,
                 (srows(gt1), srows(sc2), srows(sh2)),
                 tm=TM_SAMPLE, dils=ones, per_row=True, rows_total=t_all,
                 row_block0=seq // TM_SAMPLE, cnt_tiles=nt_p + 1, cnt_block=nt_p,
                 grid=TM_PROMPT // TM_SAMPLE, n_valid_steps=1, alias_bufs=bufs)
    x1_all, h2_all, a_all, idx_all, gk_all, cnt = bufs

    meta = _moe_meta(cnt[:, 0, :N_EXPERTS], TM_PROMPT)
    cap = _moe_cap(t_all, TM_PROMPT)
    xb, dst_all = _moe_sort(meta, a_all, idx_all, h2_all, tm=TM_PROMPT, cap=cap)
    yb = _moe_ffn(meta, xb, w_gate_up[0], b_gate_up[0], w_down[0], b_down[0])
    gt2_s = jnp.zeros((TM_PROMPT, D_MODEL), F32).at[:n_s].set(srows(gt2))
    y_p, y_s = _moe_unsort(meta, dst_all, gk_all, x1_all, prow(gt2), gt2_s, yb,
                           tm=TM_PROMPT, n_prompt_tiles=nt_p)

    def kv_state(st, rows):
        return st.reshape(1, 1, rows, 2, HEADS, HEAD_DIM)

    kv_p = [kv_state(st, st.shape[0]) for st in st_p]
    kv_s = [st.reshape(1, n_s, 1, 2, HEADS, HEAD_DIM) for st in st_s]
    pool_p = u_p[seq - POOL_BUF:].reshape(1, 1, POOL_BUF, POOL_W)
    return (y_p.reshape(1, seq, D_MODEL), y_s[:n_s].reshape(n_s, 1, D_MODEL),
            kv_p[0], kv_p[1], kv_p[2], pool_p, kv_s[0], kv_s[1], kv_s[2], pool_state_s)
```

```python
import functools
import math

import jax
import jax.numpy as jnp
from jax import lax
from jax.experimental import pallas as pl
from jax.experimental.pallas import tpu as pltpu

F32 = jnp.float32
BF16 = jnp.bfloat16

D_MODEL = 1024
HEAD_DIM = 64
HEADS = 8
GROUP_W = HEADS * HEAD_DIM
DIL_GROUPS = ((128, 1), (512, 4), (2048, 16))
N_GROUPS = len(DIL_GROUPS)
QKV_W = N_GROUPS * GROUP_W
ATT_BLK = 128
ATT_SUB = 4
POOL_WINDOWS = (2, 4, 8, 16)
POOL_W = 512
POOL_GW = 128
POOL_BUF = 15
OFF_K, OFF_V = QKV_W, 2 * QKV_W
OFF_U = 3 * QKV_W
OFF_GA = OFF_U + POOL_W
OFF_GP = OFF_GA + D_MODEL
IN_W = OFF_GP + D_MODEL
NUM_BUCKETS = 32
MAX_DISTANCE = 2048
N_EXPERTS = 32
TOP_K = 4
SWIGLU_LIMIT = 7.0
SWIGLU_ALPHA = 1.702
N_ADA = 6
EPS = 1e-6
NEG_INF = -1e30
PAST_LEN = 8192
SCALE = HEAD_DIM ** -0.5

LANES = 128
ROW_CHUNK = 16
TM_PROMPT = 512
TM_SAMPLE = 128
FFN_BLOCK = 512
FFN_SUB = 128
SEL_CHUNK = 512
VMEM_LIMIT = 56 * 1024 * 1024


def _cparams(n_axes):
    return pltpu.CompilerParams(dimension_semantics=("arbitrary",) * n_axes,
                                vmem_limit_bytes=VMEM_LIMIT)


def _const_spec(shape):
    nd = len(shape)
    return pl.BlockSpec(shape, lambda *_: (0,) * nd)


def _ada_kernel(c_ref, w_ref, b_ref, o_ref):
    c = c_ref[...]
    s = c * jax.nn.sigmoid(c)
    o_ref[...] = jnp.dot(s.astype(BF16), w_ref[...].astype(BF16),
                         preferred_element_type=F32) + b_ref[...]


def _ada(c_all, w_ada, b_ada):
    rows = c_all.shape[0]
    n = w_ada.shape[1]
    tn = 1536
    return pl.pallas_call(
        _ada_kernel,
        grid=(n // tn,),
        in_specs=[pl.BlockSpec((rows, D_MODEL), lambda j: (0, 0)),
                  pl.BlockSpec((D_MODEL, tn), lambda j: (0, j)),
                  pl.BlockSpec((1, tn), lambda j: (0, j))],
        out_specs=pl.BlockSpec((rows, tn), lambda j: (0, j)),
        out_shape=jax.ShapeDtypeStruct((rows, n), F32),
        compiler_params=_cparams(1),
        name="ada",
    )(c_all, w_ada, b_ada.reshape(1, n))


def _proj_kernel(x_ref, g_ref, sc_ref, sh_ref, w_ref, bd_ref, qg_ref, kg_ref,
                 *refs, tm, dils, st_rows):
    q_refs, k_refs, v_refs = refs[0:3], refs[3:6], refs[6:9]
    u_ref, sga_ref, sgp_ref = refs[9:12]
    st_refs = refs[12:15]
    scr = refs[15]

    x = x_ref[...]
    ms = jnp.mean(x * x, axis=-1, keepdims=True)
    h = x * lax.rsqrt(ms + EPS) * g_ref[...] * (1.0 + sc_ref[...]) + sh_ref[...]
    hb = h.astype(BF16)

    def proj(off, width):
        return jnp.dot(hb, w_ref[:, off:off + width], preferred_element_type=F32)

    def head_norm(z, gain_ref):
        zz = (z * z).astype(BF16)
        half = GROUP_W // 2
        ss = jnp.concatenate(
            [jnp.dot(zz[:, :half], bd_ref[...], preferred_element_type=F32),
             jnp.dot(zz[:, half:], bd_ref[...], preferred_element_type=F32)], axis=1)
        return z * lax.rsqrt(ss * (1.0 / HEAD_DIM) + EPS) * gain_ref[...]

    def put(out_ref, val, d):
        if d == 1:
            out_ref[...] = val.astype(out_ref.dtype)
        else:
            for c in range(GROUP_W // LANES):
                scr[c] = val[:, c * LANES:(c + 1) * LANES]
            for r in range(d):
                for c in range(GROUP_W // LANES):
                    col = r * GROUP_W + c * LANES
                    out_ref[:, col:col + LANES] = (
                        scr[c, pl.ds(r, tm // d, stride=d), :].astype(out_ref.dtype))

    for g, d in enumerate(dils):
        qn = head_norm(proj(g * GROUP_W, GROUP_W), qg_ref)
        put(q_refs[g], qn, d)
        kn = head_norm(proj(OFF_K + g * GROUP_W, GROUP_W), kg_ref)
        put(k_refs[g], kn, d)
        v = proj(OFF_V + g * GROUP_W, GROUP_W)
        put(v_refs[g], v, d)
        rb = st_rows[g]
        st_refs[g][:, 0:GROUP_W] = kn[tm - rb:, :]
        st_refs[g][:, GROUP_W:2 * GROUP_W] = v[tm - rb:, :]

    u_ref[...] = proj(OFF_U, POOL_W)
    sga_ref[...] = jax.nn.sigmoid(proj(OFF_GA, D_MODEL)).astype(BF16)
    sgp_ref[...] = jax.nn.sigmoid(proj(OFF_GP, D_MODEL)).astype(BF16)


def _mod_spec(per_row, tm):
    if per_row:
        return pl.BlockSpec((tm, D_MODEL), lambda i: (i, 0))
    return pl.BlockSpec((1, D_MODEL), lambda i: (0, 0))


def _proj(x, g1, sc1, sh1, w_in_bf, bdiag, qg, kg, *, tm, dils, per_row):
    s = x.shape[0]
    nt = s // tm
    wins = tuple(min(w, s) for w, _ in DIL_GROUPS)
    st_rows = tuple(min(tm, w) for w in wins)

    def res_spec(d):
        return pl.BlockSpec((tm // d, d * GROUP_W), lambda i: (i, 0))

    def st_spec(w, rb):
        first = nt - w // rb
        return pl.BlockSpec((rb, 2 * GROUP_W), lambda i: (jnp.maximum(i - first, 0), 0))

    qkv_shapes = [jax.ShapeDtypeStruct((s // d, d * GROUP_W), BF16) for d in dils]
    out_shape = (qkv_shapes * 3
                 + [jax.ShapeDtypeStruct((s, POOL_W), F32),
                    jax.ShapeDtypeStruct((s, D_MODEL), BF16),
                    jax.ShapeDtypeStruct((s, D_MODEL), BF16)]
                 + [jax.ShapeDtypeStruct((w, 2 * GROUP_W), F32) for w in wins])
    out_specs = ([res_spec(d) for d in dils] * 3
                 + [pl.BlockSpec((tm, POOL_W), lambda i: (i, 0)),
                    pl.BlockSpec((tm, D_MODEL), lambda i: (i, 0)),
                    pl.BlockSpec((tm, D_MODEL), lambda i: (i, 0))]
                 + [st_spec(w, rb) for w, rb in zip(wins, st_rows)])
    in_specs = [pl.BlockSpec((tm, D_MODEL), lambda i: (i, 0)),
                _const_spec((1, D_MODEL)),
                _mod_spec(per_row, tm), _mod_spec(per_row, tm),
                pl.BlockSpec((D_MODEL, IN_W), lambda i: (0, 0), pipeline_mode=pl.Buffered(1)),
                _const_spec((GROUP_W // 2, GROUP_W // 2)),
                _const_spec((1, GROUP_W)), _const_spec((1, GROUP_W))]
    outs = pl.pallas_call(
        functools.partial(_proj_kernel, tm=tm, dils=dils, st_rows=st_rows),
        grid=(nt,),
        in_specs=in_specs,
        out_specs=out_specs,
        out_shape=out_shape,
        scratch_shapes=[pltpu.VMEM((GROUP_W // LANES, tm, LANES), F32)],
        compiler_params=_cparams(1),
        name="proj",
    )(x, g1, sc1, sh1, w_in_bf, bdiag, qg, kg)
    return outs[0:3], outs[3:6], outs[6:9], outs[9], outs[10], outs[11], outs[12:15]


def _sample_group_attn(qs_ref, kns_ref, vns_ref, bself_ref, btab_ref, c_ref, os_ref, lses_ref):
    n_tok = qs_ref.shape[0]
    win = btab_ref.shape[1]
    row_w = lax.broadcasted_iota(jnp.int32, (HEADS, win), 0)
    row_e = lax.broadcasted_iota(jnp.int32, (HEADS, HEAD_DIM), 0)
    ss, s0s = [], []
    for t in range(n_tok):
        q = qs_ref[t]
        qb = q.astype(BF16)
        s = jnp.zeros((HEADS, win), F32)
        for h in range(HEADS):
            sh = jnp.dot(qb, c_ref[0, t, 0, h].astype(BF16), preferred_element_type=F32)
            s = jnp.where(row_w == h, sh, s)
        ss.append(s + btab_ref[...])
        s0s.append(jnp.sum(q * kns_ref[t], axis=-1, keepdims=True) + bself_ref[...])
    s = jnp.concatenate(ss, axis=0)
    s0 = jnp.concatenate(s0s, axis=0)
    m = jnp.maximum(jnp.max(s, axis=-1, keepdims=True), s0)
    p = jnp.exp(s - m)
    p0 = jnp.exp(s0 - m)
    l = jnp.sum(p, axis=-1, keepdims=True) + p0
    pb = p.astype(BF16)
    lse = m + jnp.log(l)
    for t in range(n_tok):
        rows = slice(t * HEADS, (t + 1) * HEADS)
        o = jnp.zeros((HEADS, HEAD_DIM), F32)
        for h in range(HEADS):
            oh = lax.dot_general(pb[rows], c_ref[0, t, 1, h].astype(BF16),
                                 (((1,), (1,)), ((), ())), preferred_element_type=F32)
            o = jnp.where(row_e == h, oh, o)
        os_ref[t] = (o + p0[rows] * vns_ref[t]) / l[rows]
        lses_ref[t] = lse[rows]


SAMPLE_IN = 6
SAMPLE_PARTS = (1, 1, 2)
SAMPLE_HOST = (((2, 1),), ((1, 0),), ((2, 0), (0, 0)))


def _attn_kernel(q_ref, kp_ref, kc_ref, vp_ref, vc_ref, r_ref, *refs, n_sample_groups):
    n_in = SAMPLE_IN * n_sample_groups
    o_ref, lse_ref = refs[n_in], refs[n_in + 1]
    bias_ref = refs[-1]
    i = pl.program_id(1)

    for sg in range(n_sample_groups):
        _sample_group_attn(*refs[SAMPLE_IN * sg:SAMPLE_IN * (sg + 1)],
                           *refs[n_in + 2 + 2 * sg:n_in + 4 + 2 * sg])

    @pl.when((pl.program_id(0) == 0) & (i == 0))
    def _():
        for h in range(HEADS):
            row = jnp.broadcast_to(r_ref[h:h + 1, :], (ATT_BLK, 2 * ATT_BLK))
            bias_ref[h] = pltpu.roll(row, 0, 1, stride=1, stride_axis=0)

    q = q_ref[...]
    k = jnp.concatenate([kp_ref[...], kc_ref[...]], axis=0)
    v = jnp.concatenate([vp_ref[...], vc_ref[...]], axis=0)
    col = lax.broadcasted_iota(jnp.int32, (ATT_BLK, 2 * ATT_BLK), 1)
    no_prev = jnp.where((col < ATT_BLK) & (i == 0), NEG_INF, 0.0)
    lane_q = lax.broadcasted_iota(jnp.int32, (ATT_SUB * ATT_BLK, LANES), 1)
    lane_v = lax.broadcasted_iota(jnp.int32, ((ATT_SUB + 1) * ATT_BLK, LANES), 1)

    def pair(h):
        return slice((h // 2) * LANES, (h // 2 + 1) * LANES)

    def mine(lane, h):
        return (lane < HEAD_DIM) == (h % 2 == 0)

    ss = []
    for h in range(HEADS):
        q2 = q[:, pair(h)]
        qm = jnp.where(mine(lane_q, h), q2, jnp.zeros_like(q2))
        k2 = k[:, pair(h)]
        for j in range(ATT_SUB):
            s = lax.dot_general(qm[j * ATT_BLK:(j + 1) * ATT_BLK], k2[j * ATT_BLK:(j + 2) * ATT_BLK],
                                (((1,), (1,)), ((), ())), preferred_element_type=F32)
            s = s + bias_ref[h]
            ss.append(s + no_prev if j == 0 else s)
    s = jnp.concatenate(ss, axis=0)
    m = jnp.max(s, axis=-1, keepdims=True)
    p = jnp.exp(s - m)
    l = jnp.sum(p, axis=-1, keepdims=True)
    pb = p.astype(BF16)
    lse = m + jnp.log(l)
    inv_l = 1.0 / l
    outs, lses = [], []
    for h in range(HEADS):
        v2 = v[:, pair(h)]
        vm = jnp.where(mine(lane_v, h), v2, jnp.zeros_like(v2))
        o_sub, lse_sub = [], []
        for j in range(ATT_SUB):
            rows = slice((h * ATT_SUB + j) * ATT_BLK, (h * ATT_SUB + j + 1) * ATT_BLK)
            o_sub.append(jnp.dot(pb[rows], vm[j * ATT_BLK:(j + 2) * ATT_BLK],
                                 preferred_element_type=F32) * inv_l[rows])
            lse_sub.append(lse[rows])
        o = jnp.concatenate(o_sub, axis=0)
        if h % 2 == 0:
            outs.append(o)
        else:
            outs[-1] = outs[-1] + o
        lses.append(jnp.concatenate(lse_sub, axis=0))
    o_ref[...] = jnp.concatenate(outs, axis=-1).astype(o_ref.dtype)
    lse_ref[...] = jnp.concatenate(
        lses + [jnp.zeros((ATT_SUB * ATT_BLK, LANES - HEADS), F32)], axis=-1)


def _attn(q, k, v, r_tab, d, sample_groups):
    rows = q.shape[0]
    step = ATT_SUB * ATT_BLK
    nblk = rows // step
    cur = pl.BlockSpec((step, GROUP_W), lambda r, i: (i, r))
    prev = pl.BlockSpec((ATT_BLK, GROUP_W), lambda r, i: (jnp.maximum(i * ATT_SUB - 1, 0), r))
    in_specs = [cur, prev, cur, prev, cur,
                pl.BlockSpec((HEADS, 2 * ATT_BLK), lambda r, i: (0, 0))]
    out_specs = [pl.BlockSpec((step, GROUP_W), lambda r, i: (i, r)),
                 pl.BlockSpec((step, LANES), lambda r, i: (i, r))]
    out_shape = [jax.ShapeDtypeStruct((rows, d * GROUP_W), BF16),
                 jax.ShapeDtypeStruct((rows, d * LANES), F32)]
    args = [q, k, k, v, v, r_tab]
    for qs, kns, vns, bself, btab, cache_t, tok_start, n_tok in sample_groups:
        assert n_tok % (d * nblk) == 0
        tok = n_tok // (d * nblk)
        assert tok_start % tok == 0
        first = tok_start // tok

        def in_idx(r, i, first=first):
            return (first + r * nblk + i, 0, 0)

        def out_idx(r, i):
            return (r * nblk + i, 0, 0)

        tok_in = pl.BlockSpec((tok, HEADS, HEAD_DIM), in_idx)
        in_specs += [tok_in, tok_in, tok_in,
                     pl.BlockSpec(bself.shape, lambda r, i: (0, 0)),
                     pl.BlockSpec(btab.shape, lambda r, i: (0, 0)),
                     pl.BlockSpec((1, tok) + cache_t.shape[2:],
                                  lambda r, i, first=first: (0, first + r * nblk + i, 0, 0, 0, 0))]
        out_specs += [pl.BlockSpec((tok, HEADS, HEAD_DIM), out_idx),
                      pl.BlockSpec((tok, HEADS, 1), out_idx)]
        out_shape += [jax.ShapeDtypeStruct((n_tok, HEADS, HEAD_DIM), F32),
                      jax.ShapeDtypeStruct((n_tok, HEADS, 1), F32)]
        args += [qs, kns, vns, bself, btab, cache_t]
    outs = pl.pallas_call(
        functools.partial(_attn_kernel, n_sample_groups=len(sample_groups)),
        grid=(d, nblk),
        in_specs=in_specs,
        out_specs=out_specs,
        out_shape=out_shape,
        scratch_shapes=[pltpu.VMEM((HEADS, ATT_BLK, 2 * ATT_BLK), F32)],
        compiler_params=_cparams(2),
        name=f"attn_d{d}",
    )(*args)
    return outs[0], outs[1], [(outs[2 + 2 * j], outs[3 + 2 * j]) for j in range(len(sample_groups))]


def _pool_sample_kernel(st_ref, u_ref, pooled_ref, new_ref):
    u = u_ref[...]
    rows = [st_ref[0, j] for j in range(POOL_BUF)]
    outs = []
    for g, w in enumerate(POOL_WINDOWS):
        sl = slice(g * POOL_GW, (g + 1) * POOL_GW)
        acc = u[:, sl]
        for j in range(POOL_BUF - (w - 1), POOL_BUF):
            acc = acc + rows[j][:, sl]
        outs.append(acc / float(w) - u[:, sl])
    pooled_ref[...] = jnp.concatenate(outs, axis=-1)
    for j in range(POOL_BUF - 1):
        new_ref[0, j] = rows[j + 1]
    new_ref[0, POOL_BUF - 1] = u


def _pool_sample(state, u):
    n = u.shape[0]
    return pl.pallas_call(
        _pool_sample_kernel,
        grid=(1,),
        in_specs=[_const_spec(state.shape), _const_spec(u.shape)],
        out_specs=[_const_spec(u.shape), _const_spec(state.shape)],
        out_shape=[jax.ShapeDtypeStruct((n, POOL_W), F32),
                   jax.ShapeDtypeStruct(state.shape, F32)],
        compiler_params=_cparams(1),
        name="pool_sample",
    )(state, u)


def _post_kernel(*refs, tm, dils, pooled_given, n_valid_steps, aliased):
    it = iter(refs)
    x_ref = next(it)
    o_refs = [next(it) for _ in range(N_GROUPS)]
    lse_refs = [next(it) for _ in range(N_GROUPS)]
    if pooled_given:
        pooled_ref = next(it)
    else:
        u_ref, uh_ref = next(it), next(it)
    sga_ref, sgp_ref = next(it), next(it)
    wpm_ref, psc_ref, wua_ref, wup_ref, wout_ref, exp_ref = (next(it) for _ in range(6))
    gt1_ref, g2_ref, sc2_ref, sh2_ref = (next(it) for _ in range(4))
    wrh_ref, wrl_ref, br_ref = (next(it) for _ in range(3))
    if aliased:
        for _ in range(6):
            next(it)
    x1_ref, h2_ref, a_ref, idx_ref, gk_ref, cnt_ref = (next(it) for _ in range(6))
    ob_scr, ls_scr = next(it), next(it)

    i = pl.program_id(0)

    def compute():
        obs, lss = [], []
        for g, d in enumerate(dils):
            if d == 1:
                obs.append(o_refs[g][...].astype(F32))
                lss.append(lse_refs[g][...])
            else:
                for r in range(d):
                    for c in range(GROUP_W // LANES):
                        col = r * GROUP_W + c * LANES
                        ob_scr[c, pl.ds(r, tm // d, stride=d), :] = (
                            o_refs[g][:, col:col + LANES].astype(F32))
                    ls_scr[pl.ds(r, tm // d, stride=d), :] = (
                        lse_refs[g][:, r * LANES:(r + 1) * LANES])
                obs.append(jnp.concatenate([ob_scr[c] for c in range(GROUP_W // LANES)],
                                           axis=-1))
                lss.append(ls_scr[...])
        mx = jnp.maximum(jnp.maximum(lss[0], lss[1]), lss[2])
        es = [jnp.exp(l - mx) for l in lss]
        den = es[0] + es[1] + es[2]
        attn_o = jnp.zeros((tm, GROUP_W), F32)
        head_lane = lax.broadcasted_iota(jnp.int32, (tm, LANES), 1) < HEADS
        for g in range(N_GROUPS):
            w = jnp.where(head_lane, es[g] / den, 0.0)
            w_hi = w.astype(BF16).astype(F32)
            w_lo = (w - w_hi).astype(BF16).astype(F32)
            lhs = (w_hi + pltpu.roll(w_lo, HEADS, 1)).astype(BF16)
            wexp = jnp.dot(lhs, exp_ref[...], preferred_element_type=F32)
            attn_o = attn_o + wexp * obs[g]

        if pooled_given:
            pooled = pooled_ref[...]
        else:
            u = u_ref[...]
            halo = jnp.where(i == 0, 0.0, uh_ref[...])
            pos = (lax.broadcasted_iota(jnp.int32, (tm, 1), 0) + i * tm + 1).astype(F32)
            outs = []
            for g, w in enumerate(POOL_WINDOWS):
                sl = slice(g * POOL_GW, (g + 1) * POOL_GW)
                a = jnp.concatenate([halo[:, sl], u[:, sl]], axis=0)
                span = 1
                while span < w:
                    n = a.shape[0] - span
                    a = a[span:, :] + a[:n, :]
                    span *= 2
                off = a.shape[0] - tm
                win_sum = a[off:, :]
                outs.append(win_sum / jnp.minimum(pos, float(w)) - u[:, sl])
            pooled = jnp.concatenate(outs, axis=-1)
        pool_parts = []
        for g in range(len(POOL_WINDOWS)):
            sl = slice(g * POOL_GW, (g + 1) * POOL_GW)
            pool_parts.append(jnp.dot(pooled[:, sl].astype(BF16), wpm_ref[g],
                                      preferred_element_type=F32))
        pool_o = jnp.concatenate(pool_parts, axis=-1) * psc_ref[...]

        up_a = jnp.dot(attn_o.astype(BF16), wua_ref[...], preferred_element_type=F32)
        up_p = jnp.dot(pool_o.astype(BF16), wup_ref[...], preferred_element_type=F32)
        merged = sga_ref[...].astype(F32) * up_a + sgp_ref[...].astype(F32) * up_p
        mo = jnp.dot(merged.astype(BF16), wout_ref[...], preferred_element_type=F32)
        x1 = x_ref[...] + gt1_ref[...] * mo
        x1_ref[...] = x1

        ms = jnp.mean(x1 * x1, axis=-1, keepdims=True)
        h2 = x1 * lax.rsqrt(ms + EPS) * g2_ref[...] * (1.0 + sc2_ref[...]) + sh2_ref[...]
        h2_hi = h2.astype(BF16)
        h2_ref[...] = h2_hi
        h2_lo = (h2 - h2_hi.astype(F32)).astype(BF16)
        logits = (jnp.dot(h2_hi, wrh_ref[...], preferred_element_type=F32)
                  + jnp.dot(h2_lo, wrh_ref[...], preferred_element_type=F32)
                  + jnp.dot(h2_hi, wrl_ref[...], preferred_element_type=F32)
                  + br_ref[...])
        lane = lax.broadcasted_iota(jnp.int32, (tm, LANES), 1).astype(F32)
        work = logits
        vals, ids = [], []
        for _ in range(TOP_K):
            m = jnp.max(work, axis=-1, keepdims=True)
            ik = jnp.min(jnp.where(work == m, lane, float(LANES)), axis=-1, keepdims=True)
            vals.append(m)
            ids.append(ik)
            work = jnp.where(lane == ik, -3e38, work)
        ex = [jnp.exp(v - vals[0]) for v in vals]
        den_k = ex[0] + ex[1] + ex[2] + ex[3]
        a = jnp.zeros((tm, LANES), F32)
        idx = jnp.zeros((tm, LANES), F32)
        gk = jnp.zeros((tm, LANES), F32)
        for kk in range(TOP_K):
            gate = ex[kk] / den_k
            a = a + jnp.where(lane == ids[kk], gate, 0.0)
            idx = jnp.where(lane == float(kk), ids[kk], idx)
            gk = jnp.where(lane == float(kk), gate, gk)
        a_ref[...] = a
        idx_ref[...] = idx
        gk_ref[...] = gk
        cnt = jnp.sum((a > 0.0).astype(F32), axis=0, keepdims=True)
        row = lax.broadcasted_iota(jnp.int32, (8, LANES), 0)
        cnt_ref[0] = jnp.where(row == 0, jnp.broadcast_to(cnt, (8, LANES)), 0.0)

    if n_valid_steps is None:
        compute()
    else:
        pl.when(i < n_valid_steps)(compute)

        @pl.when(i >= n_valid_steps)
        def _():
            x1_ref[...] = jnp.zeros(x1_ref.shape, x1_ref.dtype)
            h2_ref[...] = jnp.zeros(h2_ref.shape, h2_ref.dtype)
            a_ref[...] = jnp.zeros(a_ref.shape, a_ref.dtype)
            idx_ref[...] = jnp.zeros(idx_ref.shape, idx_ref.dtype)
            gk_ref[...] = jnp.zeros(gk_ref.shape, gk_ref.dtype)


def _post(x, o_list, lse_list, pool_in, sga, sgp, wts, mods, *, tm, dils, per_row,
          rows_total, row_block0, cnt_tiles, cnt_block, grid, n_valid_steps, alias_bufs):
    pooled_given = not isinstance(pool_in, tuple)
    nv = grid if n_valid_steps is None else n_valid_steps

    def clamp(i):
        return jnp.minimum(i, nv - 1)

    def tile_spec(width):
        return pl.BlockSpec((tm, width), lambda i: (clamp(i), 0))

    in_specs = [tile_spec(D_MODEL)]
    in_specs += [pl.BlockSpec((tm // d, d * GROUP_W), lambda i: (clamp(i), 0)) for d in dils]
    in_specs += [pl.BlockSpec((tm // d, d * LANES), lambda i: (clamp(i), 0)) for d in dils]
    args = [x, *o_list, *lse_list]
    if pooled_given:
        in_specs.append(tile_spec(POOL_W))
        args.append(pool_in)
    else:
        u = pool_in[0]
        in_specs += [tile_spec(POOL_W),
                     pl.BlockSpec((16, POOL_W),
                                  lambda i: (jnp.maximum(i * (tm // 16) - 1, 0), 0))]
        args += [u, u]
    in_specs += [tile_spec(D_MODEL), tile_spec(D_MODEL)]
    args += [sga, sgp]
    wpm, psc, wua, wup, wout, expand, g2, wrh, wrl, br = wts
    gt1, sc2, sh2 = mods

    def mspec():
        if per_row:
            return pl.BlockSpec((tm, D_MODEL), lambda i: (clamp(i), 0))
        return _const_spec((1, D_MODEL))

    in_specs += [_const_spec(wpm.shape), _const_spec(psc.shape), _const_spec(wua.shape),
                 _const_spec(wup.shape), _const_spec(wout.shape), _const_spec(expand.shape),
                 mspec(), _const_spec(g2.shape), mspec(), mspec(),
                 _const_spec(wrh.shape), _const_spec(wrl.shape), _const_spec(br.shape)]
    args += [wpm, psc, wua, wup, wout, expand, gt1, g2, sc2, sh2, wrh, wrl, br]
    aliases = {}
    if alias_bufs is not None:
        base = len(args)
        in_specs += [pl.BlockSpec(memory_space=pl.ANY)] * 6
        args += list(alias_bufs)
        aliases = {base + j: j for j in range(6)}

    def out_spec(width):
        return pl.BlockSpec((tm, width), lambda i: (row_block0 + i, 0))

    out_specs = [out_spec(D_MODEL), out_spec(D_MODEL), out_spec(LANES), out_spec(LANES),
                 out_spec(LANES),
                 pl.BlockSpec((1, 8, LANES),
                              lambda i: (cnt_block if cnt_block is not None else i, 0, 0))]
    out_shape = [jax.ShapeDtypeStruct((rows_total, D_MODEL), F32),
                 jax.ShapeDtypeStruct((rows_total, D_MODEL), BF16),
                 jax.ShapeDtypeStruct((rows_total, LANES), F32),
                 jax.ShapeDtypeStruct((rows_total, LANES), F32),
                 jax.ShapeDtypeStruct((rows_total, LANES), F32),
                 jax.ShapeDtypeStruct((cnt_tiles, 8, LANES), F32)]
    return pl.pallas_call(
        functools.partial(_post_kernel, tm=tm, dils=dils, pooled_given=pooled_given,
                          n_valid_steps=n_valid_steps, aliased=alias_bufs is not None),
        grid=(grid,),
        in_specs=in_specs,
        out_specs=out_specs,
        out_shape=out_shape,
        scratch_shapes=[pltpu.VMEM((GROUP_W // LANES, tm, LANES), F32),
                        pltpu.VMEM((tm, LANES), F32)],
        input_output_aliases=aliases,
        compiler_params=_cparams(1),
        name="post_sample" if per_row else "post",
    )(*args)


def _sort_rows(tm):
    return -(-(TOP_K * tm + N_EXPERTS * (ROW_CHUNK - 1)) // SEL_CHUNK) * SEL_CHUNK


def _for_row_pieces(n_chunks, max_pow, fn):
    big = 1 << max_pow

    def body(c, carry):
        fn(c * big, big)
        return carry

    lax.fori_loop(0, n_chunks >> max_pow, body, 0)
    for pw in range(max_pow - 1, -1, -1):
        @pl.when(((n_chunks >> pw) & 1) == 1)
        def _(pw=pw):
            fn((n_chunks >> (pw + 1)) << (pw + 1), 1 << pw)


SEG_MAX_POW = 3
TILE_MAX_POW = 5


def _moe_sort_kernel(seg_s, goff_s, nch_s, ntot_s, tstart_s, tnch_s,
                     a_ref, idx_ref, h2_ref, segv_ref, lt_ref,
                     xb_hbm, dst_ref, xs_scr, zero_scr, sem, *, tm, n_rows):
    i = pl.program_id(0)
    nt = pl.num_programs(0)
    slot = i % 2
    sel = a_ref[...] > 0.0
    ahead = jnp.dot(lt_ref[...], sel.astype(BF16), preferred_element_type=F32)
    slot1 = jnp.where(sel, segv_ref[0] + ahead + 1.0, 0.0)
    lane = lax.broadcasted_iota(jnp.int32, (tm, LANES), 1).astype(F32)
    idx = idx_ref[...]
    dst = jnp.full((tm, LANES), -1.0, F32)
    for kk in range(TOP_K):
        hit = lane == idx[:, kk:kk + 1]
        dk = jnp.sum(jnp.where(hit, slot1, 0.0), axis=-1, keepdims=True) - 1.0
        dst = jnp.where(lane == float(kk), dk, dst)
    dst_ref[...] = dst
    dst_t = dst.T
    h2 = h2_ref[...]
    for c in range(n_rows // SEL_CHUNK):
        rows = (lax.broadcasted_iota(jnp.int32, (SEL_CHUNK, tm), 0) + c * SEL_CHUNK).astype(F32)
        p = rows == dst_t[0:1, :]
        for kk in range(1, TOP_K):
            p = p | (rows == dst_t[kk:kk + 1, :])
        xs = jnp.dot(jnp.where(p, 1.0, 0.0).astype(BF16), h2, preferred_element_type=F32)
        xs_scr[slot, c * SEL_CHUNK:(c + 1) * SEL_CHUNK, :] = xs.astype(BF16)

    def rows_copy(buf, src_row, dst_row, n_chunks):
        return pltpu.make_async_copy(
            xs_scr.at[buf, pl.ds(pl.multiple_of(src_row, ROW_CHUNK), n_chunks * ROW_CHUNK)],
            xb_hbm.at[pl.ds(pl.multiple_of(dst_row, ROW_CHUNK), n_chunks * ROW_CHUNK)],
            sem.at[buf])

    def per_expert(e, carry):
        so = seg_s[i * N_EXPERTS + e]
        go = goff_s[i * N_EXPERTS + e]
        _for_row_pieces(
            nch_s[i * N_EXPERTS + e], SEG_MAX_POW,
            lambda off, n: rows_copy(slot, so + off * ROW_CHUNK, go + off * ROW_CHUNK, n).start())
        return carry

    lax.fori_loop(0, N_EXPERTS, per_expert, 0)

    def drain(buf, tile):
        _for_row_pieces(ntot_s[tile], TILE_MAX_POW, lambda off, n: rows_copy(buf, 0, 0, n).wait())

    @pl.when(i > 0)
    def _():
        drain(1 - slot, i - 1)

    @pl.when(i == nt - 1)
    def _():
        drain(slot, i)
        zero_scr[...] = jnp.zeros(zero_scr.shape, zero_scr.dtype)

        def tail_copy(dst_row, n_chunks):
            return pltpu.make_async_copy(
                zero_scr.at[pl.ds(0, n_chunks * ROW_CHUNK)],
                xb_hbm.at[pl.ds(pl.multiple_of(dst_row, ROW_CHUNK), n_chunks * ROW_CHUNK)],
                sem.at[2])

        def per_expert_tail(e, carry):
            _for_row_pieces(tnch_s[e], SEG_MAX_POW,
                            lambda off, n: tail_copy(tstart_s[e] + off * ROW_CHUNK, n).start())
            _for_row_pieces(tnch_s[e], SEG_MAX_POW, lambda off, n: tail_copy(0, n).wait())
            return carry

        lax.fori_loop(0, N_EXPERTS, per_expert_tail, 0)


def _moe_sort(meta, a_all, idx_all, h2_all, *, tm, cap):
    t_all = a_all.shape[0]
    nt = t_all // tm
    n_rows = _sort_rows(tm)
    lt = jnp.tril(jnp.ones((tm, tm), BF16), -1)
    grid_spec = pltpu.PrefetchScalarGridSpec(
        num_scalar_prefetch=6,
        grid=(nt,),
        in_specs=[pl.BlockSpec((tm, LANES), lambda i, *_: (i, 0)),
                  pl.BlockSpec((tm, LANES), lambda i, *_: (i, 0)),
                  pl.BlockSpec((tm, D_MODEL), lambda i, *_: (i, 0)),
                  pl.BlockSpec((1, 1, LANES), lambda i, *_: (i, 0, 0)),
                  pl.BlockSpec((tm, tm), lambda i, *_: (0, 0))],
        out_specs=[pl.BlockSpec(memory_space=pl.ANY),
                   pl.BlockSpec((tm, LANES), lambda i, *_: (i, 0))],
        scratch_shapes=[pltpu.VMEM((2, n_rows, D_MODEL), BF16),
                        pltpu.VMEM(((1 << SEG_MAX_POW) * ROW_CHUNK, D_MODEL), BF16),
                        pltpu.SemaphoreType.DMA((3,))],
    )
    return pl.pallas_call(
        functools.partial(_moe_sort_kernel, tm=tm, n_rows=n_rows),
        grid_spec=grid_spec,
        out_shape=[jax.ShapeDtypeStruct((cap, D_MODEL), BF16),
                   jax.ShapeDtypeStruct((t_all, LANES), F32)],
        compiler_params=_cparams(1),
        name="moe_sort",
    )(meta["seg"], meta["goff"], meta["nch"], meta["ntot"], meta["tstart"], meta["tnch"],
      a_all, idx_all, h2_all, meta["segv"], lt)


def _moe_ffn_kernel(be_s, nused_s, ord_s, next_s, rows_s,
                    x_ref, wgu_hbm, bgu_ref, wd_hbm, bd_ref, y_ref,
                    wgu_f32, wd_f32, wgu_bf, wd_bf, sem):
    b = pl.program_id(0)

    def weight_copies(e, slot):
        return (pltpu.make_async_copy(wgu_hbm.at[e], wgu_f32.at[slot], sem.at[0, slot]),
                pltpu.make_async_copy(wd_hbm.at[e], wd_f32.at[slot], sem.at[1, slot]))

    @pl.when(b < nused_s[0])
    def _():
        e = be_s[b]
        e_prev = be_s[jnp.maximum(b - 1, 0)]
        slot = ord_s[b] % 2

        @pl.when(b == 0)
        def _():
            for cp in weight_copies(e, slot):
                cp.start()

        @pl.when((b == 0) | (e != e_prev))
        def _():
            for cp in weight_copies(e, slot):
                cp.wait()
            e_next = next_s[b]

            @pl.when(e_next >= 0)
            def _():
                for cp in weight_copies(e_next, 1 - slot):
                    cp.start()

            wgu_bf[...] = wgu_f32[slot].astype(BF16)
            wd_bf[...] = wd_f32[slot].astype(BF16)

        def ffn_rows(n):
            hgu = jnp.dot(x_ref[0:n, :], wgu_bf[...], preferred_element_type=F32) + bgu_ref[0]
            d_ff = hgu.shape[1] // 2
            hg = jnp.minimum(hgu[:, :d_ff], SWIGLU_LIMIT)
            hu = jnp.clip(hgu[:, d_ff:], -SWIGLU_LIMIT, SWIGLU_LIMIT)
            act = hg * jax.nn.sigmoid(SWIGLU_ALPHA * hg) * (hu + 1.0)
            y = jnp.dot(act.astype(BF16), wd_bf[...], preferred_element_type=F32) + bd_ref[0]
            y_ref[0:n, :] = y.astype(y_ref.dtype)

        rows_here = rows_s[b]
        for n in range(FFN_SUB, FFN_BLOCK + 1, FFN_SUB):
            pl.when(rows_here == n)(functools.partial(ffn_rows, n))


def _moe_ffn(meta, xb, w_gate_up, b_gate_up, w_down, b_down):
    cap = xb.shape[0]
    nb = cap // FFN_BLOCK
    d_ff2 = w_gate_up.shape[2]

    def blk(b, be, nu, *_):
        return jnp.minimum(b, jnp.maximum(nu[0] - 1, 0))

    def row_blk(b, be, nu, *_):
        return (blk(b, be, nu), 0)

    def expert_blk(b, be, nu, *_):
        return (be[blk(b, be, nu)], 0, 0)

    grid_spec = pltpu.PrefetchScalarGridSpec(
        num_scalar_prefetch=5,
        grid=(nb,),
        in_specs=[pl.BlockSpec((FFN_BLOCK, D_MODEL), row_blk),
                  pl.BlockSpec(memory_space=pl.ANY),
                  pl.BlockSpec((1, 1, d_ff2), expert_blk),
                  pl.BlockSpec(memory_space=pl.ANY),
                  pl.BlockSpec((1, 1, D_MODEL), expert_blk)],
        out_specs=pl.BlockSpec((FFN_BLOCK, D_MODEL), row_blk),
        scratch_shapes=[pltpu.VMEM((2, D_MODEL, d_ff2), F32),
                        pltpu.VMEM((2, d_ff2 // 2, D_MODEL), F32),
                        pltpu.VMEM((D_MODEL, d_ff2), BF16),
                        pltpu.VMEM((d_ff2 // 2, D_MODEL), BF16),
                        pltpu.SemaphoreType.DMA((2, 2))],
    )
    return pl.pallas_call(
        _moe_ffn_kernel,
        grid_spec=grid_spec,
        out_shape=jax.ShapeDtypeStruct((cap, D_MODEL), BF16),
        compiler_params=_cparams(1),
        name="moe_ffn",
    )(meta["block_expert"], meta["n_used"], meta["block_ord"], meta["block_next"],
      meta["block_rows"], xb, w_gate_up,
      b_gate_up.reshape(N_EXPERTS, 1, d_ff2), w_down, b_down.reshape(N_EXPERTS, 1, D_MODEL))


def _moe_unsort_kernel(seg_s, goff_s, nch_s, ntot_s,
                       dst_ref, gk_ref, x1_ref, g2p_ref, g2s_ref, yb_hbm,
                       yp_ref, ys_ref, ybuf, sem, *, tm, n_rows, n_prompt_tiles):
    i = pl.program_id(0)
    nt = pl.num_programs(0)
    slot = i % 2

    def rows_copy(buf, src_row, dst_row, n_chunks):
        return pltpu.make_async_copy(
            yb_hbm.at[pl.ds(pl.multiple_of(src_row, ROW_CHUNK), n_chunks * ROW_CHUNK)],
            ybuf.at[buf, pl.ds(pl.multiple_of(dst_row, ROW_CHUNK), n_chunks * ROW_CHUNK)],
            sem.at[buf])

    def fetch(tile, buf):
        def per_expert(e, carry):
            so = seg_s[tile * N_EXPERTS + e]
            go = goff_s[tile * N_EXPERTS + e]
            _for_row_pieces(
                nch_s[tile * N_EXPERTS + e], SEG_MAX_POW,
                lambda off, n: rows_copy(buf, go + off * ROW_CHUNK, so + off * ROW_CHUNK,
                                         n).start())
            return carry

        lax.fori_loop(0, N_EXPERTS, per_expert, 0)

    @pl.when(i == 0)
    def _():
        ybuf[...] = jnp.zeros(ybuf.shape, ybuf.dtype)
        fetch(0, 0)

    @pl.when(i + 1 < nt)
    def _():
        fetch(i + 1, 1 - slot)

    _for_row_pieces(ntot_s[i], TILE_MAX_POW, lambda off, n: rows_copy(slot, 0, 0, n).wait())

    dst = dst_ref[...]
    gk = gk_ref[...]
    acc = jnp.zeros((tm, D_MODEL), F32)
    for c in range(n_rows // SEL_CHUNK):
        cols = (lax.broadcasted_iota(jnp.int32, (tm, SEL_CHUNK), 1) + c * SEL_CHUNK).astype(F32)
        q = jnp.zeros((tm, SEL_CHUNK), F32)
        for kk in range(TOP_K):
            q = jnp.where(cols == dst[:, kk:kk + 1], gk[:, kk:kk + 1], q)
        acc = acc + jnp.dot(q.astype(BF16), ybuf[slot, c * SEL_CHUNK:(c + 1) * SEL_CHUNK, :],
                            preferred_element_type=F32)

    @pl.when(i < n_prompt_tiles)
    def _():
        yp_ref[...] = x1_ref[...] + g2p_ref[...] * acc

    @pl.when(i >= n_prompt_tiles)
    def _():
        ys_ref[...] = x1_ref[...] + g2s_ref[...] * acc


def _moe_unsort(meta, dst_all, gk_all, x1_all, gt2_p, gt2_s, yb, *, tm, n_prompt_tiles):
    t_all = dst_all.shape[0]
    nt = t_all // tm
    n_rows = _sort_rows(tm)
    last_p = n_prompt_tiles - 1
    grid_spec = pltpu.PrefetchScalarGridSpec(
        num_scalar_prefetch=4,
        grid=(nt,),
        in_specs=[pl.BlockSpec((tm, LANES), lambda i, *_: (i, 0)),
                  pl.BlockSpec((tm, LANES), lambda i, *_: (i, 0)),
                  pl.BlockSpec((tm, D_MODEL), lambda i, *_: (i, 0)),
                  pl.BlockSpec((1, D_MODEL), lambda i, *_: (0, 0)),
                  pl.BlockSpec((tm, D_MODEL), lambda i, *_: (0, 0)),
                  pl.BlockSpec(memory_space=pl.ANY)],
        out_specs=[pl.BlockSpec((tm, D_MODEL), lambda i, *_: (jnp.minimum(i, last_p), 0)),
                   pl.BlockSpec((tm, D_MODEL), lambda i, *_: (0, 0))],
        scratch_shapes=[pltpu.VMEM((2, n_rows, D_MODEL), BF16),
                        pltpu.SemaphoreType.DMA((2,))],
    )
    return pl.pallas_call(
        functools.partial(_moe_unsort_kernel, tm=tm, n_rows=n_rows,
                          n_prompt_tiles=n_prompt_tiles),
        grid_spec=grid_spec,
        out_shape=[jax.ShapeDtypeStruct((n_prompt_tiles * tm, D_MODEL), F32),
                   jax.ShapeDtypeStruct((tm, D_MODEL), F32)],
        compiler_params=_cparams(1),
        name="moe_unsort",
    )(meta["seg"], meta["goff"], meta["nch"], meta["ntot"],
      dst_all, gk_all, x1_all, gt2_p, gt2_s, yb)


def _moe_meta(cnt, tm):
    nt = cnt.shape[0]
    cnt = cnt.astype(jnp.int32)
    cnt_pad = (cnt + ROW_CHUNK - 1) // ROW_CHUNK * ROW_CHUNK
    seg = jnp.cumsum(cnt_pad, axis=1) - cnt_pad
    rows_e = jnp.sum(cnt_pad, axis=0)
    region = (rows_e + FFN_BLOCK - 1) // FFN_BLOCK * FFN_BLOCK
    gstart = jnp.cumsum(region) - region
    goff = gstart[None, :] + jnp.cumsum(cnt_pad, axis=0) - cnt_pad
    nblk_e = region // FFN_BLOCK
    blk_end = jnp.cumsum(nblk_e)
    cap = _moe_cap(nt * tm, tm)
    blocks = jnp.arange(cap // FFN_BLOCK, dtype=jnp.int32)
    block_expert = jnp.minimum(
        jnp.sum((blk_end[None, :] <= blocks[:, None]).astype(jnp.int32), axis=1), N_EXPERTS - 1)
    segv = jnp.zeros((nt, 1, LANES), F32).at[:, 0, :N_EXPERTS].set(seg.astype(F32))
    used = nblk_e > 0
    experts = jnp.arange(N_EXPERTS, dtype=jnp.int32)
    ord_e = jnp.cumsum(used.astype(jnp.int32)) - 1
    later = (experts[None, :] > experts[:, None]) & used[None, :]
    next_e = jnp.min(jnp.where(later, experts[None, :], N_EXPERTS), axis=1)
    next_e = jnp.where(next_e < N_EXPERTS, next_e, -1).astype(jnp.int32)
    of_block = (block_expert[:, None] == experts[None, :]).astype(jnp.int32)
    first_block = jnp.sum(of_block * (blk_end - nblk_e)[None, :], axis=1)
    rows_left = jnp.sum(of_block * rows_e[None, :], axis=1) - (blocks - first_block) * FFN_BLOCK
    block_rows = jnp.clip((rows_left + FFN_SUB - 1) // FFN_SUB * FFN_SUB, FFN_SUB, FFN_BLOCK)
    return {
        "block_rows": block_rows.astype(jnp.int32),
        "block_ord": jnp.sum(of_block * ord_e[None, :].astype(jnp.int32), axis=1),
        "block_next": jnp.sum(of_block * next_e[None, :], axis=1),
        "seg": seg.reshape(-1), "goff": goff.reshape(-1).astype(jnp.int32),
        "nch": (cnt_pad // ROW_CHUNK).reshape(-1),
        "ntot": jnp.sum(cnt_pad, axis=1) // ROW_CHUNK,
        "tstart": (gstart + rows_e).astype(jnp.int32),
        "tnch": (region - rows_e) // ROW_CHUNK,
        "block_expert": block_expert,
        "n_used": blk_end[-1:].astype(jnp.int32),
        "segv": segv,
    }


def _moe_cap(t_all, tm):
    nt = t_all // tm
    worst = TOP_K * t_all + nt * N_EXPERTS * (ROW_CHUNK - 1) + N_EXPERTS * (FFN_BLOCK - ROW_CHUNK)
    return -(-worst // FFN_BLOCK) * FFN_BLOCK


def _t5_bucket(dist):
    max_exact = NUM_BUCKETS // 2
    d = dist.astype(jnp.int32)
    ratio = (jnp.log(jnp.maximum(d, 1).astype(F32) / max_exact)
             / math.log(MAX_DISTANCE / max_exact))
    large = jnp.minimum(max_exact + (ratio * (NUM_BUCKETS - max_exact)).astype(jnp.int32),
                        NUM_BUCKETS - 1)
    return jnp.where(d < max_exact, d, large)


def _step_bias(tab, dil):
    return tab[_t5_bucket(dil * jnp.arange(ATT_BLK + 1))].astype(F32).T


def _band_table(sb):
    return jnp.concatenate([sb[:, ::-1], jnp.full((HEADS, ATT_BLK - 1), NEG_INF, F32)], axis=1)


def _cache_table(sb, dil):
    on_grid = sb[:, :0:-1]
    if dil == 1:
        return on_grid
    off = jnp.full((HEADS, ATT_BLK, dil - 1), NEG_INF, F32)
    return jnp.concatenate([on_grid[:, :, None], off], axis=2).reshape(HEADS, ATT_BLK * dil)


def kernel(x_prompt, x_sample, cache_kv_w128, cache_kv_w512, cache_kv_w2048, state_pool, c_prompt,
           c_sample, w_ada, b_ada, norm_mix_g, norm_ffn_g, w_in, q_norm_g, k_norm_g, rel_bias,
           w_pool_mix, pool_scale, w_up_attn, w_up_pool, w_out, w_router, b_router, w_gate_up,
           b_gate_up, w_down, b_down):
    assert w_ada.shape[0] == 1, "one layer"
    seq = x_prompt.shape[1]
    n_s = x_sample.shape[0]
    assert x_prompt.shape[0] == 1 and x_sample.shape[1] == 1
    assert seq % (DIL_GROUPS[-1][1] * ATT_BLK * ATT_SUB) == 0 and seq % TM_PROMPT == 0
    assert n_s == TM_SAMPLE
    dils = tuple(d for _, d in DIL_GROUPS)
    caches = (cache_kv_w128, cache_kv_w512, cache_kv_w2048)

    w_in_bf = w_in[0].astype(BF16)
    heads_of = jnp.arange(GROUP_W) // HEAD_DIM
    half_heads = heads_of[:GROUP_W // 2]
    bdiag = (half_heads[:, None] == half_heads[None, :]).astype(BF16)
    qg = (jnp.tile(q_norm_g[0], HEADS) * SCALE).reshape(1, GROUP_W)
    kg = jnp.tile(k_norm_g[0], HEADS).reshape(1, GROUP_W)
    expand = ((jnp.arange(LANES)[:, None] % HEADS == heads_of[None, :])
              & (jnp.arange(LANES)[:, None] < 2 * HEADS)).astype(BF16)
    wr = jnp.zeros((D_MODEL, LANES), F32).at[:, :N_EXPERTS].set(w_router[0])
    wr_hi = wr.astype(BF16)
    wr_lo = (wr - wr_hi.astype(F32)).astype(BF16)
    br = jnp.full((1, LANES), NEG_INF, F32).at[0, :N_EXPERTS].set(b_router[0])
    wts = (w_pool_mix[0].astype(BF16), pool_scale[0].reshape(1, POOL_W),
           w_up_attn[0].astype(BF16), w_up_pool[0].astype(BF16), w_out[0].astype(BF16), expand,
           norm_ffn_g[0].reshape(1, D_MODEL), wr_hi, wr_lo, br)
    g1 = norm_mix_g[0].reshape(1, D_MODEL)

    n_c = 1 + n_s
    c_all = jnp.zeros((-(-n_c // 8) * 8, D_MODEL), F32).at[0:1].set(c_prompt).at[1:n_c].set(c_sample)
    mod = _ada(c_all, w_ada[0], b_ada[0])
    sh1, sc1, gt1, sh2, sc2, gt2 = jnp.split(mod, N_ADA, axis=-1)

    def prow(m):
        return m[0:1]

    def srows(m):
        return m[1:n_c]

    xp = x_prompt[0]
    q_p, k_p, v_p, u_p, sga_p, sgp_p, st_p = _proj(
        xp, g1, prow(sc1), prow(sh1), w_in_bf, bdiag, qg, kg,
        tm=TM_PROMPT, dils=dils, per_row=False)
    xs = x_sample[:, 0]
    ones = (1, 1, 1)
    q_s, _, _, u_s, sga_s, sgp_s, st_s = _proj(
        xs, g1, srows(sc1), srows(sh1), w_in_bf, bdiag, qg, kg,
        tm=TM_SAMPLE, dils=ones, per_row=True)
    pooled_s, pool_state_t = _pool_sample(jnp.transpose(state_pool, (0, 2, 1, 3)), u_s)
    pool_state_s = jnp.transpose(pool_state_t, (0, 2, 1, 3))

    def heads(a):
        return a.astype(F32).reshape(n_s, HEADS, HEAD_DIM)

    step_bias = [_step_bias(rel_bias[:, g * HEADS:(g + 1) * HEADS], d)
                 for g, (_, d) in enumerate(DIL_GROUPS)]

    def sample_group(g, part, n_parts):
        cache_t = jnp.transpose(caches[g], (0, 1, 3, 4, 5, 2))
        n_sub = n_s // n_parts
        return (heads(q_s[g]), heads(st_s[g][:, :GROUP_W]), heads(st_s[g][:, GROUP_W:]),
                step_bias[g][:, 0:1], _cache_table(step_bias[g], DIL_GROUPS[g][1]), cache_t,
                part * n_sub, n_sub)

    o_p, lse_p = [], []
    o_parts = [[None] * n for n in SAMPLE_PARTS]
    lse_parts = [[None] * n for n in SAMPLE_PARTS]
    for g, (_, d) in enumerate(DIL_GROUPS):
        hosted = SAMPLE_HOST[g]
        o, lse, sample_outs = _attn(
            q_p[g], k_p[g], v_p[g], _band_table(step_bias[g]), d,
            [sample_group(sg, part, SAMPLE_PARTS[sg]) for sg, part in hosted])
        o_p.append(o)
        lse_p.append(lse)
        for (sg, part), (os_g, lses_g) in zip(hosted, sample_outs):
            o_parts[sg][part] = os_g.reshape(-1, GROUP_W)
            lse_parts[sg][part] = lses_g[:, :, 0]
    o_s = [jnp.concatenate(parts, axis=0) for parts in o_parts]
    lse_s = [jnp.zeros((n_s, LANES), F32).at[:, :HEADS].set(jnp.concatenate(parts, axis=0))
             for parts in lse_parts]

    nt_p = seq // TM_PROMPT
    t_all = seq + TM_PROMPT
    bufs = _post(xp, o_p, lse_p, (u_p,), sga_p, sgp_p, wts, (prow(gt1), prow(sc2), prow(sh2)),
                 tm=TM_PROMPT, dils=dils, per_row=False, rows_total=t_all, row_block0=0,
                 cnt_tiles=nt_p + 1, cnt_block=None, grid=nt_p, n_valid_steps=None,
                 alias_bufs=None)
    bufs = _post(xs, o_s, lse_s, pooled_s, sga_s, sgp_s, wts,
                 (srows(gt1), srows(sc2), srows(sh2)),
                 tm=TM_SAMPLE, dils=ones, per_row=True, rows_total=t_all,
                 row_block0=seq // TM_SAMPLE, cnt_tiles=nt_p + 1, cnt_block=nt_p,
                 grid=TM_PROMPT // TM_SAMPLE, n_valid_steps=1, alias_bufs=bufs)
    x1_all, h2_all, a_all, idx_all, gk_all, cnt = bufs

    meta = _moe_meta(cnt[:, 0, :N_EXPERTS], TM_PROMPT)
    cap = _moe_cap(t_all, TM_PROMPT)
    xb, dst_all = _moe_sort(meta, a_all, idx_all, h2_all, tm=TM_PROMPT, cap=cap)
    yb = _moe_ffn(meta, xb, w_gate_up[0], b_gate_up[0], w_down[0], b_down[0])
    gt2_s = jnp.zeros((TM_PROMPT, D_MODEL), F32).at[:n_s].set(srows(gt2))
    y_p, y_s = _moe_unsort(meta, dst_all, gk_all, x1_all, prow(gt2), gt2_s, yb,
                           tm=TM_PROMPT, n_prompt_tiles=nt_p)

    def kv_state(st, rows):
        return st.reshape(1, 1, rows, 2, HEADS, HEAD_DIM)

    kv_p = [kv_state(st, st.shape[0]) for st in st_p]
    kv_s = [st.reshape(1, n_s, 1, 2, HEADS, HEAD_DIM) for st in st_s]
    pool_p = u_p[seq - POOL_BUF:].reshape(1, 1, POOL_BUF, POOL_W)
    return (y_p.reshape(1, seq, D_MODEL), y_s[:n_s].reshape(n_s, 1, D_MODEL),
            kv_p[0], kv_p[1], kv_p[2], pool_p, kv_s[0], kv_s[1], kv_s[2], pool_state_s)
```

```python
import functools
import math

import jax
import jax.numpy as jnp
from jax import lax
from jax.experimental import pallas as pl
from jax.experimental.pallas import tpu as pltpu

F32 = jnp.float32
BF16 = jnp.bfloat16

D_MODEL = 1024
HEAD_DIM = 64
HEADS = 8
GROUP_W = HEADS * HEAD_DIM
DIL_GROUPS = ((128, 1), (512, 4), (2048, 16))
N_GROUPS = len(DIL_GROUPS)
QKV_W = N_GROUPS * GROUP_W
ATT_BLK = 128
ATT_SUB = 4
POOL_WINDOWS = (2, 4, 8, 16)
POOL_W = 512
POOL_GW = 128
POOL_BUF = 15
OFF_K, OFF_V = QKV_W, 2 * QKV_W
OFF_U = 3 * QKV_W
OFF_GA = OFF_U + POOL_W
OFF_GP = OFF_GA + D_MODEL
IN_W = OFF_GP + D_MODEL
NUM_BUCKETS = 32
MAX_DISTANCE = 2048
N_EXPERTS = 32
TOP_K = 4
SWIGLU_LIMIT = 7.0
SWIGLU_ALPHA = 1.702
N_ADA = 6
EPS = 1e-6
NEG_INF = -1e30
PAST_LEN = 8192
SCALE = HEAD_DIM ** -0.5

LANES = 128
ROW_CHUNK = 16
TM_PROMPT = 512
TM_SAMPLE = 128
FFN_BLOCK = 512
FFN_SUB = 128
SEL_CHUNK = 512
BF16_EXACT = 256
VMEM_LIMIT = 56 * 1024 * 1024


def _cparams(n_axes):
    return pltpu.CompilerParams(dimension_semantics=("arbitrary",) * n_axes,
                                vmem_limit_bytes=VMEM_LIMIT)


def _const_spec(shape):
    nd = len(shape)
    return pl.BlockSpec(shape, lambda *_: (0,) * nd)


def _ada_kernel(c_ref, w_ref, b_ref, o_ref):
    c = c_ref[...]
    s = c * jax.nn.sigmoid(c)
    o_ref[...] = jnp.dot(s.astype(BF16), w_ref[...].astype(BF16),
                         preferred_element_type=F32) + b_ref[...]


def _ada(c_all, w_ada, b_ada):
    rows = c_all.shape[0]
    n = w_ada.shape[1]
    tn = 1536
    return pl.pallas_call(
        _ada_kernel,
        grid=(n // tn,),
        in_specs=[pl.BlockSpec((rows, D_MODEL), lambda j: (0, 0)),
                  pl.BlockSpec((D_MODEL, tn), lambda j: (0, j)),
                  pl.BlockSpec((1, tn), lambda j: (0, j))],
        out_specs=pl.BlockSpec((rows, tn), lambda j: (0, j)),
        out_shape=jax.ShapeDtypeStruct((rows, n), F32),
        compiler_params=_cparams(1),
        name="ada",
    )(c_all, w_ada, b_ada.reshape(1, n))


def _proj_kernel(x_ref, g_ref, sc_ref, sh_ref, w_ref, bd_ref, qg_ref, kg_ref,
                 *refs, tm, dils, st_rows):
    q_refs, k_refs, v_refs = refs[0:3], refs[3:6], refs[6:9]
    u_ref, sga_ref, sgp_ref = refs[9:12]
    st_refs = refs[12:15]
    scr = refs[15]

    x = x_ref[...]
    ms = jnp.mean(x * x, axis=-1, keepdims=True)
    h = x * lax.rsqrt(ms + EPS) * g_ref[...] * (1.0 + sc_ref[...]) + sh_ref[...]
    hb = h.astype(BF16)

    def proj(off, width):
        return jnp.dot(hb, w_ref[:, off:off + width], preferred_element_type=F32)

    def head_norm(z, gain_ref):
        zz = (z * z).astype(BF16)
        half = GROUP_W // 2
        ss = jnp.concatenate(
            [jnp.dot(zz[:, :half], bd_ref[...], preferred_element_type=F32),
             jnp.dot(zz[:, half:], bd_ref[...], preferred_element_type=F32)], axis=1)
        return z * lax.rsqrt(ss * (1.0 / HEAD_DIM) + EPS) * gain_ref[...]

    def put(out_ref, val, d):
        if d == 1:
            out_ref[...] = val.astype(out_ref.dtype)
        else:
            for c in range(GROUP_W // LANES):
                scr[c] = val[:, c * LANES:(c + 1) * LANES]
            for r in range(d):
                for c in range(GROUP_W // LANES):
                    col = r * GROUP_W + c * LANES
                    out_ref[:, col:col + LANES] = (
                        scr[c, pl.ds(r, tm // d, stride=d), :].astype(out_ref.dtype))

    for g, d in enumerate(dils):
        qn = head_norm(proj(g * GROUP_W, GROUP_W), qg_ref)
        put(q_refs[g], qn, d)
        kn = head_norm(proj(OFF_K + g * GROUP_W, GROUP_W), kg_ref)
        put(k_refs[g], kn, d)
        v = proj(OFF_V + g * GROUP_W, GROUP_W)
        put(v_refs[g], v, d)
        rb = st_rows[g]
        st_refs[g][:, 0:GROUP_W] = kn[tm - rb:, :]
        st_refs[g][:, GROUP_W:2 * GROUP_W] = v[tm - rb:, :]

    u_ref[...] = proj(OFF_U, POOL_W)
    sga_ref[...] = jax.nn.sigmoid(proj(OFF_GA, D_MODEL)).astype(BF16)
    sgp_ref[...] = jax.nn.sigmoid(proj(OFF_GP, D_MODEL)).astype(BF16)


def _mod_spec(per_row, tm):
    if per_row:
        return pl.BlockSpec((tm, D_MODEL), lambda i: (i, 0))
    return pl.BlockSpec((1, D_MODEL), lambda i: (0, 0))


def _proj(x, g1, sc1, sh1, w_in_bf, bdiag, qg, kg, *, tm, dils, per_row):
    s = x.shape[0]
    nt = s // tm
    wins = tuple(min(w, s) for w, _ in DIL_GROUPS)
    st_rows = tuple(min(tm, w) for w in wins)

    def res_spec(d):
        return pl.BlockSpec((tm // d, d * GROUP_W), lambda i: (i, 0))

    def st_spec(w, rb):
        first = nt - w // rb
        return pl.BlockSpec((rb, 2 * GROUP_W), lambda i: (jnp.maximum(i - first, 0), 0))

    qkv_shapes = [jax.ShapeDtypeStruct((s // d, d * GROUP_W), BF16) for d in dils]
    out_shape = (qkv_shapes * 3
                 + [jax.ShapeDtypeStruct((s, POOL_W), F32),
                    jax.ShapeDtypeStruct((s, D_MODEL), BF16),
                    jax.ShapeDtypeStruct((s, D_MODEL), BF16)]
                 + [jax.ShapeDtypeStruct((w, 2 * GROUP_W), F32) for w in wins])
    out_specs = ([res_spec(d) for d in dils] * 3
                 + [pl.BlockSpec((tm, POOL_W), lambda i: (i, 0)),
                    pl.BlockSpec((tm, D_MODEL), lambda i: (i, 0)),
                    pl.BlockSpec((tm, D_MODEL), lambda i: (i, 0))]
                 + [st_spec(w, rb) for w, rb in zip(wins, st_rows)])
    in_specs = [pl.BlockSpec((tm, D_MODEL), lambda i: (i, 0)),
                _const_spec((1, D_MODEL)),
                _mod_spec(per_row, tm), _mod_spec(per_row, tm),
                pl.BlockSpec((D_MODEL, IN_W), lambda i: (0, 0), pipeline_mode=pl.Buffered(1)),
                _const_spec((GROUP_W // 2, GROUP_W // 2)),
                _const_spec((1, GROUP_W)), _const_spec((1, GROUP_W))]
    outs = pl.pallas_call(
        functools.partial(_proj_kernel, tm=tm, dils=dils, st_rows=st_rows),
        grid=(nt,),
        in_specs=in_specs,
        out_specs=out_specs,
        out_shape=out_shape,
        scratch_shapes=[pltpu.VMEM((GROUP_W // LANES, tm, LANES), F32)],
        compiler_params=_cparams(1),
        name="proj",
    )(x, g1, sc1, sh1, w_in_bf, bdiag, qg, kg)
    return outs[0:3], outs[3:6], outs[6:9], outs[9], outs[10], outs[11], outs[12:15]


def _sample_group_attn(qs_ref, kns_ref, vns_ref, bself_ref, btab_ref, c_ref, os_ref, lses_ref):
    n_tok = qs_ref.shape[0]
    win = btab_ref.shape[1]
    row_w = lax.broadcasted_iota(jnp.int32, (HEADS, win), 0)
    row_e = lax.broadcasted_iota(jnp.int32, (HEADS, HEAD_DIM), 0)
    ss, s0s = [], []
    for t in range(n_tok):
        q = qs_ref[t]
        qb = q.astype(BF16)
        s = jnp.zeros((HEADS, win), F32)
        for h in range(HEADS):
            sh = jnp.dot(qb, c_ref[0, t, 0, h].astype(BF16), preferred_element_type=F32)
            s = jnp.where(row_w == h, sh, s)
        ss.append(s + btab_ref[...])
        s0s.append(jnp.sum(q * kns_ref[t], axis=-1, keepdims=True) + bself_ref[...])
    s = jnp.concatenate(ss, axis=0)
    s0 = jnp.concatenate(s0s, axis=0)
    m = jnp.maximum(jnp.max(s, axis=-1, keepdims=True), s0)
    p = jnp.exp(s - m)
    p0 = jnp.exp(s0 - m)
    l = jnp.sum(p, axis=-1, keepdims=True) + p0
    pb = p.astype(BF16)
    lse = m + jnp.log(l)
    for t in range(n_tok):
        rows = slice(t * HEADS, (t + 1) * HEADS)
        o = jnp.zeros((HEADS, HEAD_DIM), F32)
        for h in range(HEADS):
            oh = lax.dot_general(pb[rows], c_ref[0, t, 1, h].astype(BF16),
                                 (((1,), (1,)), ((), ())), preferred_element_type=F32)
            o = jnp.where(row_e == h, oh, o)
        os_ref[t] = (o + p0[rows] * vns_ref[t]) / l[rows]
        lses_ref[t] = lse[rows]


SAMPLE_IN = 6
SAMPLE_PARTS = (1, 1, 2)
SAMPLE_HOST = (((2, 1),), ((1, 0),), ((2, 0), (0, 0)))


def _attn_kernel(q_ref, kp_ref, kc_ref, vp_ref, vc_ref, r_ref, *refs, n_sample_groups):
    n_in = SAMPLE_IN * n_sample_groups
    o_ref, lse_ref = refs[n_in], refs[n_in + 1]
    bias_ref = refs[-1]
    i = pl.program_id(1)

    for sg in range(n_sample_groups):
        _sample_group_attn(*refs[SAMPLE_IN * sg:SAMPLE_IN * (sg + 1)],
                           *refs[n_in + 2 + 2 * sg:n_in + 4 + 2 * sg])

    @pl.when((pl.program_id(0) == 0) & (i == 0))
    def _():
        for h in range(HEADS):
            row = jnp.broadcast_to(r_ref[h:h + 1, :], (ATT_BLK, 2 * ATT_BLK))
            bias_ref[h] = pltpu.roll(row, 0, 1, stride=1, stride_axis=0)

    q = q_ref[...]
    k = jnp.concatenate([kp_ref[...], kc_ref[...]], axis=0)
    v = jnp.concatenate([vp_ref[...], vc_ref[...]], axis=0)
    col = lax.broadcasted_iota(jnp.int32, (ATT_BLK, 2 * ATT_BLK), 1)
    no_prev = jnp.where((col < ATT_BLK) & (i == 0), NEG_INF, 0.0)
    lane_q = lax.broadcasted_iota(jnp.int32, (ATT_SUB * ATT_BLK, LANES), 1)
    lane_v = lax.broadcasted_iota(jnp.int32, ((ATT_SUB + 1) * ATT_BLK, LANES), 1)

    def pair(h):
        return slice((h // 2) * LANES, (h // 2 + 1) * LANES)

    def mine(lane, h):
        return (lane < HEAD_DIM) == (h % 2 == 0)

    ss = []
    for h in range(HEADS):
        q2 = q[:, pair(h)]
        qm = jnp.where(mine(lane_q, h), q2, jnp.zeros_like(q2))
        k2 = k[:, pair(h)]
        for j in range(ATT_SUB):
            s = lax.dot_general(qm[j * ATT_BLK:(j + 1) * ATT_BLK], k2[j * ATT_BLK:(j + 2) * ATT_BLK],
                                (((1,), (1,)), ((), ())), preferred_element_type=F32)
            s = s + bias_ref[h]
            ss.append(s + no_prev if j == 0 else s)
    s = jnp.concatenate(ss, axis=0)
    m = jnp.max(s, axis=-1, keepdims=True)
    p = jnp.exp(s - m)
    l = jnp.sum(p, axis=-1, keepdims=True)
    pb = p.astype(BF16)
    lse = m + jnp.log(l)
    inv_l = 1.0 / l
    outs, lses = [], []
    for h in range(HEADS):
        v2 = v[:, pair(h)]
        vm = jnp.where(mine(lane_v, h), v2, jnp.zeros_like(v2))
        o_sub, lse_sub = [], []
        for j in range(ATT_SUB):
            rows = slice((h * ATT_SUB + j) * ATT_BLK, (h * ATT_SUB + j + 1) * ATT_BLK)
            o_sub.append(jnp.dot(pb[rows], vm[j * ATT_BLK:(j + 2) * ATT_BLK],
                                 preferred_element_type=F32) * inv_l[rows])
            lse_sub.append(lse[rows])
        o = jnp.concatenate(o_sub, axis=0)
        if h % 2 == 0:
            outs.append(o)
        else:
            outs[-1] = outs[-1] + o
        lses.append(jnp.concatenate(lse_sub, axis=0))
    o_ref[...] = jnp.concatenate(outs, axis=-1).astype(o_ref.dtype)
    lse_ref[...] = jnp.concatenate(
        lses + [jnp.zeros((ATT_SUB * ATT_BLK, LANES - HEADS), F32)], axis=-1)


def _attn(q, k, v, r_tab, d, sample_groups):
    rows = q.shape[0]
    step = ATT_SUB * ATT_BLK
    nblk = rows // step
    cur = pl.BlockSpec((step, GROUP_W), lambda r, i: (i, r))
    prev = pl.BlockSpec((ATT_BLK, GROUP_W), lambda r, i: (jnp.maximum(i * ATT_SUB - 1, 0), r))
    in_specs = [cur, prev, cur, prev, cur,
                pl.BlockSpec((HEADS, 2 * ATT_BLK), lambda r, i: (0, 0))]
    out_specs = [pl.BlockSpec((step, GROUP_W), lambda r, i: (i, r)),
                 pl.BlockSpec((step, LANES), lambda r, i: (i, r))]
    out_shape = [jax.ShapeDtypeStruct((rows, d * GROUP_W), BF16),
                 jax.ShapeDtypeStruct((rows, d * LANES), F32)]
    args = [q, k, k, v, v, r_tab]
    for qs, kns, vns, bself, btab, cache_t, tok_start, n_tok in sample_groups:
        assert n_tok % (d * nblk) == 0
        tok = n_tok // (d * nblk)
        assert tok_start % tok == 0
        first = tok_start // tok

        def in_idx(r, i, first=first):
            return (first + r * nblk + i, 0, 0)

        def out_idx(r, i):
            return (r * nblk + i, 0, 0)

        tok_in = pl.BlockSpec((tok, HEADS, HEAD_DIM), in_idx)
        in_specs += [tok_in, tok_in, tok_in,
                     pl.BlockSpec(bself.shape, lambda r, i: (0, 0)),
                     pl.BlockSpec(btab.shape, lambda r, i: (0, 0)),
                     pl.BlockSpec((1, tok) + cache_t.shape[2:],
                                  lambda r, i, first=first: (0, first + r * nblk + i, 0, 0, 0, 0))]
        out_specs += [pl.BlockSpec((tok, HEADS, HEAD_DIM), out_idx),
                      pl.BlockSpec((tok, HEADS, 1), out_idx)]
        out_shape += [jax.ShapeDtypeStruct((n_tok, HEADS, HEAD_DIM), F32),
                      jax.ShapeDtypeStruct((n_tok, HEADS, 1), F32)]
        args += [qs, kns, vns, bself, btab, cache_t]
    outs = pl.pallas_call(
        functools.partial(_attn_kernel, n_sample_groups=len(sample_groups)),
        grid=(d, nblk),
        in_specs=in_specs,
        out_specs=out_specs,
        out_shape=out_shape,
        scratch_shapes=[pltpu.VMEM((HEADS, ATT_BLK, 2 * ATT_BLK), F32)],
        compiler_params=_cparams(2),
        name=f"attn_d{d}",
    )(*args)
    return outs[0], outs[1], [(outs[2 + 2 * j], outs[3 + 2 * j]) for j in range(len(sample_groups))]


def _pool_sample_kernel(st_ref, u_ref, pooled_ref, new_ref):
    u = u_ref[...]
    rows = [st_ref[0, j] for j in range(POOL_BUF)]
    outs = []
    for g, w in enumerate(POOL_WINDOWS):
        sl = slice(g * POOL_GW, (g + 1) * POOL_GW)
        acc = u[:, sl]
        for j in range(POOL_BUF - (w - 1), POOL_BUF):
            acc = acc + rows[j][:, sl]
        outs.append(acc / float(w) - u[:, sl])
    pooled_ref[...] = jnp.concatenate(outs, axis=-1)
    for j in range(POOL_BUF - 1):
        new_ref[0, j] = rows[j + 1]
    new_ref[0, POOL_BUF - 1] = u


def _pool_sample(state, u):
    n = u.shape[0]
    return pl.pallas_call(
        _pool_sample_kernel,
        grid=(1,),
        in_specs=[_const_spec(state.shape), _const_spec(u.shape)],
        out_specs=[_const_spec(u.shape), _const_spec(state.shape)],
        out_shape=[jax.ShapeDtypeStruct((n, POOL_W), F32),
                   jax.ShapeDtypeStruct(state.shape, F32)],
        compiler_params=_cparams(1),
        name="pool_sample",
    )(state, u)


def _post_kernel(*refs, tm, dils, pooled_given, n_valid_steps, aliased):
    it = iter(refs)
    x_ref = next(it)
    o_refs = [next(it) for _ in range(N_GROUPS)]
    lse_refs = [next(it) for _ in range(N_GROUPS)]
    if pooled_given:
        pooled_ref = next(it)
    else:
        u_ref, uh_ref = next(it), next(it)
    sga_ref, sgp_ref = next(it), next(it)
    wpm_ref, psc_ref, wua_ref, wup_ref, wout_ref, exp_ref = (next(it) for _ in range(6))
    gt1_ref, g2_ref, sc2_ref, sh2_ref = (next(it) for _ in range(4))
    wrh_ref, wrl_ref, br_ref = (next(it) for _ in range(3))
    if aliased:
        for _ in range(6):
            next(it)
    x1_ref, h2_ref, a_ref, idx_ref, gk_ref, cnt_ref = (next(it) for _ in range(6))
    ob_scr, ls_scr = next(it), next(it)

    i = pl.program_id(0)

    def compute():
        obs, lss = [], []
        for g, d in enumerate(dils):
            if d == 1:
                obs.append(o_refs[g][...].astype(F32))
                lss.append(lse_refs[g][...])
            else:
                for r in range(d):
                    for c in range(GROUP_W // LANES):
                        col = r * GROUP_W + c * LANES
                        ob_scr[c, pl.ds(r, tm // d, stride=d), :] = (
                            o_refs[g][:, col:col + LANES].astype(F32))
                    ls_scr[pl.ds(r, tm // d, stride=d), :] = (
                        lse_refs[g][:, r * LANES:(r + 1) * LANES])
                obs.append(jnp.concatenate([ob_scr[c] for c in range(GROUP_W // LANES)],
                                           axis=-1))
                lss.append(ls_scr[...])
        mx = jnp.maximum(jnp.maximum(lss[0], lss[1]), lss[2])
        es = [jnp.exp(l - mx) for l in lss]
        den = es[0] + es[1] + es[2]
        attn_o = jnp.zeros((tm, GROUP_W), F32)
        head_lane = lax.broadcasted_iota(jnp.int32, (tm, LANES), 1) < HEADS
        for g in range(N_GROUPS):
            w = jnp.where(head_lane, es[g] / den, 0.0)
            w_hi = w.astype(BF16).astype(F32)
            w_lo = (w - w_hi).astype(BF16).astype(F32)
            lhs = (w_hi + pltpu.roll(w_lo, HEADS, 1)).astype(BF16)
            wexp = jnp.dot(lhs, exp_ref[...], preferred_element_type=F32)
            attn_o = attn_o + wexp * obs[g]

        if pooled_given:
            pooled = pooled_ref[...]
        else:
            u = u_ref[...]
            halo = jnp.where(i == 0, 0.0, uh_ref[...])
            pos = (lax.broadcasted_iota(jnp.int32, (tm, 1), 0) + i * tm + 1).astype(F32)
            outs = []
            for g, w in enumerate(POOL_WINDOWS):
                sl = slice(g * POOL_GW, (g + 1) * POOL_GW)
                a = jnp.concatenate([halo[:, sl], u[:, sl]], axis=0)
                span = 1
                while span < w:
                    n = a.shape[0] - span
                    a = a[span:, :] + a[:n, :]
                    span *= 2
                off = a.shape[0] - tm
                win_sum = a[off:, :]
                outs.append(win_sum / jnp.minimum(pos, float(w)) - u[:, sl])
            pooled = jnp.concatenate(outs, axis=-1)
        pool_parts = []
        for g in range(len(POOL_WINDOWS)):
            sl = slice(g * POOL_GW, (g + 1) * POOL_GW)
            pool_parts.append(jnp.dot(pooled[:, sl].astype(BF16), wpm_ref[g],
                                      preferred_element_type=F32))
        pool_o = jnp.concatenate(pool_parts, axis=-1) * psc_ref[...]

        up_a = jnp.dot(attn_o.astype(BF16), wua_ref[...], preferred_element_type=F32)
        up_p = jnp.dot(pool_o.astype(BF16), wup_ref[...], preferred_element_type=F32)
        merged = sga_ref[...].astype(F32) * up_a + sgp_ref[...].astype(F32) * up_p
        mo = jnp.dot(merged.astype(BF16), wout_ref[...], preferred_element_type=F32)
        x1 = x_ref[...] + gt1_ref[...] * mo
        x1_ref[...] = x1

        ms = jnp.mean(x1 * x1, axis=-1, keepdims=True)
        h2 = x1 * lax.rsqrt(ms + EPS) * g2_ref[...] * (1.0 + sc2_ref[...]) + sh2_ref[...]
        h2_hi = h2.astype(BF16)
        h2_ref[...] = h2_hi
        h2_lo = (h2 - h2_hi.astype(F32)).astype(BF16)
        logits = (jnp.dot(h2_hi, wrh_ref[...], preferred_element_type=F32)
                  + jnp.dot(h2_lo, wrh_ref[...], preferred_element_type=F32)
                  + jnp.dot(h2_hi, wrl_ref[...], preferred_element_type=F32)
                  + br_ref[...])
        lane = lax.broadcasted_iota(jnp.int32, (tm, LANES), 1).astype(F32)
        work = logits
        vals, ids = [], []
        for _ in range(TOP_K):
            m = jnp.max(work, axis=-1, keepdims=True)
            ik = jnp.min(jnp.where(work == m, lane, float(LANES)), axis=-1, keepdims=True)
            vals.append(m)
            ids.append(ik)
            work = jnp.where(lane == ik, -3e38, work)
        ex = [jnp.exp(v - vals[0]) for v in vals]
        den_k = ex[0] + ex[1] + ex[2] + ex[3]
        a = jnp.zeros((tm, LANES), F32)
        idx = jnp.zeros((tm, LANES), F32)
        gk = jnp.zeros((tm, LANES), F32)
        for kk in range(TOP_K):
            gate = ex[kk] / den_k
            a = a + jnp.where(lane == ids[kk], gate, 0.0)
            idx = jnp.where(lane == float(kk), ids[kk], idx)
            gk = jnp.where(lane == float(kk), gate, gk)
        a_ref[...] = a
        idx_ref[...] = idx
        gk_ref[...] = gk
        cnt = jnp.sum((a > 0.0).astype(F32), axis=0, keepdims=True)
        row = lax.broadcasted_iota(jnp.int32, (8, LANES), 0)
        cnt_ref[0] = jnp.where(row == 0, jnp.broadcast_to(cnt, (8, LANES)), 0.0)

    if n_valid_steps is None:
        compute()
    else:
        pl.when(i < n_valid_steps)(compute)

        @pl.when(i >= n_valid_steps)
        def _():
            x1_ref[...] = jnp.zeros(x1_ref.shape, x1_ref.dtype)
            h2_ref[...] = jnp.zeros(h2_ref.shape, h2_ref.dtype)
            a_ref[...] = jnp.zeros(a_ref.shape, a_ref.dtype)
            idx_ref[...] = jnp.zeros(idx_ref.shape, idx_ref.dtype)
            gk_ref[...] = jnp.zeros(gk_ref.shape, gk_ref.dtype)


def _post(x, o_list, lse_list, pool_in, sga, sgp, wts, mods, *, tm, dils, per_row,
          rows_total, row_block0, cnt_tiles, cnt_block, grid, n_valid_steps, alias_bufs):
    pooled_given = not isinstance(pool_in, tuple)
    nv = grid if n_valid_steps is None else n_valid_steps

    def clamp(i):
        return jnp.minimum(i, nv - 1)

    def tile_spec(width):
        return pl.BlockSpec((tm, width), lambda i: (clamp(i), 0))

    in_specs = [tile_spec(D_MODEL)]
    in_specs += [pl.BlockSpec((tm // d, d * GROUP_W), lambda i: (clamp(i), 0)) for d in dils]
    in_specs += [pl.BlockSpec((tm // d, d * LANES), lambda i: (clamp(i), 0)) for d in dils]
    args = [x, *o_list, *lse_list]
    if pooled_given:
        in_specs.append(tile_spec(POOL_W))
        args.append(pool_in)
    else:
        u = pool_in[0]
        in_specs += [tile_spec(POOL_W),
                     pl.BlockSpec((16, POOL_W),
                                  lambda i: (jnp.maximum(i * (tm // 16) - 1, 0), 0))]
        args += [u, u]
    in_specs += [tile_spec(D_MODEL), tile_spec(D_MODEL)]
    args += [sga, sgp]
    wpm, psc, wua, wup, wout, expand, g2, wrh, wrl, br = wts
    gt1, sc2, sh2 = mods

    def mspec():
        if per_row:
            return pl.BlockSpec((tm, D_MODEL), lambda i: (clamp(i), 0))
        return _const_spec((1, D_MODEL))

    in_specs += [_const_spec(wpm.shape), _const_spec(psc.shape), _const_spec(wua.shape),
                 _const_spec(wup.shape), _const_spec(wout.shape), _const_spec(expand.shape),
                 mspec(), _const_spec(g2.shape), mspec(), mspec(),
                 _const_spec(wrh.shape), _const_spec(wrl.shape), _const_spec(br.shape)]
    args += [wpm, psc, wua, wup, wout, expand, gt1, g2, sc2, sh2, wrh, wrl, br]
    aliases = {}
    if alias_bufs is not None:
        base = len(args)
        in_specs += [pl.BlockSpec(memory_space=pl.ANY)] * 6
        args += list(alias_bufs)
        aliases = {base + j: j for j in range(6)}

    def out_spec(width):
        return pl.BlockSpec((tm, width), lambda i: (row_block0 + i, 0))

    out_specs = [out_spec(D_MODEL), out_spec(D_MODEL), out_spec(LANES), out_spec(LANES),
                 out_spec(LANES),
                 pl.BlockSpec((1, 8, LANES),
                              lambda i: (cnt_block if cnt_block is not None else i, 0, 0))]
    out_shape = [jax.ShapeDtypeStruct((rows_total, D_MODEL), F32),
                 jax.ShapeDtypeStruct((rows_total, D_MODEL), BF16),
                 jax.ShapeDtypeStruct((rows_total, LANES), F32),
                 jax.ShapeDtypeStruct((rows_total, LANES), F32),
                 jax.ShapeDtypeStruct((rows_total, LANES), F32),
                 jax.ShapeDtypeStruct((cnt_tiles, 8, LANES), F32)]
    return pl.pallas_call(
        functools.partial(_post_kernel, tm=tm, dils=dils, pooled_given=pooled_given,
                          n_valid_steps=n_valid_steps, aliased=alias_bufs is not None),
        grid=(grid,),
        in_specs=in_specs,
        out_specs=out_specs,
        out_shape=out_shape,
        scratch_shapes=[pltpu.VMEM((GROUP_W // LANES, tm, LANES), F32),
                        pltpu.VMEM((tm, LANES), F32)],
        input_output_aliases=aliases,
        compiler_params=_cparams(1),
        name="post_sample" if per_row else "post",
    )(*args)


def _sort_rows(tm):
    return -(-(TOP_K * tm + N_EXPERTS * (ROW_CHUNK - 1)) // SEL_CHUNK) * SEL_CHUNK


def _for_row_pieces(n_chunks, max_pow, fn):
    big = 1 << max_pow

    def body(c, carry):
        fn(c * big, big)
        return carry

    lax.fori_loop(0, n_chunks >> max_pow, body, 0)
    for pw in range(max_pow - 1, -1, -1):
        @pl.when(((n_chunks >> pw) & 1) == 1)
        def _(pw=pw):
            fn((n_chunks >> (pw + 1)) << (pw + 1), 1 << pw)


SEG_MAX_POW = 3
TILE_MAX_POW = 5


def _moe_sort_kernel(seg_s, goff_s, nch_s, ntot_s, tstart_s, tnch_s,
                     a_ref, idx_ref, h2_ref, segv_ref, lt_ref,
                     xb_hbm, dst_ref, xs_scr, zero_scr, sem, *, tm, n_rows):
    i = pl.program_id(0)
    nt = pl.num_programs(0)
    slot = i % 2
    sel = a_ref[...] > 0.0
    ahead = jnp.dot(lt_ref[...], sel.astype(BF16), preferred_element_type=F32)
    slot1 = jnp.where(sel, segv_ref[0] + ahead + 1.0, 0.0)
    lane = lax.broadcasted_iota(jnp.int32, (tm, LANES), 1).astype(F32)
    idx = idx_ref[...]
    dst = jnp.full((tm, LANES), -1.0, F32)
    for kk in range(TOP_K):
        hit = lane == idx[:, kk:kk + 1]
        dk = jnp.sum(jnp.where(hit, slot1, 0.0), axis=-1, keepdims=True) - 1.0
        dst = jnp.where(lane == float(kk), dk, dst)
    dst_ref[...] = dst
    dst_t = dst.T
    h2 = h2_ref[...]
    local_rows = lax.broadcasted_iota(jnp.int32, (BF16_EXACT, tm), 0).astype(F32).astype(BF16)
    one, zero = jnp.ones((), BF16), jnp.zeros((), BF16)
    for c in range(n_rows // BF16_EXACT):
        p = None
        for kk in range(TOP_K):
            hit = local_rows == (dst_t[kk:kk + 1, :] - float(c * BF16_EXACT)).astype(BF16)
            p = hit if p is None else p | hit
        xs = jnp.dot(jnp.where(p, one, zero), h2, preferred_element_type=F32)
        xs_scr[slot, c * BF16_EXACT:(c + 1) * BF16_EXACT, :] = xs.astype(BF16)

    def rows_copy(buf, src_row, dst_row, n_chunks):
        return pltpu.make_async_copy(
            xs_scr.at[buf, pl.ds(pl.multiple_of(src_row, ROW_CHUNK), n_chunks * ROW_CHUNK)],
            xb_hbm.at[pl.ds(pl.multiple_of(dst_row, ROW_CHUNK), n_chunks * ROW_CHUNK)],
            sem.at[buf])

    def per_expert(e, carry):
        so = seg_s[i * N_EXPERTS + e]
        go = goff_s[i * N_EXPERTS + e]
        _for_row_pieces(
            nch_s[i * N_EXPERTS + e], SEG_MAX_POW,
            lambda off, n: rows_copy(slot, so + off * ROW_CHUNK, go + off * ROW_CHUNK, n).start())
        return carry

    lax.fori_loop(0, N_EXPERTS, per_expert, 0)

    def drain(buf, tile):
        _for_row_pieces(ntot_s[tile], TILE_MAX_POW, lambda off, n: rows_copy(buf, 0, 0, n).wait())

    @pl.when(i > 0)
    def _():
        drain(1 - slot, i - 1)

    @pl.when(i == nt - 1)
    def _():
        drain(slot, i)
        zero_scr[...] = jnp.zeros(zero_scr.shape, zero_scr.dtype)

        def tail_copy(dst_row, n_chunks):
            return pltpu.make_async_copy(
                zero_scr.at[pl.ds(0, n_chunks * ROW_CHUNK)],
                xb_hbm.at[pl.ds(pl.multiple_of(dst_row, ROW_CHUNK), n_chunks * ROW_CHUNK)],
                sem.at[2])

        def per_expert_tail(e, carry):
            _for_row_pieces(tnch_s[e], SEG_MAX_POW,
                            lambda off, n: tail_copy(tstart_s[e] + off * ROW_CHUNK, n).start())
            _for_row_pieces(tnch_s[e], SEG_MAX_POW, lambda off, n: tail_copy(0, n).wait())
            return carry

        lax.fori_loop(0, N_EXPERTS, per_expert_tail, 0)


def _moe_sort(meta, a_all, idx_all, h2_all, *, tm, cap):
    t_all = a_all.shape[0]
    nt = t_all // tm
    n_rows = _sort_rows(tm)
    lt = jnp.tril(jnp.ones((tm, tm), BF16), -1)
    grid_spec = pltpu.PrefetchScalarGridSpec(
        num_scalar_prefetch=6,
        grid=(nt,),
        in_specs=[pl.BlockSpec((tm, LANES), lambda i, *_: (i, 0)),
                  pl.BlockSpec((tm, LANES), lambda i, *_: (i, 0)),
                  pl.BlockSpec((tm, D_MODEL), lambda i, *_: (i, 0)),
                  pl.BlockSpec((1, 1, LANES), lambda i, *_: (i, 0, 0)),
                  pl.BlockSpec((tm, tm), lambda i, *_: (0, 0))],
        out_specs=[pl.BlockSpec(memory_space=pl.ANY),
                   pl.BlockSpec((tm, LANES), lambda i, *_: (i, 0))],
        scratch_shapes=[pltpu.VMEM((2, n_rows, D_MODEL), BF16),
                        pltpu.VMEM(((1 << SEG_MAX_POW) * ROW_CHUNK, D_MODEL), BF16),
                        pltpu.SemaphoreType.DMA((3,))],
    )
    return pl.pallas_call(
        functools.partial(_moe_sort_kernel, tm=tm, n_rows=n_rows),
        grid_spec=grid_spec,
        out_shape=[jax.ShapeDtypeStruct((cap, D_MODEL), BF16),
                   jax.ShapeDtypeStruct((t_all, LANES), F32)],
        compiler_params=_cparams(1),
        name="moe_sort",
    )(meta["seg"], meta["goff"], meta["nch"], meta["ntot"], meta["tstart"], meta["tnch"],
      a_all, idx_all, h2_all, meta["segv"], lt)


def _moe_ffn_kernel(be_s, nused_s, ord_s, next_s, rows_s,
                    x_ref, wgu_hbm, bgu_ref, wd_hbm, bd_ref, y_ref,
                    wgu_f32, wd_f32, wgu_bf, wd_bf, sem):
    b = pl.program_id(0)

    def weight_copies(e, slot):
        return (pltpu.make_async_copy(wgu_hbm.at[e], wgu_f32.at[slot], sem.at[0, slot]),
                pltpu.make_async_copy(wd_hbm.at[e], wd_f32.at[slot], sem.at[1, slot]))

    @pl.when(b < nused_s[0])
    def _():
        e = be_s[b]
        e_prev = be_s[jnp.maximum(b - 1, 0)]
        slot = ord_s[b] % 2

        @pl.when(b == 0)
        def _():
            for cp in weight_copies(e, slot):
                cp.start()

        @pl.when((b == 0) | (e != e_prev))
        def _():
            for cp in weight_copies(e, slot):
                cp.wait()
            e_next = next_s[b]

            @pl.when(e_next >= 0)
            def _():
                for cp in weight_copies(e_next, 1 - slot):
                    cp.start()

            wgu_bf[...] = wgu_f32[slot].astype(BF16)
            wd_bf[...] = wd_f32[slot].astype(BF16)

        def ffn_rows(n):
            hgu = jnp.dot(x_ref[0:n, :], wgu_bf[...], preferred_element_type=F32) + bgu_ref[0]
            d_ff = hgu.shape[1] // 2
            hg = jnp.minimum(hgu[:, :d_ff], SWIGLU_LIMIT)
            hu = jnp.clip(hgu[:, d_ff:], -SWIGLU_LIMIT, SWIGLU_LIMIT)
            act = hg * jax.nn.sigmoid(SWIGLU_ALPHA * hg) * (hu + 1.0)
            y = jnp.dot(act.astype(BF16), wd_bf[...], preferred_element_type=F32) + bd_ref[0]
            y_ref[0:n, :] = y.astype(y_ref.dtype)

        rows_here = rows_s[b]
        for n in range(FFN_SUB, FFN_BLOCK + 1, FFN_SUB):
            pl.when(rows_here == n)(functools.partial(ffn_rows, n))


def _moe_ffn(meta, xb, w_gate_up, b_gate_up, w_down, b_down):
    cap = xb.shape[0]
    nb = cap // FFN_BLOCK
    d_ff2 = w_gate_up.shape[2]

    def blk(b, be, nu, *_):
        return jnp.minimum(b, jnp.maximum(nu[0] - 1, 0))

    def row_blk(b, be, nu, *_):
        return (blk(b, be, nu), 0)

    def expert_blk(b, be, nu, *_):
        return (be[blk(b, be, nu)], 0, 0)

    grid_spec = pltpu.PrefetchScalarGridSpec(
        num_scalar_prefetch=5,
        grid=(nb,),
        in_specs=[pl.BlockSpec((FFN_BLOCK, D_MODEL), row_blk),
                  pl.BlockSpec(memory_space=pl.ANY),
                  pl.BlockSpec((1, 1, d_ff2), expert_blk),
                  pl.BlockSpec(memory_space=pl.ANY),
                  pl.BlockSpec((1, 1, D_MODEL), expert_blk)],
        out_specs=pl.BlockSpec((FFN_BLOCK, D_MODEL), row_blk),
        scratch_shapes=[pltpu.VMEM((2, D_MODEL, d_ff2), F32),
                        pltpu.VMEM((2, d_ff2 // 2, D_MODEL), F32),
                        pltpu.VMEM((D_MODEL, d_ff2), BF16),
                        pltpu.VMEM((d_ff2 // 2, D_MODEL), BF16),
                        pltpu.SemaphoreType.DMA((2, 2))],
    )
    return pl.pallas_call(
        _moe_ffn_kernel,
        grid_spec=grid_spec,
        out_shape=jax.ShapeDtypeStruct((cap, D_MODEL), BF16),
        compiler_params=_cparams(1),
        name="moe_ffn",
    )(meta["block_expert"], meta["n_used"], meta["block_ord"], meta["block_next"],
      meta["block_rows"], xb, w_gate_up,
      b_gate_up.reshape(N_EXPERTS, 1, d_ff2), w_down, b_down.reshape(N_EXPERTS, 1, D_MODEL))


def _moe_unsort_kernel(seg_s, goff_s, nch_s, ntot_s,
                       dst_ref, gk_ref, x1_ref, g2p_ref, g2s_ref, yb_hbm,
                       yp_ref, ys_ref, ybuf, sem, *, tm, n_rows, n_prompt_tiles):
    i = pl.program_id(0)
    nt = pl.num_programs(0)
    slot = i % 2

    def rows_copy(buf, src_row, dst_row, n_chunks):
        return pltpu.make_async_copy(
            yb_hbm.at[pl.ds(pl.multiple_of(src_row, ROW_CHUNK), n_chunks * ROW_CHUNK)],
            ybuf.at[buf, pl.ds(pl.multiple_of(dst_row, ROW_CHUNK), n_chunks * ROW_CHUNK)],
            sem.at[buf])

    def fetch(tile, buf):
        def per_expert(e, carry):
            so = seg_s[tile * N_EXPERTS + e]
            go = goff_s[tile * N_EXPERTS + e]
            _for_row_pieces(
                nch_s[tile * N_EXPERTS + e], SEG_MAX_POW,
                lambda off, n: rows_copy(buf, go + off * ROW_CHUNK, so + off * ROW_CHUNK,
                                         n).start())
            return carry

        lax.fori_loop(0, N_EXPERTS, per_expert, 0)

    @pl.when(i == 0)
    def _():
        ybuf[...] = jnp.zeros(ybuf.shape, ybuf.dtype)
        fetch(0, 0)

    @pl.when(i + 1 < nt)
    def _():
        fetch(i + 1, 1 - slot)

    _for_row_pieces(ntot_s[i], TILE_MAX_POW, lambda off, n: rows_copy(slot, 0, 0, n).wait())

    dst = dst_ref[...]
    gk = gk_ref[...]
    acc = jnp.zeros((tm, D_MODEL), F32)
    for c in range(n_rows // SEL_CHUNK):
        cols = (lax.broadcasted_iota(jnp.int32, (tm, SEL_CHUNK), 1) + c * SEL_CHUNK).astype(F32)
        q = jnp.zeros((tm, SEL_CHUNK), F32)
        for kk in range(TOP_K):
            q = jnp.where(cols == dst[:, kk:kk + 1], gk[:, kk:kk + 1], q)
        acc = acc + jnp.dot(q.astype(BF16), ybuf[slot, c * SEL_CHUNK:(c + 1) * SEL_CHUNK, :],
                            preferred_element_type=F32)

    @pl.when(i < n_prompt_tiles)
    def _():
        yp_ref[...] = x1_ref[...] + g2p_ref[...] * acc

    @pl.when(i >= n_prompt_tiles)
    def _():
        ys_ref[...] = x1_ref[...] + g2s_ref[...] * acc


def _moe_unsort(meta, dst_all, gk_all, x1_all, gt2_p, gt2_s, yb, *, tm, n_prompt_tiles):
    t_all = dst_all.shape[0]
    nt = t_all // tm
    n_rows = _sort_rows(tm)
    last_p = n_prompt_tiles - 1
    grid_spec = pltpu.PrefetchScalarGridSpec(
        num_scalar_prefetch=4,
        grid=(nt,),
        in_specs=[pl.BlockSpec((tm, LANES), lambda i, *_: (i, 0)),
                  pl.BlockSpec((tm, LANES), lambda i, *_: (i, 0)),
                  pl.BlockSpec((tm, D_MODEL), lambda i, *_: (i, 0)),
                  pl.BlockSpec((1, D_MODEL), lambda i, *_: (0, 0)),
                  pl.BlockSpec((tm, D_MODEL), lambda i, *_: (0, 0)),
                  pl.BlockSpec(memory_space=pl.ANY)],
        out_specs=[pl.BlockSpec((tm, D_MODEL), lambda i, *_: (jnp.minimum(i, last_p), 0)),
                   pl.BlockSpec((tm, D_MODEL), lambda i, *_: (0, 0))],
        scratch_shapes=[pltpu.VMEM((2, n_rows, D_MODEL), BF16),
                        pltpu.SemaphoreType.DMA((2,))],
    )
    return pl.pallas_call(
        functools.partial(_moe_unsort_kernel, tm=tm, n_rows=n_rows,
                          n_prompt_tiles=n_prompt_tiles),
        grid_spec=grid_spec,
        out_shape=[jax.ShapeDtypeStruct((n_prompt_tiles * tm, D_MODEL), F32),
                   jax.ShapeDtypeStruct((tm, D_MODEL), F32)],
        compiler_params=_cparams(1),
        name="moe_unsort",
    )(meta["seg"], meta["goff"], meta["nch"], meta["ntot"],
      dst_all, gk_all, x1_all, gt2_p, gt2_s, yb)


def _moe_meta(cnt, tm):
    nt = cnt.shape[0]
    cnt = cnt.astype(jnp.int32)
    cnt_pad = (cnt + ROW_CHUNK - 1) // ROW_CHUNK * ROW_CHUNK
    seg = jnp.cumsum(cnt_pad, axis=1) - cnt_pad
    rows_e = jnp.sum(cnt_pad, axis=0)
    region = (rows_e + FFN_BLOCK - 1) // FFN_BLOCK * FFN_BLOCK
    gstart = jnp.cumsum(region) - region
    goff = gstart[None, :] + jnp.cumsum(cnt_pad, axis=0) - cnt_pad
    nblk_e = region // FFN_BLOCK
    blk_end = jnp.cumsum(nblk_e)
    cap = _moe_cap(nt * tm, tm)
    blocks = jnp.arange(cap // FFN_BLOCK, dtype=jnp.int32)
    block_expert = jnp.minimum(
        jnp.sum((blk_end[None, :] <= blocks[:, None]).astype(jnp.int32), axis=1), N_EXPERTS - 1)
    segv = jnp.zeros((nt, 1, LANES), F32).at[:, 0, :N_EXPERTS].set(seg.astype(F32))
    used = nblk_e > 0
    experts = jnp.arange(N_EXPERTS, dtype=jnp.int32)
    ord_e = jnp.cumsum(used.astype(jnp.int32)) - 1
    later = (experts[None, :] > experts[:, None]) & used[None, :]
    next_e = jnp.min(jnp.where(later, experts[None, :], N_EXPERTS), axis=1)
    next_e = jnp.where(next_e < N_EXPERTS, next_e, -1).astype(jnp.int32)
    of_block = (block_expert[:, None] == experts[None, :]).astype(jnp.int32)
    first_block = jnp.sum(of_block * (blk_end - nblk_e)[None, :], axis=1)
    rows_left = jnp.sum(of_block * rows_e[None, :], axis=1) - (blocks - first_block) * FFN_BLOCK
    block_rows = jnp.clip((rows_left + FFN_SUB - 1) // FFN_SUB * FFN_SUB, FFN_SUB, FFN_BLOCK)
    return {
        "block_rows": block_rows.astype(jnp.int32),
        "block_ord": jnp.sum(of_block * ord_e[None, :].astype(jnp.int32), axis=1),
        "block_next": jnp.sum(of_block * next_e[None, :], axis=1),
        "seg": seg.reshape(-1), "goff": goff.reshape(-1).astype(jnp.int32),
        "nch": (cnt_pad // ROW_CHUNK).reshape(-1),
        "ntot": jnp.sum(cnt_pad, axis=1) // ROW_CHUNK,
        "tstart": (gstart + rows_e).astype(jnp.int32),
        "tnch": (region - rows_e) // ROW_CHUNK,
        "block_expert": block_expert,
        "n_used": blk_end[-1:].astype(jnp.int32),
        "segv": segv,
    }


def _moe_cap(t_all, tm):
    nt = t_all // tm
    worst = TOP_K * t_all + nt * N_EXPERTS * (ROW_CHUNK - 1) + N_EXPERTS * (FFN_BLOCK - ROW_CHUNK)
    return -(-worst // FFN_BLOCK) * FFN_BLOCK


def _t5_bucket(dist):
    max_exact = NUM_BUCKETS // 2
    d = dist.astype(jnp.int32)
    ratio = (jnp.log(jnp.maximum(d, 1).astype(F32) / max_exact)
             / math.log(MAX_DISTANCE / max_exact))
    large = jnp.minimum(max_exact + (ratio * (NUM_BUCKETS - max_exact)).astype(jnp.int32),
                        NUM_BUCKETS - 1)
    return jnp.where(d < max_exact, d, large)


def _step_bias(tab, dil):
    return tab[_t5_bucket(dil * jnp.arange(ATT_BLK + 1))].astype(F32).T


def _band_table(sb):
    return jnp.concatenate([sb[:, ::-1], jnp.full((HEADS, ATT_BLK - 1), NEG_INF, F32)], axis=1)


def _cache_table(sb, dil):
    on_grid = sb[:, :0:-1]
    if dil == 1:
        return on_grid
    off = jnp.full((HEADS, ATT_BLK, dil - 1), NEG_INF, F32)
    return jnp.concatenate([on_grid[:, :, None], off], axis=2).reshape(HEADS, ATT_BLK * dil)


def kernel(x_prompt, x_sample, cache_kv_w128, cache_kv_w512, cache_kv_w2048, state_pool, c_prompt,
           c_sample, w_ada, b_ada, norm_mix_g, norm_ffn_g, w_in, q_norm_g, k_norm_g, rel_bias,
           w_pool_mix, pool_scale, w_up_attn, w_up_pool, w_out, w_router, b_router, w_gate_up,
           b_gate_up, w_down, b_down):
    assert w_ada.shape[0] == 1, "one layer"
    seq = x_prompt.shape[1]
    n_s = x_sample.shape[0]
    assert x_prompt.shape[0] == 1 and x_sample.shape[1] == 1
    assert seq % (DIL_GROUPS[-1][1] * ATT_BLK * ATT_SUB) == 0 and seq % TM_PROMPT == 0
    assert n_s == TM_SAMPLE
    dils = tuple(d for _, d in DIL_GROUPS)
    caches = (cache_kv_w128, cache_kv_w512, cache_kv_w2048)

    w_in_bf = w_in[0].astype(BF16)
    heads_of = jnp.arange(GROUP_W) // HEAD_DIM
    half_heads = heads_of[:GROUP_W // 2]
    bdiag = (half_heads[:, None] == half_heads[None, :]).astype(BF16)
    qg = (jnp.tile(q_norm_g[0], HEADS) * SCALE).reshape(1, GROUP_W)
    kg = jnp.tile(k_norm_g[0], HEADS).reshape(1, GROUP_W)
    expand = ((jnp.arange(LANES)[:, None] % HEADS == heads_of[None, :])
              & (jnp.arange(LANES)[:, None] < 2 * HEADS)).astype(BF16)
    wr = jnp.zeros((D_MODEL, LANES), F32).at[:, :N_EXPERTS].set(w_router[0])
    wr_hi = wr.astype(BF16)
    wr_lo = (wr - wr_hi.astype(F32)).astype(BF16)
    br = jnp.full((1, LANES), NEG_INF, F32).at[0, :N_EXPERTS].set(b_router[0])
    wts = (w_pool_mix[0].astype(BF16), pool_scale[0].reshape(1, POOL_W),
           w_up_attn[0].astype(BF16), w_up_pool[0].astype(BF16), w_out[0].astype(BF16), expand,
           norm_ffn_g[0].reshape(1, D_MODEL), wr_hi, wr_lo, br)
    g1 = norm_mix_g[0].reshape(1, D_MODEL)

    n_c = 1 + n_s
    c_all = jnp.zeros((-(-n_c // 8) * 8, D_MODEL), F32).at[0:1].set(c_prompt).at[1:n_c].set(c_sample)
    mod = _ada(c_all, w_ada[0], b_ada[0])
    sh1, sc1, gt1, sh2, sc2, gt2 = jnp.split(mod, N_ADA, axis=-1)

    def prow(m):
        return m[0:1]

    def srows(m):
        return m[1:n_c]

    xp = x_prompt[0]
    q_p, k_p, v_p, u_p, sga_p, sgp_p, st_p = _proj(
        xp, g1, prow(sc1), prow(sh1), w_in_bf, bdiag, qg, kg,
        tm=TM_PROMPT, dils=dils, per_row=False)
    xs = x_sample[:, 0]
    ones = (1, 1, 1)
    q_s, _, _, u_s, sga_s, sgp_s, st_s = _proj(
        xs, g1, srows(sc1), srows(sh1), w_in_bf, bdiag, qg, kg,
        tm=TM_SAMPLE, dils=ones, per_row=True)
    pooled_s, pool_state_t = _pool_sample(jnp.transpose(state_pool, (0, 2, 1, 3)), u_s)
    pool_state_s = jnp.transpose(pool_state_t, (0, 2, 1, 3))

    def heads(a):
        return a.astype(F32).reshape(n_s, HEADS, HEAD_DIM)

    step_bias = [_step_bias(rel_bias[:, g * HEADS:(g + 1) * HEADS], d)
                 for g, (_, d) in enumerate(DIL_GROUPS)]

    def sample_group(g, part, n_parts):
        cache_t = jnp.transpose(caches[g], (0, 1, 3, 4, 5, 2))
        n_sub = n_s // n_parts
        return (heads(q_s[g]), heads(st_s[g][:, :GROUP_W]), heads(st_s[g][:, GROUP_W:]),
                step_bias[g][:, 0:1], _cache_table(step_bias[g], DIL_GROUPS[g][1]), cache_t,
                part * n_sub, n_sub)

    o_p, lse_p = [], []
    o_parts = [[None] * n for n in SAMPLE_PARTS]
    lse_parts = [[None] * n for n in SAMPLE_PARTS]
    for g, (_, d) in enumerate(DIL_GROUPS):
        hosted = SAMPLE_HOST[g]
        o, lse, sample_outs = _attn(
            q_p[g], k_p[g], v_p[g], _band_table(step_bias[g]), d,
            [sample_group(sg, part, SAMPLE_PARTS[sg]) for sg, part in hosted])
        o_p.append(o)
        lse_p.append(lse)
        for (sg, part), (os_g, lses_g) in zip(hosted, sample_outs):
            o_parts[sg][part] = os_g.reshape(-1, GROUP_W)
            lse_parts[sg][part] = lses_g[:, :, 0]
    o_s = [jnp.concatenate(parts, axis=0) for parts in o_parts]
    lse_s = [jnp.zeros((n_s, LANES), F32).at[:, :HEADS].set(jnp.concatenate(parts, axis=0))
             for parts in lse_parts]

    nt_p = seq // TM_PROMPT
    t_all = seq + TM_PROMPT
    bufs = _post(xp, o_p, lse_p, (u_p,), sga_p, sgp_p, wts, (prow(gt1), prow(sc2), prow(sh2)),
                 tm=TM_PROMPT, dils=dils, per_row=False, rows_total=t_all, row_block0=0,
                 cnt_tiles=nt_p + 1, cnt_block=None, grid=nt_p, n_valid_steps=None,
                 alias_bufs=None)
    bufs = _post(xs, o_s, lse_s, pooled_s, sga_s, sgp_s, wts,
                 (srows(gt1), srows(sc2), srows(sh2)),
                 tm=TM_SAMPLE, dils=ones, per_row=True, rows_total=t_all,
                 row_block0=seq // TM_SAMPLE, cnt_tiles=nt_p + 1, cnt_block=nt_p,
                 grid=TM_PROMPT // TM_SAMPLE, n_valid_steps=1, alias_bufs=bufs)
    x1_all, h2_all, a_all, idx_all, gk_all, cnt = bufs

    meta = _moe_meta(cnt[:, 0, :N_EXPERTS], TM_PROMPT)
    cap = _moe_cap(t_all, TM_PROMPT)
    xb, dst_all = _moe_sort(meta, a_all, idx_all, h2_all, tm=TM_PROMPT, cap=cap)
    yb = _moe_ffn(meta, xb, w_gate_up[0], b_gate_up[0], w_down[0], b_down[0])
    gt2_s = jnp.zeros((TM_PROMPT, D_MODEL), F32).at[:n_s].set(srows(gt2))
    y_p, y_s = _moe_unsort(meta, dst_all, gk_all, x1_all, prow(gt2), gt2_s, yb,
                           tm=TM_PROMPT, n_prompt_tiles=nt_p)

    def kv_state(st, rows):
        return st.reshape(1, 1, rows, 2, HEADS, HEAD_DIM)

    kv_p = [kv_state(st, st.shape[0]) for st in st_p]
    kv_s = [st.reshape(1, n_s, 1, 2, HEADS, HEAD_DIM) for st in st_s]
    pool_p = u_p[seq - POOL_BUF:].reshape(1, 1, POOL_BUF, POOL_W)
    return (y_p.reshape(1, seq, D_MODEL), y_s[:n_s].reshape(n_s, 1, D_MODEL),
            kv_p[0], kv_p[1], kv_p[2], pool_p, kv_s[0], kv_s[1], kv_s[2], pool_state_s)
```

```python
import functools
import math

import jax
import jax.numpy as jnp
from jax import lax
from jax.experimental import pallas as pl
from jax.experimental.pallas import tpu as pltpu

F32 = jnp.float32
BF16 = jnp.bfloat16

D_MODEL = 1024
HEAD_DIM = 64
HEADS = 8
GROUP_W = HEADS * HEAD_DIM
DIL_GROUPS = ((128, 1), (512, 4), (2048, 16))
N_GROUPS = len(DIL_GROUPS)
QKV_W = N_GROUPS * GROUP_W
ATT_BLK = 128
ATT_SUB = 4
POOL_WINDOWS = (2, 4, 8, 16)
POOL_W = 512
POOL_GW = 128
POOL_BUF = 15
OFF_K, OFF_V = QKV_W, 2 * QKV_W
OFF_U = 3 * QKV_W
OFF_GA = OFF_U + POOL_W
OFF_GP = OFF_GA + D_MODEL
IN_W = OFF_GP + D_MODEL
NUM_BUCKETS = 32
MAX_DISTANCE = 2048
N_EXPERTS = 32
TOP_K = 4
SWIGLU_LIMIT = 7.0
SWIGLU_ALPHA = 1.702
N_ADA = 6
EPS = 1e-6
NEG_INF = -1e30
PAST_LEN = 8192
SCALE = HEAD_DIM ** -0.5

LANES = 128
ROW_CHUNK = 16
TM_PROMPT = 512
TM_SAMPLE = 128
FFN_BLOCK = 1024
FFN_SUB = 128
SEL_CHUNK = 512
BF16_EXACT = 256
VMEM_LIMIT = 56 * 1024 * 1024


def _cparams(n_axes):
    return pltpu.CompilerParams(dimension_semantics=("arbitrary",) * n_axes,
                                vmem_limit_bytes=VMEM_LIMIT)


def _const_spec(shape):
    nd = len(shape)
    return pl.BlockSpec(shape, lambda *_: (0,) * nd)


def _ada_kernel(c_ref, w_ref, b_ref, o_ref):
    c = c_ref[...]
    s = c * jax.nn.sigmoid(c)
    o_ref[...] = jnp.dot(s.astype(BF16), w_ref[...].astype(BF16),
                         preferred_element_type=F32) + b_ref[...]


def _ada(c_all, w_ada, b_ada):
    rows = c_all.shape[0]
    n = w_ada.shape[1]
    tn = 1536
    return pl.pallas_call(
        _ada_kernel,
        grid=(n // tn,),
        in_specs=[pl.BlockSpec((rows, D_MODEL), lambda j: (0, 0)),
                  pl.BlockSpec((D_MODEL, tn), lambda j: (0, j)),
                  pl.BlockSpec((1, tn), lambda j: (0, j))],
        out_specs=pl.BlockSpec((rows, tn), lambda j: (0, j)),
        out_shape=jax.ShapeDtypeStruct((rows, n), F32),
        compiler_params=_cparams(1),
        name="ada",
    )(c_all, w_ada, b_ada.reshape(1, n))


def _proj_kernel(x_ref, g_ref, sc_ref, sh_ref, w_ref, bd_ref, qg_ref, kg_ref,
                 *refs, tm, dils, st_rows):
    q_refs, k_refs, v_refs = refs[0:3], refs[3:6], refs[6:9]
    u_ref, sga_ref, sgp_ref = refs[9:12]
    st_refs = refs[12:15]
    scr = refs[15]

    x = x_ref[...]
    ms = jnp.mean(x * x, axis=-1, keepdims=True)
    h = x * lax.rsqrt(ms + EPS) * g_ref[...] * (1.0 + sc_ref[...]) + sh_ref[...]
    hb = h.astype(BF16)

    def proj(off, width):
        return jnp.dot(hb, w_ref[:, off:off + width], preferred_element_type=F32)

    def head_norm(z, gain_ref):
        zz = (z * z).astype(BF16)
        half = GROUP_W // 2
        ss = jnp.concatenate(
            [jnp.dot(zz[:, :half], bd_ref[...], preferred_element_type=F32),
             jnp.dot(zz[:, half:], bd_ref[...], preferred_element_type=F32)], axis=1)
        return z * lax.rsqrt(ss * (1.0 / HEAD_DIM) + EPS) * gain_ref[...]

    def put(out_ref, val, d):
        if d == 1:
            out_ref[...] = val.astype(out_ref.dtype)
        else:
            for c in range(GROUP_W // LANES):
                scr[c] = val[:, c * LANES:(c + 1) * LANES]
            for r in range(d):
                for c in range(GROUP_W // LANES):
                    col = r * GROUP_W + c * LANES
                    out_ref[:, col:col + LANES] = (
                        scr[c, pl.ds(r, tm // d, stride=d), :].astype(out_ref.dtype))

    for g, d in enumerate(dils):
        qn = head_norm(proj(g * GROUP_W, GROUP_W), qg_ref)
        put(q_refs[g], qn, d)
        kn = head_norm(proj(OFF_K + g * GROUP_W, GROUP_W), kg_ref)
        put(k_refs[g], kn, d)
        v = proj(OFF_V + g * GROUP_W, GROUP_W)
        put(v_refs[g], v, d)
        rb = st_rows[g]
        st_refs[g][:, 0:GROUP_W] = kn[tm - rb:, :]
        st_refs[g][:, GROUP_W:2 * GROUP_W] = v[tm - rb:, :]

    u_ref[...] = proj(OFF_U, POOL_W)
    sga_ref[...] = jax.nn.sigmoid(proj(OFF_GA, D_MODEL)).astype(BF16)
    sgp_ref[...] = jax.nn.sigmoid(proj(OFF_GP, D_MODEL)).astype(BF16)


def _mod_spec(per_row, tm):
    if per_row:
        return pl.BlockSpec((tm, D_MODEL), lambda i: (i, 0))
    return pl.BlockSpec((1, D_MODEL), lambda i: (0, 0))


def _proj(x, g1, sc1, sh1, w_in_bf, bdiag, qg, kg, *, tm, dils, per_row):
    s = x.shape[0]
    nt = s // tm
    wins = tuple(min(w, s) for w, _ in DIL_GROUPS)
    st_rows = tuple(min(tm, w) for w in wins)

    def res_spec(d):
        return pl.BlockSpec((tm // d, d * GROUP_W), lambda i: (i, 0))

    def st_spec(w, rb):
        first = nt - w // rb
        return pl.BlockSpec((rb, 2 * GROUP_W), lambda i: (jnp.maximum(i - first, 0), 0))

    qkv_shapes = [jax.ShapeDtypeStruct((s // d, d * GROUP_W), BF16) for d in dils]
    out_shape = (qkv_shapes * 3
                 + [jax.ShapeDtypeStruct((s, POOL_W), F32),
                    jax.ShapeDtypeStruct((s, D_MODEL), BF16),
                    jax.ShapeDtypeStruct((s, D_MODEL), BF16)]
                 + [jax.ShapeDtypeStruct((w, 2 * GROUP_W), F32) for w in wins])
    out_specs = ([res_spec(d) for d in dils] * 3
                 + [pl.BlockSpec((tm, POOL_W), lambda i: (i, 0)),
                    pl.BlockSpec((tm, D_MODEL), lambda i: (i, 0)),
                    pl.BlockSpec((tm, D_MODEL), lambda i: (i, 0))]
                 + [st_spec(w, rb) for w, rb in zip(wins, st_rows)])
    in_specs = [pl.BlockSpec((tm, D_MODEL), lambda i: (i, 0)),
                _const_spec((1, D_MODEL)),
                _mod_spec(per_row, tm), _mod_spec(per_row, tm),
                pl.BlockSpec((D_MODEL, IN_W), lambda i: (0, 0), pipeline_mode=pl.Buffered(1)),
                _const_spec((GROUP_W // 2, GROUP_W // 2)),
                _const_spec((1, GROUP_W)), _const_spec((1, GROUP_W))]
    outs = pl.pallas_call(
        functools.partial(_proj_kernel, tm=tm, dils=dils, st_rows=st_rows),
        grid=(nt,),
        in_specs=in_specs,
        out_specs=out_specs,
        out_shape=out_shape,
        scratch_shapes=[pltpu.VMEM((GROUP_W // LANES, tm, LANES), F32)],
        compiler_params=_cparams(1),
        name="proj",
    )(x, g1, sc1, sh1, w_in_bf, bdiag, qg, kg)
    return outs[0:3], outs[3:6], outs[6:9], outs[9], outs[10], outs[11], outs[12:15]


def _sample_group_attn(qs_ref, kns_ref, vns_ref, bself_ref, btab_ref, c_ref, os_ref, lses_ref):
    n_tok = qs_ref.shape[0]
    win = btab_ref.shape[1]
    row_w = lax.broadcasted_iota(jnp.int32, (HEADS, win), 0)
    row_e = lax.broadcasted_iota(jnp.int32, (HEADS, HEAD_DIM), 0)
    ss, s0s = [], []
    for t in range(n_tok):
        q = qs_ref[t]
        qb = q.astype(BF16)
        s = jnp.zeros((HEADS, win), F32)
        for h in range(HEADS):
            sh = jnp.dot(qb, c_ref[0, t, 0, h].astype(BF16), preferred_element_type=F32)
            s = jnp.where(row_w == h, sh, s)
        ss.append(s + btab_ref[...])
        s0s.append(jnp.sum(q * kns_ref[t], axis=-1, keepdims=True) + bself_ref[...])
    s = jnp.concatenate(ss, axis=0)
    s0 = jnp.concatenate(s0s, axis=0)
    m = jnp.maximum(jnp.max(s, axis=-1, keepdims=True), s0)
    p = jnp.exp(s - m)
    p0 = jnp.exp(s0 - m)
    l = jnp.sum(p, axis=-1, keepdims=True) + p0
    pb = p.astype(BF16)
    lse = m + jnp.log(l)
    for t in range(n_tok):
        rows = slice(t * HEADS, (t + 1) * HEADS)
        o = jnp.zeros((HEADS, HEAD_DIM), F32)
        for h in range(HEADS):
            oh = lax.dot_general(pb[rows], c_ref[0, t, 1, h].astype(BF16),
                                 (((1,), (1,)), ((), ())), preferred_element_type=F32)
            o = jnp.where(row_e == h, oh, o)
        os_ref[t] = (o + p0[rows] * vns_ref[t]) / l[rows]
        lses_ref[t] = lse[rows]


SAMPLE_IN = 6
SAMPLE_PARTS = (1, 1, 2)
SAMPLE_HOST = (((2, 1),), ((1, 0),), ((2, 0), (0, 0)))


def _attn_kernel(q_ref, kp_ref, kc_ref, vp_ref, vc_ref, r_ref, *refs, n_sample_groups):
    n_in = SAMPLE_IN * n_sample_groups
    o_ref, lse_ref = refs[n_in], refs[n_in + 1]
    bias_ref = refs[-1]
    i = pl.program_id(1)

    for sg in range(n_sample_groups):
        _sample_group_attn(*refs[SAMPLE_IN * sg:SAMPLE_IN * (sg + 1)],
                           *refs[n_in + 2 + 2 * sg:n_in + 4 + 2 * sg])

    @pl.when((pl.program_id(0) == 0) & (i == 0))
    def _():
        for h in range(HEADS):
            row = jnp.broadcast_to(r_ref[h:h + 1, :], (ATT_BLK, 2 * ATT_BLK))
            bias_ref[h] = pltpu.roll(row, 0, 1, stride=1, stride_axis=0)

    q = q_ref[...]
    k = jnp.concatenate([kp_ref[...], kc_ref[...]], axis=0)
    v = jnp.concatenate([vp_ref[...], vc_ref[...]], axis=0)
    col = lax.broadcasted_iota(jnp.int32, (ATT_BLK, 2 * ATT_BLK), 1)
    no_prev = jnp.where((col < ATT_BLK) & (i == 0), NEG_INF, 0.0)
    lane_q = lax.broadcasted_iota(jnp.int32, (ATT_SUB * ATT_BLK, LANES), 1)
    lane_v = lax.broadcasted_iota(jnp.int32, ((ATT_SUB + 1) * ATT_BLK, LANES), 1)

    def pair(h):
        return slice((h // 2) * LANES, (h // 2 + 1) * LANES)

    def mine(lane, h):
        return (lane < HEAD_DIM) == (h % 2 == 0)

    ss = []
    for h in range(HEADS):
        q2 = q[:, pair(h)]
        qm = jnp.where(mine(lane_q, h), q2, jnp.zeros_like(q2))
        k2 = k[:, pair(h)]
        for j in range(ATT_SUB):
            s = lax.dot_general(qm[j * ATT_BLK:(j + 1) * ATT_BLK], k2[j * ATT_BLK:(j + 2) * ATT_BLK],
                                (((1,), (1,)), ((), ())), preferred_element_type=F32)
            s = s + bias_ref[h]
            ss.append(s + no_prev if j == 0 else s)
    s = jnp.concatenate(ss, axis=0)
    m = jnp.max(s, axis=-1, keepdims=True)
    p = jnp.exp(s - m)
    l = jnp.sum(p, axis=-1, keepdims=True)
    pb = p.astype(BF16)
    lse = m + jnp.log(l)
    inv_l = 1.0 / l
    outs, lses = [], []
    for h in range(HEADS):
        v2 = v[:, pair(h)]
        vm = jnp.where(mine(lane_v, h), v2, jnp.zeros_like(v2))
        o_sub, lse_sub = [], []
        for j in range(ATT_SUB):
            rows = slice((h * ATT_SUB + j) * ATT_BLK, (h * ATT_SUB + j + 1) * ATT_BLK)
            o_sub.append(jnp.dot(pb[rows], vm[j * ATT_BLK:(j + 2) * ATT_BLK],
                                 preferred_element_type=F32) * inv_l[rows])
            lse_sub.append(lse[rows])
        o = jnp.concatenate(o_sub, axis=0)
        if h % 2 == 0:
            outs.append(o)
        else:
            outs[-1] = outs[-1] + o
        lses.append(jnp.concatenate(lse_sub, axis=0))
    o_ref[...] = jnp.concatenate(outs, axis=-1).astype(o_ref.dtype)
    lse_ref[...] = jnp.concatenate(
        lses + [jnp.zeros((ATT_SUB * ATT_BLK, LANES - HEADS), F32)], axis=-1)


def _attn(q, k, v, r_tab, d, sample_groups):
    rows = q.shape[0]
    step = ATT_SUB * ATT_BLK
    nblk = rows // step
    cur = pl.BlockSpec((step, GROUP_W), lambda r, i: (i, r))
    prev = pl.BlockSpec((ATT_BLK, GROUP_W), lambda r, i: (jnp.maximum(i * ATT_SUB - 1, 0), r))
    in_specs = [cur, prev, cur, prev, cur,
                pl.BlockSpec((HEADS, 2 * ATT_BLK), lambda r, i: (0, 0))]
    out_specs = [pl.BlockSpec((step, GROUP_W), lambda r, i: (i, r)),
                 pl.BlockSpec((step, LANES), lambda r, i: (i, r))]
    out_shape = [jax.ShapeDtypeStruct((rows, d * GROUP_W), BF16),
                 jax.ShapeDtypeStruct((rows, d * LANES), F32)]
    args = [q, k, k, v, v, r_tab]
    for qs, kns, vns, bself, btab, cache_t, tok_start, n_tok in sample_groups:
        assert n_tok % (d * nblk) == 0
        tok = n_tok // (d * nblk)
        assert tok_start % tok == 0
        first = tok_start // tok

        def in_idx(r, i, first=first):
            return (first + r * nblk + i, 0, 0)

        def out_idx(r, i):
            return (r * nblk + i, 0, 0)

        tok_in = pl.BlockSpec((tok, HEADS, HEAD_DIM), in_idx)
        in_specs += [tok_in, tok_in, tok_in,
                     pl.BlockSpec(bself.shape, lambda r, i: (0, 0)),
                     pl.BlockSpec(btab.shape, lambda r, i: (0, 0)),
                     pl.BlockSpec((1, tok) + cache_t.shape[2:],
                                  lambda r, i, first=first: (0, first + r * nblk + i, 0, 0, 0, 0))]
        out_specs += [pl.BlockSpec((tok, HEADS, HEAD_DIM), out_idx),
                      pl.BlockSpec((tok, HEADS, 1), out_idx)]
        out_shape += [jax.ShapeDtypeStruct((n_tok, HEADS, HEAD_DIM), F32),
                      jax.ShapeDtypeStruct((n_tok, HEADS, 1), F32)]
        args += [qs, kns, vns, bself, btab, cache_t]
    outs = pl.pallas_call(
        functools.partial(_attn_kernel, n_sample_groups=len(sample_groups)),
        grid=(d, nblk),
        in_specs=in_specs,
        out_specs=out_specs,
        out_shape=out_shape,
        scratch_shapes=[pltpu.VMEM((HEADS, ATT_BLK, 2 * ATT_BLK), F32)],
        compiler_params=_cparams(2),
        name=f"attn_d{d}",
    )(*args)
    return outs[0], outs[1], [(outs[2 + 2 * j], outs[3 + 2 * j]) for j in range(len(sample_groups))]


def _pool_sample_kernel(st_ref, u_ref, pooled_ref, new_ref):
    u = u_ref[...]
    rows = [st_ref[0, j] for j in range(POOL_BUF)]
    outs = []
    for g, w in enumerate(POOL_WINDOWS):
        sl = slice(g * POOL_GW, (g + 1) * POOL_GW)
        acc = u[:, sl]
        for j in range(POOL_BUF - (w - 1), POOL_BUF):
            acc = acc + rows[j][:, sl]
        outs.append(acc / float(w) - u[:, sl])
    pooled_ref[...] = jnp.concatenate(outs, axis=-1)
    for j in range(POOL_BUF - 1):
        new_ref[0, j] = rows[j + 1]
    new_ref[0, POOL_BUF - 1] = u


def _pool_sample(state, u):
    n = u.shape[0]
    return pl.pallas_call(
        _pool_sample_kernel,
        grid=(1,),
        in_specs=[_const_spec(state.shape), _const_spec(u.shape)],
        out_specs=[_const_spec(u.shape), _const_spec(state.shape)],
        out_shape=[jax.ShapeDtypeStruct((n, POOL_W), F32),
                   jax.ShapeDtypeStruct(state.shape, F32)],
        compiler_params=_cparams(1),
        name="pool_sample",
    )(state, u)


def _post_kernel(*refs, tm, dils, pooled_given, n_valid_steps, aliased):
    it = iter(refs)
    x_ref = next(it)
    o_refs = [next(it) for _ in range(N_GROUPS)]
    lse_refs = [next(it) for _ in range(N_GROUPS)]
    if pooled_given:
        pooled_ref = next(it)
    else:
        u_ref, uh_ref = next(it), next(it)
    sga_ref, sgp_ref = next(it), next(it)
    wpm_ref, psc_ref, wua_ref, wup_ref, wout_ref, exp_ref = (next(it) for _ in range(6))
    gt1_ref, g2_ref, sc2_ref, sh2_ref = (next(it) for _ in range(4))
    wrh_ref, wrl_ref, br_ref = (next(it) for _ in range(3))
    if aliased:
        for _ in range(6):
            next(it)
    x1_ref, h2_ref, a_ref, idx_ref, gk_ref, cnt_ref = (next(it) for _ in range(6))
    ob_scr, ls_scr = next(it), next(it)

    i = pl.program_id(0)

    def compute():
        obs, lss = [], []
        for g, d in enumerate(dils):
            if d == 1:
                obs.append(o_refs[g][...].astype(F32))
                lss.append(lse_refs[g][...])
            else:
                for r in range(d):
                    for c in range(GROUP_W // LANES):
                        col = r * GROUP_W + c * LANES
                        ob_scr[c, pl.ds(r, tm // d, stride=d), :] = (
                            o_refs[g][:, col:col + LANES].astype(F32))
                    ls_scr[pl.ds(r, tm // d, stride=d), :] = (
                        lse_refs[g][:, r * LANES:(r + 1) * LANES])
                obs.append(jnp.concatenate([ob_scr[c] for c in range(GROUP_W // LANES)],
                                           axis=-1))
                lss.append(ls_scr[...])
        mx = jnp.maximum(jnp.maximum(lss[0], lss[1]), lss[2])
        es = [jnp.exp(l - mx) for l in lss]
        den = es[0] + es[1] + es[2]
        attn_o = jnp.zeros((tm, GROUP_W), F32)
        head_lane = lax.broadcasted_iota(jnp.int32, (tm, LANES), 1) < HEADS
        for g in range(N_GROUPS):
            w = jnp.where(head_lane, es[g] / den, 0.0)
            w_hi = w.astype(BF16).astype(F32)
            w_lo = (w - w_hi).astype(BF16).astype(F32)
            lhs = (w_hi + pltpu.roll(w_lo, HEADS, 1)).astype(BF16)
            wexp = jnp.dot(lhs, exp_ref[...], preferred_element_type=F32)
            attn_o = attn_o + wexp * obs[g]

        if pooled_given:
            pooled = pooled_ref[...]
        else:
            u = u_ref[...]
            halo = jnp.where(i == 0, 0.0, uh_ref[...])
            pos = (lax.broadcasted_iota(jnp.int32, (tm, 1), 0) + i * tm + 1).astype(F32)
            outs = []
            for g, w in enumerate(POOL_WINDOWS):
                sl = slice(g * POOL_GW, (g + 1) * POOL_GW)
                a = jnp.concatenate([halo[:, sl], u[:, sl]], axis=0)
                span = 1
                while span < w:
                    n = a.shape[0] - span
                    a = a[span:, :] + a[:n, :]
                    span *= 2
                off = a.shape[0] - tm
                win_sum = a[off:, :]
                outs.append(win_sum / jnp.minimum(pos, float(w)) - u[:, sl])
            pooled = jnp.concatenate(outs, axis=-1)
        pool_parts = []
        for g in range(len(POOL_WINDOWS)):
            sl = slice(g * POOL_GW, (g + 1) * POOL_GW)
            pool_parts.append(jnp.dot(pooled[:, sl].astype(BF16), wpm_ref[g],
                                      preferred_element_type=F32))
        pool_o = jnp.concatenate(pool_parts, axis=-1) * psc_ref[...]

        up_a = jnp.dot(attn_o.astype(BF16), wua_ref[...], preferred_element_type=F32)
        up_p = jnp.dot(pool_o.astype(BF16), wup_ref[...], preferred_element_type=F32)
        merged = sga_ref[...].astype(F32) * up_a + sgp_ref[...].astype(F32) * up_p
        mo = jnp.dot(merged.astype(BF16), wout_ref[...], preferred_element_type=F32)
        x1 = x_ref[...] + gt1_ref[...] * mo
        x1_ref[...] = x1

        ms = jnp.mean(x1 * x1, axis=-1, keepdims=True)
        h2 = x1 * lax.rsqrt(ms + EPS) * g2_ref[...] * (1.0 + sc2_ref[...]) + sh2_ref[...]
        h2_hi = h2.astype(BF16)
        h2_ref[...] = h2_hi
        h2_lo = (h2 - h2_hi.astype(F32)).astype(BF16)
        logits = (jnp.dot(h2_hi, wrh_ref[...], preferred_element_type=F32)
                  + jnp.dot(h2_lo, wrh_ref[...], preferred_element_type=F32)
                  + jnp.dot(h2_hi, wrl_ref[...], preferred_element_type=F32)
                  + br_ref[...])
        lane = lax.broadcasted_iota(jnp.int32, (tm, LANES), 1).astype(F32)
        work = logits
        vals, ids = [], []
        for _ in range(TOP_K):
            m = jnp.max(work, axis=-1, keepdims=True)
            ik = jnp.min(jnp.where(work == m, lane, float(LANES)), axis=-1, keepdims=True)
            vals.append(m)
            ids.append(ik)
            work = jnp.where(lane == ik, -3e38, work)
        ex = [jnp.exp(v - vals[0]) for v in vals]
        den_k = ex[0] + ex[1] + ex[2] + ex[3]
        a = jnp.zeros((tm, LANES), F32)
        idx = jnp.zeros((tm, LANES), F32)
        gk = jnp.zeros((tm, LANES), F32)
        for kk in range(TOP_K):
            gate = ex[kk] / den_k
            a = a + jnp.where(lane == ids[kk], gate, 0.0)
            idx = jnp.where(lane == float(kk), ids[kk], idx)
            gk = jnp.where(lane == float(kk), gate, gk)
        a_ref[...] = a
        idx_ref[...] = idx
        gk_ref[...] = gk
        cnt = jnp.sum((a > 0.0).astype(F32), axis=0, keepdims=True)
        row = lax.broadcasted_iota(jnp.int32, (8, LANES), 0)
        cnt_ref[0] = jnp.where(row == 0, jnp.broadcast_to(cnt, (8, LANES)), 0.0)

    if n_valid_steps is None:
        compute()
    else:
        pl.when(i < n_valid_steps)(compute)

        @pl.when(i >= n_valid_steps)
        def _():
            x1_ref[...] = jnp.zeros(x1_ref.shape, x1_ref.dtype)
            h2_ref[...] = jnp.zeros(h2_ref.shape, h2_ref.dtype)
            a_ref[...] = jnp.zeros(a_ref.shape, a_ref.dtype)
            idx_ref[...] = jnp.zeros(idx_ref.shape, idx_ref.dtype)
            gk_ref[...] = jnp.zeros(gk_ref.shape, gk_ref.dtype)


def _post(x, o_list, lse_list, pool_in, sga, sgp, wts, mods, *, tm, dils, per_row,
          rows_total, row_block0, cnt_tiles, cnt_block, grid, n_valid_steps, alias_bufs):
    pooled_given = not isinstance(pool_in, tuple)
    nv = grid if n_valid_steps is None else n_valid_steps

    def clamp(i):
        return jnp.minimum(i, nv - 1)

    def tile_spec(width):
        return pl.BlockSpec((tm, width), lambda i: (clamp(i), 0))

    in_specs = [tile_spec(D_MODEL)]
    in_specs += [pl.BlockSpec((tm // d, d * GROUP_W), lambda i: (clamp(i), 0)) for d in dils]
    in_specs += [pl.BlockSpec((tm // d, d * LANES), lambda i: (clamp(i), 0)) for d in dils]
    args = [x, *o_list, *lse_list]
    if pooled_given:
        in_specs.append(tile_spec(POOL_W))
        args.append(pool_in)
    else:
        u = pool_in[0]
        in_specs += [tile_spec(POOL_W),
                     pl.BlockSpec((16, POOL_W),
                                  lambda i: (jnp.maximum(i * (tm // 16) - 1, 0), 0))]
        args += [u, u]
    in_specs += [tile_spec(D_MODEL), tile_spec(D_MODEL)]
    args += [sga, sgp]
    wpm, psc, wua, wup, wout, expand, g2, wrh, wrl, br = wts
    gt1, sc2, sh2 = mods

    def mspec():
        if per_row:
            return pl.BlockSpec((tm, D_MODEL), lambda i: (clamp(i), 0))
        return _const_spec((1, D_MODEL))

    in_specs += [_const_spec(wpm.shape), _const_spec(psc.shape), _const_spec(wua.shape),
                 _const_spec(wup.shape), _const_spec(wout.shape), _const_spec(expand.shape),
                 mspec(), _const_spec(g2.shape), mspec(), mspec(),
                 _const_spec(wrh.shape), _const_spec(wrl.shape), _const_spec(br.shape)]
    args += [wpm, psc, wua, wup, wout, expand, gt1, g2, sc2, sh2, wrh, wrl, br]
    aliases = {}
    if alias_bufs is not None:
        base = len(args)
        in_specs += [pl.BlockSpec(memory_space=pl.ANY)] * 6
        args += list(alias_bufs)
        aliases = {base + j: j for j in range(6)}

    def out_spec(width):
        return pl.BlockSpec((tm, width), lambda i: (row_block0 + i, 0))

    out_specs = [out_spec(D_MODEL), out_spec(D_MODEL), out_spec(LANES), out_spec(LANES),
                 out_spec(LANES),
                 pl.BlockSpec((1, 8, LANES),
                              lambda i: (cnt_block if cnt_block is not None else i, 0, 0))]
    out_shape = [jax.ShapeDtypeStruct((rows_total, D_MODEL), F32),
                 jax.ShapeDtypeStruct((rows_total, D_MODEL), BF16),
                 jax.ShapeDtypeStruct((rows_total, LANES), F32),
                 jax.ShapeDtypeStruct((rows_total, LANES), F32),
                 jax.ShapeDtypeStruct((rows_total, LANES), F32),
                 jax.ShapeDtypeStruct((cnt_tiles, 8, LANES), F32)]
    return pl.pallas_call(
        functools.partial(_post_kernel, tm=tm, dils=dils, pooled_given=pooled_given,
                          n_valid_steps=n_valid_steps, aliased=alias_bufs is not None),
        grid=(grid,),
        in_specs=in_specs,
        out_specs=out_specs,
        out_shape=out_shape,
        scratch_shapes=[pltpu.VMEM((GROUP_W // LANES, tm, LANES), F32),
                        pltpu.VMEM((tm, LANES), F32)],
        input_output_aliases=aliases,
        compiler_params=_cparams(1),
        name="post_sample" if per_row else "post",
    )(*args)


def _sort_rows(tm):
    return -(-(TOP_K * tm + N_EXPERTS * (ROW_CHUNK - 1)) // SEL_CHUNK) * SEL_CHUNK


def _for_row_pieces(n_chunks, max_pow, fn):
    big = 1 << max_pow

    def body(c, carry):
        fn(c * big, big)
        return carry

    lax.fori_loop(0, n_chunks >> max_pow, body, 0)
    for pw in range(max_pow - 1, -1, -1):
        @pl.when(((n_chunks >> pw) & 1) == 1)
        def _(pw=pw):
            fn((n_chunks >> (pw + 1)) << (pw + 1), 1 << pw)


SEG_MAX_POW = 3
TILE_MAX_POW = 5


def _moe_sort_kernel(seg_s, goff_s, nch_s, ntot_s, tstart_s, tnch_s,
                     a_ref, idx_ref, h2_ref, segv_ref, lt_ref,
                     xb_hbm, dst_ref, xs_scr, zero_scr, sem, *, tm, n_rows):
    i = pl.program_id(0)
    nt = pl.num_programs(0)
    slot = i % 2
    sel = a_ref[...] > 0.0
    ahead = jnp.dot(lt_ref[...], sel.astype(BF16), preferred_element_type=F32)
    slot1 = jnp.where(sel, segv_ref[0] + ahead + 1.0, 0.0)
    lane = lax.broadcasted_iota(jnp.int32, (tm, LANES), 1).astype(F32)
    idx = idx_ref[...]
    dst = jnp.full((tm, LANES), -1.0, F32)
    for kk in range(TOP_K):
        hit = lane == idx[:, kk:kk + 1]
        dk = jnp.sum(jnp.where(hit, slot1, 0.0), axis=-1, keepdims=True) - 1.0
        dst = jnp.where(lane == float(kk), dk, dst)
    dst_ref[...] = dst
    dst_t = dst.T
    h2 = h2_ref[...]
    local_rows = lax.broadcasted_iota(jnp.int32, (BF16_EXACT, tm), 0).astype(F32).astype(BF16)
    one, zero = jnp.ones((), BF16), jnp.zeros((), BF16)
    for c in range(n_rows // BF16_EXACT):
        p = None
        for kk in range(TOP_K):
            hit = local_rows == (dst_t[kk:kk + 1, :] - float(c * BF16_EXACT)).astype(BF16)
            p = hit if p is None else p | hit
        xs = jnp.dot(jnp.where(p, one, zero), h2, preferred_element_type=F32)
        xs_scr[slot, c * BF16_EXACT:(c + 1) * BF16_EXACT, :] = xs.astype(BF16)

    def rows_copy(buf, src_row, dst_row, n_chunks):
        return pltpu.make_async_copy(
            xs_scr.at[buf, pl.ds(pl.multiple_of(src_row, ROW_CHUNK), n_chunks * ROW_CHUNK)],
            xb_hbm.at[pl.ds(pl.multiple_of(dst_row, ROW_CHUNK), n_chunks * ROW_CHUNK)],
            sem.at[buf])

    def per_expert(e, carry):
        so = seg_s[i * N_EXPERTS + e]
        go = goff_s[i * N_EXPERTS + e]
        _for_row_pieces(
            nch_s[i * N_EXPERTS + e], SEG_MAX_POW,
            lambda off, n: rows_copy(slot, so + off * ROW_CHUNK, go + off * ROW_CHUNK, n).start())
        return carry

    lax.fori_loop(0, N_EXPERTS, per_expert, 0)

    def drain(buf, tile):
        _for_row_pieces(ntot_s[tile], TILE_MAX_POW, lambda off, n: rows_copy(buf, 0, 0, n).wait())

    @pl.when(i > 0)
    def _():
        drain(1 - slot, i - 1)

    @pl.when(i == nt - 1)
    def _():
        drain(slot, i)
        zero_scr[...] = jnp.zeros(zero_scr.shape, zero_scr.dtype)

        def tail_copy(dst_row, n_chunks):
            return pltpu.make_async_copy(
                zero_scr.at[pl.ds(0, n_chunks * ROW_CHUNK)],
                xb_hbm.at[pl.ds(pl.multiple_of(dst_row, ROW_CHUNK), n_chunks * ROW_CHUNK)],
                sem.at[2])

        def per_expert_tail(e, carry):
            _for_row_pieces(tnch_s[e], SEG_MAX_POW,
                            lambda off, n: tail_copy(tstart_s[e] + off * ROW_CHUNK, n).start())
            _for_row_pieces(tnch_s[e], SEG_MAX_POW, lambda off, n: tail_copy(0, n).wait())
            return carry

        lax.fori_loop(0, N_EXPERTS, per_expert_tail, 0)


def _moe_sort(meta, a_all, idx_all, h2_all, *, tm, cap):
    t_all = a_all.shape[0]
    nt = t_all // tm
    n_rows = _sort_rows(tm)
    lt = jnp.tril(jnp.ones((tm, tm), BF16), -1)
    grid_spec = pltpu.PrefetchScalarGridSpec(
        num_scalar_prefetch=6,
        grid=(nt,),
        in_specs=[pl.BlockSpec((tm, LANES), lambda i, *_: (i, 0)),
                  pl.BlockSpec((tm, LANES), lambda i, *_: (i, 0)),
                  pl.BlockSpec((tm, D_MODEL), lambda i, *_: (i, 0)),
                  pl.BlockSpec((1, 1, LANES), lambda i, *_: (i, 0, 0)),
                  pl.BlockSpec((tm, tm), lambda i, *_: (0, 0))],
        out_specs=[pl.BlockSpec(memory_space=pl.ANY),
                   pl.BlockSpec((tm, LANES), lambda i, *_: (i, 0))],
        scratch_shapes=[pltpu.VMEM((2, n_rows, D_MODEL), BF16),
                        pltpu.VMEM(((1 << SEG_MAX_POW) * ROW_CHUNK, D_MODEL), BF16),
                        pltpu.SemaphoreType.DMA((3,))],
    )
    return pl.pallas_call(
        functools.partial(_moe_sort_kernel, tm=tm, n_rows=n_rows),
        grid_spec=grid_spec,
        out_shape=[jax.ShapeDtypeStruct((cap, D_MODEL), BF16),
                   jax.ShapeDtypeStruct((t_all, LANES), F32)],
        compiler_params=_cparams(1),
        name="moe_sort",
    )(meta["seg"], meta["goff"], meta["nch"], meta["ntot"], meta["tstart"], meta["tnch"],
      a_all, idx_all, h2_all, meta["segv"], lt)


def _moe_ffn_kernel(be_s, nused_s, ord_s, next_s, rows_s,
                    x_ref, wgu_hbm, bgu_ref, wd_hbm, bd_ref, y_ref,
                    wgu_f32, wd_f32, wgu_bf, wd_bf, sem):
    b = pl.program_id(0)

    def weight_copies(e, slot):
        return (pltpu.make_async_copy(wgu_hbm.at[e], wgu_f32.at[slot], sem.at[0, slot]),
                pltpu.make_async_copy(wd_hbm.at[e], wd_f32.at[slot], sem.at[1, slot]))

    @pl.when(b < nused_s[0])
    def _():
        e = be_s[b]
        e_prev = be_s[jnp.maximum(b - 1, 0)]
        slot = ord_s[b] % 2

        @pl.when(b == 0)
        def _():
            for cp in weight_copies(e, slot):
                cp.start()

        @pl.when((b == 0) | (e != e_prev))
        def _():
            for cp in weight_copies(e, slot):
                cp.wait()
            e_next = next_s[b]

            @pl.when(e_next >= 0)
            def _():
                for cp in weight_copies(e_next, 1 - slot):
                    cp.start()

            wgu_bf[...] = wgu_f32[slot].astype(BF16)
            wd_bf[...] = wd_f32[slot].astype(BF16)

        def ffn_rows(n):
            hgu = jnp.dot(x_ref[0:n, :], wgu_bf[...], preferred_element_type=F32) + bgu_ref[0]
            d_ff = hgu.shape[1] // 2
            hg = jnp.minimum(hgu[:, :d_ff], SWIGLU_LIMIT)
            hu = jnp.clip(hgu[:, d_ff:], -SWIGLU_LIMIT, SWIGLU_LIMIT)
            act = hg * jax.nn.sigmoid(SWIGLU_ALPHA * hg) * (hu + 1.0)
            y = jnp.dot(act.astype(BF16), wd_bf[...], preferred_element_type=F32) + bd_ref[0]
            y_ref[0:n, :] = y.astype(y_ref.dtype)

        rows_here = rows_s[b]
        for n in range(FFN_SUB, FFN_BLOCK + 1, FFN_SUB):
            pl.when(rows_here == n)(functools.partial(ffn_rows, n))


def _moe_ffn(meta, xb, w_gate_up, b_gate_up, w_down, b_down):
    cap = xb.shape[0]
    nb = cap // FFN_BLOCK
    d_ff2 = w_gate_up.shape[2]

    def blk(b, be, nu, *_):
        return jnp.minimum(b, jnp.maximum(nu[0] - 1, 0))

    def row_blk(b, be, nu, *_):
        return (blk(b, be, nu), 0)

    def expert_blk(b, be, nu, *_):
        return (be[blk(b, be, nu)], 0, 0)

    grid_spec = pltpu.PrefetchScalarGridSpec(
        num_scalar_prefetch=5,
        grid=(nb,),
        in_specs=[pl.BlockSpec((FFN_BLOCK, D_MODEL), row_blk),
                  pl.BlockSpec(memory_space=pl.ANY),
                  pl.BlockSpec((1, 1, d_ff2), expert_blk),
                  pl.BlockSpec(memory_space=pl.ANY),
                  pl.BlockSpec((1, 1, D_MODEL), expert_blk)],
        out_specs=pl.BlockSpec((FFN_BLOCK, D_MODEL), row_blk),
        scratch_shapes=[pltpu.VMEM((2, D_MODEL, d_ff2), F32),
                        pltpu.VMEM((2, d_ff2 // 2, D_MODEL), F32),
                        pltpu.VMEM((D_MODEL, d_ff2), BF16),
                        pltpu.VMEM((d_ff2 // 2, D_MODEL), BF16),
                        pltpu.SemaphoreType.DMA((2, 2))],
    )
    return pl.pallas_call(
        _moe_ffn_kernel,
        grid_spec=grid_spec,
        out_shape=jax.ShapeDtypeStruct((cap, D_MODEL), BF16),
        compiler_params=_cparams(1),
        name="moe_ffn",
    )(meta["block_expert"], meta["n_used"], meta["block_ord"], meta["block_next"],
      meta["block_rows"], xb, w_gate_up,
      b_gate_up.reshape(N_EXPERTS, 1, d_ff2), w_down, b_down.reshape(N_EXPERTS, 1, D_MODEL))


def _moe_unsort_kernel(seg_s, goff_s, nch_s, ntot_s,
                       dst_ref, gk_ref, x1_ref, g2p_ref, g2s_ref, yb_hbm,
                       yp_ref, ys_ref, ybuf, sem, *, tm, n_rows, n_prompt_tiles):
    i = pl.program_id(0)
    nt = pl.num_programs(0)
    slot = i % 2

    def rows_copy(buf, src_row, dst_row, n_chunks):
        return pltpu.make_async_copy(
            yb_hbm.at[pl.ds(pl.multiple_of(src_row, ROW_CHUNK), n_chunks * ROW_CHUNK)],
            ybuf.at[buf, pl.ds(pl.multiple_of(dst_row, ROW_CHUNK), n_chunks * ROW_CHUNK)],
            sem.at[buf])

    def fetch(tile, buf):
        def per_expert(e, carry):
            so = seg_s[tile * N_EXPERTS + e]
            go = goff_s[tile * N_EXPERTS + e]
            _for_row_pieces(
                nch_s[tile * N_EXPERTS + e], SEG_MAX_POW,
                lambda off, n: rows_copy(buf, go + off * ROW_CHUNK, so + off * ROW_CHUNK,
                                         n).start())
            return carry

        lax.fori_loop(0, N_EXPERTS, per_expert, 0)

    @pl.when(i == 0)
    def _():
        ybuf[...] = jnp.zeros(ybuf.shape, ybuf.dtype)
        fetch(0, 0)

    @pl.when(i + 1 < nt)
    def _():
        fetch(i + 1, 1 - slot)

    _for_row_pieces(ntot_s[i], TILE_MAX_POW, lambda off, n: rows_copy(slot, 0, 0, n).wait())

    dst = dst_ref[...]
    gk = gk_ref[...]
    acc = jnp.zeros((tm, D_MODEL), F32)
    for c in range(n_rows // SEL_CHUNK):
        cols = (lax.broadcasted_iota(jnp.int32, (tm, SEL_CHUNK), 1) + c * SEL_CHUNK).astype(F32)
        q = jnp.zeros((tm, SEL_CHUNK), F32)
        for kk in range(TOP_K):
            q = jnp.where(cols == dst[:, kk:kk + 1], gk[:, kk:kk + 1], q)
        acc = acc + jnp.dot(q.astype(BF16), ybuf[slot, c * SEL_CHUNK:(c + 1) * SEL_CHUNK, :],
                            preferred_element_type=F32)

    @pl.when(i < n_prompt_tiles)
    def _():
        yp_ref[...] = x1_ref[...] + g2p_ref[...] * acc

    @pl.when(i >= n_prompt_tiles)
    def _():
        ys_ref[...] = x1_ref[...] + g2s_ref[...] * acc


def _moe_unsort(meta, dst_all, gk_all, x1_all, gt2_p, gt2_s, yb, *, tm, n_prompt_tiles):
    t_all = dst_all.shape[0]
    nt = t_all // tm
    n_rows = _sort_rows(tm)
    last_p = n_prompt_tiles - 1
    grid_spec = pltpu.PrefetchScalarGridSpec(
        num_scalar_prefetch=4,
        grid=(nt,),
        in_specs=[pl.BlockSpec((tm, LANES), lambda i, *_: (i, 0)),
                  pl.BlockSpec((tm, LANES), lambda i, *_: (i, 0)),
                  pl.BlockSpec((tm, D_MODEL), lambda i, *_: (i, 0)),
                  pl.BlockSpec((1, D_MODEL), lambda i, *_: (0, 0)),
                  pl.BlockSpec((tm, D_MODEL), lambda i, *_: (0, 0)),
                  pl.BlockSpec(memory_space=pl.ANY)],
        out_specs=[pl.BlockSpec((tm, D_MODEL), lambda i, *_: (jnp.minimum(i, last_p), 0)),
                   pl.BlockSpec((tm, D_MODEL), lambda i, *_: (0, 0))],
        scratch_shapes=[pltpu.VMEM((2, n_rows, D_MODEL), BF16),
                        pltpu.SemaphoreType.DMA((2,))],
    )
    return pl.pallas_call(
        functools.partial(_moe_unsort_kernel, tm=tm, n_rows=n_rows,
                          n_prompt_tiles=n_prompt_tiles),
        grid_spec=grid_spec,
        out_shape=[jax.ShapeDtypeStruct((n_prompt_tiles * tm, D_MODEL), F32),
                   jax.ShapeDtypeStruct((tm, D_MODEL), F32)],
        compiler_params=_cparams(1),
        name="moe_unsort",
    )(meta["seg"], meta["goff"], meta["nch"], meta["ntot"],
      dst_all, gk_all, x1_all, gt2_p, gt2_s, yb)


def _moe_meta(cnt, tm):
    nt = cnt.shape[0]
    cnt = cnt.astype(jnp.int32)
    cnt_pad = (cnt + ROW_CHUNK - 1) // ROW_CHUNK * ROW_CHUNK
    seg = jnp.cumsum(cnt_pad, axis=1) - cnt_pad
    rows_e = jnp.sum(cnt_pad, axis=0)
    region = (rows_e + FFN_BLOCK - 1) // FFN_BLOCK * FFN_BLOCK
    gstart = jnp.cumsum(region) - region
    goff = gstart[None, :] + jnp.cumsum(cnt_pad, axis=0) - cnt_pad
    nblk_e = region // FFN_BLOCK
    blk_end = jnp.cumsum(nblk_e)
    cap = _moe_cap(nt * tm, tm)
    blocks = jnp.arange(cap // FFN_BLOCK, dtype=jnp.int32)
    block_expert = jnp.minimum(
        jnp.sum((blk_end[None, :] <= blocks[:, None]).astype(jnp.int32), axis=1), N_EXPERTS - 1)
    segv = jnp.zeros((nt, 1, LANES), F32).at[:, 0, :N_EXPERTS].set(seg.astype(F32))
    used = nblk_e > 0
    experts = jnp.arange(N_EXPERTS, dtype=jnp.int32)
    ord_e = jnp.cumsum(used.astype(jnp.int32)) - 1
    later = (experts[None, :] > experts[:, None]) & used[None, :]
    next_e = jnp.min(jnp.where(later, experts[None, :], N_EXPERTS), axis=1)
    next_e = jnp.where(next_e < N_EXPERTS, next_e, -1).astype(jnp.int32)
    of_block = (block_expert[:, None] == experts[None, :]).astype(jnp.int32)
    first_block = jnp.sum(of_block * (blk_end - nblk_e)[None, :], axis=1)
    rows_left = jnp.sum(of_block * rows_e[None, :], axis=1) - (blocks - first_block) * FFN_BLOCK
    block_rows = jnp.clip((rows_left + FFN_SUB - 1) // FFN_SUB * FFN_SUB, FFN_SUB, FFN_BLOCK)
    return {
        "block_rows": block_rows.astype(jnp.int32),
        "block_ord": jnp.sum(of_block * ord_e[None, :].astype(jnp.int32), axis=1),
        "block_next": jnp.sum(of_block * next_e[None, :], axis=1),
        "seg": seg.reshape(-1), "goff": goff.reshape(-1).astype(jnp.int32),
        "nch": (cnt_pad // ROW_CHUNK).reshape(-1),
        "ntot": jnp.sum(cnt_pad, axis=1) // ROW_CHUNK,
        "tstart": (gstart + rows_e).astype(jnp.int32),
        "tnch": (region - rows_e) // ROW_CHUNK,
        "block_expert": block_expert,
        "n_used": blk_end[-1:].astype(jnp.int32),
        "segv": segv,
    }


def _moe_cap(t_all, tm):
    nt = t_all // tm
    worst = TOP_K * t_all + nt * N_EXPERTS * (ROW_CHUNK - 1) + N_EXPERTS * (FFN_BLOCK - ROW_CHUNK)
    return -(-worst // FFN_BLOCK) * FFN_BLOCK


def _t5_bucket(dist):
    max_exact = NUM_BUCKETS // 2
    d = dist.astype(jnp.int32)
    ratio = (jnp.log(jnp.maximum(d, 1).astype(F32) / max_exact)
             / math.log(MAX_DISTANCE / max_exact))
    large = jnp.minimum(max_exact + (ratio * (NUM_BUCKETS - max_exact)).astype(jnp.int32),
                        NUM_BUCKETS - 1)
    return jnp.where(d < max_exact, d, large)


def _step_bias(tab, dil):
    return tab[_t5_bucket(dil * jnp.arange(ATT_BLK + 1))].astype(F32).T


def _band_table(sb):
    return jnp.concatenate([sb[:, ::-1], jnp.full((HEADS, ATT_BLK - 1), NEG_INF, F32)], axis=1)


def _cache_table(sb, dil):
    on_grid = sb[:, :0:-1]
    if dil == 1:
        return on_grid
    off = jnp.full((HEADS, ATT_BLK, dil - 1), NEG_INF, F32)
    return jnp.concatenate([on_grid[:, :, None], off], axis=2).reshape(HEADS, ATT_BLK * dil)


def kernel(x_prompt, x_sample, cache_kv_w128, cache_kv_w512, cache_kv_w2048, state_pool, c_prompt,
           c_sample, w_ada, b_ada, norm_mix_g, norm_ffn_g, w_in, q_norm_g, k_norm_g, rel_bias,
           w_pool_mix, pool_scale, w_up_attn, w_up_pool, w_out, w_router, b_router, w_gate_up,
           b_gate_up, w_down, b_down):
    assert w_ada.shape[0] == 1, "one layer"
    seq = x_prompt.shape[1]
    n_s = x_sample.shape[0]
    assert x_prompt.shape[0] == 1 and x_sample.shape[1] == 1
    assert seq % (DIL_GROUPS[-1][1] * ATT_BLK * ATT_SUB) == 0 and seq % TM_PROMPT == 0
    assert n_s == TM_SAMPLE
    dils = tuple(d for _, d in DIL_GROUPS)
    caches = (cache_kv_w128, cache_kv_w512, cache_kv_w2048)

    w_in_bf = w_in[0].astype(BF16)
    heads_of = jnp.arange(GROUP_W) // HEAD_DIM
    half_heads = heads_of[:GROUP_W // 2]
    bdiag = (half_heads[:, None] == half_heads[None, :]).astype(BF16)
    qg = (jnp.tile(q_norm_g[0], HEADS) * SCALE).reshape(1, GROUP_W)
    kg = jnp.tile(k_norm_g[0], HEADS).reshape(1, GROUP_W)
    expand = ((jnp.arange(LANES)[:, None] % HEADS == heads_of[None, :])
              & (jnp.arange(LANES)[:, None] < 2 * HEADS)).astype(BF16)
    wr = jnp.zeros((D_MODEL, LANES), F32).at[:, :N_EXPERTS].set(w_router[0])
    wr_hi = wr.astype(BF16)
    wr_lo = (wr - wr_hi.astype(F32)).astype(BF16)
    br = jnp.full((1, LANES), NEG_INF, F32).at[0, :N_EXPERTS].set(b_router[0])
    wts = (w_pool_mix[0].astype(BF16), pool_scale[0].reshape(1, POOL_W),
           w_up_attn[0].astype(BF16), w_up_pool[0].astype(BF16), w_out[0].astype(BF16), expand,
           norm_ffn_g[0].reshape(1, D_MODEL), wr_hi, wr_lo, br)
    g1 = norm_mix_g[0].reshape(1, D_MODEL)

    n_c = 1 + n_s
    c_all = jnp.zeros((-(-n_c // 8) * 8, D_MODEL), F32).at[0:1].set(c_prompt).at[1:n_c].set(c_sample)
    mod = _ada(c_all, w_ada[0], b_ada[0])
    sh1, sc1, gt1, sh2, sc2, gt2 = jnp.split(mod, N_ADA, axis=-1)

    def prow(m):
        return m[0:1]

    def srows(m):
        return m[1:n_c]

    xp = x_prompt[0]
    q_p, k_p, v_p, u_p, sga_p, sgp_p, st_p = _proj(
        xp, g1, prow(sc1), prow(sh1), w_in_bf, bdiag, qg, kg,
        tm=TM_PROMPT, dils=dils, per_row=False)
    xs = x_sample[:, 0]
    ones = (1, 1, 1)
    q_s, _, _, u_s, sga_s, sgp_s, st_s = _proj(
        xs, g1, srows(sc1), srows(sh1), w_in_bf, bdiag, qg, kg,
        tm=TM_SAMPLE, dils=ones, per_row=True)
    pooled_s, pool_state_t = _pool_sample(jnp.transpose(state_pool, (0, 2, 1, 3)), u_s)
    pool_state_s = jnp.transpose(pool_state_t, (0, 2, 1, 3))

    def heads(a):
        return a.astype(F32).reshape(n_s, HEADS, HEAD_DIM)

    step_bias = [_step_bias(rel_bias[:, g * HEADS:(g + 1) * HEADS], d)
                 for g, (_, d) in enumerate(DIL_GROUPS)]

    def sample_group(g, part, n_parts):
        cache_t = jnp.transpose(caches[g], (0, 1, 3, 4, 5, 2))
        n_sub = n_s // n_parts
        return (heads(q_s[g]), heads(st_s[g][:, :GROUP_W]), heads(st_s[g][:, GROUP_W:]),
                step_bias[g][:, 0:1], _cache_table(step_bias[g], DIL_GROUPS[g][1]), cache_t,
                part * n_sub, n_sub)

    o_p, lse_p = [], []
    o_parts = [[None] * n for n in SAMPLE_PARTS]
    lse_parts = [[None] * n for n in SAMPLE_PARTS]
    for g, (_, d) in enumerate(DIL_GROUPS):
        hosted = SAMPLE_HOST[g]
        o, lse, sample_outs = _attn(
            q_p[g], k_p[g], v_p[g], _band_table(step_bias[g]), d,
            [sample_group(sg, part, SAMPLE_PARTS[sg]) for sg, part in hosted])
        o_p.append(o)
        lse_p.append(lse)
        for (sg, part), (os_g, lses_g) in zip(hosted, sample_outs):
            o_parts[sg][part] = os_g.reshape(-1, GROUP_W)
            lse_parts[sg][part] = lses_g[:, :, 0]
    o_s = [jnp.concatenate(parts, axis=0) for parts in o_parts]
    lse_s = [jnp.zeros((n_s, LANES), F32).at[:, :HEADS].set(jnp.concatenate(parts, axis=0))
             for parts in lse_parts]

    nt_p = seq // TM_PROMPT
    t_all = seq + TM_PROMPT
    bufs = _post(xp, o_p, lse_p, (u_p,), sga_p, sgp_p, wts, (prow(gt1), prow(sc2), prow(sh2)),
                 tm=TM_PROMPT, dils=dils, per_row=False, rows_total=t_all, row_block0=0,
                 cnt_tiles=nt_p + 1, cnt_block=None, grid=nt_p, n_valid_steps=None,
                 alias_bufs=None)
    bufs = _post(xs, o_s, lse_s, pooled_s, sga_s, sgp_s, wts,
                 (srows(gt1), srows(sc2), srows(sh2)),
                 tm=TM_SAMPLE, dils=ones, per_row=True, rows_total=t_all,
                 row_block0=seq // TM_SAMPLE, cnt_tiles=nt_p + 1, cnt_block=nt_p,
                 grid=TM_PROMPT // TM_SAMPLE, n_valid_steps=1, alias_bufs=bufs)
    x1_all, h2_all, a_all, idx_all, gk_all, cnt = bufs

    meta = _moe_meta(cnt[:, 0, :N_EXPERTS], TM_PROMPT)
    cap = _moe_cap(t_all, TM_PROMPT)
    xb, dst_all = _moe_sort(meta, a_all, idx_all, h2_all, tm=TM_PROMPT, cap=cap)
    yb = _moe_ffn(meta, xb, w_gate_up[0], b_gate_up[0], w_down[0], b_down[0])
    gt2_s = jnp.zeros((TM_PROMPT, D_MODEL), F32).at[:n_s].set(srows(gt2))
    y_p, y_s = _moe_unsort(meta, dst_all, gk_all, x1_all, prow(gt2), gt2_s, yb,
                           tm=TM_PROMPT, n_prompt_tiles=nt_p)

    def kv_state(st, rows):
        return st.reshape(1, 1, rows, 2, HEADS, HEAD_DIM)

    kv_p = [kv_state(st, st.shape[0]) for st in st_p]
    kv_s = [st.reshape(1, n_s, 1, 2, HEADS, HEAD_DIM) for st in st_s]
    pool_p = u_p[seq - POOL_BUF:].reshape(1, 1, POOL_BUF, POOL_W)
    return (y_p.reshape(1, seq, D_MODEL), y_s[:n_s].reshape(n_s, 1, D_MODEL),
            kv_p[0], kv_p[1], kv_p[2], pool_p, kv_s[0], kv_s[1], kv_s[2], pool_state_s)
```

```python
import functools
import math

import jax
import jax.numpy as jnp
from jax import lax
from jax.experimental import pallas as pl
from jax.experimental.pallas import tpu as pltpu

F32 = jnp.float32
BF16 = jnp.bfloat16

D_MODEL = 1024
HEAD_DIM = 64
HEADS = 8
GROUP_W = HEADS * HEAD_DIM
DIL_GROUPS = ((128, 1), (512, 4), (2048, 16))
N_GROUPS = len(DIL_GROUPS)
QKV_W = N_GROUPS * GROUP_W
ATT_BLK = 128
ATT_SUB = 4
POOL_WINDOWS = (2, 4, 8, 16)
POOL_W = 512
POOL_GW = 128
POOL_BUF = 15
OFF_K, OFF_V = QKV_W, 2 * QKV_W
OFF_U = 3 * QKV_W
OFF_GA = OFF_U + POOL_W
OFF_GP = OFF_GA + D_MODEL
IN_W = OFF_GP + D_MODEL
NUM_BUCKETS = 32
MAX_DISTANCE = 2048
N_EXPERTS = 32
TOP_K = 4
SWIGLU_LIMIT = 7.0
SWIGLU_ALPHA = 1.702
N_ADA = 6
EPS = 1e-6
NEG_INF = -1e30
PAST_LEN = 8192
SCALE = HEAD_DIM ** -0.5

LANES = 128
ROW_CHUNK = 16
TM_PROMPT = 512
TM_SAMPLE = 128
FFN_BLOCK = 1024
FFN_SUB = 128
SEL_CHUNK = 512
BF16_EXACT = 256
VMEM_LIMIT = 56 * 1024 * 1024


def _cparams(n_axes):
    return pltpu.CompilerParams(dimension_semantics=("arbitrary",) * n_axes,
                                vmem_limit_bytes=VMEM_LIMIT)


def _const_spec(shape):
    nd = len(shape)
    return pl.BlockSpec(shape, lambda *_: (0,) * nd)


def _ada_kernel(c_ref, w_ref, b_ref, o_ref):
    c = c_ref[...]
    s = c * jax.nn.sigmoid(c)
    o_ref[...] = jnp.dot(s.astype(BF16), w_ref[...].astype(BF16),
                         preferred_element_type=F32) + b_ref[...]


def _ada(c_all, w_ada, b_ada):
    rows = c_all.shape[0]
    n = w_ada.shape[1]
    tn = 1536
    return pl.pallas_call(
        _ada_kernel,
        grid=(n // tn,),
        in_specs=[pl.BlockSpec((rows, D_MODEL), lambda j: (0, 0)),
                  pl.BlockSpec((D_MODEL, tn), lambda j: (0, j)),
                  pl.BlockSpec((1, tn), lambda j: (0, j))],
        out_specs=pl.BlockSpec((rows, tn), lambda j: (0, j)),
        out_shape=jax.ShapeDtypeStruct((rows, n), F32),
        compiler_params=_cparams(1),
        name="ada",
    )(c_all, w_ada, b_ada.reshape(1, n))


def _proj_kernel(x_ref, g_ref, sc_ref, sh_ref, w_ref, bd_ref, qg_ref, kg_ref,
                 *refs, tm, dils, st_rows):
    q_refs, k_refs, v_refs = refs[0:3], refs[3:6], refs[6:9]
    u_ref, sga_ref, sgp_ref = refs[9:12]
    st_refs = refs[12:15]
    scr = refs[15]

    x = x_ref[...]
    ms = jnp.mean(x * x, axis=-1, keepdims=True)
    h = x * lax.rsqrt(ms + EPS) * g_ref[...] * (1.0 + sc_ref[...]) + sh_ref[...]
    hb = h.astype(BF16)

    def proj(off, width):
        return jnp.dot(hb, w_ref[:, off:off + width], preferred_element_type=F32)

    def head_norm(z, gain_ref):
        zz = (z * z).astype(BF16)
        half = GROUP_W // 2
        ss = jnp.concatenate(
            [jnp.dot(zz[:, :half], bd_ref[...], preferred_element_type=F32),
             jnp.dot(zz[:, half:], bd_ref[...], preferred_element_type=F32)], axis=1)
        return z * lax.rsqrt(ss * (1.0 / HEAD_DIM) + EPS) * gain_ref[...]

    def put(out_ref, val, d):
        if d == 1:
            out_ref[...] = val.astype(out_ref.dtype)
        else:
            for c in range(GROUP_W // LANES):
                scr[c] = val[:, c * LANES:(c + 1) * LANES]
            for r in range(d):
                for c in range(GROUP_W // LANES):
                    col = r * GROUP_W + c * LANES
                    out_ref[:, col:col + LANES] = (
                        scr[c, pl.ds(r, tm // d, stride=d), :].astype(out_ref.dtype))

    for g, d in enumerate(dils):
        qn = head_norm(proj(g * GROUP_W, GROUP_W), qg_ref)
        put(q_refs[g], qn, d)
        kn = head_norm(proj(OFF_K + g * GROUP_W, GROUP_W), kg_ref)
        put(k_refs[g], kn, d)
        v = proj(OFF_V + g * GROUP_W, GROUP_W)
        put(v_refs[g], v, d)
        rb = st_rows[g]
        st_refs[g][:, 0:GROUP_W] = kn[tm - rb:, :]
        st_refs[g][:, GROUP_W:2 * GROUP_W] = v[tm - rb:, :]

    u_ref[...] = proj(OFF_U, POOL_W)
    sga_ref[...] = jax.nn.sigmoid(proj(OFF_GA, D_MODEL)).astype(BF16)
    sgp_ref[...] = jax.nn.sigmoid(proj(OFF_GP, D_MODEL)).astype(BF16)


def _mod_spec(per_row, tm):
    if per_row:
        return pl.BlockSpec((tm, D_MODEL), lambda i: (i, 0))
    return pl.BlockSpec((1, D_MODEL), lambda i: (0, 0))


def _proj(x, g1, sc1, sh1, w_in_bf, bdiag, qg, kg, *, tm, dils, per_row):
    s = x.shape[0]
    nt = s // tm
    wins = tuple(min(w, s) for w, _ in DIL_GROUPS)
    st_rows = tuple(min(tm, w) for w in wins)

    def res_spec(d):
        return pl.BlockSpec((tm // d, d * GROUP_W), lambda i: (i, 0))

    def st_spec(w, rb):
        first = nt - w // rb
        return pl.BlockSpec((rb, 2 * GROUP_W), lambda i: (jnp.maximum(i - first, 0), 0))

    qkv_shapes = [jax.ShapeDtypeStruct((s // d, d * GROUP_W), BF16) for d in dils]
    out_shape = (qkv_shapes * 3
                 + [jax.ShapeDtypeStruct((s, POOL_W), F32),
                    jax.ShapeDtypeStruct((s, D_MODEL), BF16),
                    jax.ShapeDtypeStruct((s, D_MODEL), BF16)]
                 + [jax.ShapeDtypeStruct((w, 2 * GROUP_W), F32) for w in wins])
    out_specs = ([res_spec(d) for d in dils] * 3
                 + [pl.BlockSpec((tm, POOL_W), lambda i: (i, 0)),
                    pl.BlockSpec((tm, D_MODEL), lambda i: (i, 0)),
                    pl.BlockSpec((tm, D_MODEL), lambda i: (i, 0))]
                 + [st_spec(w, rb) for w, rb in zip(wins, st_rows)])
    in_specs = [pl.BlockSpec((tm, D_MODEL), lambda i: (i, 0)),
                _const_spec((1, D_MODEL)),
                _mod_spec(per_row, tm), _mod_spec(per_row, tm),
                pl.BlockSpec((D_MODEL, IN_W), lambda i: (0, 0), pipeline_mode=pl.Buffered(1)),
                _const_spec((GROUP_W // 2, GROUP_W // 2)),
                _const_spec((1, GROUP_W)), _const_spec((1, GROUP_W))]
    outs = pl.pallas_call(
        functools.partial(_proj_kernel, tm=tm, dils=dils, st_rows=st_rows),
        grid=(nt,),
        in_specs=in_specs,
        out_specs=out_specs,
        out_shape=out_shape,
        scratch_shapes=[pltpu.VMEM((GROUP_W // LANES, tm, LANES), F32)],
        compiler_params=_cparams(1),
        name="proj",
    )(x, g1, sc1, sh1, w_in_bf, bdiag, qg, kg)
    return outs[0:3], outs[3:6], outs[6:9], outs[9], outs[10], outs[11], outs[12:15]


def _sample_group_attn(qs_ref, kns_ref, vns_ref, bself_ref, btab_ref, c_ref, os_ref, lses_ref):
    n_tok = qs_ref.shape[0]
    win = btab_ref.shape[1]
    row_w = lax.broadcasted_iota(jnp.int32, (HEADS, win), 0)
    row_e = lax.broadcasted_iota(jnp.int32, (HEADS, HEAD_DIM), 0)
    ss, s0s = [], []
    for t in range(n_tok):
        q = qs_ref[t]
        qb = q.astype(BF16)
        s = jnp.zeros((HEADS, win), F32)
        for h in range(HEADS):
            sh = jnp.dot(qb, c_ref[0, t, 0, h].astype(BF16), preferred_element_type=F32)
            s = jnp.where(row_w == h, sh, s)
        ss.append(s + btab_ref[...])
        s0s.append(jnp.sum(q * kns_ref[t], axis=-1, keepdims=True) + bself_ref[...])
    s = jnp.concatenate(ss, axis=0)
    s0 = jnp.concatenate(s0s, axis=0)
    m = jnp.maximum(jnp.max(s, axis=-1, keepdims=True), s0)
    p = jnp.exp(s - m)
    p0 = jnp.exp(s0 - m)
    l = jnp.sum(p, axis=-1, keepdims=True) + p0
    pb = p.astype(BF16)
    lse = m + jnp.log(l)
    for t in range(n_tok):
        rows = slice(t * HEADS, (t + 1) * HEADS)
        o = jnp.zeros((HEADS, HEAD_DIM), F32)
        for h in range(HEADS):
            oh = lax.dot_general(pb[rows], c_ref[0, t, 1, h].astype(BF16),
                                 (((1,), (1,)), ((), ())), preferred_element_type=F32)
            o = jnp.where(row_e == h, oh, o)
        os_ref[t] = (o + p0[rows] * vns_ref[t]) / l[rows]
        lses_ref[t] = lse[rows]


SAMPLE_IN = 6
SAMPLE_PARTS = (1, 1, 4)
SAMPLE_HOST = (((2, 2), (2, 3)), ((1, 0), (2, 1)), ((2, 0), (0, 0)))


def _attn_kernel(q_ref, kp_ref, kc_ref, vp_ref, vc_ref, r_ref, *refs, n_sample_groups):
    n_in = SAMPLE_IN * n_sample_groups
    o_ref, lse_ref = refs[n_in], refs[n_in + 1]
    bias_ref = refs[-1]
    i = pl.program_id(1)

    for sg in range(n_sample_groups):
        _sample_group_attn(*refs[SAMPLE_IN * sg:SAMPLE_IN * (sg + 1)],
                           *refs[n_in + 2 + 2 * sg:n_in + 4 + 2 * sg])

    @pl.when((pl.program_id(0) == 0) & (i == 0))
    def _():
        for h in range(HEADS):
            row = jnp.broadcast_to(r_ref[h:h + 1, :], (ATT_BLK, 2 * ATT_BLK))
            bias_ref[h] = pltpu.roll(row, 0, 1, stride=1, stride_axis=0)

    q = q_ref[...]
    k = jnp.concatenate([kp_ref[...], kc_ref[...]], axis=0)
    v = jnp.concatenate([vp_ref[...], vc_ref[...]], axis=0)
    col = lax.broadcasted_iota(jnp.int32, (ATT_BLK, 2 * ATT_BLK), 1)
    no_prev = jnp.where((col < ATT_BLK) & (i == 0), NEG_INF, 0.0)
    lane_q = lax.broadcasted_iota(jnp.int32, (ATT_SUB * ATT_BLK, LANES), 1)
    lane_v = lax.broadcasted_iota(jnp.int32, ((ATT_SUB + 1) * ATT_BLK, LANES), 1)

    def pair(h):
        return slice((h // 2) * LANES, (h // 2 + 1) * LANES)

    def mine(lane, h):
        return (lane < HEAD_DIM) == (h % 2 == 0)

    ss = []
    for h in range(HEADS):
        q2 = q[:, pair(h)]
        qm = jnp.where(mine(lane_q, h), q2, jnp.zeros_like(q2))
        k2 = k[:, pair(h)]
        for j in range(ATT_SUB):
            s = lax.dot_general(qm[j * ATT_BLK:(j + 1) * ATT_BLK], k2[j * ATT_BLK:(j + 2) * ATT_BLK],
                                (((1,), (1,)), ((), ())), preferred_element_type=F32)
            s = s + bias_ref[h]
            ss.append(s + no_prev if j == 0 else s)
    s = jnp.concatenate(ss, axis=0)
    m = jnp.max(s, axis=-1, keepdims=True)
    p = jnp.exp(s - m)
    l = jnp.sum(p, axis=-1, keepdims=True)
    pb = p.astype(BF16)
    lse = m + jnp.log(l)
    inv_l = 1.0 / l
    outs, lses = [], []
    for h in range(HEADS):
        v2 = v[:, pair(h)]
        vm = jnp.where(mine(lane_v, h), v2, jnp.zeros_like(v2))
        o_sub, lse_sub = [], []
        for j in range(ATT_SUB):
            rows = slice((h * ATT_SUB + j) * ATT_BLK, (h * ATT_SUB + j + 1) * ATT_BLK)
            o_sub.append(jnp.dot(pb[rows], vm[j * ATT_BLK:(j + 2) * ATT_BLK],
                                 preferred_element_type=F32) * inv_l[rows])
            lse_sub.append(lse[rows])
        o = jnp.concatenate(o_sub, axis=0)
        if h % 2 == 0:
            outs.append(o)
        else:
            outs[-1] = outs[-1] + o
        lses.append(jnp.concatenate(lse_sub, axis=0))
    o_ref[...] = jnp.concatenate(outs, axis=-1).astype(o_ref.dtype)
    lse_ref[...] = jnp.concatenate(
        lses + [jnp.zeros((ATT_SUB * ATT_BLK, LANES - HEADS), F32)], axis=-1)


def _attn(q, k, v, r_tab, d, sample_groups):
    rows = q.shape[0]
    step = ATT_SUB * ATT_BLK
    nblk = rows // step
    cur = pl.BlockSpec((step, GROUP_W), lambda r, i: (i, r))
    prev = pl.BlockSpec((ATT_BLK, GROUP_W), lambda r, i: (jnp.maximum(i * ATT_SUB - 1, 0), r))
    in_specs = [cur, prev, cur, prev, cur,
                pl.BlockSpec((HEADS, 2 * ATT_BLK), lambda r, i: (0, 0))]
    out_specs = [pl.BlockSpec((step, GROUP_W), lambda r, i: (i, r)),
                 pl.BlockSpec((step, LANES), lambda r, i: (i, r))]
    out_shape = [jax.ShapeDtypeStruct((rows, d * GROUP_W), BF16),
                 jax.ShapeDtypeStruct((rows, d * LANES), F32)]
    args = [q, k, k, v, v, r_tab]
    for qs, kns, vns, bself, btab, cache_t, tok_start, n_tok in sample_groups:
        assert n_tok % (d * nblk) == 0
        tok = n_tok // (d * nblk)
        assert tok_start % tok == 0
        first = tok_start // tok

        def in_idx(r, i, first=first):
            return (first + r * nblk + i, 0, 0)

        def out_idx(r, i):
            return (r * nblk + i, 0, 0)

        tok_in = pl.BlockSpec((tok, HEADS, HEAD_DIM), in_idx)
        in_specs += [tok_in, tok_in, tok_in,
                     pl.BlockSpec(bself.shape, lambda r, i: (0, 0)),
                     pl.BlockSpec(btab.shape, lambda r, i: (0, 0)),
                     pl.BlockSpec((1, tok) + cache_t.shape[2:],
                                  lambda r, i, first=first: (0, first + r * nblk + i, 0, 0, 0, 0))]
        out_specs += [pl.BlockSpec((tok, HEADS, HEAD_DIM), out_idx),
                      pl.BlockSpec((tok, HEADS, 1), out_idx)]
        out_shape += [jax.ShapeDtypeStruct((n_tok, HEADS, HEAD_DIM), F32),
                      jax.ShapeDtypeStruct((n_tok, HEADS, 1), F32)]
        args += [qs, kns, vns, bself, btab, cache_t]
    outs = pl.pallas_call(
        functools.partial(_attn_kernel, n_sample_groups=len(sample_groups)),
        grid=(d, nblk),
        in_specs=in_specs,
        out_specs=out_specs,
        out_shape=out_shape,
        scratch_shapes=[pltpu.VMEM((HEADS, ATT_BLK, 2 * ATT_BLK), F32)],
        compiler_params=_cparams(2),
        name=f"attn_d{d}",
    )(*args)
    return outs[0], outs[1], [(outs[2 + 2 * j], outs[3 + 2 * j]) for j in range(len(sample_groups))]


def _pool_sample_kernel(st_ref, u_ref, pooled_ref, new_ref):
    u = u_ref[...]
    rows = [st_ref[0, j] for j in range(POOL_BUF)]
    outs = []
    for g, w in enumerate(POOL_WINDOWS):
        sl = slice(g * POOL_GW, (g + 1) * POOL_GW)
        acc = u[:, sl]
        for j in range(POOL_BUF - (w - 1), POOL_BUF):
            acc = acc + rows[j][:, sl]
        outs.append(acc / float(w) - u[:, sl])
    pooled_ref[...] = jnp.concatenate(outs, axis=-1)
    for j in range(POOL_BUF - 1):
        new_ref[0, j] = rows[j + 1]
    new_ref[0, POOL_BUF - 1] = u


def _pool_sample(state, u):
    n = u.shape[0]
    return pl.pallas_call(
        _pool_sample_kernel,
        grid=(1,),
        in_specs=[_const_spec(state.shape), _const_spec(u.shape)],
        out_specs=[_const_spec(u.shape), _const_spec(state.shape)],
        out_shape=[jax.ShapeDtypeStruct((n, POOL_W), F32),
                   jax.ShapeDtypeStruct(state.shape, F32)],
        compiler_params=_cparams(1),
        name="pool_sample",
    )(state, u)


def _post_kernel(*refs, tm, dils, pooled_given, n_valid_steps, aliased):
    it = iter(refs)
    x_ref = next(it)
    o_refs = [next(it) for _ in range(N_GROUPS)]
    lse_refs = [next(it) for _ in range(N_GROUPS)]
    if pooled_given:
        pooled_ref = next(it)
    else:
        u_ref, uh_ref = next(it), next(it)
    sga_ref, sgp_ref = next(it), next(it)
    wpm_ref, psc_ref, wua_ref, wup_ref, wout_ref, exp_ref = (next(it) for _ in range(6))
    gt1_ref, g2_ref, sc2_ref, sh2_ref = (next(it) for _ in range(4))
    wrh_ref, wrl_ref, br_ref = (next(it) for _ in range(3))
    if aliased:
        for _ in range(6):
            next(it)
    x1_ref, h2_ref, a_ref, idx_ref, gk_ref, cnt_ref = (next(it) for _ in range(6))
    ob_scr, ls_scr = next(it), next(it)

    i = pl.program_id(0)

    def compute():
        obs, lss = [], []
        for g, d in enumerate(dils):
            if d == 1:
                obs.append(o_refs[g][...].astype(F32))
                lss.append(lse_refs[g][...])
            else:
                for r in range(d):
                    for c in range(GROUP_W // LANES):
                        col = r * GROUP_W + c * LANES
                        ob_scr[c, pl.ds(r, tm // d, stride=d), :] = (
                            o_refs[g][:, col:col + LANES].astype(F32))
                    ls_scr[pl.ds(r, tm // d, stride=d), :] = (
                        lse_refs[g][:, r * LANES:(r + 1) * LANES])
                obs.append(jnp.concatenate([ob_scr[c] for c in range(GROUP_W // LANES)],
                                           axis=-1))
                lss.append(ls_scr[...])
        mx = jnp.maximum(jnp.maximum(lss[0], lss[1]), lss[2])
        es = [jnp.exp(l - mx) for l in lss]
        den = es[0] + es[1] + es[2]
        attn_o = jnp.zeros((tm, GROUP_W), F32)
        head_lane = lax.broadcasted_iota(jnp.int32, (tm, LANES), 1) < HEADS
        for g in range(N_GROUPS):
            w = jnp.where(head_lane, es[g] / den, 0.0)
            w_hi = w.astype(BF16).astype(F32)
            w_lo = (w - w_hi).astype(BF16).astype(F32)
            lhs = (w_hi + pltpu.roll(w_lo, HEADS, 1)).astype(BF16)
            wexp = jnp.dot(lhs, exp_ref[...], preferred_element_type=F32)
            attn_o = attn_o + wexp * obs[g]

        if pooled_given:
            pooled = pooled_ref[...]
        else:
            u = u_ref[...]
            halo = jnp.where(i == 0, 0.0, uh_ref[...])
            pos = (lax.broadcasted_iota(jnp.int32, (tm, 1), 0) + i * tm + 1).astype(F32)
            outs = []
            for g, w in enumerate(POOL_WINDOWS):
                sl = slice(g * POOL_GW, (g + 1) * POOL_GW)
                a = jnp.concatenate([halo[:, sl], u[:, sl]], axis=0)
                span = 1
                while span < w:
                    n = a.shape[0] - span
                    a = a[span:, :] + a[:n, :]
                    span *= 2
                off = a.shape[0] - tm
                win_sum = a[off:, :]
                outs.append(win_sum / jnp.minimum(pos, float(w)) - u[:, sl])
            pooled = jnp.concatenate(outs, axis=-1)
        pool_parts = []
        for g in range(len(POOL_WINDOWS)):
            sl = slice(g * POOL_GW, (g + 1) * POOL_GW)
            pool_parts.append(jnp.dot(pooled[:, sl].astype(BF16), wpm_ref[g],
                                      preferred_element_type=F32))
        pool_o = jnp.concatenate(pool_parts, axis=-1) * psc_ref[...]

        up_a = jnp.dot(attn_o.astype(BF16), wua_ref[...], preferred_element_type=F32)
        up_p = jnp.dot(pool_o.astype(BF16), wup_ref[...], preferred_element_type=F32)
        merged = sga_ref[...].astype(F32) * up_a + sgp_ref[...].astype(F32) * up_p
        mo = jnp.dot(merged.astype(BF16), wout_ref[...], preferred_element_type=F32)
        x1 = x_ref[...] + gt1_ref[...] * mo
        x1_ref[...] = x1

        ms = jnp.mean(x1 * x1, axis=-1, keepdims=True)
        h2 = x1 * lax.rsqrt(ms + EPS) * g2_ref[...] * (1.0 + sc2_ref[...]) + sh2_ref[...]
        h2_hi = h2.astype(BF16)
        h2_ref[...] = h2_hi
        h2_lo = (h2 - h2_hi.astype(F32)).astype(BF16)
        logits = (jnp.dot(h2_hi, wrh_ref[...], preferred_element_type=F32)
                  + jnp.dot(h2_lo, wrh_ref[...], preferred_element_type=F32)
                  + jnp.dot(h2_hi, wrl_ref[...], preferred_element_type=F32)
                  + br_ref[...])
        lane = lax.broadcasted_iota(jnp.int32, (tm, LANES), 1).astype(F32)
        work = logits
        vals, ids = [], []
        for _ in range(TOP_K):
            m = jnp.max(work, axis=-1, keepdims=True)
            ik = jnp.min(jnp.where(work == m, lane, float(LANES)), axis=-1, keepdims=True)
            vals.append(m)
            ids.append(ik)
            work = jnp.where(lane == ik, -3e38, work)
        ex = [jnp.exp(v - vals[0]) for v in vals]
        den_k = ex[0] + ex[1] + ex[2] + ex[3]
        a = jnp.zeros((tm, LANES), F32)
        idx = jnp.zeros((tm, LANES), F32)
        gk = jnp.zeros((tm, LANES), F32)
        for kk in range(TOP_K):
            gate = ex[kk] / den_k
            a = a + jnp.where(lane == ids[kk], gate, 0.0)
            idx = jnp.where(lane == float(kk), ids[kk], idx)
            gk = jnp.where(lane == float(kk), gate, gk)
        a_ref[...] = a
        idx_ref[...] = idx
        gk_ref[...] = gk
        cnt = jnp.sum((a > 0.0).astype(F32), axis=0, keepdims=True)
        row = lax.broadcasted_iota(jnp.int32, (8, LANES), 0)
        cnt_ref[0] = jnp.where(row == 0, jnp.broadcast_to(cnt, (8, LANES)), 0.0)

    if n_valid_steps is None:
        compute()
    else:
        pl.when(i < n_valid_steps)(compute)

        @pl.when(i >= n_valid_steps)
        def _():
            x1_ref[...] = jnp.zeros(x1_ref.shape, x1_ref.dtype)
            h2_ref[...] = jnp.zeros(h2_ref.shape, h2_ref.dtype)
            a_ref[...] = jnp.zeros(a_ref.shape, a_ref.dtype)
            idx_ref[...] = jnp.zeros(idx_ref.shape, idx_ref.dtype)
            gk_ref[...] = jnp.zeros(gk_ref.shape, gk_ref.dtype)


def _post(x, o_list, lse_list, pool_in, sga, sgp, wts, mods, *, tm, dils, per_row,
          rows_total, row_block0, cnt_tiles, cnt_block, grid, n_valid_steps, alias_bufs):
    pooled_given = not isinstance(pool_in, tuple)
    nv = grid if n_valid_steps is None else n_valid_steps

    def clamp(i):
        return jnp.minimum(i, nv - 1)

    def tile_spec(width):
        return pl.BlockSpec((tm, width), lambda i: (clamp(i), 0))

    in_specs = [tile_spec(D_MODEL)]
    in_specs += [pl.BlockSpec((tm // d, d * GROUP_W), lambda i: (clamp(i), 0)) for d in dils]
    in_specs += [pl.BlockSpec((tm // d, d * LANES), lambda i: (clamp(i), 0)) for d in dils]
    args = [x, *o_list, *lse_list]
    if pooled_given:
        in_specs.append(tile_spec(POOL_W))
        args.append(pool_in)
    else:
        u = pool_in[0]
        in_specs += [tile_spec(POOL_W),
                     pl.BlockSpec((16, POOL_W),
                                  lambda i: (jnp.maximum(i * (tm // 16) - 1, 0), 0))]
        args += [u, u]
    in_specs += [tile_spec(D_MODEL), tile_spec(D_MODEL)]
    args += [sga, sgp]
    wpm, psc, wua, wup, wout, expand, g2, wrh, wrl, br = wts
    gt1, sc2, sh2 = mods

    def mspec():
        if per_row:
            return pl.BlockSpec((tm, D_MODEL), lambda i: (clamp(i), 0))
        return _const_spec((1, D_MODEL))

    in_specs += [_const_spec(wpm.shape), _const_spec(psc.shape), _const_spec(wua.shape),
                 _const_spec(wup.shape), _const_spec(wout.shape), _const_spec(expand.shape),
                 mspec(), _const_spec(g2.shape), mspec(), mspec(),
                 _const_spec(wrh.shape), _const_spec(wrl.shape), _const_spec(br.shape)]
    args += [wpm, psc, wua, wup, wout, expand, gt1, g2, sc2, sh2, wrh, wrl, br]
    aliases = {}
    if alias_bufs is not None:
        base = len(args)
        in_specs += [pl.BlockSpec(memory_space=pl.ANY)] * 6
        args += list(alias_bufs)
        aliases = {base + j: j for j in range(6)}

    def out_spec(width):
        return pl.BlockSpec((tm, width), lambda i: (row_block0 + i, 0))

    out_specs = [out_spec(D_MODEL), out_spec(D_MODEL), out_spec(LANES), out_spec(LANES),
                 out_spec(LANES),
                 pl.BlockSpec((1, 8, LANES),
                              lambda i: (cnt_block if cnt_block is not None else i, 0, 0))]
    out_shape = [jax.ShapeDtypeStruct((rows_total, D_MODEL), F32),
                 jax.ShapeDtypeStruct((rows_total, D_MODEL), BF16),
                 jax.ShapeDtypeStruct((rows_total, LANES), F32),
                 jax.ShapeDtypeStruct((rows_total, LANES), F32),
                 jax.ShapeDtypeStruct((rows_total, LANES), F32),
                 jax.ShapeDtypeStruct((cnt_tiles, 8, LANES), F32)]
    return pl.pallas_call(
        functools.partial(_post_kernel, tm=tm, dils=dils, pooled_given=pooled_given,
                          n_valid_steps=n_valid_steps, aliased=alias_bufs is not None),
        grid=(grid,),
        in_specs=in_specs,
        out_specs=out_specs,
        out_shape=out_shape,
        scratch_shapes=[pltpu.VMEM((GROUP_W // LANES, tm, LANES), F32),
                        pltpu.VMEM((tm, LANES), F32)],
        input_output_aliases=aliases,
        compiler_params=_cparams(1),
        name="post_sample" if per_row else "post",
    )(*args)


def _sort_rows(tm):
    return -(-(TOP_K * tm + N_EXPERTS * (ROW_CHUNK - 1)) // SEL_CHUNK) * SEL_CHUNK


def _for_row_pieces(n_chunks, max_pow, fn):
    big = 1 << max_pow

    def body(c, carry):
        fn(c * big, big)
        return carry

    lax.fori_loop(0, n_chunks >> max_pow, body, 0)
    for pw in range(max_pow - 1, -1, -1):
        @pl.when(((n_chunks >> pw) & 1) == 1)
        def _(pw=pw):
            fn((n_chunks >> (pw + 1)) << (pw + 1), 1 << pw)


SEG_MAX_POW = 3
TILE_MAX_POW = 5


def _moe_sort_kernel(seg_s, goff_s, nch_s, ntot_s, tstart_s, tnch_s,
                     a_ref, idx_ref, h2_ref, segv_ref, lt_ref,
                     xb_hbm, dst_ref, xs_scr, zero_scr, sem, *, tm, n_rows):
    i = pl.program_id(0)
    nt = pl.num_programs(0)
    slot = i % 2
    sel = a_ref[...] > 0.0
    ahead = jnp.dot(lt_ref[...], sel.astype(BF16), preferred_element_type=F32)
    slot1 = jnp.where(sel, segv_ref[0] + ahead + 1.0, 0.0)
    lane = lax.broadcasted_iota(jnp.int32, (tm, LANES), 1).astype(F32)
    idx = idx_ref[...]
    dst = jnp.full((tm, LANES), -1.0, F32)
    for kk in range(TOP_K):
        hit = lane == idx[:, kk:kk + 1]
        dk = jnp.sum(jnp.where(hit, slot1, 0.0), axis=-1, keepdims=True) - 1.0
        dst = jnp.where(lane == float(kk), dk, dst)
    dst_ref[...] = dst
    dst_t = dst.T
    h2 = h2_ref[...]
    local_rows = lax.broadcasted_iota(jnp.int32, (BF16_EXACT, tm), 0).astype(F32).astype(BF16)
    one, zero = jnp.ones((), BF16), jnp.zeros((), BF16)
    for c in range(n_rows // BF16_EXACT):
        p = None
        for kk in range(TOP_K):
            hit = local_rows == (dst_t[kk:kk + 1, :] - float(c * BF16_EXACT)).astype(BF16)
            p = hit if p is None else p | hit
        xs = jnp.dot(jnp.where(p, one, zero), h2, preferred_element_type=F32)
        xs_scr[slot, c * BF16_EXACT:(c + 1) * BF16_EXACT, :] = xs.astype(BF16)

    def rows_copy(buf, src_row, dst_row, n_chunks):
        return pltpu.make_async_copy(
            xs_scr.at[buf, pl.ds(pl.multiple_of(src_row, ROW_CHUNK), n_chunks * ROW_CHUNK)],
            xb_hbm.at[pl.ds(pl.multiple_of(dst_row, ROW_CHUNK), n_chunks * ROW_CHUNK)],
            sem.at[buf])

    def per_expert(e, carry):
        so = seg_s[i * N_EXPERTS + e]
        go = goff_s[i * N_EXPERTS + e]
        _for_row_pieces(
            nch_s[i * N_EXPERTS + e], SEG_MAX_POW,
            lambda off, n: rows_copy(slot, so + off * ROW_CHUNK, go + off * ROW_CHUNK, n).start())
        return carry

    lax.fori_loop(0, N_EXPERTS, per_expert, 0)

    def drain(buf, tile):
        _for_row_pieces(ntot_s[tile], TILE_MAX_POW, lambda off, n: rows_copy(buf, 0, 0, n).wait())

    @pl.when(i > 0)
    def _():
        drain(1 - slot, i - 1)

    @pl.when(i == nt - 1)
    def _():
        drain(slot, i)
        zero_scr[...] = jnp.zeros(zero_scr.shape, zero_scr.dtype)

        def tail_copy(dst_row, n_chunks):
            return pltpu.make_async_copy(
                zero_scr.at[pl.ds(0, n_chunks * ROW_CHUNK)],
                xb_hbm.at[pl.ds(pl.multiple_of(dst_row, ROW_CHUNK), n_chunks * ROW_CHUNK)],
                sem.at[2])

        def per_expert_tail(e, carry):
            _for_row_pieces(tnch_s[e], SEG_MAX_POW,
                            lambda off, n: tail_copy(tstart_s[e] + off * ROW_CHUNK, n).start())
            _for_row_pieces(tnch_s[e], SEG_MAX_POW, lambda off, n: tail_copy(0, n).wait())
            return carry

        lax.fori_loop(0, N_EXPERTS, per_expert_tail, 0)


def _moe_sort(meta, a_all, idx_all, h2_all, *, tm, cap):
    t_all = a_all.shape[0]
    nt = t_all // tm
    n_rows = _sort_rows(tm)
    lt = jnp.tril(jnp.ones((tm, tm), BF16), -1)
    grid_spec = pltpu.PrefetchScalarGridSpec(
        num_scalar_prefetch=6,
        grid=(nt,),
        in_specs=[pl.BlockSpec((tm, LANES), lambda i, *_: (i, 0)),
                  pl.BlockSpec((tm, LANES), lambda i, *_: (i, 0)),
                  pl.BlockSpec((tm, D_MODEL), lambda i, *_: (i, 0)),
                  pl.BlockSpec((1, 1, LANES), lambda i, *_: (i, 0, 0)),
                  pl.BlockSpec((tm, tm), lambda i, *_: (0, 0))],
        out_specs=[pl.BlockSpec(memory_space=pl.ANY),
                   pl.BlockSpec((tm, LANES), lambda i, *_: (i, 0))],
        scratch_shapes=[pltpu.VMEM((2, n_rows, D_MODEL), BF16),
                        pltpu.VMEM(((1 << SEG_MAX_POW) * ROW_CHUNK, D_MODEL), BF16),
                        pltpu.SemaphoreType.DMA((3,))],
    )
    return pl.pallas_call(
        functools.partial(_moe_sort_kernel, tm=tm, n_rows=n_rows),
        grid_spec=grid_spec,
        out_shape=[jax.ShapeDtypeStruct((cap, D_MODEL), BF16),
                   jax.ShapeDtypeStruct((t_all, LANES), F32)],
        compiler_params=_cparams(1),
        name="moe_sort",
    )(meta["seg"], meta["goff"], meta["nch"], meta["ntot"], meta["tstart"], meta["tnch"],
      a_all, idx_all, h2_all, meta["segv"], lt)


def _moe_ffn_kernel(be_s, nused_s, ord_s, next_s, rows_s,
                    x_ref, wgu_hbm, bgu_ref, wd_hbm, bd_ref, y_ref,
                    wgu_f32, wd_f32, wgu_bf, wd_bf, sem):
    b = pl.program_id(0)

    def weight_copies(e, slot):
        return (pltpu.make_async_copy(wgu_hbm.at[e], wgu_f32.at[slot], sem.at[0, slot]),
                pltpu.make_async_copy(wd_hbm.at[e], wd_f32.at[slot], sem.at[1, slot]))

    @pl.when(b < nused_s[0])
    def _():
        e = be_s[b]
        e_prev = be_s[jnp.maximum(b - 1, 0)]
        slot = ord_s[b] % 2

        @pl.when(b == 0)
        def _():
            for cp in weight_copies(e, slot):
                cp.start()

        @pl.when((b == 0) | (e != e_prev))
        def _():
            for cp in weight_copies(e, slot):
                cp.wait()
            e_next = next_s[b]

            @pl.when(e_next >= 0)
            def _():
                for cp in weight_copies(e_next, 1 - slot):
                    cp.start()

            wgu_bf[...] = wgu_f32[slot].astype(BF16)
            wd_bf[...] = wd_f32[slot].astype(BF16)

        def ffn_rows(n):
            hgu = jnp.dot(x_ref[0:n, :], wgu_bf[...], preferred_element_type=F32) + bgu_ref[0]
            d_ff = hgu.shape[1] // 2
            hg = jnp.minimum(hgu[:, :d_ff], SWIGLU_LIMIT)
            hu = jnp.clip(hgu[:, d_ff:], -SWIGLU_LIMIT, SWIGLU_LIMIT)
            act = hg * jax.nn.sigmoid(SWIGLU_ALPHA * hg) * (hu + 1.0)
            y = jnp.dot(act.astype(BF16), wd_bf[...], preferred_element_type=F32) + bd_ref[0]
            y_ref[0:n, :] = y.astype(y_ref.dtype)

        rows_here = rows_s[b]
        for n in range(FFN_SUB, FFN_BLOCK + 1, FFN_SUB):
            pl.when(rows_here == n)(functools.partial(ffn_rows, n))


def _moe_ffn(meta, xb, w_gate_up, b_gate_up, w_down, b_down):
    cap = xb.shape[0]
    nb = cap // FFN_BLOCK
    d_ff2 = w_gate_up.shape[2]

    def blk(b, be, nu, *_):
        return jnp.minimum(b, jnp.maximum(nu[0] - 1, 0))

    def row_blk(b, be, nu, *_):
        return (blk(b, be, nu), 0)

    def expert_blk(b, be, nu, *_):
        return (be[blk(b, be, nu)], 0, 0)

    grid_spec = pltpu.PrefetchScalarGridSpec(
        num_scalar_prefetch=5,
        grid=(nb,),
        in_specs=[pl.BlockSpec((FFN_BLOCK, D_MODEL), row_blk),
                  pl.BlockSpec(memory_space=pl.ANY),
                  pl.BlockSpec((1, 1, d_ff2), expert_blk),
                  pl.BlockSpec(memory_space=pl.ANY),
                  pl.BlockSpec((1, 1, D_MODEL), expert_blk)],
        out_specs=pl.BlockSpec((FFN_BLOCK, D_MODEL), row_blk),
        scratch_shapes=[pltpu.VMEM((2, D_MODEL, d_ff2), F32),
                        pltpu.VMEM((2, d_ff2 // 2, D_MODEL), F32),
                        pltpu.VMEM((D_MODEL, d_ff2), BF16),
                        pltpu.VMEM((d_ff2 // 2, D_MODEL), BF16),
                        pltpu.SemaphoreType.DMA((2, 2))],
    )
    return pl.pallas_call(
        _moe_ffn_kernel,
        grid_spec=grid_spec,
        out_shape=jax.ShapeDtypeStruct((cap, D_MODEL), BF16),
        compiler_params=_cparams(1),
        name="moe_ffn",
    )(meta["block_expert"], meta["n_used"], meta["block_ord"], meta["block_next"],
      meta["block_rows"], xb, w_gate_up,
      b_gate_up.reshape(N_EXPERTS, 1, d_ff2), w_down, b_down.reshape(N_EXPERTS, 1, D_MODEL))


def _moe_unsort_kernel(seg_s, goff_s, nch_s, ntot_s,
                       dst_ref, gk_ref, x1_ref, g2p_ref, g2s_ref, yb_hbm,
                       yp_ref, ys_ref, ybuf, sem, *, tm, n_rows, n_prompt_tiles):
    i = pl.program_id(0)
    nt = pl.num_programs(0)
    slot = i % 2

    def rows_copy(buf, src_row, dst_row, n_chunks):
        return pltpu.make_async_copy(
            yb_hbm.at[pl.ds(pl.multiple_of(src_row, ROW_CHUNK), n_chunks * ROW_CHUNK)],
            ybuf.at[buf, pl.ds(pl.multiple_of(dst_row, ROW_CHUNK), n_chunks * ROW_CHUNK)],
            sem.at[buf])

    def fetch(tile, buf):
        def per_expert(e, carry):
            so = seg_s[tile * N_EXPERTS + e]
            go = goff_s[tile * N_EXPERTS + e]
            _for_row_pieces(
                nch_s[tile * N_EXPERTS + e], SEG_MAX_POW,
                lambda off, n: rows_copy(buf, go + off * ROW_CHUNK, so + off * ROW_CHUNK,
                                         n).start())
            return carry

        lax.fori_loop(0, N_EXPERTS, per_expert, 0)

    @pl.when(i == 0)
    def _():
        ybuf[...] = jnp.zeros(ybuf.shape, ybuf.dtype)
        fetch(0, 0)

    @pl.when(i + 1 < nt)
    def _():
        fetch(i + 1, 1 - slot)

    _for_row_pieces(ntot_s[i], TILE_MAX_POW, lambda off, n: rows_copy(slot, 0, 0, n).wait())

    dst = dst_ref[...]
    gk = gk_ref[...]
    acc = jnp.zeros((tm, D_MODEL), F32)
    for c in range(n_rows // SEL_CHUNK):
        cols = (lax.broadcasted_iota(jnp.int32, (tm, SEL_CHUNK), 1) + c * SEL_CHUNK).astype(F32)
        q = jnp.zeros((tm, SEL_CHUNK), F32)
        for kk in range(TOP_K):
            q = jnp.where(cols == dst[:, kk:kk + 1], gk[:, kk:kk + 1], q)
        acc = acc + jnp.dot(q.astype(BF16), ybuf[slot, c * SEL_CHUNK:(c + 1) * SEL_CHUNK, :],
                            preferred_element_type=F32)

    @pl.when(i < n_prompt_tiles)
    def _():
        yp_ref[...] = x1_ref[...] + g2p_ref[...] * acc

    @pl.when(i >= n_prompt_tiles)
    def _():
        ys_ref[...] = x1_ref[...] + g2s_ref[...] * acc


def _moe_unsort(meta, dst_all, gk_all, x1_all, gt2_p, gt2_s, yb, *, tm, n_prompt_tiles):
    t_all = dst_all.shape[0]
    nt = t_all // tm
    n_rows = _sort_rows(tm)
    last_p = n_prompt_tiles - 1
    grid_spec = pltpu.PrefetchScalarGridSpec(
        num_scalar_prefetch=4,
        grid=(nt,),
        in_specs=[pl.BlockSpec((tm, LANES), lambda i, *_: (i, 0)),
                  pl.BlockSpec((tm, LANES), lambda i, *_: (i, 0)),
                  pl.BlockSpec((tm, D_MODEL), lambda i, *_: (i, 0)),
                  pl.BlockSpec((1, D_MODEL), lambda i, *_: (0, 0)),
                  pl.BlockSpec((tm, D_MODEL), lambda i, *_: (0, 0)),
                  pl.BlockSpec(memory_space=pl.ANY)],
        out_specs=[pl.BlockSpec((tm, D_MODEL), lambda i, *_: (jnp.minimum(i, last_p), 0)),
                   pl.BlockSpec((tm, D_MODEL), lambda i, *_: (0, 0))],
        scratch_shapes=[pltpu.VMEM((2, n_rows, D_MODEL), BF16),
                        pltpu.SemaphoreType.DMA((2,))],
    )
    return pl.pallas_call(
        functools.partial(_moe_unsort_kernel, tm=tm, n_rows=n_rows,
                          n_prompt_tiles=n_prompt_tiles),
        grid_spec=grid_spec,
        out_shape=[jax.ShapeDtypeStruct((n_prompt_tiles * tm, D_MODEL), F32),
                   jax.ShapeDtypeStruct((tm, D_MODEL), F32)],
        compiler_params=_cparams(1),
        name="moe_unsort",
    )(meta["seg"], meta["goff"], meta["nch"], meta["ntot"],
      dst_all, gk_all, x1_all, gt2_p, gt2_s, yb)


def _moe_meta(cnt, tm):
    nt = cnt.shape[0]
    cnt = cnt.astype(jnp.int32)
    cnt_pad = (cnt + ROW_CHUNK - 1) // ROW_CHUNK * ROW_CHUNK
    seg = jnp.cumsum(cnt_pad, axis=1) - cnt_pad
    rows_e = jnp.sum(cnt_pad, axis=0)
    region = (rows_e + FFN_BLOCK - 1) // FFN_BLOCK * FFN_BLOCK
    gstart = jnp.cumsum(region) - region
    goff = gstart[None, :] + jnp.cumsum(cnt_pad, axis=0) - cnt_pad
    nblk_e = region // FFN_BLOCK
    blk_end = jnp.cumsum(nblk_e)
    cap = _moe_cap(nt * tm, tm)
    blocks = jnp.arange(cap // FFN_BLOCK, dtype=jnp.int32)
    block_expert = jnp.minimum(
        jnp.sum((blk_end[None, :] <= blocks[:, None]).astype(jnp.int32), axis=1), N_EXPERTS - 1)
    segv = jnp.zeros((nt, 1, LANES), F32).at[:, 0, :N_EXPERTS].set(seg.astype(F32))
    used = nblk_e > 0
    experts = jnp.arange(N_EXPERTS, dtype=jnp.int32)
    ord_e = jnp.cumsum(used.astype(jnp.int32)) - 1
    later = (experts[None, :] > experts[:, None]) & used[None, :]
    next_e = jnp.min(jnp.where(later, experts[None, :], N_EXPERTS), axis=1)
    next_e = jnp.where(next_e < N_EXPERTS, next_e, -1).astype(jnp.int32)
    of_block = (block_expert[:, None] == experts[None, :]).astype(jnp.int32)
    first_block = jnp.sum(of_block * (blk_end - nblk_e)[None, :], axis=1)
    rows_left = jnp.sum(of_block * rows_e[None, :], axis=1) - (blocks - first_block) * FFN_BLOCK
    block_rows = jnp.clip((rows_left + FFN_SUB - 1) // FFN_SUB * FFN_SUB, FFN_SUB, FFN_BLOCK)
    return {
        "block_rows": block_rows.astype(jnp.int32),
        "block_ord": jnp.sum(of_block * ord_e[None, :].astype(jnp.int32), axis=1),
        "block_next": jnp.sum(of_block * next_e[None, :], axis=1),
        "seg": seg.reshape(-1), "goff": goff.reshape(-1).astype(jnp.int32),
        "nch": (cnt_pad // ROW_CHUNK).reshape(-1),
        "ntot": jnp.sum(cnt_pad, axis=1) // ROW_CHUNK,
        "tstart": (gstart + rows_e).astype(jnp.int32),
        "tnch": (region - rows_e) // ROW_CHUNK,
        "block_expert": block_expert,
        "n_used": blk_end[-1:].astype(jnp.int32),
        "segv": segv,
    }


def _moe_cap(t_all, tm):
    nt = t_all // tm
    worst = TOP_K * t_all + nt * N_EXPERTS * (ROW_CHUNK - 1) + N_EXPERTS * (FFN_BLOCK - ROW_CHUNK)
    return -(-worst // FFN_BLOCK) * FFN_BLOCK


def _t5_bucket(dist):
    max_exact = NUM_BUCKETS // 2
    d = dist.astype(jnp.int32)
    ratio = (jnp.log(jnp.maximum(d, 1).astype(F32) / max_exact)
             / math.log(MAX_DISTANCE / max_exact))
    large = jnp.minimum(max_exact + (ratio * (NUM_BUCKETS - max_exact)).astype(jnp.int32),
                        NUM_BUCKETS - 1)
    return jnp.where(d < max_exact, d, large)


def _step_bias(tab, dil):
    return tab[_t5_bucket(dil * jnp.arange(ATT_BLK + 1))].astype(F32).T


def _band_table(sb):
    return jnp.concatenate([sb[:, ::-1], jnp.full((HEADS, ATT_BLK - 1), NEG_INF, F32)], axis=1)


def _cache_table(sb, dil):
    on_grid = sb[:, :0:-1]
    if dil == 1:
        return on_grid
    off = jnp.full((HEADS, ATT_BLK, dil - 1), NEG_INF, F32)
    return jnp.concatenate([on_grid[:, :, None], off], axis=2).reshape(HEADS, ATT_BLK * dil)


def kernel(x_prompt, x_sample, cache_kv_w128, cache_kv_w512, cache_kv_w2048, state_pool, c_prompt,
           c_sample, w_ada, b_ada, norm_mix_g, norm_ffn_g, w_in, q_norm_g, k_norm_g, rel_bias,
           w_pool_mix, pool_scale, w_up_attn, w_up_pool, w_out, w_router, b_router, w_gate_up,
           b_gate_up, w_down, b_down):
    assert w_ada.shape[0] == 1, "one layer"
    seq = x_prompt.shape[1]
    n_s = x_sample.shape[0]
    assert x_prompt.shape[0] == 1 and x_sample.shape[1] == 1
    assert seq % (DIL_GROUPS[-1][1] * ATT_BLK * ATT_SUB) == 0 and seq % TM_PROMPT == 0
    assert n_s == TM_SAMPLE
    dils = tuple(d for _, d in DIL_GROUPS)
    caches = (cache_kv_w128, cache_kv_w512, cache_kv_w2048)

    w_in_bf = w_in[0].astype(BF16)
    heads_of = jnp.arange(GROUP_W) // HEAD_DIM
    half_heads = heads_of[:GROUP_W // 2]
    bdiag = (half_heads[:, None] == half_heads[None, :]).astype(BF16)
    qg = (jnp.tile(q_norm_g[0], HEADS) * SCALE).reshape(1, GROUP_W)
    kg = jnp.tile(k_norm_g[0], HEADS).reshape(1, GROUP_W)
    expand = ((jnp.arange(LANES)[:, None] % HEADS == heads_of[None, :])
              & (jnp.arange(LANES)[:, None] < 2 * HEADS)).astype(BF16)
    wr = jnp.zeros((D_MODEL, LANES), F32).at[:, :N_EXPERTS].set(w_router[0])
    wr_hi = wr.astype(BF16)
    wr_lo = (wr - wr_hi.astype(F32)).astype(BF16)
    br = jnp.full((1, LANES), NEG_INF, F32).at[0, :N_EXPERTS].set(b_router[0])
    wts = (w_pool_mix[0].astype(BF16), pool_scale[0].reshape(1, POOL_W),
           w_up_attn[0].astype(BF16), w_up_pool[0].astype(BF16), w_out[0].astype(BF16), expand,
           norm_ffn_g[0].reshape(1, D_MODEL), wr_hi, wr_lo, br)
    g1 = norm_mix_g[0].reshape(1, D_MODEL)

    n_c = 1 + n_s
    c_all = jnp.zeros((-(-n_c // 8) * 8, D_MODEL), F32).at[0:1].set(c_prompt).at[1:n_c].set(c_sample)
    mod = _ada(c_all, w_ada[0], b_ada[0])
    sh1, sc1, gt1, sh2, sc2, gt2 = jnp.split(mod, N_ADA, axis=-1)

    def prow(m):
        return m[0:1]

    def srows(m):
        return m[1:n_c]

    xp = x_prompt[0]
    q_p, k_p, v_p, u_p, sga_p, sgp_p, st_p = _proj(
        xp, g1, prow(sc1), prow(sh1), w_in_bf, bdiag, qg, kg,
        tm=TM_PROMPT, dils=dils, per_row=False)
    xs = x_sample[:, 0]
    ones = (1, 1, 1)
    q_s, _, _, u_s, sga_s, sgp_s, st_s = _proj(
        xs, g1, srows(sc1), srows(sh1), w_in_bf, bdiag, qg, kg,
        tm=TM_SAMPLE, dils=ones, per_row=True)
    pooled_s, pool_state_t = _pool_sample(jnp.transpose(state_pool, (0, 2, 1, 3)), u_s)
    pool_state_s = jnp.transpose(pool_state_t, (0, 2, 1, 3))

    def heads(a):
        return a.astype(F32).reshape(n_s, HEADS, HEAD_DIM)

    step_bias = [_step_bias(rel_bias[:, g * HEADS:(g + 1) * HEADS], d)
                 for g, (_, d) in enumerate(DIL_GROUPS)]

    def sample_group(g, part, n_parts):
        cache_t = jnp.transpose(caches[g], (0, 1, 3, 4, 5, 2))
        n_sub = n_s // n_parts
        return (heads(q_s[g]), heads(st_s[g][:, :GROUP_W]), heads(st_s[g][:, GROUP_W:]),
                step_bias[g][:, 0:1], _cache_table(step_bias[g], DIL_GROUPS[g][1]), cache_t,
                part * n_sub, n_sub)

    o_p, lse_p = [], []
    o_parts = [[None] * n for n in SAMPLE_PARTS]
    lse_parts = [[None] * n for n in SAMPLE_PARTS]
    for g, (_, d) in enumerate(DIL_GROUPS):
        hosted = SAMPLE_HOST[g]
        o, lse, sample_outs = _attn(
            q_p[g], k_p[g], v_p[g], _band_table(step_bias[g]), d,
            [sample_group(sg, part, SAMPLE_PARTS[sg]) for sg, part in hosted])
        o_p.append(o)
        lse_p.append(lse)
        for (sg, part), (os_g, lses_g) in zip(hosted, sample_outs):
            o_parts[sg][part] = os_g.reshape(-1, GROUP_W)
            lse_parts[sg][part] = lses_g[:, :, 0]
    o_s = [jnp.concatenate(parts, axis=0) for parts in o_parts]
    lse_s = [jnp.zeros((n_s, LANES), F32).at[:, :HEADS].set(jnp.concatenate(parts, axis=0))
             for parts in lse_parts]

    nt_p = seq // TM_PROMPT
    t_all = seq + TM_PROMPT
    bufs = _post(xp, o_p, lse_p, (u_p,), sga_p, sgp_p, wts, (prow(gt1), prow(sc2), prow(sh2)),
                 tm=TM_PROMPT, dils=dils, per_row=False, rows_total=t_all, row_block0=0,
                 cnt_tiles=nt_p + 1, cnt_block=None, grid=nt_p, n_valid_steps=None,
                 alias_bufs=None)
    bufs = _post(xs, o_s, lse_s, pooled_s, sga_s, sgp_s, wts,
                 (srows(gt1), srows(sc2), srows(sh2)),
                 tm=TM_SAMPLE, dils=ones, per_row=True, rows_total=t_all,
                 row_block0=seq // TM_SAMPLE, cnt_tiles=nt_p + 1, cnt_block=nt_p,
                 grid=TM_PROMPT // TM_SAMPLE, n_valid_steps=1, alias_bufs=bufs)
    x1_all, h2_all, a_all, idx_all, gk_all, cnt = bufs

    meta = _moe_meta(cnt[:, 0, :N_EXPERTS], TM_PROMPT)
    cap = _moe_cap(t_all, TM_PROMPT)
    xb, dst_all = _moe_sort(meta, a_all, idx_all, h2_all, tm=TM_PROMPT, cap=cap)
    yb = _moe_ffn(meta, xb, w_gate_up[0], b_gate_up[0], w_down[0], b_down[0])
    gt2_s = jnp.zeros((TM_PROMPT, D_MODEL), F32).at[:n_s].set(srows(gt2))
    y_p, y_s = _moe_unsort(meta, dst_all, gk_all, x1_all, prow(gt2), gt2_s, yb,
                           tm=TM_PROMPT, n_prompt_tiles=nt_p)

    def kv_state(st, rows):
        return st.reshape(1, 1, rows, 2, HEADS, HEAD_DIM)

    kv_p = [kv_state(st, st.shape[0]) for st in st_p]
    kv_s = [st.reshape(1, n_s, 1, 2, HEADS, HEAD_DIM) for st in st_s]
    pool_p = u_p[seq - POOL_BUF:].reshape(1, 1, POOL_BUF, POOL_W)
    return (y_p.reshape(1, seq, D_MODEL), y_s[:n_s].reshape(n_s, 1, D_MODEL),
            kv_p[0], kv_p[1], kv_p[2], pool_p, kv_s[0], kv_s[1], kv_s[2], pool_state_s)
```

```python
import functools
import math

import jax
import jax.numpy as jnp
from jax import lax
from jax.experimental import pallas as pl
from jax.experimental.pallas import tpu as pltpu

F32 = jnp.float32
BF16 = jnp.bfloat16

D_MODEL = 1024
HEAD_DIM = 64
HEADS = 8
GROUP_W = HEADS * HEAD_DIM
DIL_GROUPS = ((128, 1), (512, 4), (2048, 16))
N_GROUPS = len(DIL_GROUPS)
QKV_W = N_GROUPS * GROUP_W
ATT_BLK = 128
ATT_SUB = 4
POOL_WINDOWS = (2, 4, 8, 16)
POOL_W = 512
POOL_GW = 128
POOL_BUF = 15
POOL_HALO = 16
OFF_K, OFF_V = QKV_W, 2 * QKV_W
OFF_U = 3 * QKV_W
OFF_GA = OFF_U + POOL_W
OFF_GP = OFF_GA + D_MODEL
IN_W = OFF_GP + D_MODEL
NUM_BUCKETS = 32
MAX_DISTANCE = 2048
N_EXPERTS = 32
TOP_K = 4
SWIGLU_LIMIT = 7.0
SWIGLU_ALPHA = 1.702
N_ADA = 6
EPS = 1e-6
NEG_INF = -1e30
PAST_LEN = 8192
SCALE = HEAD_DIM ** -0.5

LANES = 128
ROW_CHUNK = 16
TM_PROMPT = 512
TM_SAMPLE = 128
FFN_BLOCK = 1024
FFN_SUB = 128
SEL_CHUNK = 512
BF16_EXACT = 256
VMEM_LIMIT = 56 * 1024 * 1024


def _cparams(n_axes):
    return pltpu.CompilerParams(dimension_semantics=("arbitrary",) * n_axes,
                                vmem_limit_bytes=VMEM_LIMIT)


def _const_spec(shape):
    nd = len(shape)
    return pl.BlockSpec(shape, lambda *_: (0,) * nd)


def _ada_kernel(c_ref, w_ref, b_ref, o_ref):
    c = c_ref[...]
    s = c * jax.nn.sigmoid(c)
    o_ref[...] = jnp.dot(s.astype(BF16), w_ref[...].astype(BF16),
                         preferred_element_type=F32) + b_ref[...]


def _ada(c_all, w_ada, b_ada):
    rows = c_all.shape[0]
    n = w_ada.shape[1]
    tn = 1536
    return pl.pallas_call(
        _ada_kernel,
        grid=(n // tn,),
        in_specs=[pl.BlockSpec((rows, D_MODEL), lambda j: (0, 0)),
                  pl.BlockSpec((D_MODEL, tn), lambda j: (0, j)),
                  pl.BlockSpec((1, tn), lambda j: (0, j))],
        out_specs=pl.BlockSpec((rows, tn), lambda j: (0, j)),
        out_shape=jax.ShapeDtypeStruct((rows, n), F32),
        compiler_params=_cparams(1),
        name="ada",
    )(c_all, w_ada, b_ada.reshape(1, n))


def _proj_kernel(x_ref, g_ref, sc_ref, sh_ref, w_ref, bd_ref, qg_ref, kg_ref,
                 *refs, tm, dils, st_rows):
    q_refs, k_refs, v_refs = refs[0:3], refs[3:6], refs[6:9]
    u_ref, sga_ref, sgp_ref = refs[9:12]
    st_refs = refs[12:15]
    scr = refs[15]

    x = x_ref[...]
    ms = jnp.mean(x * x, axis=-1, keepdims=True)
    h = x * lax.rsqrt(ms + EPS) * g_ref[...] * (1.0 + sc_ref[...]) + sh_ref[...]
    hb = h.astype(BF16)

    def proj(off, width):
        return jnp.dot(hb, w_ref[:, off:off + width], preferred_element_type=F32)

    def head_norm(z, gain_ref):
        zz = (z * z).astype(BF16)
        half = GROUP_W // 2
        ss = jnp.concatenate(
            [jnp.dot(zz[:, :half], bd_ref[...], preferred_element_type=F32),
             jnp.dot(zz[:, half:], bd_ref[...], preferred_element_type=F32)], axis=1)
        return z * lax.rsqrt(ss * (1.0 / HEAD_DIM) + EPS) * gain_ref[...]

    def put(out_ref, val, d):
        if d == 1:
            out_ref[...] = val.astype(out_ref.dtype)
        else:
            for c in range(GROUP_W // LANES):
                scr[c] = val[:, c * LANES:(c + 1) * LANES]
            for r in range(d):
                for c in range(GROUP_W // LANES):
                    col = r * GROUP_W + c * LANES
                    out_ref[:, col:col + LANES] = (
                        scr[c, pl.ds(r, tm // d, stride=d), :].astype(out_ref.dtype))

    for g, d in enumerate(dils):
        qn = head_norm(proj(g * GROUP_W, GROUP_W), qg_ref)
        put(q_refs[g], qn, d)
        kn = head_norm(proj(OFF_K + g * GROUP_W, GROUP_W), kg_ref)
        put(k_refs[g], kn, d)
        v = proj(OFF_V + g * GROUP_W, GROUP_W)
        put(v_refs[g], v, d)
        rb = st_rows[g]
        st_refs[g][:, 0:GROUP_W] = kn[tm - rb:, :]
        st_refs[g][:, GROUP_W:2 * GROUP_W] = v[tm - rb:, :]

    u_ref[...] = proj(OFF_U, POOL_W)
    sga_ref[...] = jax.nn.sigmoid(proj(OFF_GA, D_MODEL)).astype(BF16)
    sgp_ref[...] = jax.nn.sigmoid(proj(OFF_GP, D_MODEL)).astype(BF16)


def _mod_spec(per_row, tm):
    if per_row:
        return pl.BlockSpec((tm, D_MODEL), lambda i: (i, 0))
    return pl.BlockSpec((1, D_MODEL), lambda i: (0, 0))


def _proj(x, g1, sc1, sh1, w_in_bf, bdiag, qg, kg, *, tm, dils, per_row):
    s = x.shape[0]
    nt = s // tm
    wins = tuple(min(w, s) for w, _ in DIL_GROUPS)
    st_rows = tuple(min(tm, w) for w in wins)

    def res_spec(d):
        return pl.BlockSpec((tm // d, d * GROUP_W), lambda i: (i, 0))

    def st_spec(w, rb):
        first = nt - w // rb
        return pl.BlockSpec((rb, 2 * GROUP_W), lambda i: (jnp.maximum(i - first, 0), 0))

    qkv_shapes = [jax.ShapeDtypeStruct((s // d, d * GROUP_W), BF16) for d in dils]
    out_shape = (qkv_shapes * 3
                 + [jax.ShapeDtypeStruct((s, POOL_W), F32),
                    jax.ShapeDtypeStruct((s, D_MODEL), BF16),
                    jax.ShapeDtypeStruct((s, D_MODEL), BF16)]
                 + [jax.ShapeDtypeStruct((w, 2 * GROUP_W), F32) for w in wins])
    out_specs = ([res_spec(d) for d in dils] * 3
                 + [pl.BlockSpec((tm, POOL_W), lambda i: (i, 0)),
                    pl.BlockSpec((tm, D_MODEL), lambda i: (i, 0)),
                    pl.BlockSpec((tm, D_MODEL), lambda i: (i, 0))]
                 + [st_spec(w, rb) for w, rb in zip(wins, st_rows)])
    in_specs = [pl.BlockSpec((tm, D_MODEL), lambda i: (i, 0)),
                _const_spec((1, D_MODEL)),
                _mod_spec(per_row, tm), _mod_spec(per_row, tm),
                pl.BlockSpec((D_MODEL, IN_W), lambda i: (0, 0), pipeline_mode=pl.Buffered(1)),
                _const_spec((GROUP_W // 2, GROUP_W // 2)),
                _const_spec((1, GROUP_W)), _const_spec((1, GROUP_W))]
    outs = pl.pallas_call(
        functools.partial(_proj_kernel, tm=tm, dils=dils, st_rows=st_rows),
        grid=(nt,),
        in_specs=in_specs,
        out_specs=out_specs,
        out_shape=out_shape,
        scratch_shapes=[pltpu.VMEM((GROUP_W // LANES, tm, LANES), F32)],
        compiler_params=_cparams(1),
        name="proj",
    )(x, g1, sc1, sh1, w_in_bf, bdiag, qg, kg)
    return outs[0:3], outs[3:6], outs[6:9], outs[9], outs[10], outs[11], outs[12:15]


def _sample_group_attn(qs_ref, kns_ref, vns_ref, bself_ref, btab_ref, c_ref, os_ref, lses_ref):
    n_tok = qs_ref.shape[0]
    win = btab_ref.shape[1]
    row_w = lax.broadcasted_iota(jnp.int32, (HEADS, win), 0)
    row_e = lax.broadcasted_iota(jnp.int32, (HEADS, HEAD_DIM), 0)
    ss, s0s = [], []
    for t in range(n_tok):
        q = qs_ref[t]
        qb = q.astype(BF16)
        s = jnp.zeros((HEADS, win), F32)
        for h in range(HEADS):
            sh = jnp.dot(qb, c_ref[0, t, 0, h].astype(BF16), preferred_element_type=F32)
            s = jnp.where(row_w == h, sh, s)
        ss.append(s + btab_ref[...])
        s0s.append(jnp.sum(q * kns_ref[t], axis=-1, keepdims=True) + bself_ref[...])
    s = jnp.concatenate(ss, axis=0)
    s0 = jnp.concatenate(s0s, axis=0)
    m = jnp.maximum(jnp.max(s, axis=-1, keepdims=True), s0)
    p = jnp.exp(s - m)
    p0 = jnp.exp(s0 - m)
    l = jnp.sum(p, axis=-1, keepdims=True) + p0
    pb = p.astype(BF16)
    lse = m + jnp.log(l)
    for t in range(n_tok):
        rows = slice(t * HEADS, (t + 1) * HEADS)
        o = jnp.zeros((HEADS, HEAD_DIM), F32)
        for h in range(HEADS):
            oh = lax.dot_general(pb[rows], c_ref[0, t, 1, h].astype(BF16),
                                 (((1,), (1,)), ((), ())), preferred_element_type=F32)
            o = jnp.where(row_e == h, oh, o)
        os_ref[t] = (o + p0[rows] * vns_ref[t]) / l[rows]
        lses_ref[t] = lse[rows]


SAMPLE_IN = 6
SAMPLE_PARTS = (1, 1, 2)
SAMPLE_HOST = (((2, 1),), ((1, 0),), ((2, 0), (0, 0)))


def _attn_kernel(q_ref, kp_ref, kc_ref, vp_ref, vc_ref, r_ref, *refs, n_sample_groups):
    n_in = SAMPLE_IN * n_sample_groups
    o_ref, lse_ref = refs[n_in], refs[n_in + 1]
    bias_ref = refs[-1]
    i = pl.program_id(1)

    for sg in range(n_sample_groups):
        _sample_group_attn(*refs[SAMPLE_IN * sg:SAMPLE_IN * (sg + 1)],
                           *refs[n_in + 2 + 2 * sg:n_in + 4 + 2 * sg])

    @pl.when((pl.program_id(0) == 0) & (i == 0))
    def _():
        for h in range(HEADS):
            row = jnp.broadcast_to(r_ref[h:h + 1, :], (ATT_BLK, 2 * ATT_BLK))
            bias_ref[h] = pltpu.roll(row, 0, 1, stride=1, stride_axis=0)

    q = q_ref[...]
    k = jnp.concatenate([kp_ref[...], kc_ref[...]], axis=0)
    v = jnp.concatenate([vp_ref[...], vc_ref[...]], axis=0)
    col = lax.broadcasted_iota(jnp.int32, (ATT_BLK, 2 * ATT_BLK), 1)
    no_prev = jnp.where((col < ATT_BLK) & (i == 0), NEG_INF, 0.0)
    lane_q = lax.broadcasted_iota(jnp.int32, (ATT_SUB * ATT_BLK, LANES), 1)
    lane_v = lax.broadcasted_iota(jnp.int32, ((ATT_SUB + 1) * ATT_BLK, LANES), 1)

    def pair(h):
        return slice((h // 2) * LANES, (h // 2 + 1) * LANES)

    def mine(lane, h):
        return (lane < HEAD_DIM) == (h % 2 == 0)

    ss = []
    for h in range(HEADS):
        q2 = q[:, pair(h)]
        qm = jnp.where(mine(lane_q, h), q2, jnp.zeros_like(q2))
        k2 = k[:, pair(h)]
        for j in range(ATT_SUB):
            s = lax.dot_general(qm[j * ATT_BLK:(j + 1) * ATT_BLK], k2[j * ATT_BLK:(j + 2) * ATT_BLK],
                                (((1,), (1,)), ((), ())), preferred_element_type=F32)
            s = s + bias_ref[h]
            ss.append(s + no_prev if j == 0 else s)
    s = jnp.concatenate(ss, axis=0)
    m = jnp.max(s, axis=-1, keepdims=True)
    p = jnp.exp(s - m)
    l = jnp.sum(p, axis=-1, keepdims=True)
    pb = p.astype(BF16)
    lse = m + jnp.log(l)
    inv_l = 1.0 / l
    outs, lses = [], []
    for h in range(HEADS):
        v2 = v[:, pair(h)]
        vm = jnp.where(mine(lane_v, h), v2, jnp.zeros_like(v2))
        o_sub, lse_sub = [], []
        for j in range(ATT_SUB):
            rows = slice((h * ATT_SUB + j) * ATT_BLK, (h * ATT_SUB + j + 1) * ATT_BLK)
            o_sub.append(jnp.dot(pb[rows], vm[j * ATT_BLK:(j + 2) * ATT_BLK],
                                 preferred_element_type=F32) * inv_l[rows])
            lse_sub.append(lse[rows])
        o = jnp.concatenate(o_sub, axis=0)
        if h % 2 == 0:
            outs.append(o)
        else:
            outs[-1] = outs[-1] + o
        lses.append(jnp.concatenate(lse_sub, axis=0))
    o_ref[...] = jnp.concatenate(outs, axis=-1).astype(o_ref.dtype)
    lse_ref[...] = jnp.concatenate(
        lses + [jnp.zeros((ATT_SUB * ATT_BLK, LANES - HEADS), F32)], axis=-1)


def _attn(q, k, v, r_tab, d, sample_groups):
    rows = q.shape[0]
    step = ATT_SUB * ATT_BLK
    nblk = rows // step
    cur = pl.BlockSpec((step, GROUP_W), lambda r, i: (i, r))
    prev = pl.BlockSpec((ATT_BLK, GROUP_W), lambda r, i: (jnp.maximum(i * ATT_SUB - 1, 0), r))
    in_specs = [cur, prev, cur, prev, cur,
                pl.BlockSpec((HEADS, 2 * ATT_BLK), lambda r, i: (0, 0))]
    out_specs = [pl.BlockSpec((step, GROUP_W), lambda r, i: (i, r)),
                 pl.BlockSpec((step, LANES), lambda r, i: (i, r))]
    out_shape = [jax.ShapeDtypeStruct((rows, d * GROUP_W), BF16),
                 jax.ShapeDtypeStruct((rows, d * LANES), F32)]
    args = [q, k, k, v, v, r_tab]
    for qs, kns, vns, bself, btab, cache_t, tok_start, n_tok in sample_groups:
        assert n_tok % (d * nblk) == 0
        tok = n_tok // (d * nblk)
        assert tok_start % tok == 0
        first = tok_start // tok

        def in_idx(r, i, first=first):
            return (first + r * nblk + i, 0, 0)

        def out_idx(r, i):
            return (r * nblk + i, 0, 0)

        tok_in = pl.BlockSpec((tok, HEADS, HEAD_DIM), in_idx)
        in_specs += [tok_in, tok_in, tok_in,
                     pl.BlockSpec(bself.shape, lambda r, i: (0, 0)),
                     pl.BlockSpec(btab.shape, lambda r, i: (0, 0)),
                     pl.BlockSpec((1, tok) + cache_t.shape[2:],
                                  lambda r, i, first=first: (0, first + r * nblk + i, 0, 0, 0, 0))]
        out_specs += [pl.BlockSpec((tok, HEADS, HEAD_DIM), out_idx),
                      pl.BlockSpec((tok, HEADS, 1), out_idx)]
        out_shape += [jax.ShapeDtypeStruct((n_tok, HEADS, HEAD_DIM), F32),
                      jax.ShapeDtypeStruct((n_tok, HEADS, 1), F32)]
        args += [qs, kns, vns, bself, btab, cache_t]
    outs = pl.pallas_call(
        functools.partial(_attn_kernel, n_sample_groups=len(sample_groups)),
        grid=(d, nblk),
        in_specs=in_specs,
        out_specs=out_specs,
        out_shape=out_shape,
        scratch_shapes=[pltpu.VMEM((HEADS, ATT_BLK, 2 * ATT_BLK), F32)],
        compiler_params=_cparams(2),
        name=f"attn_d{d}",
    )(*args)
    return outs[0], outs[1], [(outs[2 + 2 * j], outs[3 + 2 * j]) for j in range(len(sample_groups))]


def _pool_sample_kernel(st_ref, u_ref, pooled_ref, new_ref):
    u = u_ref[...]
    rows = [st_ref[0, j] for j in range(POOL_BUF)]
    outs = []
    for g, w in enumerate(POOL_WINDOWS):
        sl = slice(g * POOL_GW, (g + 1) * POOL_GW)
        acc = u[:, sl]
        for j in range(POOL_BUF - (w - 1), POOL_BUF):
            acc = acc + rows[j][:, sl]
        outs.append(acc / float(w) - u[:, sl])
    pooled_ref[...] = jnp.concatenate(outs, axis=-1)
    for j in range(POOL_BUF - 1):
        new_ref[0, j] = rows[j + 1]
    new_ref[0, POOL_BUF - 1] = u


def _pool_sample(state, u):
    n = u.shape[0]
    return pl.pallas_call(
        _pool_sample_kernel,
        grid=(1,),
        in_specs=[_const_spec(state.shape), _const_spec(u.shape)],
        out_specs=[_const_spec(u.shape), _const_spec(state.shape)],
        out_shape=[jax.ShapeDtypeStruct((n, POOL_W), F32),
                   jax.ShapeDtypeStruct(state.shape, F32)],
        compiler_params=_cparams(1),
        name="pool_sample",
    )(state, u)


def _post_kernel(*refs, tm, dils, pooled_given, n_valid_steps, aliased):
    it = iter(refs)
    x_ref = next(it)
    o_refs = [next(it) for _ in range(N_GROUPS)]
    lse_refs = [next(it) for _ in range(N_GROUPS)]
    if pooled_given:
        pooled_ref = next(it)
    else:
        u_ref, uh_ref = next(it), next(it)
    sga_ref, sgp_ref = next(it), next(it)
    wpm_ref, psc_ref, wua_ref, wup_ref, wout_ref, exp_ref = (next(it) for _ in range(6))
    gt1_ref, g2_ref, sc2_ref, sh2_ref = (next(it) for _ in range(4))
    wrh_ref, wrl_ref, br_ref = (next(it) for _ in range(3))
    if aliased:
        for _ in range(6):
            next(it)
    x1_ref, h2_ref, a_ref, idx_ref, gk_ref, cnt_ref = (next(it) for _ in range(6))
    ob_scr, ls_scr = next(it), next(it)

    i = pl.program_id(0)

    def compute():
        obs, lss = [], []
        for g, d in enumerate(dils):
            if d == 1:
                obs.append(o_refs[g][...].astype(F32))
                lss.append(lse_refs[g][...])
            else:
                for r in range(d):
                    for c in range(GROUP_W // LANES):
                        col = r * GROUP_W + c * LANES
                        ob_scr[c, pl.ds(r, tm // d, stride=d), :] = (
                            o_refs[g][:, col:col + LANES].astype(F32))
                    ls_scr[pl.ds(r, tm // d, stride=d), :] = (
                        lse_refs[g][:, r * LANES:(r + 1) * LANES])
                obs.append(jnp.concatenate([ob_scr[c] for c in range(GROUP_W // LANES)],
                                           axis=-1))
                lss.append(ls_scr[...])
        mx = jnp.maximum(jnp.maximum(lss[0], lss[1]), lss[2])
        es = [jnp.exp(l - mx) for l in lss]
        den = es[0] + es[1] + es[2]
        attn_o = jnp.zeros((tm, GROUP_W), F32)
        head_lane = lax.broadcasted_iota(jnp.int32, (tm, LANES), 1) < HEADS
        for g in range(N_GROUPS):
            w = jnp.where(head_lane, es[g] / den, 0.0)
            w_hi = w.astype(BF16).astype(F32)
            w_lo = (w - w_hi).astype(BF16).astype(F32)
            lhs = (w_hi + pltpu.roll(w_lo, HEADS, 1)).astype(BF16)
            wexp = jnp.dot(lhs, exp_ref[...], preferred_element_type=F32)
            attn_o = attn_o + wexp * obs[g]

        if pooled_given:
            pooled = pooled_ref[...]
        else:
            u = u_ref[...]
            halo = jnp.where(i == 0, 0.0, uh_ref[...])
            pos = (lax.broadcasted_iota(jnp.int32, (tm, 1), 0) + i * tm + 1).astype(F32)
            outs = []
            for g, w in enumerate(POOL_WINDOWS):
                sl = slice(g * POOL_GW, (g + 1) * POOL_GW)
                a = jnp.concatenate([halo[:, sl], u[:, sl]], axis=0)
                span = 1
                while span < w:
                    n = a.shape[0] - span
                    a = a[span:, :] + a[:n, :]
                    span *= 2
                off = a.shape[0] - tm
                win_sum = a[off:, :]
                outs.append(win_sum / jnp.minimum(pos, float(w)) - u[:, sl])
            pooled = jnp.concatenate(outs, axis=-1)
        pool_parts = []
        for g in range(len(POOL_WINDOWS)):
            sl = slice(g * POOL_GW, (g + 1) * POOL_GW)
            pool_parts.append(jnp.dot(pooled[:, sl].astype(BF16), wpm_ref[g],
                                      preferred_element_type=F32))
        pool_o = jnp.concatenate(pool_parts, axis=-1) * psc_ref[...]

        up_a = jnp.dot(attn_o.astype(BF16), wua_ref[...], preferred_element_type=F32)
        up_p = jnp.dot(pool_o.astype(BF16), wup_ref[...], preferred_element_type=F32)
        merged = sga_ref[...].astype(F32) * up_a + sgp_ref[...].astype(F32) * up_p
        mo = jnp.dot(merged.astype(BF16), wout_ref[...], preferred_element_type=F32)
        x1 = x_ref[...] + gt1_ref[...] * mo
        x1_ref[...] = x1

        ms = jnp.mean(x1 * x1, axis=-1, keepdims=True)
        h2 = x1 * lax.rsqrt(ms + EPS) * g2_ref[...] * (1.0 + sc2_ref[...]) + sh2_ref[...]
        h2_hi = h2.astype(BF16)
        h2_ref[...] = h2_hi
        h2_lo = (h2 - h2_hi.astype(F32)).astype(BF16)
        logits = (jnp.dot(h2_hi, wrh_ref[...], preferred_element_type=F32)
                  + jnp.dot(h2_lo, wrh_ref[...], preferred_element_type=F32)
                  + jnp.dot(h2_hi, wrl_ref[...], preferred_element_type=F32)
                  + br_ref[...])
        lane = lax.broadcasted_iota(jnp.int32, (tm, LANES), 1).astype(F32)
        work = logits
        vals, ids = [], []
        for _ in range(TOP_K):
            m = jnp.max(work, axis=-1, keepdims=True)
            ik = jnp.min(jnp.where(work == m, lane, float(LANES)), axis=-1, keepdims=True)
            vals.append(m)
            ids.append(ik)
            work = jnp.where(lane == ik, -3e38, work)
        ex = [jnp.exp(v - vals[0]) for v in vals]
        den_k = ex[0] + ex[1] + ex[2] + ex[3]
        a = jnp.zeros((tm, LANES), F32)
        idx = jnp.zeros((tm, LANES), F32)
        gk = jnp.zeros((tm, LANES), F32)
        for kk in range(TOP_K):
            gate = ex[kk] / den_k
            a = a + jnp.where(lane == ids[kk], gate, 0.0)
            idx = jnp.where(lane == float(kk), ids[kk], idx)
            gk = jnp.where(lane == float(kk), gate, gk)
        a_ref[...] = a
        idx_ref[...] = idx
        gk_ref[...] = gk
        cnt = jnp.sum((a > 0.0).astype(F32), axis=0, keepdims=True)
        row = lax.broadcasted_iota(jnp.int32, (8, LANES), 0)
        cnt_ref[0] = jnp.where(row == 0, jnp.broadcast_to(cnt, (8, LANES)), 0.0)

    if n_valid_steps is None:
        compute()
    else:
        pl.when(i < n_valid_steps)(compute)

        @pl.when(i >= n_valid_steps)
        def _():
            x1_ref[...] = jnp.zeros(x1_ref.shape, x1_ref.dtype)
            h2_ref[...] = jnp.zeros(h2_ref.shape, h2_ref.dtype)
            a_ref[...] = jnp.zeros(a_ref.shape, a_ref.dtype)
            idx_ref[...] = jnp.zeros(idx_ref.shape, idx_ref.dtype)
            gk_ref[...] = jnp.zeros(gk_ref.shape, gk_ref.dtype)


def _post(x, o_list, lse_list, pool_in, sga, sgp, wts, mods, *, tm, dils, per_row,
          rows_total, row_block0, cnt_tiles, cnt_block, grid, n_valid_steps, alias_bufs):
    pooled_given = not isinstance(pool_in, tuple)
    nv = grid if n_valid_steps is None else n_valid_steps

    def clamp(i):
        return jnp.minimum(i, nv - 1)

    def tile_spec(width):
        return pl.BlockSpec((tm, width), lambda i: (clamp(i), 0))

    in_specs = [tile_spec(D_MODEL)]
    in_specs += [pl.BlockSpec((tm // d, d * GROUP_W), lambda i: (clamp(i), 0)) for d in dils]
    in_specs += [pl.BlockSpec((tm // d, d * LANES), lambda i: (clamp(i), 0)) for d in dils]
    args = [x, *o_list, *lse_list]
    if pooled_given:
        in_specs.append(tile_spec(POOL_W))
        args.append(pool_in)
    else:
        u = pool_in[0]
        in_specs += [tile_spec(POOL_W),
                     pl.BlockSpec((POOL_HALO, POOL_W),
                                  lambda i: (jnp.maximum(i * (tm // POOL_HALO) - 1, 0), 0))]
        args += [u, u]
    in_specs += [tile_spec(D_MODEL), tile_spec(D_MODEL)]
    args += [sga, sgp]
    wpm, psc, wua, wup, wout, expand, g2, wrh, wrl, br = wts
    gt1, sc2, sh2 = mods

    def mspec():
        if per_row:
            return pl.BlockSpec((tm, D_MODEL), lambda i: (clamp(i), 0))
        return _const_spec((1, D_MODEL))

    in_specs += [_const_spec(wpm.shape), _const_spec(psc.shape), _const_spec(wua.shape),
                 _const_spec(wup.shape), _const_spec(wout.shape), _const_spec(expand.shape),
                 mspec(), _const_spec(g2.shape), mspec(), mspec(),
                 _const_spec(wrh.shape), _const_spec(wrl.shape), _const_spec(br.shape)]
    args += [wpm, psc, wua, wup, wout, expand, gt1, g2, sc2, sh2, wrh, wrl, br]
    aliases = {}
    if alias_bufs is not None:
        base = len(args)
        in_specs += [pl.BlockSpec(memory_space=pl.ANY)] * 6
        args += list(alias_bufs)
        aliases = {base + j: j for j in range(6)}

    def out_spec(width):
        return pl.BlockSpec((tm, width), lambda i: (row_block0 + i, 0))

    out_specs = [out_spec(D_MODEL), out_spec(D_MODEL), out_spec(LANES), out_spec(LANES),
                 out_spec(LANES),
                 pl.BlockSpec((1, 8, LANES),
                              lambda i: (cnt_block if cnt_block is not None else i, 0, 0))]
    out_shape = [jax.ShapeDtypeStruct((rows_total, D_MODEL), F32),
                 jax.ShapeDtypeStruct((rows_total, D_MODEL), BF16),
                 jax.ShapeDtypeStruct((rows_total, LANES), F32),
                 jax.ShapeDtypeStruct((rows_total, LANES), F32),
                 jax.ShapeDtypeStruct((rows_total, LANES), F32),
                 jax.ShapeDtypeStruct((cnt_tiles, 8, LANES), F32)]
    return pl.pallas_call(
        functools.partial(_post_kernel, tm=tm, dils=dils, pooled_given=pooled_given,
                          n_valid_steps=n_valid_steps, aliased=alias_bufs is not None),
        grid=(grid,),
        in_specs=in_specs,
        out_specs=out_specs,
        out_shape=out_shape,
        scratch_shapes=[pltpu.VMEM((GROUP_W // LANES, tm, LANES), F32),
                        pltpu.VMEM((tm, LANES), F32)],
        input_output_aliases=aliases,
        compiler_params=_cparams(1),
        name="post_sample" if per_row else "post",
    )(*args)


def _sort_rows(tm):
    return -(-(TOP_K * tm + N_EXPERTS * (ROW_CHUNK - 1)) // SEL_CHUNK) * SEL_CHUNK


def _for_row_pieces(n_chunks, max_pow, fn):
    big = 1 << max_pow

    def body(c, carry):
        fn(c * big, big)
        return carry

    lax.fori_loop(0, n_chunks >> max_pow, body, 0)
    for pw in range(max_pow - 1, -1, -1):
        @pl.when(((n_chunks >> pw) & 1) == 1)
        def _(pw=pw):
            fn((n_chunks >> (pw + 1)) << (pw + 1), 1 << pw)


SEG_MAX_POW = 3
TILE_MAX_POW = 5


def _moe_sort_kernel(seg_s, goff_s, nch_s, ntot_s, tstart_s, tnch_s,
                     a_ref, idx_ref, h2_ref, segv_ref, lt_ref,
                     xb_hbm, dst_ref, xs_scr, zero_scr, sem, *, tm, n_rows):
    i = pl.program_id(0)
    nt = pl.num_programs(0)
    slot = i % 2
    sel = a_ref[...] > 0.0
    ahead = jnp.dot(lt_ref[...], sel.astype(BF16), preferred_element_type=F32)
    slot1 = jnp.where(sel, segv_ref[0] + ahead + 1.0, 0.0)
    lane = lax.broadcasted_iota(jnp.int32, (tm, LANES), 1).astype(F32)
    idx = idx_ref[...]
    dst = jnp.full((tm, LANES), -1.0, F32)
    for kk in range(TOP_K):
        hit = lane == idx[:, kk:kk + 1]
        dk = jnp.sum(jnp.where(hit, slot1, 0.0), axis=-1, keepdims=True) - 1.0
        dst = jnp.where(lane == float(kk), dk, dst)
    dst_ref[...] = dst
    dst_t = dst.T
    h2 = h2_ref[...]
    local_rows = lax.broadcasted_iota(jnp.int32, (BF16_EXACT, tm), 0).astype(F32).astype(BF16)
    one, zero = jnp.ones((), BF16), jnp.zeros((), BF16)
    for c in range(n_rows // BF16_EXACT):
        p = None
        for kk in range(TOP_K):
            hit = local_rows == (dst_t[kk:kk + 1, :] - float(c * BF16_EXACT)).astype(BF16)
            p = hit if p is None else p | hit
        xs = jnp.dot(jnp.where(p, one, zero), h2, preferred_element_type=F32)
        xs_scr[slot, c * BF16_EXACT:(c + 1) * BF16_EXACT, :] = xs.astype(BF16)

    def rows_copy(buf, src_row, dst_row, n_chunks):
        return pltpu.make_async_copy(
            xs_scr.at[buf, pl.ds(pl.multiple_of(src_row, ROW_CHUNK), n_chunks * ROW_CHUNK)],
            xb_hbm.at[pl.ds(pl.multiple_of(dst_row, ROW_CHUNK), n_chunks * ROW_CHUNK)],
            sem.at[buf])

    def per_expert(e, carry):
        so = seg_s[i * N_EXPERTS + e]
        go = goff_s[i * N_EXPERTS + e]
        _for_row_pieces(
            nch_s[i * N_EXPERTS + e], SEG_MAX_POW,
            lambda off, n: rows_copy(slot, so + off * ROW_CHUNK, go + off * ROW_CHUNK, n).start())
        return carry

    lax.fori_loop(0, N_EXPERTS, per_expert, 0)

    def drain(buf, tile):
        _for_row_pieces(ntot_s[tile], TILE_MAX_POW, lambda off, n: rows_copy(buf, 0, 0, n).wait())

    @pl.when(i > 0)
    def _():
        drain(1 - slot, i - 1)

    @pl.when(i == nt - 1)
    def _():
        drain(slot, i)
        zero_scr[...] = jnp.zeros(zero_scr.shape, zero_scr.dtype)

        def tail_copy(dst_row, n_chunks):
            return pltpu.make_async_copy(
                zero_scr.at[pl.ds(0, n_chunks * ROW_CHUNK)],
                xb_hbm.at[pl.ds(pl.multiple_of(dst_row, ROW_CHUNK), n_chunks * ROW_CHUNK)],
                sem.at[2])

        def per_expert_tail(e, carry):
            _for_row_pieces(tnch_s[e], SEG_MAX_POW,
                            lambda off, n: tail_copy(tstart_s[e] + off * ROW_CHUNK, n).start())
            _for_row_pieces(tnch_s[e], SEG_MAX_POW, lambda off, n: tail_copy(0, n).wait())
            return carry

        lax.fori_loop(0, N_EXPERTS, per_expert_tail, 0)


def _moe_sort(meta, a_all, idx_all, h2_all, *, tm, cap):
    t_all = a_all.shape[0]
    nt = t_all // tm
    n_rows = _sort_rows(tm)
    lt = jnp.tril(jnp.ones((tm, tm), BF16), -1)
    grid_spec = pltpu.PrefetchScalarGridSpec(
        num_scalar_prefetch=6,
        grid=(nt,),
        in_specs=[pl.BlockSpec((tm, LANES), lambda i, *_: (i, 0)),
                  pl.BlockSpec((tm, LANES), lambda i, *_: (i, 0)),
                  pl.BlockSpec((tm, D_MODEL), lambda i, *_: (i, 0)),
                  pl.BlockSpec((1, 1, LANES), lambda i, *_: (i, 0, 0)),
                  pl.BlockSpec((tm, tm), lambda i, *_: (0, 0))],
        out_specs=[pl.BlockSpec(memory_space=pl.ANY),
                   pl.BlockSpec((tm, LANES), lambda i, *_: (i, 0))],
        scratch_shapes=[pltpu.VMEM((2, n_rows, D_MODEL), BF16),
                        pltpu.VMEM(((1 << SEG_MAX_POW) * ROW_CHUNK, D_MODEL), BF16),
                        pltpu.SemaphoreType.DMA((3,))],
    )
    return pl.pallas_call(
        functools.partial(_moe_sort_kernel, tm=tm, n_rows=n_rows),
        grid_spec=grid_spec,
        out_shape=[jax.ShapeDtypeStruct((cap, D_MODEL), BF16),
                   jax.ShapeDtypeStruct((t_all, LANES), F32)],
        compiler_params=_cparams(1),
        name="moe_sort",
    )(meta["seg"], meta["goff"], meta["nch"], meta["ntot"], meta["tstart"], meta["tnch"],
      a_all, idx_all, h2_all, meta["segv"], lt)


def _moe_ffn_kernel(be_s, nused_s, ord_s, next_s, rows_s,
                    x_ref, wgu_hbm, bgu_ref, wd_hbm, bd_ref, y_ref,
                    wgu_f32, wd_f32, wgu_bf, wd_bf, sem):
    b = pl.program_id(0)

    def weight_copies(e, slot):
        return (pltpu.make_async_copy(wgu_hbm.at[e], wgu_f32.at[slot], sem.at[0, slot]),
                pltpu.make_async_copy(wd_hbm.at[e], wd_f32.at[slot], sem.at[1, slot]))

    @pl.when(b < nused_s[0])
    def _():
        e = be_s[b]
        e_prev = be_s[jnp.maximum(b - 1, 0)]
        slot = ord_s[b] % 2

        @pl.when(b == 0)
        def _():
            for cp in weight_copies(e, slot):
                cp.start()

        @pl.when((b == 0) | (e != e_prev))
        def _():
            for cp in weight_copies(e, slot):
                cp.wait()
            e_next = next_s[b]

            @pl.when(e_next >= 0)
            def _():
                for cp in weight_copies(e_next, 1 - slot):
                    cp.start()

            wgu_bf[...] = wgu_f32[slot].astype(BF16)
            wd_bf[...] = wd_f32[slot].astype(BF16)

        def ffn_rows(n):
            hgu = jnp.dot(x_ref[0:n, :], wgu_bf[...], preferred_element_type=F32) + bgu_ref[0]
            d_ff = hgu.shape[1] // 2
            hg = jnp.minimum(hgu[:, :d_ff], SWIGLU_LIMIT)
            hu = jnp.clip(hgu[:, d_ff:], -SWIGLU_LIMIT, SWIGLU_LIMIT)
            act = hg * jax.nn.sigmoid(SWIGLU_ALPHA * hg) * (hu + 1.0)
            y = jnp.dot(act.astype(BF16), wd_bf[...], preferred_element_type=F32) + bd_ref[0]
            y_ref[0:n, :] = y.astype(y_ref.dtype)

        rows_here = rows_s[b]
        for n in range(FFN_SUB, FFN_BLOCK + 1, FFN_SUB):
            pl.when(rows_here == n)(functools.partial(ffn_rows, n))


def _moe_ffn(meta, xb, w_gate_up, b_gate_up, w_down, b_down):
    cap = xb.shape[0]
    nb = cap // FFN_BLOCK
    d_ff2 = w_gate_up.shape[2]

    def blk(b, be, nu, *_):
        return jnp.minimum(b, jnp.maximum(nu[0] - 1, 0))

    def row_blk(b, be, nu, *_):
        return (blk(b, be, nu), 0)

    def expert_blk(b, be, nu, *_):
        return (be[blk(b, be, nu)], 0, 0)

    grid_spec = pltpu.PrefetchScalarGridSpec(
        num_scalar_prefetch=5,
        grid=(nb,),
        in_specs=[pl.BlockSpec((FFN_BLOCK, D_MODEL), row_blk),
                  pl.BlockSpec(memory_space=pl.ANY),
                  pl.BlockSpec((1, 1, d_ff2), expert_blk),
                  pl.BlockSpec(memory_space=pl.ANY),
                  pl.BlockSpec((1, 1, D_MODEL), expert_blk)],
        out_specs=pl.BlockSpec((FFN_BLOCK, D_MODEL), row_blk),
        scratch_shapes=[pltpu.VMEM((2, D_MODEL, d_ff2), F32),
                        pltpu.VMEM((2, d_ff2 // 2, D_MODEL), F32),
                        pltpu.VMEM((D_MODEL, d_ff2), BF16),
                        pltpu.VMEM((d_ff2 // 2, D_MODEL), BF16),
                        pltpu.SemaphoreType.DMA((2, 2))],
    )
    return pl.pallas_call(
        _moe_ffn_kernel,
        grid_spec=grid_spec,
        out_shape=jax.ShapeDtypeStruct((cap, D_MODEL), BF16),
        compiler_params=_cparams(1),
        name="moe_ffn",
    )(meta["block_expert"], meta["n_used"], meta["block_ord"], meta["block_next"],
      meta["block_rows"], xb, w_gate_up,
      b_gate_up.reshape(N_EXPERTS, 1, d_ff2), w_down, b_down.reshape(N_EXPERTS, 1, D_MODEL))


def _moe_unsort_kernel(seg_s, goff_s, nch_s, ntot_s,
                       dst_ref, gk_ref, x1_ref, g2p_ref, g2s_ref, yb_hbm,
                       yp_ref, ys_ref, ybuf, sem, *, tm, n_rows, n_prompt_tiles):
    i = pl.program_id(0)
    nt = pl.num_programs(0)
    slot = i % 2

    def rows_copy(buf, src_row, dst_row, n_chunks):
        return pltpu.make_async_copy(
            yb_hbm.at[pl.ds(pl.multiple_of(src_row, ROW_CHUNK), n_chunks * ROW_CHUNK)],
            ybuf.at[buf, pl.ds(pl.multiple_of(dst_row, ROW_CHUNK), n_chunks * ROW_CHUNK)],
            sem.at[buf])

    def fetch(tile, buf):
        def per_expert(e, carry):
            so = seg_s[tile * N_EXPERTS + e]
            go = goff_s[tile * N_EXPERTS + e]
            _for_row_pieces(
                nch_s[tile * N_EXPERTS + e], SEG_MAX_POW,
                lambda off, n: rows_copy(buf, go + off * ROW_CHUNK, so + off * ROW_CHUNK,
                                         n).start())
            return carry

        lax.fori_loop(0, N_EXPERTS, per_expert, 0)

    @pl.when(i == 0)
    def _():
        ybuf[...] = jnp.zeros(ybuf.shape, ybuf.dtype)
        fetch(0, 0)

    @pl.when(i + 1 < nt)
    def _():
        fetch(i + 1, 1 - slot)

    _for_row_pieces(ntot_s[i], TILE_MAX_POW, lambda off, n: rows_copy(slot, 0, 0, n).wait())

    dst = dst_ref[...]
    gk = gk_ref[...]
    acc = jnp.zeros((tm, D_MODEL), F32)
    for c in range(n_rows // SEL_CHUNK):
        cols = (lax.broadcasted_iota(jnp.int32, (tm, SEL_CHUNK), 1) + c * SEL_CHUNK).astype(F32)
        q = jnp.zeros((tm, SEL_CHUNK), F32)
        for kk in range(TOP_K):
            q = jnp.where(cols == dst[:, kk:kk + 1], gk[:, kk:kk + 1], q)
        acc = acc + jnp.dot(q.astype(BF16), ybuf[slot, c * SEL_CHUNK:(c + 1) * SEL_CHUNK, :],
                            preferred_element_type=F32)

    @pl.when(i < n_prompt_tiles)
    def _():
        yp_ref[...] = x1_ref[...] + g2p_ref[...] * acc

    @pl.when(i >= n_prompt_tiles)
    def _():
        ys_ref[...] = x1_ref[...] + g2s_ref[...] * acc


def _moe_unsort(meta, dst_all, gk_all, x1_all, gt2_p, gt2_s, yb, *, tm, n_prompt_tiles):
    t_all = dst_all.shape[0]
    nt = t_all // tm
    n_rows = _sort_rows(tm)
    last_p = n_prompt_tiles - 1
    grid_spec = pltpu.PrefetchScalarGridSpec(
        num_scalar_prefetch=4,
        grid=(nt,),
        in_specs=[pl.BlockSpec((tm, LANES), lambda i, *_: (i, 0)),
                  pl.BlockSpec((tm, LANES), lambda i, *_: (i, 0)),
                  pl.BlockSpec((tm, D_MODEL), lambda i, *_: (i, 0)),
                  pl.BlockSpec((1, D_MODEL), lambda i, *_: (0, 0)),
                  pl.BlockSpec((tm, D_MODEL), lambda i, *_: (0, 0)),
                  pl.BlockSpec(memory_space=pl.ANY)],
        out_specs=[pl.BlockSpec((tm, D_MODEL), lambda i, *_: (jnp.minimum(i, last_p), 0)),
                   pl.BlockSpec((tm, D_MODEL), lambda i, *_: (0, 0))],
        scratch_shapes=[pltpu.VMEM((2, n_rows, D_MODEL), BF16),
                        pltpu.SemaphoreType.DMA((2,))],
    )
    return pl.pallas_call(
        functools.partial(_moe_unsort_kernel, tm=tm, n_rows=n_rows,
                          n_prompt_tiles=n_prompt_tiles),
        grid_spec=grid_spec,
        out_shape=[jax.ShapeDtypeStruct((n_prompt_tiles * tm, D_MODEL), F32),
                   jax.ShapeDtypeStruct((tm, D_MODEL), F32)],
        compiler_params=_cparams(1),
        name="moe_unsort",
    )(meta["seg"], meta["goff"], meta["nch"], meta["ntot"],
      dst_all, gk_all, x1_all, gt2_p, gt2_s, yb)


def _moe_meta(cnt, tm):
    nt = cnt.shape[0]
    cnt = cnt.astype(jnp.int32)
    cnt_pad = (cnt + ROW_CHUNK - 1) // ROW_CHUNK * ROW_CHUNK
    seg = jnp.cumsum(cnt_pad, axis=1) - cnt_pad
    rows_e = jnp.sum(cnt_pad, axis=0)
    region = (rows_e + FFN_BLOCK - 1) // FFN_BLOCK * FFN_BLOCK
    gstart = jnp.cumsum(region) - region
    goff = gstart[None, :] + jnp.cumsum(cnt_pad, axis=0) - cnt_pad
    nblk_e = region // FFN_BLOCK
    blk_end = jnp.cumsum(nblk_e)
    cap = _moe_cap(nt * tm, tm)
    blocks = jnp.arange(cap // FFN_BLOCK, dtype=jnp.int32)
    block_expert = jnp.minimum(
        jnp.sum((blk_end[None, :] <= blocks[:, None]).astype(jnp.int32), axis=1), N_EXPERTS - 1)
    segv = jnp.zeros((nt, 1, LANES), F32).at[:, 0, :N_EXPERTS].set(seg.astype(F32))
    used = nblk_e > 0
    experts = jnp.arange(N_EXPERTS, dtype=jnp.int32)
    ord_e = jnp.cumsum(used.astype(jnp.int32)) - 1
    later = (experts[None, :] > experts[:, None]) & used[None, :]
    next_e = jnp.min(jnp.where(later, experts[None, :], N_EXPERTS), axis=1)
    next_e = jnp.where(next_e < N_EXPERTS, next_e, -1).astype(jnp.int32)
    of_block = (block_expert[:, None] == experts[None, :]).astype(jnp.int32)
    first_block = jnp.sum(of_block * (blk_end - nblk_e)[None, :], axis=1)
    rows_left = jnp.sum(of_block * rows_e[None, :], axis=1) - (blocks - first_block) * FFN_BLOCK
    block_rows = jnp.clip((rows_left + FFN_SUB - 1) // FFN_SUB * FFN_SUB, FFN_SUB, FFN_BLOCK)
    return {
        "block_rows": block_rows.astype(jnp.int32),
        "block_ord": jnp.sum(of_block * ord_e[None, :].astype(jnp.int32), axis=1),
        "block_next": jnp.sum(of_block * next_e[None, :], axis=1),
        "seg": seg.reshape(-1), "goff": goff.reshape(-1).astype(jnp.int32),
        "nch": (cnt_pad // ROW_CHUNK).reshape(-1),
        "ntot": jnp.sum(cnt_pad, axis=1) // ROW_CHUNK,
        "tstart": (gstart + rows_e).astype(jnp.int32),
        "tnch": (region - rows_e) // ROW_CHUNK,
        "block_expert": block_expert,
        "n_used": blk_end[-1:].astype(jnp.int32),
        "segv": segv,
    }


def _moe_cap(t_all, tm):
    nt = t_all // tm
    worst = TOP_K * t_all + nt * N_EXPERTS * (ROW_CHUNK - 1) + N_EXPERTS * (FFN_BLOCK - ROW_CHUNK)
    return -(-worst // FFN_BLOCK) * FFN_BLOCK


def _t5_bucket(dist):
    max_exact = NUM_BUCKETS // 2
    d = dist.astype(jnp.int32)
    ratio = (jnp.log(jnp.maximum(d, 1).astype(F32) / max_exact)
             / math.log(MAX_DISTANCE / max_exact))
    large = jnp.minimum(max_exact + (ratio * (NUM_BUCKETS - max_exact)).astype(jnp.int32),
                        NUM_BUCKETS - 1)
    return jnp.where(d < max_exact, d, large)


def _step_bias(tab, dil):
    return tab[_t5_bucket(dil * jnp.arange(ATT_BLK + 1))].astype(F32).T


def _band_table(sb):
    return jnp.concatenate([sb[:, ::-1], jnp.full((HEADS, ATT_BLK - 1), NEG_INF, F32)], axis=1)


def _cache_table(sb, dil):
    on_grid = sb[:, :0:-1]
    if dil == 1:
        return on_grid
    off = jnp.full((HEADS, ATT_BLK, dil - 1), NEG_INF, F32)
    return jnp.concatenate([on_grid[:, :, None], off], axis=2).reshape(HEADS, ATT_BLK * dil)


def kernel(x_prompt, x_sample, cache_kv_w128, cache_kv_w512, cache_kv_w2048, state_pool, c_prompt,
           c_sample, w_ada, b_ada, norm_mix_g, norm_ffn_g, w_in, q_norm_g, k_norm_g, rel_bias,
           w_pool_mix, pool_scale, w_up_attn, w_up_pool, w_out, w_router, b_router, w_gate_up,
           b_gate_up, w_down, b_down):
    assert w_ada.shape[0] == 1, "one layer"
    seq = x_prompt.shape[1]
    n_s = x_sample.shape[0]
    assert x_prompt.shape[0] == 1 and x_sample.shape[1] == 1
    assert seq % (DIL_GROUPS[-1][1] * ATT_BLK * ATT_SUB) == 0 and seq % TM_PROMPT == 0
    assert n_s == TM_SAMPLE
    dils = tuple(d for _, d in DIL_GROUPS)
    caches = (cache_kv_w128, cache_kv_w512, cache_kv_w2048)

    w_in_bf = w_in[0].astype(BF16)
    heads_of = jnp.arange(GROUP_W) // HEAD_DIM
    half_heads = heads_of[:GROUP_W // 2]
    bdiag = (half_heads[:, None] == half_heads[None, :]).astype(BF16)
    qg = (jnp.tile(q_norm_g[0], HEADS) * SCALE).reshape(1, GROUP_W)
    kg = jnp.tile(k_norm_g[0], HEADS).reshape(1, GROUP_W)
    expand = ((jnp.arange(LANES)[:, None] % HEADS == heads_of[None, :])
              & (jnp.arange(LANES)[:, None] < 2 * HEADS)).astype(BF16)
    wr = jnp.zeros((D_MODEL, LANES), F32).at[:, :N_EXPERTS].set(w_router[0])
    wr_hi = wr.astype(BF16)
    wr_lo = (wr - wr_hi.astype(F32)).astype(BF16)
    br = jnp.full((1, LANES), NEG_INF, F32).at[0, :N_EXPERTS].set(b_router[0])
    wts = (w_pool_mix[0].astype(BF16), pool_scale[0].reshape(1, POOL_W),
           w_up_attn[0].astype(BF16), w_up_pool[0].astype(BF16), w_out[0].astype(BF16), expand,
           norm_ffn_g[0].reshape(1, D_MODEL), wr_hi, wr_lo, br)
    g1 = norm_mix_g[0].reshape(1, D_MODEL)

    n_c = 1 + n_s
    c_all = jnp.zeros((-(-n_c // 8) * 8, D_MODEL), F32).at[0:1].set(c_prompt).at[1:n_c].set(c_sample)
    mod = _ada(c_all, w_ada[0], b_ada[0])
    sh1, sc1, gt1, sh2, sc2, gt2 = jnp.split(mod, N_ADA, axis=-1)

    def prow(m):
        return m[0:1]

    def srows(m):
        return m[1:n_c]

    xp = x_prompt[0]
    q_p, k_p, v_p, u_p, sga_p, sgp_p, st_p = _proj(
        xp, g1, prow(sc1), prow(sh1), w_in_bf, bdiag, qg, kg,
        tm=TM_PROMPT, dils=dils, per_row=False)
    xs = x_sample[:, 0]
    ones = (1, 1, 1)
    q_s, _, _, u_s, sga_s, sgp_s, st_s = _proj(
        xs, g1, srows(sc1), srows(sh1), w_in_bf, bdiag, qg, kg,
        tm=TM_SAMPLE, dils=ones, per_row=True)
    pooled_s, pool_state_t = _pool_sample(jnp.transpose(state_pool, (0, 2, 1, 3)), u_s)
    pool_state_s = jnp.transpose(pool_state_t, (0, 2, 1, 3))

    def heads(a):
        return a.astype(F32).reshape(n_s, HEADS, HEAD_DIM)

    step_bias = [_step_bias(rel_bias[:, g * HEADS:(g + 1) * HEADS], d)
                 for g, (_, d) in enumerate(DIL_GROUPS)]

    def sample_group(g, part, n_parts):
        cache_t = jnp.transpose(caches[g], (0, 1, 3, 4, 5, 2))
        n_sub = n_s // n_parts
        return (heads(q_s[g]), heads(st_s[g][:, :GROUP_W]), heads(st_s[g][:, GROUP_W:]),
                step_bias[g][:, 0:1], _cache_table(step_bias[g], DIL_GROUPS[g][1]), cache_t,
                part * n_sub, n_sub)

    o_p, lse_p = [], []
    o_parts = [[None] * n for n in SAMPLE_PARTS]
    lse_parts = [[None] * n for n in SAMPLE_PARTS]
    for g, (_, d) in enumerate(DIL_GROUPS):
        hosted = SAMPLE_HOST[g]
        o, lse, sample_outs = _attn(
            q_p[g], k_p[g], v_p[g], _band_table(step_bias[g]), d,
            [sample_group(sg, part, SAMPLE_PARTS[sg]) for sg, part in hosted])
        o_p.append(o)
        lse_p.append(lse)
        for (sg, part), (os_g, lses_g) in zip(hosted, sample_outs):
            o_parts[sg][part] = os_g.reshape(-1, GROUP_W)
            lse_parts[sg][part] = lses_g[:, :, 0]
    o_s = [jnp.concatenate(parts, axis=0) for parts in o_parts]
    lse_s = [jnp.zeros((n_s, LANES), F32).at[:, :HEADS].set(jnp.concatenate(parts, axis=0))
             for parts in lse_parts]

    nt_p = seq // TM_PROMPT
    t_all = seq + TM_PROMPT
    bufs = _post(xp, o_p, lse_p, (u_p,), sga_p, sgp_p, wts, (prow(gt1), prow(sc2), prow(sh2)),
                 tm=TM_PROMPT, dils=dils, per_row=False, rows_total=t_all, row_block0=0,
                 cnt_tiles=nt_p + 1, cnt_block=None, grid=nt_p, n_valid_steps=None,
                 alias_bufs=None)
    bufs = _post(xs, o_s, lse_s, pooled_s, sga_s, sgp_s, wts,
                 (srows(gt1), srows(sc2), srows(sh2)),
                 tm=TM_SAMPLE, dils=ones, per_row=True, rows_total=t_all,
                 row_block0=seq // TM_SAMPLE, cnt_tiles=nt_p + 1, cnt_block=nt_p,
                 grid=TM_PROMPT // TM_SAMPLE, n_valid_steps=1, alias_bufs=bufs)
    x1_all, h2_all, a_all, idx_all, gk_all, cnt = bufs

    meta = _moe_meta(cnt[:, 0, :N_EXPERTS], TM_PROMPT)
    cap = _moe_cap(t_all, TM_PROMPT)
    xb, dst_all = _moe_sort(meta, a_all, idx_all, h2_all, tm=TM_PROMPT, cap=cap)
    yb = _moe_ffn(meta, xb, w_gate_up[0], b_gate_up[0], w_down[0], b_down[0])
    gt2_s = jnp.zeros((TM_PROMPT, D_MODEL), F32).at[:n_s].set(srows(gt2))
    y_p, y_s = _moe_unsort(meta, dst_all, gk_all, x1_all, prow(gt2), gt2_s, yb,
                           tm=TM_PROMPT, n_prompt_tiles=nt_p)

    def kv_state(st, rows):
        return st.reshape(1, 1, rows, 2, HEADS, HEAD_DIM)

    kv_p = [kv_state(st, st.shape[0]) for st in st_p]
    kv_s = [st.reshape(1, n_s, 1, 2, HEADS, HEAD_DIM) for st in st_s]
    pool_p = u_p[seq - POOL_BUF:].reshape(1, 1, POOL_BUF, POOL_W)
    return (y_p.reshape(1, seq, D_MODEL), y_s[:n_s].reshape(n_s, 1, D_MODEL),
            kv_p[0], kv_p[1], kv_p[2], pool_p, kv_s[0], kv_s[1], kv_s[2], pool_state_s)
```

```python
import functools
import math

import jax
import jax.numpy as jnp
from jax import lax
from jax.experimental import pallas as pl
from jax.experimental.pallas import tpu as pltpu

F32 = jnp.float32
BF16 = jnp.bfloat16

D_MODEL = 1024
HEAD_DIM = 64
HEADS = 8
GROUP_W = HEADS * HEAD_DIM
DIL_GROUPS = ((128, 1), (512, 4), (2048, 16))
N_GROUPS = len(DIL_GROUPS)
QKV_W = N_GROUPS * GROUP_W
ATT_BLK = 128
ATT_SUB = 4
POOL_WINDOWS = (2, 4, 8, 16)
POOL_W = 512
POOL_GW = 128
POOL_BUF = 15
POOL_HALO = 16
OFF_K, OFF_V = QKV_W, 2 * QKV_W
OFF_U = 3 * QKV_W
OFF_GA = OFF_U + POOL_W
OFF_GP = OFF_GA + D_MODEL
IN_W = OFF_GP + D_MODEL
NUM_BUCKETS = 32
MAX_DISTANCE = 2048
N_EXPERTS = 32
TOP_K = 4
SWIGLU_LIMIT = 7.0
SWIGLU_ALPHA = 1.702
N_ADA = 6
EPS = 1e-6
NEG_INF = -1e30
PAST_LEN = 8192
SCALE = HEAD_DIM ** -0.5

LANES = 128
ROW_CHUNK = 16
TM_PROMPT = 512
TM_SAMPLE = 128
FFN_BLOCK = 1024
FFN_SUB = 128
WEIGHT_DMA_PRIORITY = 1
SEL_CHUNK = 512
BF16_EXACT = 256
VMEM_LIMIT = 56 * 1024 * 1024


def _cparams(n_axes):
    return pltpu.CompilerParams(dimension_semantics=("arbitrary",) * n_axes,
                                vmem_limit_bytes=VMEM_LIMIT)


def _const_spec(shape):
    nd = len(shape)
    return pl.BlockSpec(shape, lambda *_: (0,) * nd)


def _ada_kernel(c_ref, w_ref, b_ref, o_ref):
    c = c_ref[...]
    s = c * jax.nn.sigmoid(c)
    o_ref[...] = jnp.dot(s.astype(BF16), w_ref[...].astype(BF16),
                         preferred_element_type=F32) + b_ref[...]


def _ada(c_all, w_ada, b_ada):
    rows = c_all.shape[0]
    n = w_ada.shape[1]
    tn = 1536
    return pl.pallas_call(
        _ada_kernel,
        grid=(n // tn,),
        in_specs=[pl.BlockSpec((rows, D_MODEL), lambda j: (0, 0)),
                  pl.BlockSpec((D_MODEL, tn), lambda j: (0, j)),
                  pl.BlockSpec((1, tn), lambda j: (0, j))],
        out_specs=pl.BlockSpec((rows, tn), lambda j: (0, j)),
        out_shape=jax.ShapeDtypeStruct((rows, n), F32),
        compiler_params=_cparams(1),
        name="ada",
    )(c_all, w_ada, b_ada.reshape(1, n))


def _proj_kernel(x_ref, g_ref, sc_ref, sh_ref, w_ref, bd_ref, qg_ref, kg_ref,
                 *refs, tm, dils, st_rows):
    q_refs, k_refs, v_refs = refs[0:3], refs[3:6], refs[6:9]
    u_ref, sga_ref, sgp_ref = refs[9:12]
    st_refs = refs[12:15]
    scr = refs[15]

    x = x_ref[...]
    ms = jnp.mean(x * x, axis=-1, keepdims=True)
    h = x * lax.rsqrt(ms + EPS) * g_ref[...] * (1.0 + sc_ref[...]) + sh_ref[...]
    hb = h.astype(BF16)

    def proj(off, width):
        return jnp.dot(hb, w_ref[:, off:off + width], preferred_element_type=F32)

    def head_norm(z, gain_ref):
        zz = (z * z).astype(BF16)
        half = GROUP_W // 2
        ss = jnp.concatenate(
            [jnp.dot(zz[:, :half], bd_ref[...], preferred_element_type=F32),
             jnp.dot(zz[:, half:], bd_ref[...], preferred_element_type=F32)], axis=1)
        return z * lax.rsqrt(ss * (1.0 / HEAD_DIM) + EPS) * gain_ref[...]

    def put(out_ref, val, d):
        if d == 1:
            out_ref[...] = val.astype(out_ref.dtype)
        else:
            for c in range(GROUP_W // LANES):
                scr[c] = val[:, c * LANES:(c + 1) * LANES]
            for r in range(d):
                for c in range(GROUP_W // LANES):
                    col = r * GROUP_W + c * LANES
                    out_ref[:, col:col + LANES] = (
                        scr[c, pl.ds(r, tm // d, stride=d), :].astype(out_ref.dtype))

    for g, d in enumerate(dils):
        qn = head_norm(proj(g * GROUP_W, GROUP_W), qg_ref)
        put(q_refs[g], qn, d)
        kn = head_norm(proj(OFF_K + g * GROUP_W, GROUP_W), kg_ref)
        put(k_refs[g], kn, d)
        v = proj(OFF_V + g * GROUP_W, GROUP_W)
        put(v_refs[g], v, d)
        rb = st_rows[g]
        st_refs[g][:, 0:GROUP_W] = kn[tm - rb:, :]
        st_refs[g][:, GROUP_W:2 * GROUP_W] = v[tm - rb:, :]

    u_ref[...] = proj(OFF_U, POOL_W)
    sga_ref[...] = jax.nn.sigmoid(proj(OFF_GA, D_MODEL)).astype(BF16)
    sgp_ref[...] = jax.nn.sigmoid(proj(OFF_GP, D_MODEL)).astype(BF16)


def _mod_spec(per_row, tm):
    if per_row:
        return pl.BlockSpec((tm, D_MODEL), lambda i: (i, 0))
    return pl.BlockSpec((1, D_MODEL), lambda i: (0, 0))


def _proj(x, g1, sc1, sh1, w_in_bf, bdiag, qg, kg, *, tm, dils, per_row):
    s = x.shape[0]
    nt = s // tm
    wins = tuple(min(w, s) for w, _ in DIL_GROUPS)
    st_rows = tuple(min(tm, w) for w in wins)

    def res_spec(d):
        return pl.BlockSpec((tm // d, d * GROUP_W), lambda i: (i, 0))

    def st_spec(w, rb):
        first = nt - w // rb
        return pl.BlockSpec((rb, 2 * GROUP_W), lambda i: (jnp.maximum(i - first, 0), 0))

    qkv_shapes = [jax.ShapeDtypeStruct((s // d, d * GROUP_W), BF16) for d in dils]
    out_shape = (qkv_shapes * 3
                 + [jax.ShapeDtypeStruct((s, POOL_W), F32),
                    jax.ShapeDtypeStruct((s, D_MODEL), BF16),
                    jax.ShapeDtypeStruct((s, D_MODEL), BF16)]
                 + [jax.ShapeDtypeStruct((w, 2 * GROUP_W), F32) for w in wins])
    out_specs = ([res_spec(d) for d in dils] * 3
                 + [pl.BlockSpec((tm, POOL_W), lambda i: (i, 0)),
                    pl.BlockSpec((tm, D_MODEL), lambda i: (i, 0)),
                    pl.BlockSpec((tm, D_MODEL), lambda i: (i, 0))]
                 + [st_spec(w, rb) for w, rb in zip(wins, st_rows)])
    in_specs = [pl.BlockSpec((tm, D_MODEL), lambda i: (i, 0)),
                _const_spec((1, D_MODEL)),
                _mod_spec(per_row, tm), _mod_spec(per_row, tm),
                pl.BlockSpec((D_MODEL, IN_W), lambda i: (0, 0), pipeline_mode=pl.Buffered(1)),
                _const_spec((GROUP_W // 2, GROUP_W // 2)),
                _const_spec((1, GROUP_W)), _const_spec((1, GROUP_W))]
    outs = pl.pallas_call(
        functools.partial(_proj_kernel, tm=tm, dils=dils, st_rows=st_rows),
        grid=(nt,),
        in_specs=in_specs,
        out_specs=out_specs,
        out_shape=out_shape,
        scratch_shapes=[pltpu.VMEM((GROUP_W // LANES, tm, LANES), F32)],
        compiler_params=_cparams(1),
        name="proj",
    )(x, g1, sc1, sh1, w_in_bf, bdiag, qg, kg)
    return outs[0:3], outs[3:6], outs[6:9], outs[9], outs[10], outs[11], outs[12:15]


def _sample_group_attn(qs_ref, kns_ref, vns_ref, bself_ref, btab_ref, c_ref, os_ref, lses_ref):
    n_tok = qs_ref.shape[0]
    win = btab_ref.shape[1]
    row_w = lax.broadcasted_iota(jnp.int32, (HEADS, win), 0)
    row_e = lax.broadcasted_iota(jnp.int32, (HEADS, HEAD_DIM), 0)
    ss, s0s = [], []
    for t in range(n_tok):
        q = qs_ref[t]
        qb = q.astype(BF16)
        s = jnp.zeros((HEADS, win), F32)
        for h in range(HEADS):
            sh = jnp.dot(qb, c_ref[0, t, 0, h].astype(BF16), preferred_element_type=F32)
            s = jnp.where(row_w == h, sh, s)
        ss.append(s + btab_ref[...])
        s0s.append(jnp.sum(q * kns_ref[t], axis=-1, keepdims=True) + bself_ref[...])
    s = jnp.concatenate(ss, axis=0)
    s0 = jnp.concatenate(s0s, axis=0)
    m = jnp.maximum(jnp.max(s, axis=-1, keepdims=True), s0)
    p = jnp.exp(s - m)
    p0 = jnp.exp(s0 - m)
    l = jnp.sum(p, axis=-1, keepdims=True) + p0
    pb = p.astype(BF16)
    lse = m + jnp.log(l)
    for t in range(n_tok):
        rows = slice(t * HEADS, (t + 1) * HEADS)
        o = jnp.zeros((HEADS, HEAD_DIM), F32)
        for h in range(HEADS):
            oh = lax.dot_general(pb[rows], c_ref[0, t, 1, h].astype(BF16),
                                 (((1,), (1,)), ((), ())), preferred_element_type=F32)
            o = jnp.where(row_e == h, oh, o)
        os_ref[t] = (o + p0[rows] * vns_ref[t]) / l[rows]
        lses_ref[t] = lse[rows]


SAMPLE_IN = 6
SAMPLE_PARTS = (1, 1, 2)
SAMPLE_HOST = (((2, 1),), ((1, 0),), ((2, 0), (0, 0)))


def _attn_kernel(q_ref, kp_ref, kc_ref, vp_ref, vc_ref, r_ref, *refs, n_sample_groups):
    n_in = SAMPLE_IN * n_sample_groups
    o_ref, lse_ref = refs[n_in], refs[n_in + 1]
    bias_ref = refs[-1]
    i = pl.program_id(1)

    for sg in range(n_sample_groups):
        _sample_group_attn(*refs[SAMPLE_IN * sg:SAMPLE_IN * (sg + 1)],
                           *refs[n_in + 2 + 2 * sg:n_in + 4 + 2 * sg])

    @pl.when((pl.program_id(0) == 0) & (i == 0))
    def _():
        for h in range(HEADS):
            row = jnp.broadcast_to(r_ref[h:h + 1, :], (ATT_BLK, 2 * ATT_BLK))
            bias_ref[h] = pltpu.roll(row, 0, 1, stride=1, stride_axis=0)

    q = q_ref[...]
    k = jnp.concatenate([kp_ref[...], kc_ref[...]], axis=0)
    v = jnp.concatenate([vp_ref[...], vc_ref[...]], axis=0)
    col = lax.broadcasted_iota(jnp.int32, (ATT_BLK, 2 * ATT_BLK), 1)
    no_prev = jnp.where((col < ATT_BLK) & (i == 0), NEG_INF, 0.0)
    lane_q = lax.broadcasted_iota(jnp.int32, (ATT_SUB * ATT_BLK, LANES), 1)
    lane_v = lax.broadcasted_iota(jnp.int32, ((ATT_SUB + 1) * ATT_BLK, LANES), 1)

    def pair(h):
        return slice((h // 2) * LANES, (h // 2 + 1) * LANES)

    def mine(lane, h):
        return (lane < HEAD_DIM) == (h % 2 == 0)

    ss = []
    for h in range(HEADS):
        q2 = q[:, pair(h)]
        qm = jnp.where(mine(lane_q, h), q2, jnp.zeros_like(q2))
        k2 = k[:, pair(h)]
        for j in range(ATT_SUB):
            s = lax.dot_general(qm[j * ATT_BLK:(j + 1) * ATT_BLK], k2[j * ATT_BLK:(j + 2) * ATT_BLK],
                                (((1,), (1,)), ((), ())), preferred_element_type=F32)
            s = s + bias_ref[h]
            ss.append(s + no_prev if j == 0 else s)
    s = jnp.concatenate(ss, axis=0)
    m = jnp.max(s, axis=-1, keepdims=True)
    p = jnp.exp(s - m)
    l = jnp.sum(p, axis=-1, keepdims=True)
    pb = p.astype(BF16)
    lse = m + jnp.log(l)
    inv_l = 1.0 / l
    outs, lses = [], []
    for h in range(HEADS):
        v2 = v[:, pair(h)]
        vm = jnp.where(mine(lane_v, h), v2, jnp.zeros_like(v2))
        o_sub, lse_sub = [], []
        for j in range(ATT_SUB):
            rows = slice((h * ATT_SUB + j) * ATT_BLK, (h * ATT_SUB + j + 1) * ATT_BLK)
            o_sub.append(jnp.dot(pb[rows], vm[j * ATT_BLK:(j + 2) * ATT_BLK],
                                 preferred_element_type=F32) * inv_l[rows])
            lse_sub.append(lse[rows])
        o = jnp.concatenate(o_sub, axis=0)
        if h % 2 == 0:
            outs.append(o)
        else:
            outs[-1] = outs[-1] + o
        lses.append(jnp.concatenate(lse_sub, axis=0))
    o_ref[...] = jnp.concatenate(outs, axis=-1).astype(o_ref.dtype)
    lse_ref[...] = jnp.concatenate(
        lses + [jnp.zeros((ATT_SUB * ATT_BLK, LANES - HEADS), F32)], axis=-1)


def _attn(q, k, v, r_tab, d, sample_groups):
    rows = q.shape[0]
    step = ATT_SUB * ATT_BLK
    nblk = rows // step
    cur = pl.BlockSpec((step, GROUP_W), lambda r, i: (i, r))
    prev = pl.BlockSpec((ATT_BLK, GROUP_W), lambda r, i: (jnp.maximum(i * ATT_SUB - 1, 0), r))
    in_specs = [cur, prev, cur, prev, cur,
                pl.BlockSpec((HEADS, 2 * ATT_BLK), lambda r, i: (0, 0))]
    out_specs = [pl.BlockSpec((step, GROUP_W), lambda r, i: (i, r)),
                 pl.BlockSpec((step, LANES), lambda r, i: (i, r))]
    out_shape = [jax.ShapeDtypeStruct((rows, d * GROUP_W), BF16),
                 jax.ShapeDtypeStruct((rows, d * LANES), F32)]
    args = [q, k, k, v, v, r_tab]
    for qs, kns, vns, bself, btab, cache_t, tok_start, n_tok in sample_groups:
        assert n_tok % (d * nblk) == 0
        tok = n_tok // (d * nblk)
        assert tok_start % tok == 0
        first = tok_start // tok

        def in_idx(r, i, first=first):
            return (first + r * nblk + i, 0, 0)

        def out_idx(r, i):
            return (r * nblk + i, 0, 0)

        tok_in = pl.BlockSpec((tok, HEADS, HEAD_DIM), in_idx)
        in_specs += [tok_in, tok_in, tok_in,
                     pl.BlockSpec(bself.shape, lambda r, i: (0, 0)),
                     pl.BlockSpec(btab.shape, lambda r, i: (0, 0)),
                     pl.BlockSpec((1, tok) + cache_t.shape[2:],
                                  lambda r, i, first=first: (0, first + r * nblk + i, 0, 0, 0, 0))]
        out_specs += [pl.BlockSpec((tok, HEADS, HEAD_DIM), out_idx),
                      pl.BlockSpec((tok, HEADS, 1), out_idx)]
        out_shape += [jax.ShapeDtypeStruct((n_tok, HEADS, HEAD_DIM), F32),
                      jax.ShapeDtypeStruct((n_tok, HEADS, 1), F32)]
        args += [qs, kns, vns, bself, btab, cache_t]
    outs = pl.pallas_call(
        functools.partial(_attn_kernel, n_sample_groups=len(sample_groups)),
        grid=(d, nblk),
        in_specs=in_specs,
        out_specs=out_specs,
        out_shape=out_shape,
        scratch_shapes=[pltpu.VMEM((HEADS, ATT_BLK, 2 * ATT_BLK), F32)],
        compiler_params=_cparams(2),
        name=f"attn_d{d}",
    )(*args)
    return outs[0], outs[1], [(outs[2 + 2 * j], outs[3 + 2 * j]) for j in range(len(sample_groups))]


def _pool_sample_kernel(st_ref, u_ref, pooled_ref, new_ref):
    u = u_ref[...]
    rows = [st_ref[0, j] for j in range(POOL_BUF)]
    outs = []
    for g, w in enumerate(POOL_WINDOWS):
        sl = slice(g * POOL_GW, (g + 1) * POOL_GW)
        acc = u[:, sl]
        for j in range(POOL_BUF - (w - 1), POOL_BUF):
            acc = acc + rows[j][:, sl]
        outs.append(acc / float(w) - u[:, sl])
    pooled_ref[...] = jnp.concatenate(outs, axis=-1)
    for j in range(POOL_BUF - 1):
        new_ref[0, j] = rows[j + 1]
    new_ref[0, POOL_BUF - 1] = u


def _pool_sample(state, u):
    n = u.shape[0]
    return pl.pallas_call(
        _pool_sample_kernel,
        grid=(1,),
        in_specs=[_const_spec(state.shape), _const_spec(u.shape)],
        out_specs=[_const_spec(u.shape), _const_spec(state.shape)],
        out_shape=[jax.ShapeDtypeStruct((n, POOL_W), F32),
                   jax.ShapeDtypeStruct(state.shape, F32)],
        compiler_params=_cparams(1),
        name="pool_sample",
    )(state, u)


def _post_kernel(*refs, tm, dils, pooled_given, n_valid_steps, aliased):
    it = iter(refs)
    x_ref = next(it)
    o_refs = [next(it) for _ in range(N_GROUPS)]
    lse_refs = [next(it) for _ in range(N_GROUPS)]
    if pooled_given:
        pooled_ref = next(it)
    else:
        u_ref, uh_ref = next(it), next(it)
    sga_ref, sgp_ref = next(it), next(it)
    wpm_ref, psc_ref, wua_ref, wup_ref, wout_ref, exp_ref = (next(it) for _ in range(6))
    gt1_ref, g2_ref, sc2_ref, sh2_ref = (next(it) for _ in range(4))
    wrh_ref, wrl_ref, br_ref = (next(it) for _ in range(3))
    if aliased:
        for _ in range(6):
            next(it)
    x1_ref, h2_ref, a_ref, idx_ref, gk_ref, cnt_ref = (next(it) for _ in range(6))
    ob_scr, ls_scr = next(it), next(it)

    i = pl.program_id(0)

    def compute():
        obs, lss = [], []
        for g, d in enumerate(dils):
            if d == 1:
                obs.append(o_refs[g][...].astype(F32))
                lss.append(lse_refs[g][...])
            else:
                for r in range(d):
                    for c in range(GROUP_W // LANES):
                        col = r * GROUP_W + c * LANES
                        ob_scr[c, pl.ds(r, tm // d, stride=d), :] = (
                            o_refs[g][:, col:col + LANES].astype(F32))
                    ls_scr[pl.ds(r, tm // d, stride=d), :] = (
                        lse_refs[g][:, r * LANES:(r + 1) * LANES])
                obs.append(jnp.concatenate([ob_scr[c] for c in range(GROUP_W // LANES)],
                                           axis=-1))
                lss.append(ls_scr[...])
        mx = jnp.maximum(jnp.maximum(lss[0], lss[1]), lss[2])
        es = [jnp.exp(l - mx) for l in lss]
        den = es[0] + es[1] + es[2]
        attn_o = jnp.zeros((tm, GROUP_W), F32)
        head_lane = lax.broadcasted_iota(jnp.int32, (tm, LANES), 1) < HEADS
        for g in range(N_GROUPS):
            w = jnp.where(head_lane, es[g] / den, 0.0)
            w_hi = w.astype(BF16).astype(F32)
            w_lo = (w - w_hi).astype(BF16).astype(F32)
            lhs = (w_hi + pltpu.roll(w_lo, HEADS, 1)).astype(BF16)
            wexp = jnp.dot(lhs, exp_ref[...], preferred_element_type=F32)
            attn_o = attn_o + wexp * obs[g]

        if pooled_given:
            pooled = pooled_ref[...]
        else:
            u = u_ref[...]
            halo = jnp.where(i == 0, 0.0, uh_ref[...])
            pos = (lax.broadcasted_iota(jnp.int32, (tm, 1), 0) + i * tm + 1).astype(F32)
            outs = []
            for g, w in enumerate(POOL_WINDOWS):
                sl = slice(g * POOL_GW, (g + 1) * POOL_GW)
                a = jnp.concatenate([halo[:, sl], u[:, sl]], axis=0)
                span = 1
                while span < w:
                    n = a.shape[0] - span
                    a = a[span:, :] + a[:n, :]
                    span *= 2
                off = a.shape[0] - tm
                win_sum = a[off:, :]
                outs.append(win_sum / jnp.minimum(pos, float(w)) - u[:, sl])
            pooled = jnp.concatenate(outs, axis=-1)
        pool_parts = []
        for g in range(len(POOL_WINDOWS)):
            sl = slice(g * POOL_GW, (g + 1) * POOL_GW)
            pool_parts.append(jnp.dot(pooled[:, sl].astype(BF16), wpm_ref[g],
                                      preferred_element_type=F32))
        pool_o = jnp.concatenate(pool_parts, axis=-1) * psc_ref[...]

        up_a = jnp.dot(attn_o.astype(BF16), wua_ref[...], preferred_element_type=F32)
        up_p = jnp.dot(pool_o.astype(BF16), wup_ref[...], preferred_element_type=F32)
        merged = sga_ref[...].astype(F32) * up_a + sgp_ref[...].astype(F32) * up_p
        mo = jnp.dot(merged.astype(BF16), wout_ref[...], preferred_element_type=F32)
        x1 = x_ref[...] + gt1_ref[...] * mo
        x1_ref[...] = x1

        ms = jnp.mean(x1 * x1, axis=-1, keepdims=True)
        h2 = x1 * lax.rsqrt(ms + EPS) * g2_ref[...] * (1.0 + sc2_ref[...]) + sh2_ref[...]
        h2_hi = h2.astype(BF16)
        h2_ref[...] = h2_hi
        h2_lo = (h2 - h2_hi.astype(F32)).astype(BF16)
        logits = (jnp.dot(h2_hi, wrh_ref[...], preferred_element_type=F32)
                  + jnp.dot(h2_lo, wrh_ref[...], preferred_element_type=F32)
                  + jnp.dot(h2_hi, wrl_ref[...], preferred_element_type=F32)
                  + br_ref[...])
        lane = lax.broadcasted_iota(jnp.int32, (tm, LANES), 1).astype(F32)
        work = logits
        vals, ids = [], []
        for _ in range(TOP_K):
            m = jnp.max(work, axis=-1, keepdims=True)
            ik = jnp.min(jnp.where(work == m, lane, float(LANES)), axis=-1, keepdims=True)
            vals.append(m)
            ids.append(ik)
            work = jnp.where(lane == ik, -3e38, work)
        ex = [jnp.exp(v - vals[0]) for v in vals]
        den_k = ex[0] + ex[1] + ex[2] + ex[3]
        a = jnp.zeros((tm, LANES), F32)
        idx = jnp.zeros((tm, LANES), F32)
        gk = jnp.zeros((tm, LANES), F32)
        for kk in range(TOP_K):
            gate = ex[kk] / den_k
            a = a + jnp.where(lane == ids[kk], gate, 0.0)
            idx = jnp.where(lane == float(kk), ids[kk], idx)
            gk = jnp.where(lane == float(kk), gate, gk)
        a_ref[...] = a
        idx_ref[...] = idx
        gk_ref[...] = gk
        cnt = jnp.sum((a > 0.0).astype(F32), axis=0, keepdims=True)
        row = lax.broadcasted_iota(jnp.int32, (8, LANES), 0)
        cnt_ref[0] = jnp.where(row == 0, jnp.broadcast_to(cnt, (8, LANES)), 0.0)

    if n_valid_steps is None:
        compute()
    else:
        pl.when(i < n_valid_steps)(compute)

        @pl.when(i >= n_valid_steps)
        def _():
            x1_ref[...] = jnp.zeros(x1_ref.shape, x1_ref.dtype)
            h2_ref[...] = jnp.zeros(h2_ref.shape, h2_ref.dtype)
            a_ref[...] = jnp.zeros(a_ref.shape, a_ref.dtype)
            idx_ref[...] = jnp.zeros(idx_ref.shape, idx_ref.dtype)
            gk_ref[...] = jnp.zeros(gk_ref.shape, gk_ref.dtype)


def _post(x, o_list, lse_list, pool_in, sga, sgp, wts, mods, *, tm, dils, per_row,
          rows_total, row_block0, cnt_tiles, cnt_block, grid, n_valid_steps, alias_bufs):
    pooled_given = not isinstance(pool_in, tuple)
    nv = grid if n_valid_steps is None else n_valid_steps

    def clamp(i):
        return jnp.minimum(i, nv - 1)

    def tile_spec(width):
        return pl.BlockSpec((tm, width), lambda i: (clamp(i), 0))

    in_specs = [tile_spec(D_MODEL)]
    in_specs += [pl.BlockSpec((tm // d, d * GROUP_W), lambda i: (clamp(i), 0)) for d in dils]
    in_specs += [pl.BlockSpec((tm // d, d * LANES), lambda i: (clamp(i), 0)) for d in dils]
    args = [x, *o_list, *lse_list]
    if pooled_given:
        in_specs.append(tile_spec(POOL_W))
        args.append(pool_in)
    else:
        u = pool_in[0]
        in_specs += [tile_spec(POOL_W),
                     pl.BlockSpec((POOL_HALO, POOL_W),
                                  lambda i: (jnp.maximum(i * (tm // POOL_HALO) - 1, 0), 0))]
        args += [u, u]
    in_specs += [tile_spec(D_MODEL), tile_spec(D_MODEL)]
    args += [sga, sgp]
    wpm, psc, wua, wup, wout, expand, g2, wrh, wrl, br = wts
    gt1, sc2, sh2 = mods

    def mspec():
        if per_row:
            return pl.BlockSpec((tm, D_MODEL), lambda i: (clamp(i), 0))
        return _const_spec((1, D_MODEL))

    in_specs += [_const_spec(wpm.shape), _const_spec(psc.shape), _const_spec(wua.shape),
                 _const_spec(wup.shape), _const_spec(wout.shape), _const_spec(expand.shape),
                 mspec(), _const_spec(g2.shape), mspec(), mspec(),
                 _const_spec(wrh.shape), _const_spec(wrl.shape), _const_spec(br.shape)]
    args += [wpm, psc, wua, wup, wout, expand, gt1, g2, sc2, sh2, wrh, wrl, br]
    aliases = {}
    if alias_bufs is not None:
        base = len(args)
        in_specs += [pl.BlockSpec(memory_space=pl.ANY)] * 6
        args += list(alias_bufs)
        aliases = {base + j: j for j in range(6)}

    def out_spec(width):
        return pl.BlockSpec((tm, width), lambda i: (row_block0 + i, 0))

    out_specs = [out_spec(D_MODEL), out_spec(D_MODEL), out_spec(LANES), out_spec(LANES),
                 out_spec(LANES),
                 pl.BlockSpec((1, 8, LANES),
                              lambda i: (cnt_block if cnt_block is not None else i, 0, 0))]
    out_shape = [jax.ShapeDtypeStruct((rows_total, D_MODEL), F32),
                 jax.ShapeDtypeStruct((rows_total, D_MODEL), BF16),
                 jax.ShapeDtypeStruct((rows_total, LANES), F32),
                 jax.ShapeDtypeStruct((rows_total, LANES), F32),
                 jax.ShapeDtypeStruct((rows_total, LANES), F32),
                 jax.ShapeDtypeStruct((cnt_tiles, 8, LANES), F32)]
    return pl.pallas_call(
        functools.partial(_post_kernel, tm=tm, dils=dils, pooled_given=pooled_given,
                          n_valid_steps=n_valid_steps, aliased=alias_bufs is not None),
        grid=(grid,),
        in_specs=in_specs,
        out_specs=out_specs,
        out_shape=out_shape,
        scratch_shapes=[pltpu.VMEM((GROUP_W // LANES, tm, LANES), F32),
                        pltpu.VMEM((tm, LANES), F32)],
        input_output_aliases=aliases,
        compiler_params=_cparams(1),
        name="post_sample" if per_row else "post",
    )(*args)


def _sort_rows(tm):
    return -(-(TOP_K * tm + N_EXPERTS * (ROW_CHUNK - 1)) // SEL_CHUNK) * SEL_CHUNK


def _for_row_pieces(n_chunks, max_pow, fn):
    big = 1 << max_pow

    def body(c, carry):
        fn(c * big, big)
        return carry

    lax.fori_loop(0, n_chunks >> max_pow, body, 0)
    for pw in range(max_pow - 1, -1, -1):
        @pl.when(((n_chunks >> pw) & 1) == 1)
        def _(pw=pw):
            fn((n_chunks >> (pw + 1)) << (pw + 1), 1 << pw)


SEG_MAX_POW = 3
TILE_MAX_POW = 5


def _moe_sort_kernel(seg_s, goff_s, nch_s, ntot_s, tstart_s, tnch_s,
                     a_ref, idx_ref, h2_ref, segv_ref, lt_ref,
                     xb_hbm, dst_ref, xs_scr, zero_scr, sem, *, tm, n_rows):
    i = pl.program_id(0)
    nt = pl.num_programs(0)
    slot = i % 2
    sel = a_ref[...] > 0.0
    ahead = jnp.dot(lt_ref[...], sel.astype(BF16), preferred_element_type=F32)
    slot1 = jnp.where(sel, segv_ref[0] + ahead + 1.0, 0.0)
    lane = lax.broadcasted_iota(jnp.int32, (tm, LANES), 1).astype(F32)
    idx = idx_ref[...]
    dst = jnp.full((tm, LANES), -1.0, F32)
    for kk in range(TOP_K):
        hit = lane == idx[:, kk:kk + 1]
        dk = jnp.sum(jnp.where(hit, slot1, 0.0), axis=-1, keepdims=True) - 1.0
        dst = jnp.where(lane == float(kk), dk, dst)
    dst_ref[...] = dst
    dst_t = dst.T
    h2 = h2_ref[...]
    local_rows = lax.broadcasted_iota(jnp.int32, (BF16_EXACT, tm), 0).astype(F32).astype(BF16)
    one, zero = jnp.ones((), BF16), jnp.zeros((), BF16)
    for c in range(n_rows // BF16_EXACT):
        p = None
        for kk in range(TOP_K):
            hit = local_rows == (dst_t[kk:kk + 1, :] - float(c * BF16_EXACT)).astype(BF16)
            p = hit if p is None else p | hit
        xs = jnp.dot(jnp.where(p, one, zero), h2, preferred_element_type=F32)
        xs_scr[slot, c * BF16_EXACT:(c + 1) * BF16_EXACT, :] = xs.astype(BF16)

    def rows_copy(buf, src_row, dst_row, n_chunks):
        return pltpu.make_async_copy(
            xs_scr.at[buf, pl.ds(pl.multiple_of(src_row, ROW_CHUNK), n_chunks * ROW_CHUNK)],
            xb_hbm.at[pl.ds(pl.multiple_of(dst_row, ROW_CHUNK), n_chunks * ROW_CHUNK)],
            sem.at[buf])

    def per_expert(e, carry):
        so = seg_s[i * N_EXPERTS + e]
        go = goff_s[i * N_EXPERTS + e]
        _for_row_pieces(
            nch_s[i * N_EXPERTS + e], SEG_MAX_POW,
            lambda off, n: rows_copy(slot, so + off * ROW_CHUNK, go + off * ROW_CHUNK, n).start())
        return carry

    lax.fori_loop(0, N_EXPERTS, per_expert, 0)

    def drain(buf, tile):
        _for_row_pieces(ntot_s[tile], TILE_MAX_POW, lambda off, n: rows_copy(buf, 0, 0, n).wait())

    @pl.when(i > 0)
    def _():
        drain(1 - slot, i - 1)

    @pl.when(i == nt - 1)
    def _():
        drain(slot, i)
        zero_scr[...] = jnp.zeros(zero_scr.shape, zero_scr.dtype)

        def tail_copy(dst_row, n_chunks):
            return pltpu.make_async_copy(
                zero_scr.at[pl.ds(0, n_chunks * ROW_CHUNK)],
                xb_hbm.at[pl.ds(pl.multiple_of(dst_row, ROW_CHUNK), n_chunks * ROW_CHUNK)],
                sem.at[2])

        def per_expert_tail(e, carry):
            _for_row_pieces(tnch_s[e], SEG_MAX_POW,
                            lambda off, n: tail_copy(tstart_s[e] + off * ROW_CHUNK, n).start())
            _for_row_pieces(tnch_s[e], SEG_MAX_POW, lambda off, n: tail_copy(0, n).wait())
            return carry

        lax.fori_loop(0, N_EXPERTS, per_expert_tail, 0)


def _moe_sort(meta, a_all, idx_all, h2_all, *, tm, cap):
    t_all = a_all.shape[0]
    nt = t_all // tm
    n_rows = _sort_rows(tm)
    lt = jnp.tril(jnp.ones((tm, tm), BF16), -1)
    grid_spec = pltpu.PrefetchScalarGridSpec(
        num_scalar_prefetch=6,
        grid=(nt,),
        in_specs=[pl.BlockSpec((tm, LANES), lambda i, *_: (i, 0)),
                  pl.BlockSpec((tm, LANES), lambda i, *_: (i, 0)),
                  pl.BlockSpec((tm, D_MODEL), lambda i, *_: (i, 0)),
                  pl.BlockSpec((1, 1, LANES), lambda i, *_: (i, 0, 0)),
                  pl.BlockSpec((tm, tm), lambda i, *_: (0, 0))],
        out_specs=[pl.BlockSpec(memory_space=pl.ANY),
                   pl.BlockSpec((tm, LANES), lambda i, *_: (i, 0))],
        scratch_shapes=[pltpu.VMEM((2, n_rows, D_MODEL), BF16),
                        pltpu.VMEM(((1 << SEG_MAX_POW) * ROW_CHUNK, D_MODEL), BF16),
                        pltpu.SemaphoreType.DMA((3,))],
    )
    return pl.pallas_call(
        functools.partial(_moe_sort_kernel, tm=tm, n_rows=n_rows),
        grid_spec=grid_spec,
        out_shape=[jax.ShapeDtypeStruct((cap, D_MODEL), BF16),
                   jax.ShapeDtypeStruct((t_all, LANES), F32)],
        compiler_params=_cparams(1),
        name="moe_sort",
    )(meta["seg"], meta["goff"], meta["nch"], meta["ntot"], meta["tstart"], meta["tnch"],
      a_all, idx_all, h2_all, meta["segv"], lt)


def _moe_ffn_kernel(be_s, nused_s, ord_s, next_s, rows_s,
                    x_ref, wgu_hbm, bgu_ref, wd_hbm, bd_ref, y_ref,
                    wgu_f32, wd_f32, wgu_bf, wd_bf, sem):
    b = pl.program_id(0)

    def weight_copies(e, slot):
        return (pltpu.make_async_copy(wgu_hbm.at[e], wgu_f32.at[slot], sem.at[0, slot]),
                pltpu.make_async_copy(wd_hbm.at[e], wd_f32.at[slot], sem.at[1, slot]))

    @pl.when(b < nused_s[0])
    def _():
        e = be_s[b]
        e_prev = be_s[jnp.maximum(b - 1, 0)]
        slot = ord_s[b] % 2

        @pl.when(b == 0)
        def _():
            for cp in weight_copies(e, slot):
                cp.start(priority=WEIGHT_DMA_PRIORITY)

        @pl.when((b == 0) | (e != e_prev))
        def _():
            for cp in weight_copies(e, slot):
                cp.wait()
            e_next = next_s[b]

            @pl.when(e_next >= 0)
            def _():
                for cp in weight_copies(e_next, 1 - slot):
                    cp.start(priority=WEIGHT_DMA_PRIORITY)

            wgu_bf[...] = wgu_f32[slot].astype(BF16)
            wd_bf[...] = wd_f32[slot].astype(BF16)

        def ffn_rows(n):
            hgu = jnp.dot(x_ref[0:n, :], wgu_bf[...], preferred_element_type=F32) + bgu_ref[0]
            d_ff = hgu.shape[1] // 2
            hg = jnp.minimum(hgu[:, :d_ff], SWIGLU_LIMIT)
            hu = jnp.clip(hgu[:, d_ff:], -SWIGLU_LIMIT, SWIGLU_LIMIT)
            act = hg * jax.nn.sigmoid(SWIGLU_ALPHA * hg) * (hu + 1.0)
            y = jnp.dot(act.astype(BF16), wd_bf[...], preferred_element_type=F32) + bd_ref[0]
            y_ref[0:n, :] = y.astype(y_ref.dtype)

        rows_here = rows_s[b]
        for n in range(FFN_SUB, FFN_BLOCK + 1, FFN_SUB):
            pl.when(rows_here == n)(functools.partial(ffn_rows, n))


def _moe_ffn(meta, xb, w_gate_up, b_gate_up, w_down, b_down):
    cap = xb.shape[0]
    nb = cap // FFN_BLOCK
    d_ff2 = w_gate_up.shape[2]

    def blk(b, be, nu, *_):
        return jnp.minimum(b, jnp.maximum(nu[0] - 1, 0))

    def row_blk(b, be, nu, *_):
        return (blk(b, be, nu), 0)

    def expert_blk(b, be, nu, *_):
        return (be[blk(b, be, nu)], 0, 0)

    grid_spec = pltpu.PrefetchScalarGridSpec(
        num_scalar_prefetch=5,
        grid=(nb,),
        in_specs=[pl.BlockSpec((FFN_BLOCK, D_MODEL), row_blk),
                  pl.BlockSpec(memory_space=pl.ANY),
                  pl.BlockSpec((1, 1, d_ff2), expert_blk),
                  pl.BlockSpec(memory_space=pl.ANY),
                  pl.BlockSpec((1, 1, D_MODEL), expert_blk)],
        out_specs=pl.BlockSpec((FFN_BLOCK, D_MODEL), row_blk),
        scratch_shapes=[pltpu.VMEM((2, D_MODEL, d_ff2), F32),
                        pltpu.VMEM((2, d_ff2 // 2, D_MODEL), F32),
                        pltpu.VMEM((D_MODEL, d_ff2), BF16),
                        pltpu.VMEM((d_ff2 // 2, D_MODEL), BF16),
                        pltpu.SemaphoreType.DMA((2, 2))],
    )
    return pl.pallas_call(
        _moe_ffn_kernel,
        grid_spec=grid_spec,
        out_shape=jax.ShapeDtypeStruct((cap, D_MODEL), BF16),
        compiler_params=_cparams(1),
        name="moe_ffn",
    )(meta["block_expert"], meta["n_used"], meta["block_ord"], meta["block_next"],
      meta["block_rows"], xb, w_gate_up,
      b_gate_up.reshape(N_EXPERTS, 1, d_ff2), w_down, b_down.reshape(N_EXPERTS, 1, D_MODEL))


def _moe_unsort_kernel(seg_s, goff_s, nch_s, ntot_s,
                       dst_ref, gk_ref, x1_ref, g2p_ref, g2s_ref, yb_hbm,
                       yp_ref, ys_ref, ybuf, sem, *, tm, n_rows, n_prompt_tiles):
    i = pl.program_id(0)
    nt = pl.num_programs(0)
    slot = i % 2

    def rows_copy(buf, src_row, dst_row, n_chunks):
        return pltpu.make_async_copy(
            yb_hbm.at[pl.ds(pl.multiple_of(src_row, ROW_CHUNK), n_chunks * ROW_CHUNK)],
            ybuf.at[buf, pl.ds(pl.multiple_of(dst_row, ROW_CHUNK), n_chunks * ROW_CHUNK)],
            sem.at[buf])

    def fetch(tile, buf):
        def per_expert(e, carry):
            so = seg_s[tile * N_EXPERTS + e]
            go = goff_s[tile * N_EXPERTS + e]
            _for_row_pieces(
                nch_s[tile * N_EXPERTS + e], SEG_MAX_POW,
                lambda off, n: rows_copy(buf, go + off * ROW_CHUNK, so + off * ROW_CHUNK,
                                         n).start())
            return carry

        lax.fori_loop(0, N_EXPERTS, per_expert, 0)

    @pl.when(i == 0)
    def _():
        ybuf[...] = jnp.zeros(ybuf.shape, ybuf.dtype)
        fetch(0, 0)

    @pl.when(i + 1 < nt)
    def _():
        fetch(i + 1, 1 - slot)

    _for_row_pieces(ntot_s[i], TILE_MAX_POW, lambda off, n: rows_copy(slot, 0, 0, n).wait())

    dst = dst_ref[...]
    gk = gk_ref[...]
    acc = jnp.zeros((tm, D_MODEL), F32)
    for c in range(n_rows // SEL_CHUNK):
        cols = (lax.broadcasted_iota(jnp.int32, (tm, SEL_CHUNK), 1) + c * SEL_CHUNK).astype(F32)
        q = jnp.zeros((tm, SEL_CHUNK), F32)
        for kk in range(TOP_K):
            q = jnp.where(cols == dst[:, kk:kk + 1], gk[:, kk:kk + 1], q)
        acc = acc + jnp.dot(q.astype(BF16), ybuf[slot, c * SEL_CHUNK:(c + 1) * SEL_CHUNK, :],
                            preferred_element_type=F32)

    @pl.when(i < n_prompt_tiles)
    def _():
        yp_ref[...] = x1_ref[...] + g2p_ref[...] * acc

    @pl.when(i >= n_prompt_tiles)
    def _():
        ys_ref[...] = x1_ref[...] + g2s_ref[...] * acc


def _moe_unsort(meta, dst_all, gk_all, x1_all, gt2_p, gt2_s, yb, *, tm, n_prompt_tiles):
    t_all = dst_all.shape[0]
    nt = t_all // tm
    n_rows = _sort_rows(tm)
    last_p = n_prompt_tiles - 1
    grid_spec = pltpu.PrefetchScalarGridSpec(
        num_scalar_prefetch=4,
        grid=(nt,),
        in_specs=[pl.BlockSpec((tm, LANES), lambda i, *_: (i, 0)),
                  pl.BlockSpec((tm, LANES), lambda i, *_: (i, 0)),
                  pl.BlockSpec((tm, D_MODEL), lambda i, *_: (i, 0)),
                  pl.BlockSpec((1, D_MODEL), lambda i, *_: (0, 0)),
                  pl.BlockSpec((tm, D_MODEL), lambda i, *_: (0, 0)),
                  pl.BlockSpec(memory_space=pl.ANY)],
        out_specs=[pl.BlockSpec((tm, D_MODEL), lambda i, *_: (jnp.minimum(i, last_p), 0)),
                   pl.BlockSpec((tm, D_MODEL), lambda i, *_: (0, 0))],
        scratch_shapes=[pltpu.VMEM((2, n_rows, D_MODEL), BF16),
                        pltpu.SemaphoreType.DMA((2,))],
    )
    return pl.pallas_call(
        functools.partial(_moe_unsort_kernel, tm=tm, n_rows=n_rows,
                          n_prompt_tiles=n_prompt_tiles),
        grid_spec=grid_spec,
        out_shape=[jax.ShapeDtypeStruct((n_prompt_tiles * tm, D_MODEL), F32),
                   jax.ShapeDtypeStruct((tm, D_MODEL), F32)],
        compiler_params=_cparams(1),
        name="moe_unsort",
    )(meta["seg"], meta["goff"], meta["nch"], meta["ntot"],
      dst_all, gk_all, x1_all, gt2_p, gt2_s, yb)


def _moe_meta(cnt, tm):
    nt = cnt.shape[0]
    cnt = cnt.astype(jnp.int32)
    cnt_pad = (cnt + ROW_CHUNK - 1) // ROW_CHUNK * ROW_CHUNK
    seg = jnp.cumsum(cnt_pad, axis=1) - cnt_pad
    rows_e = jnp.sum(cnt_pad, axis=0)
    region = (rows_e + FFN_BLOCK - 1) // FFN_BLOCK * FFN_BLOCK
    gstart = jnp.cumsum(region) - region
    goff = gstart[None, :] + jnp.cumsum(cnt_pad, axis=0) - cnt_pad
    nblk_e = region // FFN_BLOCK
    blk_end = jnp.cumsum(nblk_e)
    cap = _moe_cap(nt * tm, tm)
    blocks = jnp.arange(cap // FFN_BLOCK, dtype=jnp.int32)
    block_expert = jnp.minimum(
        jnp.sum((blk_end[None, :] <= blocks[:, None]).astype(jnp.int32), axis=1), N_EXPERTS - 1)
    segv = jnp.zeros((nt, 1, LANES), F32).at[:, 0, :N_EXPERTS].set(seg.astype(F32))
    used = nblk_e > 0
    experts = jnp.arange(N_EXPERTS, dtype=jnp.int32)
    ord_e = jnp.cumsum(used.astype(jnp.int32)) - 1
    later = (experts[None, :] > experts[:, None]) & used[None, :]
    next_e = jnp.min(jnp.where(later, experts[None, :], N_EXPERTS), axis=1)
    next_e = jnp.where(next_e < N_EXPERTS, next_e, -1).astype(jnp.int32)
    of_block = (block_expert[:, None] == experts[None, :]).astype(jnp.int32)
    first_block = jnp.sum(of_block * (blk_end - nblk_e)[None, :], axis=1)
    rows_left = jnp.sum(of_block * rows_e[None, :], axis=1) - (blocks - first_block) * FFN_BLOCK
    block_rows = jnp.clip((rows_left + FFN_SUB - 1) // FFN_SUB * FFN_SUB, FFN_SUB, FFN_BLOCK)
    return {
        "block_rows": block_rows.astype(jnp.int32),
        "block_ord": jnp.sum(of_block * ord_e[None, :].astype(jnp.int32), axis=1),
        "block_next": jnp.sum(of_block * next_e[None, :], axis=1),
        "seg": seg.reshape(-1), "goff": goff.reshape(-1).astype(jnp.int32),
        "nch": (cnt_pad // ROW_CHUNK).reshape(-1),
        "ntot": jnp.sum(cnt_pad, axis=1) // ROW_CHUNK,
        "tstart": (gstart + rows_e).astype(jnp.int32),
        "tnch": (region - rows_e) // ROW_CHUNK,
        "block_expert": block_expert,
        "n_used": blk_end[-1:].astype(jnp.int32),
        "segv": segv,
    }


def _moe_cap(t_all, tm):
    nt = t_all // tm
    worst = TOP_K * t_all + nt * N_EXPERTS * (ROW_CHUNK - 1) + N_EXPERTS * (FFN_BLOCK - ROW_CHUNK)
    return -(-worst // FFN_BLOCK) * FFN_BLOCK


def _t5_bucket(dist):
    max_exact = NUM_BUCKETS // 2
    d = dist.astype(jnp.int32)
    ratio = (jnp.log(jnp.maximum(d, 1).astype(F32) / max_exact)
             / math.log(MAX_DISTANCE / max_exact))
    large = jnp.minimum(max_exact + (ratio * (NUM_BUCKETS - max_exact)).astype(jnp.int32),
                        NUM_BUCKETS - 1)
    return jnp.where(d < max_exact, d, large)


def _step_bias(tab, dil):
    return tab[_t5_bucket(dil * jnp.arange(ATT_BLK + 1))].astype(F32).T


def _band_table(sb):
    return jnp.concatenate([sb[:, ::-1], jnp.full((HEADS, ATT_BLK - 1), NEG_INF, F32)], axis=1)


def _cache_table(sb, dil):
    on_grid = sb[:, :0:-1]
    if dil == 1:
        return on_grid
    off = jnp.full((HEADS, ATT_BLK, dil - 1), NEG_INF, F32)
    return jnp.concatenate([on_grid[:, :, None], off], axis=2).reshape(HEADS, ATT_BLK * dil)


def kernel(x_prompt, x_sample, cache_kv_w128, cache_kv_w512, cache_kv_w2048, state_pool, c_prompt,
           c_sample, w_ada, b_ada, norm_mix_g, norm_ffn_g, w_in, q_norm_g, k_norm_g, rel_bias,
           w_pool_mix, pool_scale, w_up_attn, w_up_pool, w_out, w_router, b_router, w_gate_up,
           b_gate_up, w_down, b_down):
    assert w_ada.shape[0] == 1, "one layer"
    seq = x_prompt.shape[1]
    n_s = x_sample.shape[0]
    assert x_prompt.shape[0] == 1 and x_sample.shape[1] == 1
    assert seq % (DIL_GROUPS[-1][1] * ATT_BLK * ATT_SUB) == 0 and seq % TM_PROMPT == 0
    assert n_s == TM_SAMPLE
    dils = tuple(d for _, d in DIL_GROUPS)
    caches = (cache_kv_w128, cache_kv_w512, cache_kv_w2048)

    w_in_bf = w_in[0].astype(BF16)
    heads_of = jnp.arange(GROUP_W) // HEAD_DIM
    half_heads = heads_of[:GROUP_W // 2]
    bdiag = (half_heads[:, None] == half_heads[None, :]).astype(BF16)
    qg = (jnp.tile(q_norm_g[0], HEADS) * SCALE).reshape(1, GROUP_W)
    kg = jnp.tile(k_norm_g[0], HEADS).reshape(1, GROUP_W)
    expand = ((jnp.arange(LANES)[:, None] % HEADS == heads_of[None, :])
              & (jnp.arange(LANES)[:, None] < 2 * HEADS)).astype(BF16)
    wr = jnp.zeros((D_MODEL, LANES), F32).at[:, :N_EXPERTS].set(w_router[0])
    wr_hi = wr.astype(BF16)
    wr_lo = (wr - wr_hi.astype(F32)).astype(BF16)
    br = jnp.full((1, LANES), NEG_INF, F32).at[0, :N_EXPERTS].set(b_router[0])
    wts = (w_pool_mix[0].astype(BF16), pool_scale[0].reshape(1, POOL_W),
           w_up_attn[0].astype(BF16), w_up_pool[0].astype(BF16), w_out[0].astype(BF16), expand,
           norm_ffn_g[0].reshape(1, D_MODEL), wr_hi, wr_lo, br)
    g1 = norm_mix_g[0].reshape(1, D_MODEL)

    n_c = 1 + n_s
    c_all = jnp.zeros((-(-n_c // 8) * 8, D_MODEL), F32).at[0:1].set(c_prompt).at[1:n_c].set(c_sample)
    mod = _ada(c_all, w_ada[0], b_ada[0])
    sh1, sc1, gt1, sh2, sc2, gt2 = jnp.split(mod, N_ADA, axis=-1)

    def prow(m):
        return m[0:1]

    def srows(m):
        return m[1:n_c]

    xp = x_prompt[0]
    q_p, k_p, v_p, u_p, sga_p, sgp_p, st_p = _proj(
        xp, g1, prow(sc1), prow(sh1), w_in_bf, bdiag, qg, kg,
        tm=TM_PROMPT, dils=dils, per_row=False)
    xs = x_sample[:, 0]
    ones = (1, 1, 1)
    q_s, _, _, u_s, sga_s, sgp_s, st_s = _proj(
        xs, g1, srows(sc1), srows(sh1), w_in_bf, bdiag, qg, kg,
        tm=TM_SAMPLE, dils=ones, per_row=True)
    pooled_s, pool_state_t = _pool_sample(jnp.transpose(state_pool, (0, 2, 1, 3)), u_s)
    pool_state_s = jnp.transpose(pool_state_t, (0, 2, 1, 3))

    def heads(a):
        return a.astype(F32).reshape(n_s, HEADS, HEAD_DIM)

    step_bias = [_step_bias(rel_bias[:, g * HEADS:(g + 1) * HEADS], d)
                 for g, (_, d) in enumerate(DIL_GROUPS)]

    def sample_group(g, part, n_parts):
        cache_t = jnp.transpose(caches[g], (0, 1, 3, 4, 5, 2))
        n_sub = n_s // n_parts
        return (heads(q_s[g]), heads(st_s[g][:, :GROUP_W]), heads(st_s[g][:, GROUP_W:]),
                step_bias[g][:, 0:1], _cache_table(step_bias[g], DIL_GROUPS[g][1]), cache_t,
                part * n_sub, n_sub)

    o_p, lse_p = [], []
    o_parts = [[None] * n for n in SAMPLE_PARTS]
    lse_parts = [[None] * n for n in SAMPLE_PARTS]
    for g, (_, d) in enumerate(DIL_GROUPS):
        hosted = SAMPLE_HOST[g]
        o, lse, sample_outs = _attn(
            q_p[g], k_p[g], v_p[g], _band_table(step_bias[g]), d,
            [sample_group(sg, part, SAMPLE_PARTS[sg]) for sg, part in hosted])
        o_p.append(o)
        lse_p.append(lse)
        for (sg, part), (os_g, lses_g) in zip(hosted, sample_outs):
            o_parts[sg][part] = os_g.reshape(-1, GROUP_W)
            lse_parts[sg][part] = lses_g[:, :, 0]
    o_s = [jnp.concatenate(parts, axis=0) for parts in o_parts]
    lse_s = [jnp.zeros((n_s, LANES), F32).at[:, :HEADS].set(jnp.concatenate(parts, axis=0))
             for parts in lse_parts]

    nt_p = seq // TM_PROMPT
    t_all = seq + TM_PROMPT
    bufs = _post(xp, o_p, lse_p, (u_p,), sga_p, sgp_p, wts, (prow(gt1), prow(sc2), prow(sh2)),
                 tm=TM_PROMPT, dils=dils, per_row=False, rows_total=t_all, row_block0=0,
                 cnt_tiles=nt_p + 1, cnt_block=None, grid=nt_p, n_valid_steps=None,
                 alias_bufs=None)
    bufs = _post(xs, o_s, lse_s, pooled_s, sga_s, sgp_s, wts,
                 (srows(gt1), srows(sc2), srows(sh2)),
                 tm=TM_SAMPLE, dils=ones, per_row=True, rows_total=t_all,
                 row_block0=seq // TM_SAMPLE, cnt_tiles=nt_p + 1, cnt_block=nt_p,
                 grid=TM_PROMPT // TM_SAMPLE, n_valid_steps=1, alias_bufs=bufs)
    x1_all, h2_all, a_all, idx_all, gk_all, cnt = bufs

    meta = _moe_meta(cnt[:, 0, :N_EXPERTS], TM_PROMPT)
    cap = _moe_cap(t_all, TM_PROMPT)
    xb, dst_all = _moe_sort(meta, a_all, idx_all, h2_all, tm=TM_PROMPT, cap=cap)
    yb = _moe_ffn(meta, xb, w_gate_up[0], b_gate_up[0], w_down[0], b_down[0])
    gt2_s = jnp.zeros((TM_PROMPT, D_MODEL), F32).at[:n_s].set(srows(gt2))
    y_p, y_s = _moe_unsort(meta, dst_all, gk_all, x1_all, prow(gt2), gt2_s, yb,
                           tm=TM_PROMPT, n_prompt_tiles=nt_p)

    def kv_state(st, rows):
        return st.reshape(1, 1, rows, 2, HEADS, HEAD_DIM)

    kv_p = [kv_state(st, st.shape[0]) for st in st_p]
    kv_s = [st.reshape(1, n_s, 1, 2, HEADS, HEAD_DIM) for st in st_s]
    pool_p = u_p[seq - POOL_BUF:].reshape(1, 1, POOL_BUF, POOL_W)
    return (y_p.reshape(1, seq, D_MODEL), y_s[:n_s].reshape(n_s, 1, D_MODEL),
            kv_p[0], kv_p[1], kv_p[2], pool_p, kv_s[0], kv_s[1], kv_s[2], pool_state_s)
```
